```python
import jax, jax.numpy as jnp
from jax import lax
import numpy as np

D_MODEL = 1024
BATCH = 2
SEQ = 8192
DEPTH = 1
DEC_BATCH = 32
DEC_SEQ = 8
PAST_LEN = 16384
PAGE_SIZE = 128

H_A = 8
HD_A = D_MODEL // 16
W_A = H_A * HD_A
MOBA_BLOCK = 256
MOBA_TOPK = 3
Q_BLOCK = 64
H_G = 4
DK_G = D_MODEL // 16
DV_G = D_MODEL // 8
QK_G = H_G * DK_G
V_G = H_G * DV_G
GLA_RANK = 16
GLA_TAU = 16.0
GLA_CHUNK = 64
D_FF = 4 * D_MODEL
EPS = 1e-6
SPLIT_SIZES = (W_A, W_A, W_A, QK_G, QK_G, V_G, V_G, GLA_RANK, D_MODEL, D_MODEL)
D_IN = sum(SPLIT_SIZES)

kernel_name = "moba_gla_gated_hybrid_step"


def rms_norm(x, g):
    xf = x.astype(jnp.float32)
    y = xf * lax.rsqrt(jnp.mean(xf * xf, axis=-1, keepdims=True) + EPS)
    return (y * g.astype(jnp.float32)).astype(x.dtype)


def alibi_slopes():
    return jnp.asarray(2.0 ** (-8.0 * (np.arange(H_A) + 1) / H_A), dtype=jnp.float32)


def moba_attention(q, k_parts, v_parts, q_pos):
    B, T, H, hd = q.shape
    L = sum(p.shape[1] for p in k_parts)
    nb = -(-L // MOBA_BLOCK)
    pad = nb * MOBA_BLOCK - L
    zpad = jnp.zeros((B, pad, H, hd), q.dtype)
    k_all = jnp.concatenate(list(k_parts) + [zpad], axis=1)
    v_all = jnp.concatenate(list(v_parts) + [zpad], axis=1)
    kb = k_all.reshape(B, nb, MOBA_BLOCK, H, hd)
    vb = v_all.reshape(B, nb, MOBA_BLOCK, H, hd)
    kmean = jnp.mean(kb, axis=2)
    k_eff = min(MOBA_TOPK, nb)
    slopes = alibi_slopes()
    scale = HD_A ** -0.5
    bi = jnp.arange(B)[:, None, None, None]
    hi = jnp.arange(H)[None, None, :, None]
    blk_ar = jnp.arange(nb)
    in_blk = jnp.arange(MOBA_BLOCK, dtype=jnp.int32)

    def attend_chunk(args):
        q_c, pos_c = args
        bt = pos_c // MOBA_BLOCK
        gate = jnp.einsum('bqhd,bnhd->bqhn', q_c, kmean).astype(jnp.float32)
        past_ok = blk_ar[None, :] < bt[:, None]
        gate = jnp.where(past_ok[None, :, None, :], gate, -jnp.inf)
        _, sel = lax.top_k(gate, k_eff)
        sel_ok = sel < bt[None, :, None, None]
        Qn = q_c.shape[1]
        own = jnp.broadcast_to(bt[None, :, None, None], (B, Qn, H, 1)).astype(sel.dtype)
        blocks = jnp.concatenate([sel, own], axis=-1)
        blk_ok = jnp.concatenate([sel_ok, jnp.ones_like(sel_ok[..., :1])], axis=-1)
        kg = kb[bi, blocks, :, hi]
        vg = vb[bi, blocks, :, hi]
        kpos = blocks[..., None].astype(jnp.int32) * MOBA_BLOCK + in_blk
        dist = pos_c[None, :, None, None, None] - kpos
        ok = blk_ok[..., None] & (dist >= 0)
        logits = jnp.einsum('bqhd,bqhnsd->bqhns', q_c, kg).astype(jnp.float32) * scale
        logits = logits - slopes[None, None, :, None, None] * dist.astype(jnp.float32)
        logits = jnp.where(ok, logits, -jnp.inf)
        shp = logits.shape
        p = jax.nn.softmax(logits.reshape(B, Qn, H, -1), axis=-1).reshape(shp)
        return jnp.einsum('bqhns,bqhnsd->bqhd', p.astype(vg.dtype), vg)

    qb = min(Q_BLOCK, T)
    nc = T // qb
    qc = jnp.moveaxis(q.reshape(B, nc, qb, H, hd), 1, 0)
    pc = q_pos.reshape(nc, qb)
    out = lax.map(attend_chunk, (qc, pc))
    return jnp.moveaxis(out, 0, 1).reshape(B, T, H, hd)


def gla_chunked(q, k, v, log_a, s0, chunk):
    B, T, H, DK = q.shape
    DV = v.shape[-1]
    n = T // chunk

    def to_chunks(a):
        return a.astype(jnp.float32).reshape(B, n, chunk, H, a.shape[-1]).transpose(1, 0, 3, 2, 4)

    tri = jnp.tril(jnp.ones((chunk, chunk), dtype=bool))

    def step(S, inp):
        qc, kc, vc, ac = inp
        b = jnp.cumsum(ac, axis=2)
        rel = jnp.where(tri[None, None, :, :, None], b[:, :, :, None, :] - b[:, :, None, :, :], -jnp.inf)
        attn = jnp.einsum('bhtd,bhsd,bhtsd->bhts', qc, kc, jnp.exp(rel))
        o = jnp.einsum('bhts,bhse->bhte', attn, vc) + jnp.einsum('bhtd,bhde->bhte', qc * jnp.exp(b), S)
        b_last = b[:, :, -1:, :]
        S = jnp.exp(b_last[:, :, 0, :])[..., None] * S + jnp.einsum('bhsd,bhse->bhde', kc * jnp.exp(b_last - b), vc)
        return S, o

    S, o = lax.scan(step, s0.astype(jnp.float32), (to_chunks(q), to_chunks(k), to_chunks(v), to_chunks(log_a)))
    o = o.transpose(1, 0, 3, 2, 4).reshape(B, T, H, DV)
    return o.astype(v.dtype), S.astype(s0.dtype)


def decoder_layer(x, k_past, v_past, q_pos, gla_s0, gla_chunk,
                  w_in, w_gla_gate, b_gla_gate, g_gla_norm, w_branch_a, w_branch_b, w_out,
                  w_up, w_down, g_pre_mix, g_post_mix, g_pre_mlp, g_post_mlp):
    B, T, _ = x.shape
    h = rms_norm(x, g_pre_mix)
    z = h @ w_in
    points = np.cumsum(SPLIT_SIZES)[:-1].tolist()
    qa, ka, va, qg, kg, vg, rg, lr, ga, gb = jnp.split(z, points, axis=-1)
    qa = qa.reshape(B, T, H_A, HD_A)
    ka = ka.reshape(B, T, H_A, HD_A)
    va = va.reshape(B, T, H_A, HD_A)
    k_parts = [ka] if k_past is None else [k_past, ka]
    v_parts = [va] if v_past is None else [v_past, va]
    oa = moba_attention(qa, k_parts, v_parts, q_pos)
    log_a = jax.nn.log_sigmoid((lr @ w_gla_gate + b_gla_gate).astype(jnp.float32)) / GLA_TAU
    og, s_new = gla_chunked(qg.reshape(B, T, H_G, DK_G) * (DK_G ** -0.5),
                            kg.reshape(B, T, H_G, DK_G),
                            vg.reshape(B, T, H_G, DV_G),
                            log_a.reshape(B, T, H_G, DK_G), gla_s0, gla_chunk)
    og = rms_norm(og, g_gla_norm).reshape(B, T, V_G) * jax.nn.silu(rg)
    merged = (jax.nn.sigmoid(ga) * (oa.reshape(B, T, W_A) @ w_branch_a)
              + jax.nn.sigmoid(gb) * (og @ w_branch_b))
    x = x + rms_norm(merged @ w_out, g_post_mix)
    f = jnp.square(jax.nn.relu(rms_norm(x, g_pre_mlp) @ w_up)) @ w_down
    x = x + rms_norm(f, g_post_mlp)
    return x, ka, va, s_new


def setup_inputs(seed: int = 0) -> dict:
    key = jax.random.key(seed)
    ks = jax.random.split(key, 24)
    n_pages = PAST_LEN // PAGE_SIZE
    n_used = DEC_BATCH * n_pages
    n_phys = n_used + n_used // 4 + 1
    f32 = jnp.float32

    def nrm(k, shape, scale):
        return jax.random.normal(k, shape, f32) * scale

    def gain(k, n):
        return 1.0 + 0.02 * jax.random.normal(k, (DEPTH, n), f32)

    page_table = jax.random.permutation(ks[0], n_phys)[:n_used].reshape(DEC_BATCH, n_pages).astype(jnp.int32)
    return {
        "x_prompt": nrm(ks[1], (BATCH, SEQ, D_MODEL), 1.0),
        "x_sample": nrm(ks[2], (DEC_BATCH, DEC_SEQ, D_MODEL), 1.0),
        "cache_k": nrm(ks[3], (DEPTH, n_phys, PAGE_SIZE, H_A, HD_A), 1.0),
        "cache_v": nrm(ks[4], (DEPTH, n_phys, PAGE_SIZE, H_A, HD_A), 1.0),
        "page_table": page_table,
        "state_gla": nrm(ks[5], (DEPTH, DEC_BATCH, H_G, DK_G, DV_G), 0.5),
        "w_in": nrm(ks[6], (DEPTH, D_MODEL, D_IN), D_MODEL ** -0.5),
        "w_gla_gate": nrm(ks[7], (DEPTH, GLA_RANK, QK_G), GLA_RANK ** -0.5),
        "b_gla_gate": nrm(ks[8], (DEPTH, QK_G), 0.1),
        "g_gla_norm": gain(ks[9], DV_G),
        "w_branch_a": nrm(ks[10], (DEPTH, W_A, D_MODEL), W_A ** -0.5),
        "w_branch_b": nrm(ks[11], (DEPTH, V_G, D_MODEL), V_G ** -0.5),
        "w_out": nrm(ks[12], (DEPTH, D_MODEL, D_MODEL), D_MODEL ** -0.5),
        "w_up": nrm(ks[13], (DEPTH, D_MODEL, D_FF), D_MODEL ** -0.5),
        "w_down": nrm(ks[14], (DEPTH, D_FF, D_MODEL), D_FF ** -0.5),
        "g_pre_mix": gain(ks[15], D_MODEL),
        "g_post_mix": gain(ks[16], D_MODEL),
        "g_pre_mlp": gain(ks[17], D_MODEL),
        "g_post_mlp": gain(ks[18], D_MODEL),
    }


def reference(x_prompt, x_sample, cache_k, cache_v, page_table, state_gla,
              w_in, w_gla_gate, b_gla_gate, g_gla_norm, w_branch_a, w_branch_b, w_out,
              w_up, w_down, g_pre_mix, g_post_mix, g_pre_mlp, g_post_mlp):
    Bp, Tp, _ = x_prompt.shape
    Bs, Ts, _ = x_sample.shape
    n_pages = page_table.shape[1]
    past_len = n_pages * PAGE_SIZE
    pos_prompt = jnp.arange(Tp, dtype=jnp.int32)
    pos_sample = past_len + jnp.arange(Ts, dtype=jnp.int32)
    gla_chunk_prompt = min(GLA_CHUNK, Tp)
    hp, hs = x_prompt, x_sample
    kp_l, vp_l, sp_l, ks_l, vs_l, ss_l = [], [], [], [], [], []
    for l in range(DEPTH):
        w = (w_in[l], w_gla_gate[l], b_gla_gate[l], g_gla_norm[l], w_branch_a[l], w_branch_b[l],
             w_out[l], w_up[l], w_down[l], g_pre_mix[l], g_post_mix[l], g_pre_mlp[l], g_post_mlp[l])
        s0_prompt = jnp.zeros((Bp, H_G, DK_G, DV_G), state_gla.dtype)
        hp, kp, vp, sp = decoder_layer(hp, None, None, pos_prompt, s0_prompt, gla_chunk_prompt, *w)
        k_past = cache_k[l][page_table].reshape(Bs, past_len, H_A, HD_A)
        v_past = cache_v[l][page_table].reshape(Bs, past_len, H_A, HD_A)
        hs, ksn, vsn, ssn = decoder_layer(hs, k_past, v_past, pos_sample, state_gla[l], Ts, *w)
        kp_l.append(kp); vp_l.append(vp); sp_l.append(sp)
        ks_l.append(ksn); vs_l.append(vsn); ss_l.append(ssn)
    return (hp, hs, jnp.stack(kp_l), jnp.stack(vp_l), jnp.stack(sp_l),
            jnp.stack(ks_l), jnp.stack(vs_l), jnp.stack(ss_l))
```

```python
import functools

import numpy as np
import jax
import jax.numpy as jnp
from jax import lax
from jax.experimental import pallas as pl
from jax.experimental.pallas import tpu as pltpu

F32 = jnp.float32
BF16 = jnp.bfloat16

LANES = 128
SUBLANES = 8
VMEM_LIMIT_BYTES = 56 * 1024 * 1024

EPS = 1e-6
MOBA_BLOCK = 256
MOBA_TOPK = 3
GLA_TAU = 16.0
NEG_BIG = -1e30
PENALTY = -30000.0

_NT = (((1,), (1,)), ((), ()))
_TN = (((0,), (0,)), ((), ()))


def _params(*sem):
    return pltpu.CompilerParams(dimension_semantics=sem, vmem_limit_bytes=VMEM_LIMIT_BYTES)


def _sigmoid(x):
    return 1.0 / (1.0 + jnp.exp(-x))


def _rms(x, g):
    return x * lax.rsqrt(jnp.mean(x * x, axis=-1, keepdims=True) + EPS) * g


def _bdot(a, b):
    return jnp.dot(a.astype(BF16), b.astype(BF16), preferred_element_type=F32)


def _bdot_nt(a, b):
    return lax.dot_general(a.astype(BF16), b.astype(BF16), _NT, preferred_element_type=F32)


def _proj_kernel(x_ref, g_ref, wm_ref, wlr_ref, wgt_ref, wgg_ref, bgg_ref,
                 qa_ref, ka_ref, va_ref, qg_ref, kg_ref, vg_ref, rg_ref, la_ref, sga_ref, sgb_ref, kmean_ref,
                 *, w_a, qk_g, v_g, d_model):
    hb = _rms(x_ref[...], g_ref[...]).astype(BF16)

    def proj(lo, n):
        return jnp.dot(hb, wm_ref[:, lo:lo + n], preferred_element_type=F32)

    qa_ref[...] = proj(0, w_a)
    ka = proj(w_a, w_a)
    ka_ref[...] = ka
    kmean_ref[0] = jnp.sum(ka, axis=0, keepdims=True) * (1.0 / ka.shape[0])
    va_ref[...] = proj(2 * w_a, w_a)
    lo = 3 * w_a
    qg_ref[...] = proj(lo, qk_g)
    kg_ref[...] = proj(lo + qk_g, qk_g)
    vg_ref[...] = proj(lo + 2 * qk_g, v_g)
    rg_ref[...] = proj(lo + 2 * qk_g + v_g, v_g)
    lr = jnp.dot(hb, wlr_ref[...], preferred_element_type=F32)
    xg = jnp.dot(lr.astype(BF16), wgg_ref[...], preferred_element_type=F32) + bgg_ref[...]
    log_sig = jnp.minimum(xg, 0.0) - jnp.log1p(jnp.exp(-jnp.abs(xg)))
    la_ref[...] = log_sig * (1.0 / GLA_TAU)
    ga = jnp.dot(hb, wgt_ref[:, :d_model], preferred_element_type=F32)
    sga_ref[...] = _sigmoid(ga).astype(BF16)
    gb = jnp.dot(hb, wgt_ref[:, d_model:], preferred_element_type=F32)
    sgb_ref[...] = _sigmoid(gb).astype(BF16)


def _proj(x, g, wm, wlr, wgt, wgg, bgg, *, w_a, qk_g, v_g):
    n, d_model = x.shape
    tm = MOBA_BLOCK
    assert n % tm == 0
    nt = n // tm
    row = lambda w: pl.BlockSpec((tm, w), lambda i: (i, 0))
    full = lambda a: pl.BlockSpec(a.shape, lambda i: (0,) * a.ndim)
    widths = (w_a, w_a, w_a, qk_g, qk_g, v_g, v_g, qk_g)
    out_shape = [jax.ShapeDtypeStruct((n, w), F32) for w in widths]
    out_shape += [jax.ShapeDtypeStruct((n, d_model), BF16)] * 2
    out_shape += [jax.ShapeDtypeStruct((nt, 1, w_a), F32)]
    out_specs = [row(w) for w in widths] + [row(d_model)] * 2
    out_specs += [pl.BlockSpec((1, 1, w_a), lambda i: (i, 0, 0))]
    return pl.pallas_call(
        functools.partial(_proj_kernel, w_a=w_a, qk_g=qk_g, v_g=v_g, d_model=d_model),
        grid=(nt,),
        in_specs=[row(d_model), full(g), full(wm), full(wlr), full(wgt), full(wgg), full(bgg)],
        out_specs=out_specs,
        out_shape=out_shape,
        compiler_params=_params("parallel"),
    )(x, g, wm, wlr, wgt, wgg, bgg)


def _top_blocks(gate, n_valid, lane):
    lane_f = lane.astype(F32)
    avail = jnp.where(lane < n_valid, 1.0, 0.0)
    sel = jnp.zeros(gate.shape, F32)
    for _ in range(MOBA_TOPK):
        cur = jnp.where(avail > 0.0, gate, -jnp.inf)
        mx = jnp.max(cur, axis=-1, keepdims=True)
        cand = jnp.where(avail > 0.0, jnp.where(cur == mx, lane_f, float(LANES)), float(LANES))
        first = jnp.min(cand, axis=-1, keepdims=True)
        pick = lane_f == first
        sel = jnp.where(pick, 1.0, sel)
        avail = jnp.where(pick, 0.0, avail)
    return sel


def _moba_prompt_kernel(slopes_ref, q_ref, k_ref, v_ref, kc_ref, km_ref, o_ref, kaug_ref, vb_ref, *, hd):
    hp = pl.program_id(1)
    i = pl.program_id(2)
    blk = MOBA_BLOCK
    t = k_ref.shape[1]

    @pl.when(i == 0)
    def _():
        def conv(c, carry):
            rows = pl.ds(pl.multiple_of(c * blk, blk), blk)
            kaug_ref[rows, :LANES] = k_ref[0, rows, :].astype(BF16)
            kaug_ref[rows, LANES:] = kc_ref[rows, :]
            vb_ref[rows, :] = v_ref[0, rows, :].astype(BF16)
            return carry
        lax.fori_loop(0, t // blk, conv, 0)

    lane = lax.broadcasted_iota(jnp.int32, (blk, LANES), 1)
    row2 = lax.broadcasted_iota(jnp.int32, (blk, blk), 0)
    col2 = lax.broadcasted_iota(jnp.int32, (blk, blk), 1)
    q = q_ref[0]
    scale = hd ** -0.5
    diag = pl.ds(pl.multiple_of(i * blk, blk), blk)
    kd = kaug_ref[diag, :]
    vd = vb_ref[diag, :]

    outs = []
    for hh in range(LANES // hd):
        head_lanes = (lane >= hh * hd) & (lane < (hh + 1) * hd)
        qh = jnp.where(head_lanes, q, 0.0)
        slope = slopes_ref[hp * (LANES // hd) + hh]
        gate = lax.dot_general(qh, km_ref[0], _NT, precision=lax.Precision.HIGHEST,
                               preferred_element_type=F32)
        sel = _top_blocks(gate, i, lane)
        n_blk = t // blk
        qx = jnp.where(lane < n_blk, jnp.where((sel > 0.0) | (lane == i), 0.0, PENALTY),
                       jnp.where(lane == n_blk, slope, 0.0))
        q_aug = jnp.concatenate([(qh * scale).astype(BF16), qx.astype(BF16)], axis=1)

        s = lax.dot_general(q_aug, kd, _NT, preferred_element_type=F32)
        s = jnp.where(row2 >= col2, s, NEG_BIG)
        m = jnp.max(s, axis=-1, keepdims=True)
        p = jnp.exp(s - m)
        l = jnp.sum(p, axis=-1, keepdims=True)
        acc = jnp.dot(p.astype(BF16), vd, preferred_element_type=F32)

        def body(n, carry, q_aug=q_aug, slope=slope):
            m, l, acc = carry
            rows = pl.ds(pl.multiple_of(n * blk, blk), blk)
            s = lax.dot_general(q_aug, kaug_ref[rows, :], _NT, preferred_element_type=F32)
            c = -slope * (blk * (i - n)).astype(F32)
            m_new = jnp.maximum(m, jnp.max(s, axis=-1, keepdims=True) + c)
            alpha = jnp.exp(m - m_new)
            p = jnp.exp(s - (m_new - c))
            l = alpha * l + jnp.sum(p, axis=-1, keepdims=True)
            acc = alpha * acc + jnp.dot(p.astype(BF16), vb_ref[rows, :], preferred_element_type=F32)
            return m_new, l, acc

        m, l, acc = lax.fori_loop(0, i, body, (m, l, acc))
        outs.append(acc / l)

    out = outs[0]
    for hh in range(1, len(outs)):
        out = jnp.where(lane >= hh * hd, outs[hh], out)
    o_ref[0] = out


def _moba_consts(t, n_lanes=LANES):
    n_blk = t // MOBA_BLOCK
    assert n_blk < n_lanes
    pos = np.arange(t)
    c = np.zeros((t, n_lanes), np.float32)
    c[pos, pos // MOBA_BLOCK] = 1.0
    c[:, n_blk] = pos % MOBA_BLOCK
    return jnp.asarray(c, dtype=BF16)


def _moba_prompt(qa, ka, va, kmean, slopes, *, hd):
    b, t, w = qa.shape
    blk = MOBA_BLOCK
    n_pair = w // LANES
    n_blk = t // blk
    kc = _moba_consts(t)
    km = jnp.pad(kmean, ((0, 0), (0, LANES - n_blk), (0, 0)))
    return pl.pallas_call(
        functools.partial(_moba_prompt_kernel, hd=hd),
        grid=(b, n_pair, n_blk),
        in_specs=[
            pl.BlockSpec(memory_space=pltpu.SMEM),
            pl.BlockSpec((1, blk, LANES), lambda bi, hp, i: (bi, i, hp)),
            pl.BlockSpec((1, t, LANES), lambda bi, hp, i: (bi, 0, hp)),
            pl.BlockSpec((1, t, LANES), lambda bi, hp, i: (bi, 0, hp)),
            pl.BlockSpec((t, LANES), lambda bi, hp, i: (0, 0)),
            pl.BlockSpec((1, LANES, LANES), lambda bi, hp, i: (bi, 0, hp)),
        ],
        out_specs=pl.BlockSpec((1, blk, LANES), lambda bi, hp, i: (bi, i, hp)),
        out_shape=jax.ShapeDtypeStruct((b, t, w), F32),
        scratch_shapes=[pltpu.VMEM((t, 2 * LANES), BF16), pltpu.VMEM((t, LANES), BF16)],
        compiler_params=_params("parallel", "parallel", "arbitrary"),
    )(slopes, qa, ka, va, kc, km)


def _page_specs(n_per_step, page, width):
    def spec(p):
        return pl.BlockSpec((1, page, width), lambda bi, j, pt: (pt[bi, j * n_per_step + p], 0, 0))
    return [spec(p) for p in range(n_per_step)]


def _cache_kmean_kernel(pt_ref, *refs, n_per_step, pages_per_block):
    page_refs, o_ref = refs[:n_per_step], refs[n_per_step]
    page = page_refs[0].shape[1]
    inv = 1.0 / (page * pages_per_block)
    for g in range(n_per_step // pages_per_block):
        acc = jnp.sum(page_refs[g * pages_per_block][0], axis=0, keepdims=True)
        for p in range(1, pages_per_block):
            acc = acc + jnp.sum(page_refs[g * pages_per_block + p][0], axis=0, keepdims=True)
        o_ref[0, g] = acc * inv


def _cache_kmean(cache, page_table, *, n_per_step):
    b, n_pages = page_table.shape
    _, page, w = cache.shape
    ppb = MOBA_BLOCK // page
    assert n_pages % n_per_step == 0 and n_per_step % ppb == 0
    n_blk = n_pages // ppb
    g = n_per_step // ppb
    return pl.pallas_call(
        functools.partial(_cache_kmean_kernel, n_per_step=n_per_step, pages_per_block=ppb),
        grid_spec=pltpu.PrefetchScalarGridSpec(
            num_scalar_prefetch=1,
            grid=(b, n_pages // n_per_step),
            in_specs=_page_specs(n_per_step, page, w),
            out_specs=pl.BlockSpec((1, g, 1, w), lambda bi, j, pt: (bi, j, 0, 0)),
        ),
        out_shape=jax.ShapeDtypeStruct((b, n_blk, 1, w), F32),
        compiler_params=_params("parallel", "arbitrary"),
    )(page_table, *([cache] * n_per_step))


def _moba_sample_kernel(pt_ref, slopes_ref, q_ref, kn_ref, vn_ref, km_ref, *refs,
                        n_per_step, pages_per_block, hd, n_head):
    k_refs, v_refs = refs[:n_per_step], refs[n_per_step:2 * n_per_step]
    o_ref, qbd_ref, pen_ref, m_ref, l_ref, acc_ref = refs[2 * n_per_step:]
    j = pl.program_id(1)
    n_steps = pl.num_programs(1)
    ts = q_ref.shape[1]
    w = n_head * hd
    rows = n_head * ts
    page = k_refs[0].shape[1]
    n_blk = km_ref.shape[1]
    past = n_blk * MOBA_BLOCK
    scale = hd ** -0.5

    row_head = lax.broadcasted_iota(jnp.int32, (rows, 1), 0) // ts
    row_q = lax.broadcasted_iota(jnp.int32, (rows, 1), 0) % ts
    slope = jnp.zeros((rows, 1), F32)
    for h in range(n_head):
        slope = jnp.where(row_head == h, slopes_ref[h], slope)

    @pl.when(j == 0)
    def _():
        q = q_ref[0]
        qt = jnp.concatenate([q] * n_head, axis=0)
        col_head = lax.broadcasted_iota(jnp.int32, (rows, w), 1) // hd
        qbd = jnp.where(col_head == row_head, qt, 0.0)
        qbd_ref[...] = qbd
        gate = lax.dot_general(qbd, km_ref[0], _NT, precision=lax.Precision.HIGHEST,
                               preferred_element_type=F32)
        lane = lax.broadcasted_iota(jnp.int32, gate.shape, 1)
        sel = _top_blocks(gate, n_blk, lane)
        pen_ref[...] = jnp.where(sel > 0.0, 0.0, NEG_BIG)
        kn = kn_ref[0]
        s = _bdot_nt(qbd * scale, kn)
        kq = lax.broadcasted_iota(jnp.int32, (rows, ts), 1)
        s = jnp.where(kq <= row_q, s + slope * kq.astype(F32), NEG_BIG)
        m = jnp.max(s, axis=-1, keepdims=True)
        p = jnp.exp(s - m)
        m_ref[...] = m
        l_ref[...] = jnp.sum(p, axis=-1, keepdims=True)
        acc_ref[...] = _bdot(p, vn_ref[0])

    qs = (qbd_ref[...] * scale).astype(BF16)
    pen = pen_ref[...]
    pen_lane = lax.broadcasted_iota(jnp.int32, pen.shape, 1)
    key_in_page = lax.broadcasted_iota(jnp.int32, (1, page), 1)
    m = m_ref[...]
    l = l_ref[...]
    acc = acc_ref[...]
    for p_i in range(n_per_step):
        page_idx = j * n_per_step + p_i
        blk_idx = page_idx // pages_per_block
        s = lax.dot_general(qs, k_refs[p_i][0].astype(BF16), _NT, preferred_element_type=F32)
        rel = (page_idx * page - past + key_in_page).astype(F32)
        pen_col = jnp.sum(jnp.where(pen_lane == blk_idx, pen, 0.0), axis=-1, keepdims=True)
        s = s + slope * rel + pen_col
        m_new = jnp.maximum(m, jnp.max(s, axis=-1, keepdims=True))
        alpha = jnp.exp(m - m_new)
        p = jnp.exp(s - m_new)
        l = alpha * l + jnp.sum(p, axis=-1, keepdims=True)
        acc = alpha * acc + jnp.dot(p.astype(BF16), v_refs[p_i][0].astype(BF16), preferred_element_type=F32)
        m = m_new
    m_ref[...] = m
    l_ref[...] = l
    acc_ref[...] = acc

    @pl.when(j == n_steps - 1)
    def _():
        res = acc / l
        col_head = lax.broadcasted_iota(jnp.int32, (rows, w), 1) // hd
        res = jnp.where(col_head == row_head, res, 0.0)
        out = res[0:ts]
        for h in range(1, n_head):
            out = out + res[h * ts:(h + 1) * ts]
        o_ref[0] = out


def _moba_sample(qa, ka, va, cache_k, cache_v, page_table, kmean, slopes, *, hd, n_per_step):
    b, ts, w = qa.shape
    _, page, _ = cache_k.shape
    n_pages = page_table.shape[1]
    ppb = MOBA_BLOCK // page
    n_blk = n_pages // ppb
    n_head = w // hd
    rows = n_head * ts
    per_b = lambda shape: pl.BlockSpec(shape, lambda bi, j, pt: (bi, 0, 0))
    return pl.pallas_call(
        functools.partial(_moba_sample_kernel, n_per_step=n_per_step, pages_per_block=ppb, hd=hd, n_head=n_head),
        grid_spec=pltpu.PrefetchScalarGridSpec(
            num_scalar_prefetch=1,
            grid=(b, n_pages // n_per_step),
            in_specs=[pl.BlockSpec(memory_space=pltpu.SMEM),
                      per_b((1, ts, w)), per_b((1, ts, w)), per_b((1, ts, w)), per_b((1, n_blk, w))]
                     + _page_specs(n_per_step, page, w) * 2,
            out_specs=per_b((1, ts, w)),
            scratch_shapes=[pltpu.VMEM((rows, w), F32), pltpu.VMEM((rows, n_blk), F32),
                            pltpu.VMEM((rows, 1), F32), pltpu.VMEM((rows, 1), F32), pltpu.VMEM((rows, w), F32)],
        ),
        out_shape=jax.ShapeDtypeStruct((b, ts, w), F32),
        compiler_params=_params("parallel", "arbitrary"),
    )(page_table, slopes, qa, ka, va, kmean, *([cache_k] * n_per_step), *([cache_v] * n_per_step))


def _gla_kernel(q_ref, k_ref, v_ref, la_ref, rg_ref, gn_ref, s0_ref, o_ref, sout_ref, st_ref, *, dk, dv):
    c_idx = pl.program_id(1)
    n_chunks = pl.num_programs(1)
    c = q_ref.shape[1]
    n_pair = q_ref.shape[2] // LANES
    per_pair = LANES // dk
    assert dv == LANES and c % SUBLANES == 0

    @pl.when(c_idx == 0)
    def _():
        for p in range(n_pair):
            st_ref[p] = s0_ref[0, p].T

    row = lax.broadcasted_iota(jnp.int32, (c, LANES), 0)
    lane = lax.broadcasted_iota(jnp.int32, (c, LANES), 1)
    row2 = lax.broadcasted_iota(jnp.int32, (c, c), 0)
    col2 = lax.broadcasted_iota(jnp.int32, (c, c), 1)
    tri = jnp.where(row2 >= col2, 1.0, 0.0)
    b_all = jnp.dot(tri, la_ref[0], precision=lax.Precision.HIGHEST, preferred_element_type=F32)
    nb8 = c // SUBLANES
    sub = lax.broadcasted_iota(jnp.int32, (nb8, SUBLANES, LANES), 1)
    lane3 = lax.broadcasted_iota(jnp.int32, (nb8, SUBLANES, LANES), 2)
    lane_st = lax.broadcasted_iota(jnp.int32, (dv, LANES), 1)

    def bcast_row(x3, jj):
        return jnp.broadcast_to(x3[:, jj:jj + 1, :], x3.shape)

    for p in range(n_pair):
        cols = slice(p * LANES, (p + 1) * LANES)
        q = q_ref[0, :, cols] * (dk ** -0.5)
        k = k_ref[0, :, cols]
        b = b_all[:, cols]
        vs = [v_ref[0, :, (p * per_pair + hh) * dv:(p * per_pair + hh + 1) * dv] for hh in range(per_pair)]
        head_lanes = [(lane >= hh * dk) & (lane < (hh + 1) * dk) for hh in range(per_pair)]
        head_lanes3 = [(lane3 >= hh * dk) & (lane3 < (hh + 1) * dk) for hh in range(per_pair)]
        st = st_ref[p]

        qe = q * jnp.exp(b)
        o = [_bdot_nt(jnp.where(head_lanes[hh], qe, 0.0), st) for hh in range(per_pair)]

        q3 = q.reshape(nb8, SUBLANES, LANES)
        k3 = k.reshape(nb8, SUBLANES, LANES)
        b3 = b.reshape(nb8, SUBLANES, LANES)
        v3 = [v.reshape(nb8, SUBLANES, dv) for v in vs]
        o3 = [jnp.zeros((nb8, SUBLANES, dv), F32) for _ in range(per_pair)]
        for jj in range(SUBLANES):
            e = jnp.exp(jnp.minimum(b3 - bcast_row(b3, jj), 0.0))
            term = jnp.where(sub >= jj, q3 * bcast_row(k3, jj) * e, 0.0)
            for hh in range(per_pair):
                a = jnp.sum(jnp.where(head_lanes3[hh], term, 0.0), axis=-1, keepdims=True)
                o3[hh] = o3[hh] + a * bcast_row(v3[hh], jj)
        o = [o[hh] + o3[hh].reshape(c, dv) for hh in range(per_pair)]

        attn = [jnp.zeros((c, c), F32) for _ in range(per_pair)]
        m_half = SUBLANES
        while 2 * m_half <= c:
            span = 2 * m_half
            bnd = jnp.broadcast_to(b.reshape(c // span, span, LANES)[:, m_half - 1:m_half, :],
                                   (c // span, span, LANES)).reshape(c, LANES)
            upper = (row % span) >= m_half
            qm = jnp.where(upper, q * jnp.exp(jnp.minimum(b - bnd, 0.0)), 0.0)
            km = jnp.where(upper, 0.0, k * jnp.exp(jnp.minimum(bnd - b, 0.0)))
            same = (row2 // span) == (col2 // span)
            for hh in range(per_pair):
                a = _bdot_nt(jnp.where(head_lanes[hh], qm, 0.0), km)
                attn[hh] = attn[hh] + jnp.where(same, a, 0.0)
            m_half = span
        if c > SUBLANES:
            o = [o[hh] + _bdot(attn[hh], vs[hh]) for hh in range(per_pair)]

        b_last = b[c - 1:c, :]
        kk = (k * jnp.exp(b_last - b)).astype(BF16)
        upd = lax.dot_general(vs[0].astype(BF16), kk, _TN, preferred_element_type=F32)
        for hh in range(1, per_pair):
            u = lax.dot_general(vs[hh].astype(BF16), kk, _TN, preferred_element_type=F32)
            upd = jnp.where(lane_st >= hh * dk, u, upd)
        st_new = st * jnp.exp(b_last) + upd
        st_ref[p] = st_new

        for hh in range(per_pair):
            h = p * per_pair + hh
            rg = rg_ref[0, :, h * dv:(h + 1) * dv]
            o_ref[0, :, h * dv:(h + 1) * dv] = _rms(o[hh], gn_ref[...]) * (rg * _sigmoid(rg))

        @pl.when(c_idx == n_chunks - 1)
        def _(p=p, st_new=st_new):
            sout_ref[0, p] = st_new.T


def _gla(qg, kg, vg, la, rg, g_norm, s0, *, chunk, dk, dv):
    b, t, wk = qg.shape
    wv = vg.shape[2]
    n_head = wk // dk
    n_pair = wk // LANES
    s0p = s0.reshape(b, n_pair, LANES, dv)
    tok = lambda w: pl.BlockSpec((1, chunk, w), lambda bi, ci: (bi, ci, 0))
    st_spec = pl.BlockSpec((1, n_pair, LANES, dv), lambda bi, ci: (bi, 0, 0, 0))
    og, s_new = pl.pallas_call(
        functools.partial(_gla_kernel, dk=dk, dv=dv),
        grid=(b, t // chunk),
        in_specs=[tok(wk), tok(wk), tok(wv), tok(wk), tok(wv),
                  pl.BlockSpec(g_norm.shape, lambda bi, ci: (0, 0)), st_spec],
        out_specs=[tok(wv), st_spec],
        out_shape=[jax.ShapeDtypeStruct((b, t, wv), F32), jax.ShapeDtypeStruct((b, n_pair, LANES, dv), F32)],
        scratch_shapes=[pltpu.VMEM((n_pair, dv, LANES), F32)],
        compiler_params=_params("parallel", "arbitrary"),
    )(qg, kg, vg, la, rg, g_norm, s0p)
    return og, s_new.reshape(b, n_head, dk, dv)


def _merge_kernel(x_ref, oa_ref, og_ref, sga_ref, sgb_ref, wa_ref, wb_ref, wo_ref, g_ref, y_ref):
    merged = (sga_ref[...].astype(F32) * _bdot(oa_ref[...], wa_ref[...])
              + sgb_ref[...].astype(F32) * _bdot(og_ref[...], wb_ref[...]))
    y_ref[...] = x_ref[...] + _rms(_bdot(merged, wo_ref[...]), g_ref[...])


def _mlp_kernel(x_ref, wu_ref, wd_ref, g1_ref, g2_ref, y_ref):
    x = x_ref[...]
    u = _bdot(_rms(x, g1_ref[...]), wu_ref[...])
    u = jnp.square(jnp.maximum(u, 0.0))
    y_ref[...] = x + _rms(_bdot(u, wd_ref[...]), g2_ref[...])


def _rowwise_call(kernel, row_inputs, const_inputs, out_width, tm):
    n = row_inputs[0].shape[0]
    assert n % tm == 0
    row = lambda a: pl.BlockSpec((tm, a.shape[1]), lambda i: (i, 0))
    full = lambda a: pl.BlockSpec(a.shape, lambda i: (0,) * a.ndim)
    return pl.pallas_call(
        kernel,
        grid=(n // tm,),
        in_specs=[row(a) for a in row_inputs] + [full(a) for a in const_inputs],
        out_specs=pl.BlockSpec((tm, out_width), lambda i: (i, 0)),
        out_shape=jax.ShapeDtypeStruct((n, out_width), F32),
        compiler_params=_params("parallel"),
    )(*row_inputs, *const_inputs)


def _gla_chunk(t):
    c = SUBLANES
    while c * 2 <= min(t, LANES) and t % (c * 2) == 0:
        c *= 2
    return c


def kernel(x_prompt, x_sample, cache_k, cache_v, page_table, state_gla, w_in, w_gla_gate, b_gla_gate, g_gla_norm,
           w_branch_a, w_branch_b, w_out, w_up, w_down, g_pre_mix, g_post_mix, g_pre_mlp, g_post_mlp):
    bp, tp, d_model = x_prompt.shape
    bs, ts, _ = x_sample.shape
    depth, n_phys, page, n_head, hd = cache_k.shape
    _, _, n_head_g, dk, dv = state_gla.shape
    w_a = n_head * hd
    qk_g = n_head_g * dk
    v_g = n_head_g * dv
    rank = w_gla_gate.shape[1]
    n_main = 3 * w_a + 2 * qk_g + 2 * v_g
    slopes = jnp.asarray(2.0 ** (-8.0 * (np.arange(n_head) + 1) / n_head), dtype=F32)

    hp = x_prompt.reshape(bp * tp, d_model)
    hs = x_sample.reshape(bs * ts, d_model)
    outs = [[] for _ in range(6)]
    for l in range(depth):
        wm = w_in[l, :, :n_main].astype(BF16)
        wlr = jnp.pad(w_in[l, :, n_main:n_main + rank], ((0, 0), (0, LANES - rank))).astype(BF16)
        wgt = w_in[l, :, n_main + rank:].astype(BF16)
        wgg = jnp.pad(w_gla_gate[l], ((0, LANES - rank), (0, 0))).astype(BF16)
        bgg = b_gla_gate[l][None, :]
        wa, wb, wo = (w[l].astype(BF16) for w in (w_branch_a, w_branch_b, w_out))
        wu, wd = w_up[l].astype(BF16), w_down[l].astype(BF16)
        g_mix, g_pm, g_mlp, g_pl, g_gn = (g[l][None, :] for g in
                                          (g_pre_mix, g_post_mix, g_pre_mlp, g_post_mlp, g_gla_norm))
        ck = cache_k[l].reshape(n_phys, page, w_a)
        cv = cache_v[l].reshape(n_phys, page, w_a)

        def layer(x, b, t, moba, s0):
            qa, ka, va, qg, kg, vg, rg, la, sga, sgb, kmean = _proj(
                x, g_mix, wm, wlr, wgt, wgg, bgg, w_a=w_a, qk_g=qk_g, v_g=v_g)
            r3 = lambda a: a.reshape(b, t, a.shape[-1])
            oa = moba(r3(qa), r3(ka), r3(va), kmean)
            og, s_new = _gla(r3(qg), r3(kg), r3(vg), r3(la), r3(rg), g_gn, s0, chunk=_gla_chunk(t), dk=dk, dv=dv)
            n = b * t
            x = _rowwise_call(_merge_kernel, [x, oa.reshape(n, w_a), og.reshape(n, v_g), sga, sgb],
                              [wa, wb, wo, g_pm], d_model, MOBA_BLOCK)
            x = _rowwise_call(_mlp_kernel, [x], [wu, wd, g_mlp, g_pl], d_model, MOBA_BLOCK)
            return x, ka.reshape(b, t, n_head, hd), va.reshape(b, t, n_head, hd), s_new

        def moba_prompt(qa, ka, va, kmean):
            return _moba_prompt(qa, ka, va, kmean.reshape(bp, tp // MOBA_BLOCK, w_a), slopes, hd=hd)

        def moba_sample(qa, ka, va, kmean_unused):
            n_pages = page_table.shape[1]
            n_per_step = 16 if n_pages % 16 == 0 else MOBA_BLOCK // page
            km = _cache_kmean(ck, page_table, n_per_step=n_per_step)
            km = km.reshape(bs, km.shape[1], w_a)
            return _moba_sample(qa, ka, va, ck, cv, page_table, km, slopes, hd=hd, n_per_step=n_per_step)

        hp, kp, vp, sp = layer(hp, bp, tp, moba_prompt, jnp.zeros((bp, n_head_g, dk, dv), state_gla.dtype))
        hs, ksn, vsn, ssn = layer(hs, bs, ts, moba_sample, state_gla[l])
        for lst, val in zip(outs, (kp, vp, sp, ksn, vsn, ssn)):
            lst.append(val)
    return (hp.reshape(bp, tp, d_model), hs.reshape(bs, ts, d_model)) + tuple(jnp.stack(o) for o in outs)
```

```python
import functools

import ml_dtypes
import numpy as np
import jax
import jax.numpy as jnp
from jax import lax
from jax.experimental import pallas as pl
from jax.experimental.pallas import tpu as pltpu

F32 = jnp.float32
BF16 = jnp.bfloat16

LANES = 128
SUBLANES = 8
VMEM_LIMIT_BYTES = 56 * 1024 * 1024

EPS = 1e-6
MOBA_BLOCK = 256
MOBA_TOPK = 3
GLA_TAU = 16.0
NEG_BIG = -1e30
PENALTY = -30000.0
LOG2E = float(np.log2(np.e))

_NT = (((1,), (1,)), ((), ()))
_TN = (((0,), (0,)), ((), ()))


def _params(*sem):
    return pltpu.CompilerParams(dimension_semantics=sem, vmem_limit_bytes=VMEM_LIMIT_BYTES)


def _sigmoid(x):
    return 1.0 / (1.0 + jnp.exp(-x))


def _rms(x, g):
    return x * lax.rsqrt(jnp.mean(x * x, axis=-1, keepdims=True) + EPS) * g


def _bdot(a, b):
    return jnp.dot(a.astype(BF16), b.astype(BF16), preferred_element_type=F32)


def _bdot_nt(a, b):
    return lax.dot_general(a.astype(BF16), b.astype(BF16), _NT, preferred_element_type=F32)


def _proj_kernel(x_ref, g_ref, wm_ref, wlr_ref, wgt_ref, wgg_ref, bgg_ref,
                 qa_ref, ka_ref, va_ref, qg_ref, kg_ref, vg_ref, rg_ref, la_ref, sga_ref, sgb_ref, kmean_ref,
                 *, w_a, qk_g, v_g, d_model):
    hb = _rms(x_ref[...], g_ref[...]).astype(BF16)

    def proj(lo, n):
        return jnp.dot(hb, wm_ref[:, lo:lo + n], preferred_element_type=F32)

    qa_ref[...] = proj(0, w_a)
    ka = proj(w_a, w_a)
    ka_ref[...] = ka
    kmean_ref[0] = jnp.sum(ka, axis=0, keepdims=True) * (1.0 / ka.shape[0])
    va_ref[...] = proj(2 * w_a, w_a)
    lo = 3 * w_a
    qg_ref[...] = proj(lo, qk_g)
    kg_ref[...] = proj(lo + qk_g, qk_g)
    vg_ref[...] = proj(lo + 2 * qk_g, v_g)
    rg_ref[...] = proj(lo + 2 * qk_g + v_g, v_g)
    lr = jnp.dot(hb, wlr_ref[...], preferred_element_type=F32)
    xg = jnp.dot(lr.astype(BF16), wgg_ref[...], preferred_element_type=F32) + bgg_ref[...]
    log_sig = jnp.minimum(xg, 0.0) - jnp.log1p(jnp.exp(-jnp.abs(xg)))
    la_ref[...] = log_sig * (1.0 / GLA_TAU)
    ga = jnp.dot(hb, wgt_ref[:, :d_model], preferred_element_type=F32)
    sga_ref[...] = _sigmoid(ga).astype(BF16)
    gb = jnp.dot(hb, wgt_ref[:, d_model:], preferred_element_type=F32)
    sgb_ref[...] = _sigmoid(gb).astype(BF16)


def _proj(x, g, wm, wlr, wgt, wgg, bgg, *, w_a, qk_g, v_g):
    n, d_model = x.shape
    tm = MOBA_BLOCK
    assert n % tm == 0
    nt = n // tm
    row = lambda w: pl.BlockSpec((tm, w), lambda i: (i, 0))
    full = lambda a: pl.BlockSpec(a.shape, lambda i: (0,) * a.ndim)
    widths = (w_a, w_a, w_a, qk_g, qk_g, v_g, v_g, qk_g)
    out_shape = [jax.ShapeDtypeStruct((n, w), F32) for w in widths]
    out_shape += [jax.ShapeDtypeStruct((n, d_model), BF16)] * 2
    out_shape += [jax.ShapeDtypeStruct((nt, 1, w_a), F32)]
    out_specs = [row(w) for w in widths] + [row(d_model)] * 2
    out_specs += [pl.BlockSpec((1, 1, w_a), lambda i: (i, 0, 0))]
    return pl.pallas_call(
        functools.partial(_proj_kernel, w_a=w_a, qk_g=qk_g, v_g=v_g, d_model=d_model),
        grid=(nt,),
        in_specs=[row(d_model), full(g), full(wm), full(wlr), full(wgt), full(wgg), full(bgg)],
        out_specs=out_specs,
        out_shape=out_shape,
        compiler_params=_params("parallel"),
        name="proj",
    )(x, g, wm, wlr, wgt, wgg, bgg)


def _top_blocks(gate, candidate, lane):
    lane_f = lane.astype(F32)
    no_lane = float(gate.shape[-1])
    avail = jnp.where(candidate, 1.0, 0.0)
    sel = jnp.zeros(gate.shape, F32)
    for _ in range(MOBA_TOPK):
        cur = jnp.where(avail > 0.0, gate, -jnp.inf)
        mx = jnp.max(cur, axis=-1, keepdims=True)
        cand = jnp.where(avail > 0.0, jnp.where(cur == mx, lane_f, no_lane), no_lane)
        first = jnp.min(cand, axis=-1, keepdims=True)
        pick = lane_f == first
        sel = jnp.where(pick, 1.0, sel)
        avail = jnp.where(pick, 0.0, avail)
    return sel


def _alibi_slopes(n_head):
    return 2.0 ** (-8.0 * (np.arange(n_head) + 1) / n_head)


_SLOPE_PARTS = 3
_AUG_EXTRA = 2 * _SLOPE_PARTS


def _moba_prompt_kernel(sl_ref, q_ref, k_ref, v_ref, kc_ref, km_ref, o_ref, kaug_ref, vb_ref, *, hd):
    hp = pl.program_id(1)
    qi = pl.program_id(2)
    blk = MOBA_BLOCK
    t = k_ref.shape[1]
    tq = q_ref.shape[1]
    n_blk = t // blk
    per_pair = LANES // hd

    @pl.when(qi == 0)
    def _():
        def conv(c, carry):
            rows = pl.ds(pl.multiple_of(c * blk, blk), blk)
            kaug_ref[rows, :LANES] = k_ref[0, rows, :].astype(BF16)
            kaug_ref[rows, LANES:] = kc_ref[rows, :]
            vb_ref[rows, :] = v_ref[0, rows, :].astype(BF16)
            return carry
        lax.fori_loop(0, t // blk, conv, 0)

    lane = lax.broadcasted_iota(jnp.int32, (tq, LANES), 1)
    own = qi * (tq // blk) + lax.broadcasted_iota(jnp.int32, (tq, LANES), 0) // blk
    row2 = lax.broadcasted_iota(jnp.int32, (tq, tq), 0)
    col2 = lax.broadcasted_iota(jnp.int32, (tq, tq), 1)
    q = q_ref[0]
    qk_scale = hd ** -0.5 * LOG2E

    q_aug, tile_bias = [], []
    for hh in range(per_pair):
        h = hp * per_pair + hh
        head_lanes = (lane >= hh * hd) & (lane < (hh + 1) * hd)
        qh = jnp.where(head_lanes, q, 0.0)
        gate = lax.dot_general(qh, km_ref[0], _NT, precision=lax.Precision.HIGHEST,
                               preferred_element_type=F32)
        sel = _top_blocks(gate, lane < own, lane)
        parts = [sl_ref[1 + i, h] for i in range(_SLOPE_PARTS)]
        qx = jnp.where(lane < n_blk, jnp.where((sel > 0.0) | (lane == own), 0.0, PENALTY), 0.0)
        for off, val in enumerate(parts + [part * blk for part in parts]):
            qx = jnp.where(lane == n_blk + off, val, qx)
        q_aug.append(jnp.concatenate([(qh * qk_scale).astype(BF16), qx.astype(BF16)], axis=1))
        tile_bias.append(sl_ref[0, h] * tq)

    def logits(j, causal):
        rows = pl.ds(pl.multiple_of(j * tq, tq), tq)
        kt = kaug_ref[rows, :]
        ss = [lax.dot_general(qa, kt, _NT, preferred_element_type=F32) for qa in q_aug]
        if causal:
            ss = [jnp.where(row2 >= col2, s, NEG_BIG) for s in ss]
        return ss, vb_ref[rows, :]

    ss, vt = logits(qi, True)
    carry = []
    for s in ss:
        m = jnp.max(s, axis=-1, keepdims=True)
        p = jnp.exp2(s - m)
        carry += [m, jnp.sum(p, axis=-1, keepdims=True), jnp.dot(p.astype(BF16), vt, preferred_element_type=F32)]

    def body(j, carry):
        ss, vt = logits(j, False)
        out = []
        for hh, s in enumerate(ss):
            m, l, acc = carry[3 * hh:3 * hh + 3]
            c = -tile_bias[hh] * (qi - j).astype(F32)
            m_new = jnp.maximum(m, jnp.max(s, axis=-1, keepdims=True) + c)
            alpha = jnp.exp2(m - m_new)
            p = jnp.exp2(s - (m_new - c))
            l = alpha * l + jnp.sum(p, axis=-1, keepdims=True)
            acc = alpha * acc + jnp.dot(p.astype(BF16), vt, preferred_element_type=F32)
            out += [m_new, l, acc]
        return tuple(out)

    carry = lax.fori_loop(0, qi, body, tuple(carry))
    out = carry[2] / carry[1]
    for hh in range(1, per_pair):
        out = jnp.where(lane >= hh * hd, carry[3 * hh + 2] / carry[3 * hh + 1], out)
    o_ref[0] = out


def _moba_consts(t, tq):
    n_blk = t // MOBA_BLOCK
    assert n_blk + _AUG_EXTRA <= LANES
    pos = np.arange(t)
    c = np.zeros((t, LANES), np.float32)
    c[pos, pos // MOBA_BLOCK] = 1.0
    for i in range(_SLOPE_PARTS):
        c[:, n_blk + i] = pos % MOBA_BLOCK
        c[:, n_blk + _SLOPE_PARTS + i] = (pos % tq) // MOBA_BLOCK
    return jnp.asarray(c, dtype=BF16)


def _moba_prompt(qa, ka, va, kmean, *, hd):
    b, t, w = qa.shape
    blk = MOBA_BLOCK
    n_pair = w // LANES
    n_blk = t // blk
    tq = 2 * blk if t % (2 * blk) == 0 else blk
    sl2 = _alibi_slopes(w // hd) * LOG2E
    parts, rest = [], sl2
    for _ in range(_SLOPE_PARTS):
        parts.append(rest.astype(ml_dtypes.bfloat16).astype(np.float64))
        rest = rest - parts[-1]
    slopes = jnp.asarray(np.stack([sl2] + parts), dtype=F32)
    kc = _moba_consts(t, tq)
    km = jnp.pad(kmean, ((0, 0), (0, LANES - n_blk), (0, 0)))
    return pl.pallas_call(
        functools.partial(_moba_prompt_kernel, hd=hd),
        grid=(b, n_pair, t // tq),
        in_specs=[
            pl.BlockSpec(memory_space=pltpu.SMEM),
            pl.BlockSpec((1, tq, LANES), lambda bi, hp, i: (bi, i, hp)),
            pl.BlockSpec((1, t, LANES), lambda bi, hp, i: (bi, 0, hp)),
            pl.BlockSpec((1, t, LANES), lambda bi, hp, i: (bi, 0, hp)),
            pl.BlockSpec((t, LANES), lambda bi, hp, i: (0, 0)),
            pl.BlockSpec((1, LANES, LANES), lambda bi, hp, i: (bi, 0, hp)),
        ],
        out_specs=pl.BlockSpec((1, tq, LANES), lambda bi, hp, i: (bi, i, hp)),
        out_shape=jax.ShapeDtypeStruct((b, t, w), F32),
        scratch_shapes=[pltpu.VMEM((t, 2 * LANES), BF16), pltpu.VMEM((t, LANES), BF16)],
        compiler_params=_params("parallel", "parallel", "arbitrary"),
        name="moba_prompt",
    )(slopes, qa, ka, va, kc, km)


def _moba_sample_kernel(pt_ref, slopes_ref, q_ref, kn_ref, vn_ref, *refs, n_per_step, pages_per_block, ts):
    k_refs, v_refs = refs[:n_per_step], refs[n_per_step:2 * n_per_step]
    o_ref, s_ref, ksum_ref, sel_ref, m_ref, l_ref, acc_ref = refs[2 * n_per_step:]
    j = pl.program_id(1)
    n_k_steps = pl.num_programs(1) // 2
    _, page, n_head, hd = k_refs[0].shape
    rows = q_ref.shape[1]
    cols = page * n_head
    n_blk = ksum_ref.shape[0] // n_head
    past = n_blk * MOBA_BLOCK
    scale = hd ** -0.5

    row_head = lax.broadcasted_iota(jnp.int32, (rows, 1), 0) // ts
    row_q = lax.broadcasted_iota(jnp.int32, (rows, 1), 0) % ts
    slope = jnp.zeros((rows, 1), F32)
    for h in range(n_head):
        slope = jnp.where(row_head == h, slopes_ref[h], slope)
    qs = (q_ref[0] * scale).astype(BF16)

    @pl.when(j < n_k_steps)
    def _():
        blk_sum = None
        for p_i in range(n_per_step):
            page_idx = j * n_per_step + p_i
            k3 = k_refs[p_i][0]
            page_sum = jnp.sum(k3, axis=0)
            blk_sum = page_sum if p_i % pages_per_block == 0 else blk_sum + page_sum
            if p_i % pages_per_block == pages_per_block - 1:
                dst = pl.multiple_of((page_idx // pages_per_block) * n_head, n_head)
                ksum_ref[pl.ds(dst, n_head), :] = blk_sum
            s = lax.dot_general(qs, k3.reshape(cols, hd).astype(BF16), _NT, preferred_element_type=F32)
            s_ref[:, pl.ds(pl.multiple_of(page_idx * cols, cols), cols)] = s

    @pl.when(j == n_k_steps)
    def _():
        gate = lax.dot_general(q_ref[0], ksum_ref[...], _NT, precision=lax.Precision.HIGHEST,
                               preferred_element_type=F32)
        lane = lax.broadcasted_iota(jnp.int32, gate.shape, 1)
        sel_ref[...] = _top_blocks(gate, (lane % n_head) == row_head, lane)
        s = lax.dot_general(qs, kn_ref[0].astype(BF16), _NT, preferred_element_type=F32)
        col = lax.broadcasted_iota(jnp.int32, s.shape, 1)
        ok = ((col % n_head) == row_head) & ((col // n_head) <= row_q)
        s = jnp.where(ok, s + slope * (col // n_head).astype(F32), NEG_BIG)
        m = jnp.max(s, axis=-1, keepdims=True)
        p = jnp.exp(s - m)
        m_ref[...] = m
        l_ref[...] = jnp.sum(p, axis=-1, keepdims=True)
        acc_ref[...] = _bdot(p, vn_ref[0])

    @pl.when(j >= n_k_steps)
    def _():
        col = lax.broadcasted_iota(jnp.int32, (rows, cols), 1)
        head_match = (col % n_head) == row_head
        in_page_bias = slope * (col // n_head).astype(F32)
        sel = sel_ref[...]
        sel_blk = lax.broadcasted_iota(jnp.int32, sel.shape, 1) // n_head
        m = m_ref[...]
        l = l_ref[...]
        acc = acc_ref[...]
        for p_i in range(n_per_step):
            page_idx = (j - n_k_steps) * n_per_step + p_i
            blk_idx = page_idx // pages_per_block
            picked = jnp.max(jnp.where(sel_blk == blk_idx, sel, 0.0), axis=-1, keepdims=True)
            col_bias = slope * (page_idx * page - past).astype(F32) + jnp.where(picked > 0.0, 0.0, NEG_BIG)
            s = s_ref[:, pl.ds(pl.multiple_of(page_idx * cols, cols), cols)]
            s = jnp.where(head_match, s + in_page_bias + col_bias, NEG_BIG)
            m_new = jnp.maximum(m, jnp.max(s, axis=-1, keepdims=True))
            alpha = jnp.exp(m - m_new)
            p = jnp.exp(s - m_new)
            l = alpha * l + jnp.sum(p, axis=-1, keepdims=True)
            v2 = v_refs[p_i][0].reshape(cols, hd).astype(BF16)
            acc = alpha * acc + jnp.dot(p.astype(BF16), v2, preferred_element_type=F32)
            m = m_new
        m_ref[...] = m
        l_ref[...] = l
        acc_ref[...] = acc

    @pl.when(j == pl.num_programs(1) - 1)
    def _():
        o_ref[0] = acc_ref[...] / l_ref[...]


def _moba_sample(qa, ka, va, cache_k, cache_v, page_table, *, n_per_step):
    b, ts, n_head, hd = qa.shape
    _, page, _, _ = cache_k.shape
    n_pages = page_table.shape[1]
    ppb = MOBA_BLOCK // page
    assert n_pages % n_per_step == 0 and n_per_step % ppb == 0 and n_head == SUBLANES
    n_blk = n_pages // ppb
    n_k_steps = n_pages // n_per_step
    rows = n_head * ts
    q_hq = jnp.swapaxes(qa, 1, 2).reshape(b, rows, hd)
    per_b = lambda a: pl.BlockSpec((1,) + a.shape[1:], lambda bi, j, pt: (bi, 0, 0))
    kn, vn = ka.reshape(b, rows, hd), va.reshape(b, rows, hd)

    def page_specs(step_of):
        def spec(p):
            return pl.BlockSpec((1, page, n_head, hd),
                                lambda bi, j, pt: (pt[bi, step_of(j) * n_per_step + p], 0, 0, 0))
        return [spec(p) for p in range(n_per_step)]

    out = pl.pallas_call(
        functools.partial(_moba_sample_kernel, n_per_step=n_per_step, pages_per_block=ppb, ts=ts),
        grid_spec=pltpu.PrefetchScalarGridSpec(
            num_scalar_prefetch=1,
            grid=(b, 2 * n_k_steps),
            in_specs=[pl.BlockSpec(memory_space=pltpu.SMEM), per_b(q_hq), per_b(kn), per_b(vn)]
                     + page_specs(lambda j: jnp.minimum(j, n_k_steps - 1))
                     + page_specs(lambda j: jnp.maximum(j - n_k_steps, 0)),
            out_specs=per_b(q_hq),
            scratch_shapes=[pltpu.VMEM((rows, n_pages * page * n_head), F32),
                            pltpu.VMEM((n_blk * n_head, hd), F32),
                            pltpu.VMEM((rows, n_blk * n_head), F32),
                            pltpu.VMEM((rows, 1), F32), pltpu.VMEM((rows, 1), F32), pltpu.VMEM((rows, hd), F32)],
        ),
        out_shape=jax.ShapeDtypeStruct((b, rows, hd), F32),
        compiler_params=_params("parallel", "arbitrary"),
        name="moba_sample",
    )(page_table, jnp.asarray(_alibi_slopes(n_head), dtype=F32), q_hq, kn, vn,
      *([cache_k] * n_per_step), *([cache_v] * n_per_step))
    return jnp.swapaxes(out.reshape(b, n_head, ts, hd), 1, 2)


def _gla_kernel(q_ref, k_ref, v_ref, la_ref, rg_ref, gn_ref, s0_ref, o_ref, sout_ref, st_ref, *, dk, dv):
    c_idx = pl.program_id(1)
    n_chunks = pl.num_programs(1)
    c = q_ref.shape[1]
    n_pair = q_ref.shape[2] // LANES
    per_pair = LANES // dk
    assert dv == LANES and c % SUBLANES == 0

    @pl.when(c_idx == 0)
    def _():
        for p in range(n_pair):
            st_ref[p] = s0_ref[0, p].T

    row = lax.broadcasted_iota(jnp.int32, (c, LANES), 0)
    lane = lax.broadcasted_iota(jnp.int32, (c, LANES), 1)
    row2 = lax.broadcasted_iota(jnp.int32, (c, c), 0)
    col2 = lax.broadcasted_iota(jnp.int32, (c, c), 1)
    tri = jnp.where(row2 >= col2, 1.0, 0.0)
    b_all = jnp.dot(tri, la_ref[0], precision=lax.Precision.HIGHEST, preferred_element_type=F32)
    nb8 = c // SUBLANES
    sub = lax.broadcasted_iota(jnp.int32, (nb8, SUBLANES, LANES), 1)
    lane3 = lax.broadcasted_iota(jnp.int32, (nb8, SUBLANES, LANES), 2)
    lane_st = lax.broadcasted_iota(jnp.int32, (dv, LANES), 1)

    def bcast_row(x3, jj):
        return jnp.broadcast_to(x3[:, jj:jj + 1, :], x3.shape)

    for p in range(n_pair):
        cols = slice(p * LANES, (p + 1) * LANES)
        q = q_ref[0, :, cols] * (dk ** -0.5)
        k = k_ref[0, :, cols]
        b = b_all[:, cols]
        vs = [v_ref[0, :, (p * per_pair + hh) * dv:(p * per_pair + hh + 1) * dv] for hh in range(per_pair)]
        head_lanes = [(lane >= hh * dk) & (lane < (hh + 1) * dk) for hh in range(per_pair)]
        head_lanes3 = [(lane3 >= hh * dk) & (lane3 < (hh + 1) * dk) for hh in range(per_pair)]
        st = st_ref[p]

        qe = q * jnp.exp(b)
        o = [_bdot_nt(jnp.where(head_lanes[hh], qe, 0.0), st) for hh in range(per_pair)]

        q3 = q.reshape(nb8, SUBLANES, LANES)
        k3 = k.reshape(nb8, SUBLANES, LANES)
        b3 = b.reshape(nb8, SUBLANES, LANES)
        v3 = [v.reshape(nb8, SUBLANES, dv) for v in vs]
        o3 = [jnp.zeros((nb8, SUBLANES, dv), F32) for _ in range(per_pair)]
        for jj in range(SUBLANES):
            e = jnp.exp(jnp.minimum(b3 - bcast_row(b3, jj), 0.0))
            term = jnp.where(sub >= jj, q3 * bcast_row(k3, jj) * e, 0.0)
            for hh in range(per_pair):
                a = jnp.sum(jnp.where(head_lanes3[hh], term, 0.0), axis=-1, keepdims=True)
                o3[hh] = o3[hh] + a * bcast_row(v3[hh], jj)
        o = [o[hh] + o3[hh].reshape(c, dv) for hh in range(per_pair)]

        attn = [jnp.zeros((c, c), F32) for _ in range(per_pair)]
        m_half = SUBLANES
        while 2 * m_half <= c:
            span = 2 * m_half
            bnd = jnp.broadcast_to(b.reshape(c // span, span, LANES)[:, m_half - 1:m_half, :],
                                   (c // span, span, LANES)).reshape(c, LANES)
            upper = (row % span) >= m_half
            qm = jnp.where(upper, q * jnp.exp(jnp.minimum(b - bnd, 0.0)), 0.0)
            km = jnp.where(upper, 0.0, k * jnp.exp(jnp.minimum(bnd - b, 0.0)))
            same = (row2 // span) == (col2 // span)
            for hh in range(per_pair):
                a = _bdot_nt(jnp.where(head_lanes[hh], qm, 0.0), km)
                attn[hh] = attn[hh] + jnp.where(same, a, 0.0)
            m_half = span
        if c > SUBLANES:
            o = [o[hh] + _bdot(attn[hh], vs[hh]) for hh in range(per_pair)]

        b_last = b[c - 1:c, :]
        kk = (k * jnp.exp(b_last - b)).astype(BF16)
        upd = lax.dot_general(vs[0].astype(BF16), kk, _TN, preferred_element_type=F32)
        for hh in range(1, per_pair):
            u = lax.dot_general(vs[hh].astype(BF16), kk, _TN, preferred_element_type=F32)
            upd = jnp.where(lane_st >= hh * dk, u, upd)
        st_new = st * jnp.exp(b_last) + upd
        st_ref[p] = st_new

        for hh in range(per_pair):
            h = p * per_pair + hh
            rg = rg_ref[0, :, h * dv:(h + 1) * dv]
            o_ref[0, :, h * dv:(h + 1) * dv] = _rms(o[hh], gn_ref[...]) * (rg * _sigmoid(rg))

        @pl.when(c_idx == n_chunks - 1)
        def _(p=p, st_new=st_new):
            sout_ref[0, p] = st_new.T


def _gla(qg, kg, vg, la, rg, g_norm, s0, *, chunk, dk, dv):
    b, t, wk = qg.shape
    wv = vg.shape[2]
    n_head = wk // dk
    n_pair = wk // LANES
    s0p = s0.reshape(b, n_pair, LANES, dv)
    tok = lambda w: pl.BlockSpec((1, chunk, w), lambda bi, ci: (bi, ci, 0))
    st_spec = pl.BlockSpec((1, n_pair, LANES, dv), lambda bi, ci: (bi, 0, 0, 0))
    og, s_new = pl.pallas_call(
        functools.partial(_gla_kernel, dk=dk, dv=dv),
        grid=(b, t // chunk),
        in_specs=[tok(wk), tok(wk), tok(wv), tok(wk), tok(wv),
                  pl.BlockSpec(g_norm.shape, lambda bi, ci: (0, 0)), st_spec],
        out_specs=[tok(wv), st_spec],
        out_shape=[jax.ShapeDtypeStruct((b, t, wv), F32), jax.ShapeDtypeStruct((b, n_pair, LANES, dv), F32)],
        scratch_shapes=[pltpu.VMEM((n_pair, dv, LANES), F32)],
        compiler_params=_params("parallel", "arbitrary"),
        name="gla",
    )(qg, kg, vg, la, rg, g_norm, s0p)
    return og, s_new.reshape(b, n_head, dk, dv)


def _merge_kernel(x_ref, oa_ref, og_ref, sga_ref, sgb_ref, wa_ref, wb_ref, wo_ref, g_ref, y_ref):
    merged = (sga_ref[...].astype(F32) * _bdot(oa_ref[...], wa_ref[...])
              + sgb_ref[...].astype(F32) * _bdot(og_ref[...], wb_ref[...]))
    y_ref[...] = x_ref[...] + _rms(_bdot(merged, wo_ref[...]), g_ref[...])


def _mlp_kernel(x_ref, wu_ref, wd_ref, g1_ref, g2_ref, y_ref):
    x = x_ref[...]
    u = _bdot(_rms(x, g1_ref[...]), wu_ref[...])
    u = jnp.square(jnp.maximum(u, 0.0))
    y_ref[...] = x + _rms(_bdot(u, wd_ref[...]), g2_ref[...])


def _rowwise_call(kernel, name, row_inputs, const_inputs, out_width, tm):
    n = row_inputs[0].shape[0]
    assert n % tm == 0
    row = lambda a: pl.BlockSpec((tm, a.shape[1]), lambda i: (i, 0))
    full = lambda a: pl.BlockSpec(a.shape, lambda i: (0,) * a.ndim)
    return pl.pallas_call(
        kernel,
        grid=(n // tm,),
        in_specs=[row(a) for a in row_inputs] + [full(a) for a in const_inputs],
        out_specs=pl.BlockSpec((tm, out_width), lambda i: (i, 0)),
        out_shape=jax.ShapeDtypeStruct((n, out_width), F32),
        compiler_params=_params("parallel"),
        name=name,
    )(*row_inputs, *const_inputs)


def _gla_chunk(t):
    c = SUBLANES
    while c * 2 <= min(t, LANES) and t % (c * 2) == 0:
        c *= 2
    return c


def kernel(x_prompt, x_sample, cache_k, cache_v, page_table, state_gla, w_in, w_gla_gate, b_gla_gate, g_gla_norm,
           w_branch_a, w_branch_b, w_out, w_up, w_down, g_pre_mix, g_post_mix, g_pre_mlp, g_post_mlp):
    bp, tp, d_model = x_prompt.shape
    bs, ts, _ = x_sample.shape
    depth, n_phys, page, n_head, hd = cache_k.shape
    _, _, n_head_g, dk, dv = state_gla.shape
    w_a = n_head * hd
    qk_g = n_head_g * dk
    v_g = n_head_g * dv
    rank = w_gla_gate.shape[1]
    n_main = 3 * w_a + 2 * qk_g + 2 * v_g

    hp = x_prompt.reshape(bp * tp, d_model)
    hs = x_sample.reshape(bs * ts, d_model)
    outs = [[] for _ in range(6)]
    for l in range(depth):
        wm = w_in[l, :, :n_main].astype(BF16)
        wlr = jnp.pad(w_in[l, :, n_main:n_main + rank], ((0, 0), (0, LANES - rank))).astype(BF16)
        wgt = w_in[l, :, n_main + rank:].astype(BF16)
        wgg = jnp.pad(w_gla_gate[l], ((0, LANES - rank), (0, 0))).astype(BF16)
        bgg = b_gla_gate[l][None, :]
        wa, wb, wo = (w[l].astype(BF16) for w in (w_branch_a, w_branch_b, w_out))
        wu, wd = w_up[l].astype(BF16), w_down[l].astype(BF16)
        g_mix, g_pm, g_mlp, g_pl, g_gn = (g[l][None, :] for g in
                                          (g_pre_mix, g_post_mix, g_pre_mlp, g_post_mlp, g_gla_norm))

        def layer(x, b, t, moba, s0):
            qa, ka, va, qg, kg, vg, rg, la, sga, sgb, kmean = _proj(
                x, g_mix, wm, wlr, wgt, wgg, bgg, w_a=w_a, qk_g=qk_g, v_g=v_g)
            r3 = lambda a: a.reshape(b, t, a.shape[-1])
            oa = moba(r3(qa), r3(ka), r3(va), kmean)
            og, s_new = _gla(r3(qg), r3(kg), r3(vg), r3(la), r3(rg), g_gn, s0, chunk=_gla_chunk(t), dk=dk, dv=dv)
            n = b * t
            x = _rowwise_call(_merge_kernel, "merge", [x, oa.reshape(n, w_a), og.reshape(n, v_g), sga, sgb],
                              [wa, wb, wo, g_pm], d_model, MOBA_BLOCK)
            x = _rowwise_call(_mlp_kernel, "mlp", [x], [wu, wd, g_mlp, g_pl], d_model, MOBA_BLOCK)
            return x, ka.reshape(b, t, n_head, hd), va.reshape(b, t, n_head, hd), s_new

        def moba_prompt(qa, ka, va, kmean):
            return _moba_prompt(qa, ka, va, kmean.reshape(bp, tp // MOBA_BLOCK, w_a), hd=hd)

        def moba_sample(qa, ka, va, kmean_unused):
            n_pages = page_table.shape[1]
            n_per_step = 8 if n_pages % 8 == 0 else MOBA_BLOCK // page
            r4 = lambda a: a.reshape(bs, ts, n_head, hd)
            return _moba_sample(r4(qa), r4(ka), r4(va), cache_k[l], cache_v[l], page_table, n_per_step=n_per_step)

        hp, kp, vp, sp = layer(hp, bp, tp, moba_prompt, jnp.zeros((bp, n_head_g, dk, dv), state_gla.dtype))
        hs, ksn, vsn, ssn = layer(hs, bs, ts, moba_sample, state_gla[l])
        for lst, val in zip(outs, (kp, vp, sp, ksn, vsn, ssn)):
            lst.append(val)
    return (hp.reshape(bp, tp, d_model), hs.reshape(bs, ts, d_model)) + tuple(jnp.stack(o) for o in outs)
```

```python
import functools

import ml_dtypes
import numpy as np
import jax
import jax.numpy as jnp
from jax import lax
from jax.experimental import pallas as pl
from jax.experimental.pallas import tpu as pltpu

F32 = jnp.float32
BF16 = jnp.bfloat16

LANES = 128
SUBLANES = 8
VMEM_LIMIT_BYTES = 56 * 1024 * 1024

EPS = 1e-6
MOBA_BLOCK = 256
MOBA_TOPK = 3
GLA_TAU = 16.0
NEG_BIG = -1e30
PENALTY = -30000.0
LOG2E = float(np.log2(np.e))

_NT = (((1,), (1,)), ((), ()))
_TN = (((0,), (0,)), ((), ()))


def _params(*sem):
    return pltpu.CompilerParams(dimension_semantics=sem, vmem_limit_bytes=VMEM_LIMIT_BYTES)


def _sigmoid(x):
    return 1.0 / (1.0 + jnp.exp(-x))


def _rms(x, g):
    return x * lax.rsqrt(jnp.mean(x * x, axis=-1, keepdims=True) + EPS) * g


def _bdot(a, b):
    return jnp.dot(a.astype(BF16), b.astype(BF16), preferred_element_type=F32)


def _bdot_nt(a, b):
    return lax.dot_general(a.astype(BF16), b.astype(BF16), _NT, preferred_element_type=F32)


def _proj_kernel(x_ref, g_ref, wm_ref, wlr_ref, wgt_ref, wgg_ref, bgg_ref,
                 qa_ref, ka_ref, va_ref, qg_ref, kg_ref, vg_ref, rg_ref, la_ref, sga_ref, sgb_ref, kmean_ref,
                 *, w_a, qk_g, v_g, d_model):
    hb = _rms(x_ref[...], g_ref[...]).astype(BF16)

    def proj(lo, n):
        return jnp.dot(hb, wm_ref[:, lo:lo + n], preferred_element_type=F32)

    qa_ref[...] = proj(0, w_a)
    ka = proj(w_a, w_a)
    ka_ref[...] = ka
    kmean_ref[0] = jnp.sum(ka, axis=0, keepdims=True) * (1.0 / ka.shape[0])
    va_ref[...] = proj(2 * w_a, w_a)
    lo = 3 * w_a
    qg_ref[...] = proj(lo, qk_g)
    kg_ref[...] = proj(lo + qk_g, qk_g)
    vg_ref[...] = proj(lo + 2 * qk_g, v_g)
    rg_ref[...] = proj(lo + 2 * qk_g + v_g, v_g)
    lr = jnp.dot(hb, wlr_ref[...], preferred_element_type=F32)
    xg = jnp.dot(lr.astype(BF16), wgg_ref[...], preferred_element_type=F32) + bgg_ref[...]
    log_sig = jnp.minimum(xg, 0.0) - jnp.log1p(jnp.exp(-jnp.abs(xg)))
    la_ref[...] = log_sig * (1.0 / GLA_TAU)
    ga = jnp.dot(hb, wgt_ref[:, :d_model], preferred_element_type=F32)
    sga_ref[...] = _sigmoid(ga).astype(BF16)
    gb = jnp.dot(hb, wgt_ref[:, d_model:], preferred_element_type=F32)
    sgb_ref[...] = _sigmoid(gb).astype(BF16)


def _proj(x, g, wm, wlr, wgt, wgg, bgg, *, w_a, qk_g, v_g):
    n, d_model = x.shape
    tm = MOBA_BLOCK
    assert n % tm == 0
    nt = n // tm
    row = lambda w: pl.BlockSpec((tm, w), lambda i: (i, 0))
    full = lambda a: pl.BlockSpec(a.shape, lambda i: (0,) * a.ndim)
    widths = (w_a, w_a, w_a, qk_g, qk_g, v_g, v_g, qk_g)
    out_shape = [jax.ShapeDtypeStruct((n, w), F32) for w in widths]
    out_shape += [jax.ShapeDtypeStruct((n, d_model), BF16)] * 2
    out_shape += [jax.ShapeDtypeStruct((nt, 1, w_a), F32)]
    out_specs = [row(w) for w in widths] + [row(d_model)] * 2
    out_specs += [pl.BlockSpec((1, 1, w_a), lambda i: (i, 0, 0))]
    return pl.pallas_call(
        functools.partial(_proj_kernel, w_a=w_a, qk_g=qk_g, v_g=v_g, d_model=d_model),
        grid=(nt,),
        in_specs=[row(d_model), full(g), full(wm), full(wlr), full(wgt), full(wgg), full(bgg)],
        out_specs=out_specs,
        out_shape=out_shape,
        compiler_params=_params("parallel"),
        name="proj",
    )(x, g, wm, wlr, wgt, wgg, bgg)


def _top_blocks(gate, candidate, lane):
    lane_f = lane.astype(F32)
    no_lane = float(gate.shape[-1])
    avail = jnp.where(candidate, 1.0, 0.0)
    sel = jnp.zeros(gate.shape, F32)
    for _ in range(MOBA_TOPK):
        cur = jnp.where(avail > 0.0, gate, -jnp.inf)
        mx = jnp.max(cur, axis=-1, keepdims=True)
        cand = jnp.where(avail > 0.0, jnp.where(cur == mx, lane_f, no_lane), no_lane)
        first = jnp.min(cand, axis=-1, keepdims=True)
        pick = lane_f == first
        sel = jnp.where(pick, 1.0, sel)
        avail = jnp.where(pick, 0.0, avail)
    return sel


def _alibi_slopes(n_head):
    return 2.0 ** (-8.0 * (np.arange(n_head) + 1) / n_head)


_SLOPE_PARTS = 3
_AUG_EXTRA = 2 * _SLOPE_PARTS


def _moba_prompt_kernel(sl_ref, q_ref, k_ref, v_ref, kc_ref, km_ref, o_ref, kaug_ref, vb_ref, *, hd):
    hp = pl.program_id(1)
    qi = pl.program_id(2)
    blk = MOBA_BLOCK
    t = k_ref.shape[1]
    tq = q_ref.shape[1]
    n_blk = t // blk
    per_pair = LANES // hd

    @pl.when(qi == 0)
    def _():
        def conv(c, carry):
            rows = pl.ds(pl.multiple_of(c * blk, blk), blk)
            kaug_ref[rows, :LANES] = k_ref[0, rows, :].astype(BF16)
            kaug_ref[rows, LANES:] = kc_ref[rows, :]
            vb_ref[rows, :] = v_ref[0, rows, :].astype(BF16)
            return carry
        lax.fori_loop(0, t // blk, conv, 0)

    lane = lax.broadcasted_iota(jnp.int32, (tq, LANES), 1)
    own = qi * (tq // blk) + lax.broadcasted_iota(jnp.int32, (tq, LANES), 0) // blk
    row2 = lax.broadcasted_iota(jnp.int32, (tq, tq), 0)
    col2 = lax.broadcasted_iota(jnp.int32, (tq, tq), 1)
    q = q_ref[0]
    qk_scale = hd ** -0.5 * LOG2E

    q_aug, tile_bias = [], []
    for hh in range(per_pair):
        h = hp * per_pair + hh
        head_lanes = (lane >= hh * hd) & (lane < (hh + 1) * hd)
        qh = jnp.where(head_lanes, q, 0.0)
        gate = _bdot_nt(qh, km_ref[0])
        sel = _top_blocks(gate, lane < own, lane)
        parts = [sl_ref[1 + i, h] for i in range(_SLOPE_PARTS)]
        qx = jnp.where(lane < n_blk, jnp.where((sel > 0.0) | (lane == own), 0.0, PENALTY), 0.0)
        for off, val in enumerate(parts + [part * blk for part in parts]):
            qx = jnp.where(lane == n_blk + off, val, qx)
        q_aug.append(jnp.concatenate([(qh * qk_scale).astype(BF16), qx.astype(BF16)], axis=1))
        tile_bias.append(sl_ref[0, h] * tq)

    def logits(j, causal):
        rows = pl.ds(pl.multiple_of(j * tq, tq), tq)
        kt = kaug_ref[rows, :]
        ss = [lax.dot_general(qa, kt, _NT, preferred_element_type=F32) for qa in q_aug]
        if causal:
            ss = [jnp.where(row2 >= col2, s, NEG_BIG) for s in ss]
        return ss, vb_ref[rows, :]

    ss, vt = logits(qi, True)
    carry = []
    for s in ss:
        m = jnp.max(s, axis=-1, keepdims=True)
        p = jnp.exp2(s - m)
        carry += [m, jnp.sum(p, axis=-1, keepdims=True), jnp.dot(p.astype(BF16), vt, preferred_element_type=F32)]

    def body(j, carry):
        ss, vt = logits(j, False)
        out = []
        for hh, s in enumerate(ss):
            m, l, acc = carry[3 * hh:3 * hh + 3]
            c = -tile_bias[hh] * (qi - j).astype(F32)
            m_new = jnp.maximum(m, jnp.max(s, axis=-1, keepdims=True) + c)
            alpha = jnp.exp2(m - m_new)
            p = jnp.exp2(s - (m_new - c))
            l = alpha * l + jnp.sum(p, axis=-1, keepdims=True)
            acc = alpha * acc + jnp.dot(p.astype(BF16), vt, preferred_element_type=F32)
            out += [m_new, l, acc]
        return tuple(out)

    carry = lax.fori_loop(0, qi, body, tuple(carry))
    out = carry[2] / carry[1]
    for hh in range(1, per_pair):
        out = jnp.where(lane >= hh * hd, carry[3 * hh + 2] / carry[3 * hh + 1], out)
    o_ref[0] = out


def _moba_consts(t, tq):
    n_blk = t // MOBA_BLOCK
    assert n_blk + _AUG_EXTRA <= LANES
    pos = np.arange(t)
    c = np.zeros((t, LANES), np.float32)
    c[pos, pos // MOBA_BLOCK] = 1.0
    for i in range(_SLOPE_PARTS):
        c[:, n_blk + i] = pos % MOBA_BLOCK
        c[:, n_blk + _SLOPE_PARTS + i] = (pos % tq) // MOBA_BLOCK
    return jnp.asarray(c, dtype=BF16)


def _moba_prompt(qa, ka, va, kmean, *, hd):
    b, t, w = qa.shape
    blk = MOBA_BLOCK
    n_pair = w // LANES
    n_blk = t // blk
    tq = 2 * blk if t % (2 * blk) == 0 else blk
    sl2 = _alibi_slopes(w // hd) * LOG2E
    parts, rest = [], sl2
    for _ in range(_SLOPE_PARTS):
        parts.append(rest.astype(ml_dtypes.bfloat16).astype(np.float64))
        rest = rest - parts[-1]
    slopes = jnp.asarray(np.stack([sl2] + parts), dtype=F32)
    kc = _moba_consts(t, tq)
    km = jnp.pad(kmean, ((0, 0), (0, LANES - n_blk), (0, 0)))
    return pl.pallas_call(
        functools.partial(_moba_prompt_kernel, hd=hd),
        grid=(b, n_pair, t // tq),
        in_specs=[
            pl.BlockSpec(memory_space=pltpu.SMEM),
            pl.BlockSpec((1, tq, LANES), lambda bi, hp, i: (bi, i, hp)),
            pl.BlockSpec((1, t, LANES), lambda bi, hp, i: (bi, 0, hp)),
            pl.BlockSpec((1, t, LANES), lambda bi, hp, i: (bi, 0, hp)),
            pl.BlockSpec((t, LANES), lambda bi, hp, i: (0, 0)),
            pl.BlockSpec((1, LANES, LANES), lambda bi, hp, i: (bi, 0, hp)),
        ],
        out_specs=pl.BlockSpec((1, tq, LANES), lambda bi, hp, i: (bi, i, hp)),
        out_shape=jax.ShapeDtypeStruct((b, t, w), F32),
        scratch_shapes=[pltpu.VMEM((t, 2 * LANES), BF16), pltpu.VMEM((t, LANES), BF16)],
        compiler_params=_params("parallel", "parallel", "arbitrary"),
        name="moba_prompt",
    )(slopes, qa, ka, va, kc, km)


def _moba_sample_kernel(pt_ref, slopes_ref, q_ref, kn_ref, vn_ref, *refs, n_per_step, pages_per_block, hd):
    k_refs, v_refs = refs[:n_per_step], refs[n_per_step:2 * n_per_step]
    o_ref, qbd_ref, s_ref, p_ref, ksum_ref, l_ref, acc_ref = refs[2 * n_per_step:]
    j = pl.program_id(1)
    n_k_steps = pl.num_programs(1) // 2
    ts, w = q_ref.shape[1:]
    n_head = w // hd
    rows = n_head * ts
    page = k_refs[0].shape[3]
    blk = page * pages_per_block
    past = s_ref.shape[1]
    n_blk = past // blk
    scale = hd ** -0.5

    row_head = lax.broadcasted_iota(jnp.int32, (rows, 1), 0) // ts
    row_q = lax.broadcasted_iota(jnp.int32, (rows, 1), 0) % ts
    col_head = lax.broadcasted_iota(jnp.int32, (rows, w), 1) // hd
    slope = jnp.zeros((rows, 1), F32)
    for h in range(n_head):
        slope = jnp.where(row_head == h, slopes_ref[h], slope)

    @pl.when(j == 0)
    def _():
        qt = jnp.concatenate([q_ref[0]] * n_head, axis=0)
        qbd_ref[...] = jnp.where(col_head == row_head, qt, 0.0)
        ksum_ref[...] = jnp.zeros(ksum_ref.shape, F32)

    @pl.when(j < n_k_steps)
    def _():
        qs = (qbd_ref[...] * scale).astype(BF16)
        lane = lax.broadcasted_iota(jnp.int32, ksum_ref.shape, 1)
        ksum = ksum_ref[...]
        for g in range(n_per_step // pages_per_block):
            total = None
            for p_i in range(g * pages_per_block, (g + 1) * pages_per_block):
                page_idx = j * n_per_step + p_i
                kt = k_refs[p_i][0].reshape(w, page)
                total = kt if total is None else total + kt
                s = jnp.dot(qs, kt.astype(BF16), preferred_element_type=F32)
                s_ref[:, pl.ds(pl.multiple_of(page_idx * page, page), page)] = s
            blk_idx = (j * n_per_step) // pages_per_block + g
            ksum = jnp.where(lane == blk_idx, jnp.sum(total, axis=1, keepdims=True), ksum)
        ksum_ref[...] = ksum

    @pl.when(j == n_k_steps)
    def _():
        qbd = qbd_ref[...]
        gate = _bdot(qbd, ksum_ref[...])
        lane = lax.broadcasted_iota(jnp.int32, gate.shape, 1)
        sel = _top_blocks(gate, lane < n_blk, lane)
        s_own = _bdot_nt(qbd * scale, kn_ref[0])
        kq = lax.broadcasted_iota(jnp.int32, s_own.shape, 1)
        s_own = jnp.where(kq <= row_q, s_own + slope * kq.astype(F32), NEG_BIG)
        key_in_blk = lax.broadcasted_iota(jnp.int32, (1, blk), 1)

        def add_bias(b_i, m_wide):
            cols = pl.ds(pl.multiple_of(b_i * blk, blk), blk)
            picked = jnp.max(jnp.where(lane == b_i, sel, 0.0), axis=-1, keepdims=True)
            rel = (b_i * blk - past + key_in_blk).astype(F32)
            s = s_ref[:, cols] + slope * rel + jnp.where(picked > 0.0, 0.0, NEG_BIG)
            s_ref[:, cols] = s
            return jnp.maximum(m_wide, s)

        unroll = 8 if n_blk % 8 == 0 else 1
        m_wide = lax.fori_loop(0, n_blk, add_bias, jnp.full((rows, blk), NEG_BIG, F32), unroll=unroll)
        m = jnp.maximum(jnp.max(m_wide, axis=-1, keepdims=True), jnp.max(s_own, axis=-1, keepdims=True))

        def probs(b_i, l_wide):
            cols = pl.ds(pl.multiple_of(b_i * blk, blk), blk)
            p = jnp.exp(s_ref[:, cols] - m)
            p_ref[:, cols] = p.astype(BF16)
            return l_wide + p

        l_wide = lax.fori_loop(0, n_blk, probs, jnp.zeros((rows, blk), F32), unroll=unroll)
        p_own = jnp.exp(s_own - m)
        l_ref[...] = jnp.sum(l_wide, axis=-1, keepdims=True) + jnp.sum(p_own, axis=-1, keepdims=True)
        acc_ref[...] = _bdot(p_own, vn_ref[0])

    @pl.when(j >= n_k_steps)
    def _():
        acc = acc_ref[...]
        for p_i in range(n_per_step):
            page_idx = (j - n_k_steps) * n_per_step + p_i
            p = p_ref[:, pl.ds(pl.multiple_of(page_idx * page, page), page)]
            vt = v_refs[p_i][0].reshape(w, page).astype(BF16)
            acc = acc + lax.dot_general(p, vt, _NT, preferred_element_type=F32)
        acc_ref[...] = acc

    @pl.when(j == pl.num_programs(1) - 1)
    def _():
        res = jnp.where(col_head == row_head, acc_ref[...] / l_ref[...], 0.0)
        out = res[0:ts]
        for h in range(1, n_head):
            out = out + res[h * ts:(h + 1) * ts]
        o_ref[0] = out


def _moba_sample(qa, ka, va, cache_kt, cache_vt, page_table, *, hd, n_per_step):
    b, ts, w = qa.shape
    _, n_head, _, page = cache_kt.shape
    n_pages = page_table.shape[1]
    ppb = MOBA_BLOCK // page
    assert n_pages % n_per_step == 0 and n_per_step % ppb == 0 and n_pages // ppb <= LANES
    n_k_steps = n_pages // n_per_step
    rows = n_head * ts
    per_b = lambda a: pl.BlockSpec((1,) + a.shape[1:], lambda bi, j, pt: (bi, 0, 0))

    def page_specs(step_of):
        def spec(p):
            return pl.BlockSpec((1, n_head, hd, page),
                                lambda bi, j, pt: (pt[bi, step_of(j) * n_per_step + p], 0, 0, 0))
        return [spec(p) for p in range(n_per_step)]

    return pl.pallas_call(
        functools.partial(_moba_sample_kernel, n_per_step=n_per_step, pages_per_block=ppb, hd=hd),
        grid_spec=pltpu.PrefetchScalarGridSpec(
            num_scalar_prefetch=1,
            grid=(b, 2 * n_k_steps),
            in_specs=[pl.BlockSpec(memory_space=pltpu.SMEM), per_b(qa), per_b(ka), per_b(va)]
                     + page_specs(lambda j: jnp.minimum(j, n_k_steps - 1))
                     + page_specs(lambda j: jnp.maximum(j - n_k_steps, 0)),
            out_specs=per_b(qa),
            scratch_shapes=[pltpu.VMEM((rows, w), F32),
                            pltpu.VMEM((rows, n_pages * page), F32),
                            pltpu.VMEM((rows, n_pages * page), BF16),
                            pltpu.VMEM((w, LANES), F32),
                            pltpu.VMEM((rows, 1), F32), pltpu.VMEM((rows, w), F32)],
        ),
        out_shape=jax.ShapeDtypeStruct((b, ts, w), F32),
        compiler_params=_params("parallel", "arbitrary"),
        name="moba_sample",
    )(page_table, jnp.asarray(_alibi_slopes(n_head), dtype=F32), qa, ka, va,
      *([cache_kt] * n_per_step), *([cache_vt] * n_per_step))


def _gla_kernel(q_ref, k_ref, v_ref, la_ref, rg_ref, gn_ref, s0_ref, o_ref, sout_ref, st_ref, *, dk, dv):
    c_idx = pl.program_id(1)
    n_chunks = pl.num_programs(1)
    c = q_ref.shape[1]
    n_pair = q_ref.shape[2] // LANES
    per_pair = LANES // dk
    assert dv == LANES and c % SUBLANES == 0

    @pl.when(c_idx == 0)
    def _():
        for p in range(n_pair):
            st_ref[p] = s0_ref[0, p].T

    row = lax.broadcasted_iota(jnp.int32, (c, LANES), 0)
    lane = lax.broadcasted_iota(jnp.int32, (c, LANES), 1)
    row2 = lax.broadcasted_iota(jnp.int32, (c, c), 0)
    col2 = lax.broadcasted_iota(jnp.int32, (c, c), 1)
    tri = jnp.where(row2 >= col2, 1.0, 0.0)
    b_all = jnp.dot(tri, la_ref[0], precision=lax.Precision.HIGHEST, preferred_element_type=F32)
    nb8 = c // SUBLANES
    sub = lax.broadcasted_iota(jnp.int32, (nb8, SUBLANES, LANES), 1)
    lane3 = lax.broadcasted_iota(jnp.int32, (nb8, SUBLANES, LANES), 2)
    lane_st = lax.broadcasted_iota(jnp.int32, (dv, LANES), 1)

    def bcast_row(x3, jj):
        return jnp.broadcast_to(x3[:, jj:jj + 1, :], x3.shape)

    for p in range(n_pair):
        cols = slice(p * LANES, (p + 1) * LANES)
        q = q_ref[0, :, cols] * (dk ** -0.5)
        k = k_ref[0, :, cols]
        b = b_all[:, cols]
        vs = [v_ref[0, :, (p * per_pair + hh) * dv:(p * per_pair + hh + 1) * dv] for hh in range(per_pair)]
        head_lanes = [(lane >= hh * dk) & (lane < (hh + 1) * dk) for hh in range(per_pair)]
        head_lanes3 = [(lane3 >= hh * dk) & (lane3 < (hh + 1) * dk) for hh in range(per_pair)]
        st = st_ref[p]

        qe = q * jnp.exp(b)
        o = [_bdot_nt(jnp.where(head_lanes[hh], qe, 0.0), st) for hh in range(per_pair)]

        q3 = q.reshape(nb8, SUBLANES, LANES)
        k3 = k.reshape(nb8, SUBLANES, LANES)
        b3 = b.reshape(nb8, SUBLANES, LANES)
        v3 = [v.reshape(nb8, SUBLANES, dv) for v in vs]
        o3 = [jnp.zeros((nb8, SUBLANES, dv), F32) for _ in range(per_pair)]
        for jj in range(SUBLANES):
            e = jnp.exp(jnp.minimum(b3 - bcast_row(b3, jj), 0.0))
            term = jnp.where(sub >= jj, q3 * bcast_row(k3, jj) * e, 0.0)
            for hh in range(per_pair):
                a = jnp.sum(jnp.where(head_lanes3[hh], term, 0.0), axis=-1, keepdims=True)
                o3[hh] = o3[hh] + a * bcast_row(v3[hh], jj)
        o = [o[hh] + o3[hh].reshape(c, dv) for hh in range(per_pair)]

        attn = [jnp.zeros((c, c), F32) for _ in range(per_pair)]
        m_half = SUBLANES
        while 2 * m_half <= c:
            span = 2 * m_half
            bnd = jnp.broadcast_to(b.reshape(c // span, span, LANES)[:, m_half - 1:m_half, :],
                                   (c // span, span, LANES)).reshape(c, LANES)
            upper = (row % span) >= m_half
            qm = jnp.where(upper, q * jnp.exp(jnp.minimum(b - bnd, 0.0)), 0.0)
            km = jnp.where(upper, 0.0, k * jnp.exp(jnp.minimum(bnd - b, 0.0)))
            same = (row2 // span) == (col2 // span)
            for hh in range(per_pair):
                a = _bdot_nt(jnp.where(head_lanes[hh], qm, 0.0), km)
                attn[hh] = attn[hh] + jnp.where(same, a, 0.0)
            m_half = span
        if c > SUBLANES:
            o = [o[hh] + _bdot(attn[hh], vs[hh]) for hh in range(per_pair)]

        b_last = b[c - 1:c, :]
        kk = (k * jnp.exp(b_last - b)).astype(BF16)
        upd = lax.dot_general(vs[0].astype(BF16), kk, _TN, preferred_element_type=F32)
        for hh in range(1, per_pair):
            u = lax.dot_general(vs[hh].astype(BF16), kk, _TN, preferred_element_type=F32)
            upd = jnp.where(lane_st >= hh * dk, u, upd)
        st_new = st * jnp.exp(b_last) + upd
        st_ref[p] = st_new

        for hh in range(per_pair):
            h = p * per_pair + hh
            rg = rg_ref[0, :, h * dv:(h + 1) * dv]
            o_ref[0, :, h * dv:(h + 1) * dv] = _rms(o[hh], gn_ref[...]) * (rg * _sigmoid(rg))

        @pl.when(c_idx == n_chunks - 1)
        def _(p=p, st_new=st_new):
            sout_ref[0, p] = st_new.T


def _gla(qg, kg, vg, la, rg, g_norm, s0, *, chunk, dk, dv):
    b, t, wk = qg.shape
    wv = vg.shape[2]
    n_head = wk // dk
    n_pair = wk // LANES
    s0p = s0.reshape(b, n_pair, LANES, dv)
    tok = lambda w: pl.BlockSpec((1, chunk, w), lambda bi, ci: (bi, ci, 0))
    st_spec = pl.BlockSpec((1, n_pair, LANES, dv), lambda bi, ci: (bi, 0, 0, 0))
    og, s_new = pl.pallas_call(
        functools.partial(_gla_kernel, dk=dk, dv=dv),
        grid=(b, t // chunk),
        in_specs=[tok(wk), tok(wk), tok(wv), tok(wk), tok(wv),
                  pl.BlockSpec(g_norm.shape, lambda bi, ci: (0, 0)), st_spec],
        out_specs=[tok(wv), st_spec],
        out_shape=[jax.ShapeDtypeStruct((b, t, wv), F32), jax.ShapeDtypeStruct((b, n_pair, LANES, dv), F32)],
        scratch_shapes=[pltpu.VMEM((n_pair, dv, LANES), F32)],
        compiler_params=_params("parallel", "arbitrary"),
        name="gla",
    )(qg, kg, vg, la, rg, g_norm, s0p)
    return og, s_new.reshape(b, n_head, dk, dv)


def _merge_kernel(x_ref, oa_ref, og_ref, sga_ref, sgb_ref, wa_ref, wb_ref, wo_ref, g_ref, y_ref):
    merged = (sga_ref[...].astype(F32) * _bdot(oa_ref[...], wa_ref[...])
              + sgb_ref[...].astype(F32) * _bdot(og_ref[...], wb_ref[...]))
    y_ref[...] = x_ref[...] + _rms(_bdot(merged, wo_ref[...]), g_ref[...])


def _mlp_kernel(x_ref, wu_ref, wd_ref, g1_ref, g2_ref, y_ref):
    x = x_ref[...]
    u = _bdot(_rms(x, g1_ref[...]), wu_ref[...])
    u = jnp.square(jnp.maximum(u, 0.0))
    y_ref[...] = x + _rms(_bdot(u, wd_ref[...]), g2_ref[...])


def _rowwise_call(kernel, name, row_inputs, const_inputs, out_width, tm):
    n = row_inputs[0].shape[0]
    assert n % tm == 0
    row = lambda a: pl.BlockSpec((tm, a.shape[1]), lambda i: (i, 0))
    full = lambda a: pl.BlockSpec(a.shape, lambda i: (0,) * a.ndim)
    return pl.pallas_call(
        kernel,
        grid=(n // tm,),
        in_specs=[row(a) for a in row_inputs] + [full(a) for a in const_inputs],
        out_specs=pl.BlockSpec((tm, out_width), lambda i: (i, 0)),
        out_shape=jax.ShapeDtypeStruct((n, out_width), F32),
        compiler_params=_params("parallel"),
        name=name,
    )(*row_inputs, *const_inputs)


def _gla_chunk(t):
    c = SUBLANES
    while c * 2 <= min(t, LANES) and t % (c * 2) == 0:
        c *= 2
    return c


def kernel(x_prompt, x_sample, cache_k, cache_v, page_table, state_gla, w_in, w_gla_gate, b_gla_gate, g_gla_norm,
           w_branch_a, w_branch_b, w_out, w_up, w_down, g_pre_mix, g_post_mix, g_pre_mlp, g_post_mlp):
    bp, tp, d_model = x_prompt.shape
    bs, ts, _ = x_sample.shape
    depth, n_phys, page, n_head, hd = cache_k.shape
    _, _, n_head_g, dk, dv = state_gla.shape
    w_a = n_head * hd
    qk_g = n_head_g * dk
    v_g = n_head_g * dv
    rank = w_gla_gate.shape[1]
    n_main = 3 * w_a + 2 * qk_g + 2 * v_g

    hp = x_prompt.reshape(bp * tp, d_model)
    hs = x_sample.reshape(bs * ts, d_model)
    outs = [[] for _ in range(6)]
    for l in range(depth):
        wm = w_in[l, :, :n_main].astype(BF16)
        wlr = jnp.pad(w_in[l, :, n_main:n_main + rank], ((0, 0), (0, LANES - rank))).astype(BF16)
        wgt = w_in[l, :, n_main + rank:].astype(BF16)
        wgg = jnp.pad(w_gla_gate[l], ((0, LANES - rank), (0, 0))).astype(BF16)
        bgg = b_gla_gate[l][None, :]
        wa, wb, wo = (w[l].astype(BF16) for w in (w_branch_a, w_branch_b, w_out))
        wu, wd = w_up[l].astype(BF16), w_down[l].astype(BF16)
        g_mix, g_pm, g_mlp, g_pl, g_gn = (g[l][None, :] for g in
                                          (g_pre_mix, g_post_mix, g_pre_mlp, g_post_mlp, g_gla_norm))

        def layer(x, b, t, moba, s0):
            qa, ka, va, qg, kg, vg, rg, la, sga, sgb, kmean = _proj(
                x, g_mix, wm, wlr, wgt, wgg, bgg, w_a=w_a, qk_g=qk_g, v_g=v_g)
            r3 = lambda a: a.reshape(b, t, a.shape[-1])
            oa = moba(r3(qa), r3(ka), r3(va), kmean)
            og, s_new = _gla(r3(qg), r3(kg), r3(vg), r3(la), r3(rg), g_gn, s0, chunk=_gla_chunk(t), dk=dk, dv=dv)
            n = b * t
            x = _rowwise_call(_merge_kernel, "merge", [x, oa.reshape(n, w_a), og.reshape(n, v_g), sga, sgb],
                              [wa, wb, wo, g_pm], d_model, MOBA_BLOCK)
            x = _rowwise_call(_mlp_kernel, "mlp", [x], [wu, wd, g_mlp, g_pl], d_model, MOBA_BLOCK)
            return x, ka.reshape(b, t, n_head, hd), va.reshape(b, t, n_head, hd), s_new

        def moba_prompt(qa, ka, va, kmean):
            return _moba_prompt(qa, ka, va, kmean.reshape(bp, tp // MOBA_BLOCK, w_a), hd=hd)

        def moba_sample(qa, ka, va, kmean_unused):
            n_pages = page_table.shape[1]
            n_per_step = 16 if n_pages % 16 == 0 else MOBA_BLOCK // page
            to_stored = lambda c: jnp.transpose(c[l], (0, 2, 3, 1))
            return _moba_sample(qa, ka, va, to_stored(cache_k), to_stored(cache_v), page_table,
                                hd=hd, n_per_step=n_per_step)

        hp, kp, vp, sp = layer(hp, bp, tp, moba_prompt, jnp.zeros((bp, n_head_g, dk, dv), state_gla.dtype))
        hs, ksn, vsn, ssn = layer(hs, bs, ts, moba_sample, state_gla[l])
        for lst, val in zip(outs, (kp, vp, sp, ksn, vsn, ssn)):
            lst.append(val)
    return (hp.reshape(bp, tp, d_model), hs.reshape(bs, ts, d_model)) + tuple(jnp.stack(o) for o in outs)
```

```python
import functools

import ml_dtypes
import numpy as np
import jax
import jax.numpy as jnp
from jax import lax
from jax.experimental import pallas as pl
from jax.experimental.pallas import tpu as pltpu

F32 = jnp.float32
BF16 = jnp.bfloat16

LANES = 128
SUBLANES = 8
VMEM_LIMIT_BYTES = 56 * 1024 * 1024

EPS = 1e-6
MOBA_BLOCK = 256
MOBA_TOPK = 3
GLA_TAU = 16.0
NEG_BIG = -1e30
PENALTY = -30000.0
LOG2E = float(np.log2(np.e))

_NT = (((1,), (1,)), ((), ()))
_TN = (((0,), (0,)), ((), ()))


def _params(*sem):
    return pltpu.CompilerParams(dimension_semantics=sem, vmem_limit_bytes=VMEM_LIMIT_BYTES)


def _sigmoid(x):
    return 1.0 / (1.0 + jnp.exp(-x))


def _rms(x, g):
    return x * lax.rsqrt(jnp.mean(x * x, axis=-1, keepdims=True) + EPS) * g


def _bdot(a, b):
    return jnp.dot(a.astype(BF16), b.astype(BF16), preferred_element_type=F32)


def _bdot_nt(a, b):
    return lax.dot_general(a.astype(BF16), b.astype(BF16), _NT, preferred_element_type=F32)


def _proj_kernel(x_ref, g_ref, wm_ref, wlr_ref, wgt_ref, wgg_ref, bgg_ref,
                 qa_ref, ka_ref, va_ref, qg_ref, kg_ref, vg_ref, rg_ref, la_ref, sga_ref, sgb_ref, kmean_ref,
                 *, w_a, qk_g, v_g, d_model):
    hb = _rms(x_ref[...], g_ref[...]).astype(BF16)

    def proj(lo, n):
        return jnp.dot(hb, wm_ref[:, lo:lo + n], preferred_element_type=F32)

    qa_ref[...] = proj(0, w_a)
    ka = proj(w_a, w_a)
    ka_ref[...] = ka
    kmean_ref[0] = jnp.sum(ka, axis=0, keepdims=True) * (1.0 / ka.shape[0])
    va_ref[...] = proj(2 * w_a, w_a)
    lo = 3 * w_a
    qg_ref[...] = proj(lo, qk_g)
    kg_ref[...] = proj(lo + qk_g, qk_g)
    vg_ref[...] = proj(lo + 2 * qk_g, v_g)
    rg_ref[...] = proj(lo + 2 * qk_g + v_g, v_g)
    lr = jnp.dot(hb, wlr_ref[...], preferred_element_type=F32)
    xg = jnp.dot(lr.astype(BF16), wgg_ref[...], preferred_element_type=F32) + bgg_ref[...]
    log_sig = jnp.minimum(xg, 0.0) - jnp.log1p(jnp.exp(-jnp.abs(xg)))
    la_ref[...] = log_sig * (1.0 / GLA_TAU)
    ga = jnp.dot(hb, wgt_ref[:, :d_model], preferred_element_type=F32)
    sga_ref[...] = _sigmoid(ga).astype(BF16)
    gb = jnp.dot(hb, wgt_ref[:, d_model:], preferred_element_type=F32)
    sgb_ref[...] = _sigmoid(gb).astype(BF16)


def _proj(x, g, wm, wlr, wgt, wgg, bgg, *, w_a, qk_g, v_g):
    n, d_model = x.shape
    tm = MOBA_BLOCK
    assert n % tm == 0
    nt = n // tm
    row = lambda w: pl.BlockSpec((tm, w), lambda i: (i, 0))
    full = lambda a: pl.BlockSpec(a.shape, lambda i: (0,) * a.ndim)
    widths = (w_a, w_a, w_a, qk_g, qk_g, v_g, v_g, qk_g)
    out_shape = [jax.ShapeDtypeStruct((n, w), F32) for w in widths]
    out_shape += [jax.ShapeDtypeStruct((n, d_model), BF16)] * 2
    out_shape += [jax.ShapeDtypeStruct((nt, 1, w_a), F32)]
    out_specs = [row(w) for w in widths] + [row(d_model)] * 2
    out_specs += [pl.BlockSpec((1, 1, w_a), lambda i: (i, 0, 0))]
    return pl.pallas_call(
        functools.partial(_proj_kernel, w_a=w_a, qk_g=qk_g, v_g=v_g, d_model=d_model),
        grid=(nt,),
        in_specs=[row(d_model), full(g), full(wm), full(wlr), full(wgt), full(wgg), full(bgg)],
        out_specs=out_specs,
        out_shape=out_shape,
        compiler_params=_params("parallel"),
        name="proj",
    )(x, g, wm, wlr, wgt, wgg, bgg)


def _top_blocks(gate, candidate, lane):
    lane_f = lane.astype(F32)
    no_lane = float(gate.shape[-1])
    avail = jnp.where(candidate, 1.0, 0.0)
    sel = jnp.zeros(gate.shape, F32)
    for _ in range(MOBA_TOPK):
        cur = jnp.where(avail > 0.0, gate, -jnp.inf)
        mx = jnp.max(cur, axis=-1, keepdims=True)
        cand = jnp.where(avail > 0.0, jnp.where(cur == mx, lane_f, no_lane), no_lane)
        first = jnp.min(cand, axis=-1, keepdims=True)
        pick = lane_f == first
        sel = jnp.where(pick, 1.0, sel)
        avail = jnp.where(pick, 0.0, avail)
    return sel


def _alibi_slopes(n_head):
    return 2.0 ** (-8.0 * (np.arange(n_head) + 1) / n_head)


_SLOPE_PARTS = 3
_AUG_EXTRA = 2 * _SLOPE_PARTS


def _moba_prompt_kernel(sl_ref, q_ref, k_ref, v_ref, kc_ref, km_ref, o_ref, kaug_ref, vt_ref, *, hd, tk, cbw):
    hp = pl.program_id(1)
    qi = pl.program_id(2)
    blk = MOBA_BLOCK
    t = k_ref.shape[1]
    tq = q_ref.shape[1]
    n_blk = t // blk
    per_pair = LANES // hd

    @pl.when(qi == 0)
    def _():
        def conv(c, carry):
            rows = pl.ds(pl.multiple_of(c * blk, blk), blk)
            kaug_ref[rows, :LANES] = k_ref[0, rows, :].astype(BF16)
            kaug_ref[rows, LANES:] = kc_ref[rows, :]
            vt_ref[:, rows] = v_ref[0, rows, :].T.astype(BF16)
            return carry
        lax.fori_loop(0, t // blk, conv, 0)

    lane = lax.broadcasted_iota(jnp.int32, (tq, LANES), 1)
    own = qi * (tq // blk) + lax.broadcasted_iota(jnp.int32, (tq, LANES), 0) // blk
    key2 = lax.broadcasted_iota(jnp.int32, (tk, cbw), 0)
    qry2 = lax.broadcasted_iota(jnp.int32, (tk, cbw), 1)
    q = q_ref[0]
    qk_scale = hd ** -0.5 * LOG2E
    tiles_per_q = tq // tk
    n_cb = tq // cbw

    q_aug, tile_bias = [], []
    for hh in range(per_pair):
        h = hp * per_pair + hh
        head_lanes = (lane >= hh * hd) & (lane < (hh + 1) * hd)
        qh = jnp.where(head_lanes, q, 0.0)
        gate = _bdot_nt(qh, km_ref[0])
        sel = _top_blocks(gate, lane < own, lane)
        parts = [sl_ref[1 + i, h] for i in range(_SLOPE_PARTS)]
        qx = jnp.where(lane < n_blk, jnp.where((sel > 0.0) | (lane == own), 0.0, PENALTY), 0.0)
        for off, val in enumerate(parts + [part * blk for part in parts]):
            qx = jnp.where(lane == n_blk + off, val, qx)
        q_aug.append(jnp.concatenate([(qh * qk_scale).astype(BF16), qx.astype(BF16)], axis=1))
        tile_bias.append(sl_ref[0, h] * tq)

    chains = [(hh, cb) for hh in range(per_pair) for cb in range(n_cb)]

    def tile_keys(j):
        return pl.ds(pl.multiple_of(j * tk, tk), tk)

    def absorb(stat, s, vt, c):
        m, l, acc = stat
        m_new = jnp.maximum(m, jnp.max(s, axis=0, keepdims=True) + c)
        alpha = jnp.exp2(m - m_new)
        p = jnp.exp2(s - (m_new - c))
        return (m_new, alpha * l + jnp.sum(p, axis=0, keepdims=True),
                alpha * acc + jnp.dot(vt, p.astype(BF16), preferred_element_type=F32))

    def logits(kt, hh, cb):
        return lax.dot_general(kt, q_aug[hh][cb * cbw:(cb + 1) * cbw], _NT, preferred_element_type=F32)

    stats = [(jnp.full((1, cbw), NEG_BIG, F32), jnp.zeros((1, cbw), F32), jnp.zeros((LANES, cbw), F32))
             for _ in chains]

    for d in range(tiles_per_q):
        keys = tile_keys(qi * tiles_per_q + d)
        kt, vt = kaug_ref[keys, :], vt_ref[:, keys]
        for ci, (hh, cb) in enumerate(chains):
            k_lo, q_lo = d * tk, cb * cbw
            if k_lo > q_lo + cbw - 1:
                continue
            s = logits(kt, hh, cb)
            if k_lo + tk - 1 > q_lo:
                s = jnp.where(key2 + k_lo <= qry2 + q_lo, s, NEG_BIG)
            stats[ci] = absorb(stats[ci], s, vt, 0.0)

    def body(j, flat):
        keys = tile_keys(j)
        kt, vt = kaug_ref[keys, :], vt_ref[:, keys]
        out = []
        for ci, (hh, cb) in enumerate(chains):
            c = -tile_bias[hh] * (qi - j // tiles_per_q).astype(F32)
            out += absorb(flat[3 * ci:3 * ci + 3], logits(kt, hh, cb), vt, c)
        return tuple(out)

    flat = lax.fori_loop(0, qi * tiles_per_q, body, tuple(x for stat in stats for x in stat))
    dim = lax.broadcasted_iota(jnp.int32, (LANES, cbw), 0)
    for cb in range(n_cb):
        out = None
        for hh in range(per_pair):
            _, l, acc = flat[3 * chains.index((hh, cb)):][:3]
            out = acc / l if out is None else jnp.where(dim >= hh * hd, acc / l, out)
        o_ref[0, cb * cbw:(cb + 1) * cbw, :] = out.T


def _moba_consts(t, tq):
    n_blk = t // MOBA_BLOCK
    assert n_blk + _AUG_EXTRA <= LANES
    pos = np.arange(t)
    c = np.zeros((t, LANES), np.float32)
    c[pos, pos // MOBA_BLOCK] = 1.0
    for i in range(_SLOPE_PARTS):
        c[:, n_blk + i] = pos % MOBA_BLOCK
        c[:, n_blk + _SLOPE_PARTS + i] = (pos % tq) // MOBA_BLOCK
    return jnp.asarray(c, dtype=BF16)


def _moba_prompt(qa, ka, va, kmean, *, hd):
    b, t, w = qa.shape
    blk = MOBA_BLOCK
    n_pair = w // LANES
    n_blk = t // blk
    tk = 2 * blk if t % (2 * blk) == 0 else blk
    tq = 2 * tk if t % (2 * tk) == 0 else tk
    sl2 = _alibi_slopes(w // hd) * LOG2E
    parts, rest = [], sl2
    for _ in range(_SLOPE_PARTS):
        parts.append(rest.astype(ml_dtypes.bfloat16).astype(np.float64))
        rest = rest - parts[-1]
    slopes = jnp.asarray(np.stack([sl2] + parts), dtype=F32)
    kc = _moba_consts(t, tq)
    km = jnp.pad(kmean, ((0, 0), (0, LANES - n_blk), (0, 0)))
    return pl.pallas_call(
        functools.partial(_moba_prompt_kernel, hd=hd, tk=tk, cbw=tq),
        grid=(b, n_pair, t // tq),
        in_specs=[
            pl.BlockSpec(memory_space=pltpu.SMEM),
            pl.BlockSpec((1, tq, LANES), lambda bi, hp, i: (bi, i, hp)),
            pl.BlockSpec((1, t, LANES), lambda bi, hp, i: (bi, 0, hp)),
            pl.BlockSpec((1, t, LANES), lambda bi, hp, i: (bi, 0, hp)),
            pl.BlockSpec((t, LANES), lambda bi, hp, i: (0, 0)),
            pl.BlockSpec((1, LANES, LANES), lambda bi, hp, i: (bi, 0, hp)),
        ],
        out_specs=pl.BlockSpec((1, tq, LANES), lambda bi, hp, i: (bi, i, hp)),
        out_shape=jax.ShapeDtypeStruct((b, t, w), F32),
        scratch_shapes=[pltpu.VMEM((t, 2 * LANES), BF16), pltpu.VMEM((LANES, t), BF16)],
        compiler_params=_params("parallel", "parallel", "arbitrary"),
        name="moba_prompt",
    )(slopes, qa, ka, va, kc, km)


def _moba_sample_kernel(pt_ref, slopes_ref, q_ref, kn_ref, vn_ref, *refs, n_per_step, pages_per_block, hd):
    k_refs, v_refs = refs[:n_per_step], refs[n_per_step:2 * n_per_step]
    o_ref, qbd_ref, s_ref, p_ref, ksum_ref, l_ref, acc_ref = refs[2 * n_per_step:]
    j = pl.program_id(1)
    n_k_steps = pl.num_programs(1) // 2
    ts, w = q_ref.shape[1:]
    n_head = w // hd
    rows = n_head * ts
    page = k_refs[0].shape[3]
    blk = page * pages_per_block
    past = s_ref.shape[1]
    n_blk = past // blk
    scale = hd ** -0.5

    row_head = lax.broadcasted_iota(jnp.int32, (rows, 1), 0) // ts
    row_q = lax.broadcasted_iota(jnp.int32, (rows, 1), 0) % ts
    col_head = lax.broadcasted_iota(jnp.int32, (rows, w), 1) // hd
    slope = jnp.zeros((rows, 1), F32)
    for h in range(n_head):
        slope = jnp.where(row_head == h, slopes_ref[h], slope)

    @pl.when(j == 0)
    def _():
        qt = jnp.concatenate([q_ref[0]] * n_head, axis=0)
        qbd_ref[...] = jnp.where(col_head == row_head, qt, 0.0)
        ksum_ref[...] = jnp.zeros(ksum_ref.shape, F32)

    @pl.when(j < n_k_steps)
    def _():
        qs = (qbd_ref[...] * scale).astype(BF16)
        lane = lax.broadcasted_iota(jnp.int32, ksum_ref.shape, 1)
        ksum = ksum_ref[...]
        for g in range(n_per_step // pages_per_block):
            total = None
            for p_i in range(g * pages_per_block, (g + 1) * pages_per_block):
                page_idx = j * n_per_step + p_i
                kt = k_refs[p_i][0].reshape(w, page)
                total = kt if total is None else total + kt
                s = jnp.dot(qs, kt.astype(BF16), preferred_element_type=F32)
                s_ref[:, pl.ds(pl.multiple_of(page_idx * page, page), page)] = s
            blk_idx = (j * n_per_step) // pages_per_block + g
            ksum = jnp.where(lane == blk_idx, jnp.sum(total, axis=1, keepdims=True), ksum)
        ksum_ref[...] = ksum

    @pl.when(j == n_k_steps)
    def _():
        qbd = qbd_ref[...]
        gate = _bdot(qbd, ksum_ref[...])
        lane = lax.broadcasted_iota(jnp.int32, gate.shape, 1)
        sel = _top_blocks(gate, lane < n_blk, lane)
        s_own = _bdot_nt(qbd * scale, kn_ref[0])
        kq = lax.broadcasted_iota(jnp.int32, s_own.shape, 1)
        s_own = jnp.where(kq <= row_q, s_own + slope * kq.astype(F32), NEG_BIG)
        key_in_blk = lax.broadcasted_iota(jnp.int32, (1, blk), 1)

        def add_bias(b_i, m_wide):
            cols = pl.ds(pl.multiple_of(b_i * blk, blk), blk)
            picked = jnp.max(jnp.where(lane == b_i, sel, 0.0), axis=-1, keepdims=True)
            rel = (b_i * blk - past + key_in_blk).astype(F32)
            s = s_ref[:, cols] + slope * rel + jnp.where(picked > 0.0, 0.0, NEG_BIG)
            s_ref[:, cols] = s
            return jnp.maximum(m_wide, s)

        unroll = 8 if n_blk % 8 == 0 else 1
        m_wide = lax.fori_loop(0, n_blk, add_bias, jnp.full((rows, blk), NEG_BIG, F32), unroll=unroll)
        m = jnp.maximum(jnp.max(m_wide, axis=-1, keepdims=True), jnp.max(s_own, axis=-1, keepdims=True))

        def probs(b_i, l_wide):
            cols = pl.ds(pl.multiple_of(b_i * blk, blk), blk)
            p = jnp.exp(s_ref[:, cols] - m)
            p_ref[:, cols] = p.astype(BF16)
            return l_wide + p

        l_wide = lax.fori_loop(0, n_blk, probs, jnp.zeros((rows, blk), F32), unroll=unroll)
        p_own = jnp.exp(s_own - m)
        l_ref[...] = jnp.sum(l_wide, axis=-1, keepdims=True) + jnp.sum(p_own, axis=-1, keepdims=True)
        acc_ref[...] = _bdot(p_own, vn_ref[0])

    @pl.when(j >= n_k_steps)
    def _():
        acc = acc_ref[...]
        for p_i in range(n_per_step):
            page_idx = (j - n_k_steps) * n_per_step + p_i
            p = p_ref[:, pl.ds(pl.multiple_of(page_idx * page, page), page)]
            vt = v_refs[p_i][0].reshape(w, page).astype(BF16)
            acc = acc + lax.dot_general(p, vt, _NT, preferred_element_type=F32)
        acc_ref[...] = acc

    @pl.when(j == pl.num_programs(1) - 1)
    def _():
        res = jnp.where(col_head == row_head, acc_ref[...] / l_ref[...], 0.0)
        out = res[0:ts]
        for h in range(1, n_head):
            out = out + res[h * ts:(h + 1) * ts]
        o_ref[0] = out


def _moba_sample(qa, ka, va, cache_kt, cache_vt, page_table, *, hd, n_per_step):
    b, ts, w = qa.shape
    _, n_head, _, page = cache_kt.shape
    n_pages = page_table.shape[1]
    ppb = MOBA_BLOCK // page
    assert n_pages % n_per_step == 0 and n_per_step % ppb == 0 and n_pages // ppb <= LANES
    n_k_steps = n_pages // n_per_step
    rows = n_head * ts
    per_b = lambda a: pl.BlockSpec((1,) + a.shape[1:], lambda bi, j, pt: (bi, 0, 0))

    def page_specs(step_of):
        def spec(p):
            return pl.BlockSpec((1, n_head, hd, page),
                                lambda bi, j, pt: (pt[bi, step_of(j) * n_per_step + p], 0, 0, 0))
        return [spec(p) for p in range(n_per_step)]

    return pl.pallas_call(
        functools.partial(_moba_sample_kernel, n_per_step=n_per_step, pages_per_block=ppb, hd=hd),
        grid_spec=pltpu.PrefetchScalarGridSpec(
            num_scalar_prefetch=1,
            grid=(b, 2 * n_k_steps),
            in_specs=[pl.BlockSpec(memory_space=pltpu.SMEM), per_b(qa), per_b(ka), per_b(va)]
                     + page_specs(lambda j: jnp.minimum(j, n_k_steps - 1))
                     + page_specs(lambda j: jnp.maximum(j - n_k_steps, 0)),
            out_specs=per_b(qa),
            scratch_shapes=[pltpu.VMEM((rows, w), F32),
                            pltpu.VMEM((rows, n_pages * page), F32),
                            pltpu.VMEM((rows, n_pages * page), BF16),
                            pltpu.VMEM((w, LANES), F32),
                            pltpu.VMEM((rows, 1), F32), pltpu.VMEM((rows, w), F32)],
        ),
        out_shape=jax.ShapeDtypeStruct((b, ts, w), F32),
        compiler_params=_params("parallel", "arbitrary"),
        name="moba_sample",
    )(page_table, jnp.asarray(_alibi_slopes(n_head), dtype=F32), qa, ka, va,
      *([cache_kt] * n_per_step), *([cache_vt] * n_per_step))


def _gla_kernel(q_ref, k_ref, v_ref, la_ref, rg_ref, gn_ref, s0_ref, o_ref, sout_ref, st_ref, *, dk, dv):
    c_idx = pl.program_id(1)
    n_chunks = pl.num_programs(1)
    c = q_ref.shape[1]
    n_pair = q_ref.shape[2] // LANES
    per_pair = LANES // dk
    assert dv == LANES and c % SUBLANES == 0

    @pl.when(c_idx == 0)
    def _():
        for p in range(n_pair):
            st_ref[p] = s0_ref[0, p].T

    row = lax.broadcasted_iota(jnp.int32, (c, LANES), 0)
    lane = lax.broadcasted_iota(jnp.int32, (c, LANES), 1)
    row2 = lax.broadcasted_iota(jnp.int32, (c, c), 0)
    col2 = lax.broadcasted_iota(jnp.int32, (c, c), 1)
    tri = jnp.where(row2 >= col2, 1.0, 0.0)
    b_all = jnp.dot(tri, la_ref[0], precision=lax.Precision.HIGHEST, preferred_element_type=F32)
    nb8 = c // SUBLANES
    sub = lax.broadcasted_iota(jnp.int32, (nb8, SUBLANES, LANES), 1)
    lane3 = lax.broadcasted_iota(jnp.int32, (nb8, SUBLANES, LANES), 2)
    lane_st = lax.broadcasted_iota(jnp.int32, (dv, LANES), 1)

    def bcast_row(x3, jj):
        return jnp.broadcast_to(x3[:, jj:jj + 1, :], x3.shape)

    for p in range(n_pair):
        cols = slice(p * LANES, (p + 1) * LANES)
        q = q_ref[0, :, cols] * (dk ** -0.5)
        k = k_ref[0, :, cols]
        b = b_all[:, cols]
        vs = [v_ref[0, :, (p * per_pair + hh) * dv:(p * per_pair + hh + 1) * dv] for hh in range(per_pair)]
        head_lanes = [(lane >= hh * dk) & (lane < (hh + 1) * dk) for hh in range(per_pair)]
        head_lanes3 = [(lane3 >= hh * dk) & (lane3 < (hh + 1) * dk) for hh in range(per_pair)]
        st = st_ref[p]

        qe = q * jnp.exp(b)
        o = [_bdot_nt(jnp.where(head_lanes[hh], qe, 0.0), st) for hh in range(per_pair)]

        q3 = q.reshape(nb8, SUBLANES, LANES)
        k3 = k.reshape(nb8, SUBLANES, LANES)
        b3 = b.reshape(nb8, SUBLANES, LANES)
        v3 = [v.reshape(nb8, SUBLANES, dv) for v in vs]
        o3 = [jnp.zeros((nb8, SUBLANES, dv), F32) for _ in range(per_pair)]
        for jj in range(SUBLANES):
            e = jnp.exp(jnp.minimum(b3 - bcast_row(b3, jj), 0.0))
            term = jnp.where(sub >= jj, q3 * bcast_row(k3, jj) * e, 0.0)
            for hh in range(per_pair):
                a = jnp.sum(jnp.where(head_lanes3[hh], term, 0.0), axis=-1, keepdims=True)
                o3[hh] = o3[hh] + a * bcast_row(v3[hh], jj)
        o = [o[hh] + o3[hh].reshape(c, dv) for hh in range(per_pair)]

        attn = [jnp.zeros((c, c), F32) for _ in range(per_pair)]
        m_half = SUBLANES
        while 2 * m_half <= c:
            span = 2 * m_half
            bnd = jnp.broadcast_to(b.reshape(c // span, span, LANES)[:, m_half - 1:m_half, :],
                                   (c // span, span, LANES)).reshape(c, LANES)
            upper = (row % span) >= m_half
            qm = jnp.where(upper, q * jnp.exp(jnp.minimum(b - bnd, 0.0)), 0.0)
            km = jnp.where(upper, 0.0, k * jnp.exp(jnp.minimum(bnd - b, 0.0)))
            same = (row2 // span) == (col2 // span)
            for hh in range(per_pair):
                a = _bdot_nt(jnp.where(head_lanes[hh], qm, 0.0), km)
                attn[hh] = attn[hh] + jnp.where(same, a, 0.0)
            m_half = span
        if c > SUBLANES:
            o = [o[hh] + _bdot(attn[hh], vs[hh]) for hh in range(per_pair)]

        b_last = b[c - 1:c, :]
        kk = (k * jnp.exp(b_last - b)).astype(BF16)
        upd = lax.dot_general(vs[0].astype(BF16), kk, _TN, preferred_element_type=F32)
        for hh in range(1, per_pair):
            u = lax.dot_general(vs[hh].astype(BF16), kk, _TN, preferred_element_type=F32)
            upd = jnp.where(lane_st >= hh * dk, u, upd)
        st_new = st * jnp.exp(b_last) + upd
        st_ref[p] = st_new

        for hh in range(per_pair):
            h = p * per_pair + hh
            rg = rg_ref[0, :, h * dv:(h + 1) * dv]
            o_ref[0, :, h * dv:(h + 1) * dv] = _rms(o[hh], gn_ref[...]) * (rg * _sigmoid(rg))

        @pl.when(c_idx == n_chunks - 1)
        def _(p=p, st_new=st_new):
            sout_ref[0, p] = st_new.T


def _gla(qg, kg, vg, la, rg, g_norm, s0, *, chunk, dk, dv):
    b, t, wk = qg.shape
    wv = vg.shape[2]
    n_head = wk // dk
    n_pair = wk // LANES
    s0p = s0.reshape(b, n_pair, LANES, dv)
    tok = lambda w: pl.BlockSpec((1, chunk, w), lambda bi, ci: (bi, ci, 0))
    st_spec = pl.BlockSpec((1, n_pair, LANES, dv), lambda bi, ci: (bi, 0, 0, 0))
    og, s_new = pl.pallas_call(
        functools.partial(_gla_kernel, dk=dk, dv=dv),
        grid=(b, t // chunk),
        in_specs=[tok(wk), tok(wk), tok(wv), tok(wk), tok(wv),
                  pl.BlockSpec(g_norm.shape, lambda bi, ci: (0, 0)), st_spec],
        out_specs=[tok(wv), st_spec],
        out_shape=[jax.ShapeDtypeStruct((b, t, wv), F32), jax.ShapeDtypeStruct((b, n_pair, LANES, dv), F32)],
        scratch_shapes=[pltpu.VMEM((n_pair, dv, LANES), F32)],
        compiler_params=_params("parallel", "arbitrary"),
        name="gla",
    )(qg, kg, vg, la, rg, g_norm, s0p)
    return og, s_new.reshape(b, n_head, dk, dv)


def _merge_kernel(x_ref, oa_ref, og_ref, sga_ref, sgb_ref, wa_ref, wb_ref, wo_ref, g_ref, y_ref):
    merged = (sga_ref[...].astype(F32) * _bdot(oa_ref[...], wa_ref[...])
              + sgb_ref[...].astype(F32) * _bdot(og_ref[...], wb_ref[...]))
    y_ref[...] = x_ref[...] + _rms(_bdot(merged, wo_ref[...]), g_ref[...])


def _mlp_kernel(x_ref, wu_ref, wd_ref, g1_ref, g2_ref, y_ref):
    x = x_ref[...]
    u = _bdot(_rms(x, g1_ref[...]), wu_ref[...])
    u = jnp.square(jnp.maximum(u, 0.0))
    y_ref[...] = x + _rms(_bdot(u, wd_ref[...]), g2_ref[...])


def _rowwise_call(kernel, name, row_inputs, const_inputs, out_width, tm):
    n = row_inputs[0].shape[0]
    assert n % tm == 0
    row = lambda a: pl.BlockSpec((tm, a.shape[1]), lambda i: (i, 0))
    full = lambda a: pl.BlockSpec(a.shape, lambda i: (0,) * a.ndim)
    return pl.pallas_call(
        kernel,
        grid=(n // tm,),
        in_specs=[row(a) for a in row_inputs] + [full(a) for a in const_inputs],
        out_specs=pl.BlockSpec((tm, out_width), lambda i: (i, 0)),
        out_shape=jax.ShapeDtypeStruct((n, out_width), F32),
        compiler_params=_params("parallel"),
        name=name,
    )(*row_inputs, *const_inputs)


def _gla_chunk(t):
    c = SUBLANES
    while c * 2 <= min(t, LANES) and t % (c * 2) == 0:
        c *= 2
    return c


def kernel(x_prompt, x_sample, cache_k, cache_v, page_table, state_gla, w_in, w_gla_gate, b_gla_gate, g_gla_norm,
           w_branch_a, w_branch_b, w_out, w_up, w_down, g_pre_mix, g_post_mix, g_pre_mlp, g_post_mlp):
    bp, tp, d_model = x_prompt.shape
    bs, ts, _ = x_sample.shape
    depth, n_phys, page, n_head, hd = cache_k.shape
    _, _, n_head_g, dk, dv = state_gla.shape
    w_a = n_head * hd
    qk_g = n_head_g * dk
    v_g = n_head_g * dv
    rank = w_gla_gate.shape[1]
    n_main = 3 * w_a + 2 * qk_g + 2 * v_g

    hp = x_prompt.reshape(bp * tp, d_model)
    hs = x_sample.reshape(bs * ts, d_model)
    outs = [[] for _ in range(6)]
    for l in range(depth):
        wm = w_in[l, :, :n_main].astype(BF16)
        wlr = jnp.pad(w_in[l, :, n_main:n_main + rank], ((0, 0), (0, LANES - rank))).astype(BF16)
        wgt = w_in[l, :, n_main + rank:].astype(BF16)
        wgg = jnp.pad(w_gla_gate[l], ((0, LANES - rank), (0, 0))).astype(BF16)
        bgg = b_gla_gate[l][None, :]
        wa, wb, wo = (w[l].astype(BF16) for w in (w_branch_a, w_branch_b, w_out))
        wu, wd = w_up[l].astype(BF16), w_down[l].astype(BF16)
        g_mix, g_pm, g_mlp, g_pl, g_gn = (g[l][None, :] for g in
                                          (g_pre_mix, g_post_mix, g_pre_mlp, g_post_mlp, g_gla_norm))

        def layer(x, b, t, moba, s0):
            qa, ka, va, qg, kg, vg, rg, la, sga, sgb, kmean = _proj(
                x, g_mix, wm, wlr, wgt, wgg, bgg, w_a=w_a, qk_g=qk_g, v_g=v_g)
            r3 = lambda a: a.reshape(b, t, a.shape[-1])
            oa = moba(r3(qa), r3(ka), r3(va), kmean)
            og, s_new = _gla(r3(qg), r3(kg), r3(vg), r3(la), r3(rg), g_gn, s0, chunk=_gla_chunk(t), dk=dk, dv=dv)
            n = b * t
            x = _rowwise_call(_merge_kernel, "merge", [x, oa.reshape(n, w_a), og.reshape(n, v_g), sga, sgb],
                              [wa, wb, wo, g_pm], d_model, MOBA_BLOCK)
            x = _rowwise_call(_mlp_kernel, "mlp", [x], [wu, wd, g_mlp, g_pl], d_model, MOBA_BLOCK)
            return x, ka.reshape(b, t, n_head, hd), va.reshape(b, t, n_head, hd), s_new

        def moba_prompt(qa, ka, va, kmean):
            return _moba_prompt(qa, ka, va, kmean.reshape(bp, tp // MOBA_BLOCK, w_a), hd=hd)

        def moba_sample(qa, ka, va, kmean_unused):
            n_pages = page_table.shape[1]
            n_per_step = 32 if n_pages % 32 == 0 else MOBA_BLOCK // page
            to_stored = lambda c: jnp.transpose(c[l], (0, 2, 3, 1))
            return _moba_sample(qa, ka, va, to_stored(cache_k), to_stored(cache_v), page_table,
                                hd=hd, n_per_step=n_per_step)

        hp, kp, vp, sp = layer(hp, bp, tp, moba_prompt, jnp.zeros((bp, n_head_g, dk, dv), state_gla.dtype))
        hs, ksn, vsn, ssn = layer(hs, bs, ts, moba_sample, state_gla[l])
        for lst, val in zip(outs, (kp, vp, sp, ksn, vsn, ssn)):
            lst.append(val)
    return (hp.reshape(bp, tp, d_model), hs.reshape(bs, ts, d_model)) + tuple(jnp.stack(o) for o in outs)
```

```python
import functools

import ml_dtypes
import numpy as np
import jax
import jax.numpy as jnp
from jax import lax
from jax.experimental import pallas as pl
from jax.experimental.pallas import tpu as pltpu

F32 = jnp.float32
BF16 = jnp.bfloat16

LANES = 128
SUBLANES = 8
VMEM_LIMIT_BYTES = 56 * 1024 * 1024

EPS = 1e-6
MOBA_BLOCK = 256
MOBA_TOPK = 3
GLA_TAU = 16.0
NEG_BIG = -1e30
PENALTY = -30000.0
LOG2E = float(np.log2(np.e))

_NT = (((1,), (1,)), ((), ()))
_TN = (((0,), (0,)), ((), ()))


def _params(*sem):
    return pltpu.CompilerParams(dimension_semantics=sem, vmem_limit_bytes=VMEM_LIMIT_BYTES)


def _sigmoid(x):
    return 1.0 / (1.0 + jnp.exp(-x))


def _rms(x, g):
    return x * lax.rsqrt(jnp.mean(x * x, axis=-1, keepdims=True) + EPS) * g


def _bdot(a, b):
    return jnp.dot(a.astype(BF16), b.astype(BF16), preferred_element_type=F32)


def _bdot_nt(a, b):
    return lax.dot_general(a.astype(BF16), b.astype(BF16), _NT, preferred_element_type=F32)


def _proj_kernel(x_ref, g_ref, wm_ref, wlr_ref, wgt_ref, wgg_ref, bgg_ref,
                 qa_ref, ka_ref, va_ref, qg_ref, kg_ref, vg_ref, rg_ref, la_ref, sga_ref, sgb_ref, kmean_ref,
                 *, w_a, qk_g, v_g, d_model, kv_transposed):
    hb = _rms(x_ref[...], g_ref[...]).astype(BF16)

    def proj(lo, n):
        return jnp.dot(hb, wm_ref[:, lo:lo + n], preferred_element_type=F32)

    qa_ref[...] = proj(0, w_a)
    ka = proj(w_a, w_a)
    va = proj(2 * w_a, w_a)
    kmean_ref[0] = jnp.sum(ka, axis=0, keepdims=True) * (1.0 / ka.shape[0])
    if kv_transposed:
        ka_ref[0] = ka.T
        va_ref[0] = va.T
    else:
        ka_ref[...] = ka
        va_ref[...] = va
    lo = 3 * w_a
    qg_ref[...] = proj(lo, qk_g)
    kg_ref[...] = proj(lo + qk_g, qk_g)
    vg_ref[...] = proj(lo + 2 * qk_g, v_g)
    rg_ref[...] = proj(lo + 2 * qk_g + v_g, v_g)
    lr = jnp.dot(hb, wlr_ref[...], preferred_element_type=F32)
    xg = jnp.dot(lr.astype(BF16), wgg_ref[...], preferred_element_type=F32) + bgg_ref[...]
    log_sig = jnp.minimum(xg, 0.0) - jnp.log1p(jnp.exp(-jnp.abs(xg)))
    la_ref[...] = log_sig * (1.0 / GLA_TAU)
    ga = jnp.dot(hb, wgt_ref[:, :d_model], preferred_element_type=F32)
    sga_ref[...] = _sigmoid(ga).astype(BF16)
    gb = jnp.dot(hb, wgt_ref[:, d_model:], preferred_element_type=F32)
    sgb_ref[...] = _sigmoid(gb).astype(BF16)


def _proj(x, g, wm, wlr, wgt, wgg, bgg, *, w_a, qk_g, v_g, kv_seq_len=None):
    n, d_model = x.shape
    tm = MOBA_BLOCK
    assert n % tm == 0
    nt = n // tm
    row = lambda w: pl.BlockSpec((tm, w), lambda i: (i, 0))
    full = lambda a: pl.BlockSpec(a.shape, lambda i: (0,) * a.ndim)
    widths = (w_a, w_a, w_a, qk_g, qk_g, v_g, v_g, qk_g)
    out_shape = [jax.ShapeDtypeStruct((n, w), F32) for w in widths]
    out_shape += [jax.ShapeDtypeStruct((n, d_model), BF16)] * 2
    out_shape += [jax.ShapeDtypeStruct((nt, 1, w_a), F32)]
    out_specs = [row(w) for w in widths] + [row(d_model)] * 2
    out_specs += [pl.BlockSpec((1, 1, w_a), lambda i: (i, 0, 0))]
    if kv_seq_len is not None:
        assert kv_seq_len % tm == 0
        seq_tiles = kv_seq_len // tm
        for i in (1, 2):
            out_shape[i] = jax.ShapeDtypeStruct((n // kv_seq_len, w_a, kv_seq_len), F32)
            out_specs[i] = pl.BlockSpec((1, w_a, tm), lambda i: (i // seq_tiles, 0, i % seq_tiles))
    return pl.pallas_call(
        functools.partial(_proj_kernel, w_a=w_a, qk_g=qk_g, v_g=v_g, d_model=d_model,
                          kv_transposed=kv_seq_len is not None),
        grid=(nt,),
        in_specs=[row(d_model), full(g), full(wm), full(wlr), full(wgt), full(wgg), full(bgg)],
        out_specs=out_specs,
        out_shape=out_shape,
        compiler_params=_params("parallel"),
        name="proj",
    )(x, g, wm, wlr, wgt, wgg, bgg)


def _top_blocks(gate, candidate, lane):
    lane_f = lane.astype(F32)
    no_lane = float(gate.shape[-1])
    avail = jnp.where(candidate, 1.0, 0.0)
    sel = jnp.zeros(gate.shape, F32)
    for _ in range(MOBA_TOPK):
        cur = jnp.where(avail > 0.0, gate, -jnp.inf)
        mx = jnp.max(cur, axis=-1, keepdims=True)
        cand = jnp.where(avail > 0.0, jnp.where(cur == mx, lane_f, no_lane), no_lane)
        first = jnp.min(cand, axis=-1, keepdims=True)
        pick = lane_f == first
        sel = jnp.where(pick, 1.0, sel)
        avail = jnp.where(pick, 0.0, avail)
    return sel


def _alibi_slopes(n_head):
    return 2.0 ** (-8.0 * (np.arange(n_head) + 1) / n_head)


_SLOPE_PARTS = 3
_AUG_EXTRA = 2 * _SLOPE_PARTS


def _moba_prompt_kernel(sl_ref, q_ref, kt_ref, vt_ref, kc_ref, km_ref, o_ref, kaug_ref, vaug_ref, *, hd, tk):
    hp = pl.program_id(1)
    qi = pl.program_id(2)
    blk = MOBA_BLOCK
    t = kt_ref.shape[2]
    tq = q_ref.shape[1]
    n_blk = t // blk
    per_pair = LANES // hd
    n_acc = vaug_ref.shape[0]

    @pl.when(qi == 0)
    def _():
        vaug_ref[LANES:, :] = jnp.ones((n_acc - LANES, t), BF16)

        def conv(c, carry):
            keys = pl.ds(pl.multiple_of(c * blk, blk), blk)
            kaug_ref[keys, :LANES] = kt_ref[0, :, keys].T.astype(BF16)
            kaug_ref[keys, LANES:] = kc_ref[keys, :]
            vaug_ref[:LANES, keys] = vt_ref[0, :, keys].astype(BF16)
            return carry
        lax.fori_loop(0, t // blk, conv, 0)

    lane = lax.broadcasted_iota(jnp.int32, (tq, LANES), 1)
    own = qi * (tq // blk) + lax.broadcasted_iota(jnp.int32, (tq, LANES), 0) // blk
    key2 = lax.broadcasted_iota(jnp.int32, (tk, tk), 0)
    qry2 = lax.broadcasted_iota(jnp.int32, (tk, tk), 1)
    q = q_ref[0]
    qk_scale = hd ** -0.5 * LOG2E
    tiles_per_q = tq // tk

    q_aug, tile_bias = [], []
    for hh in range(per_pair):
        h = hp * per_pair + hh
        head_lanes = (lane >= hh * hd) & (lane < (hh + 1) * hd)
        qh = jnp.where(head_lanes, q, 0.0)
        gate = _bdot_nt(qh, km_ref[0])
        sel = _top_blocks(gate, lane < own, lane)
        parts = [sl_ref[1 + i, h] for i in range(_SLOPE_PARTS)]
        qx = jnp.where(lane < n_blk, jnp.where((sel > 0.0) | (lane == own), 0.0, PENALTY), 0.0)
        for off, val in enumerate(parts + [part * blk for part in parts]):
            qx = jnp.where(lane == n_blk + off, val, qx)
        q_aug.append(jnp.concatenate([(qh * qk_scale).astype(BF16), qx.astype(BF16)], axis=1))
        tile_bias.append(sl_ref[0, h] * tq)

    def tile_keys(j):
        return pl.ds(pl.multiple_of(j * tk, tk), tk)

    def absorb(stat, s, va, c):
        m, acc = stat
        m_new = jnp.maximum(m, jnp.max(s, axis=0, keepdims=True) + c)
        p = jnp.exp2(s - (m_new - c)).astype(BF16)
        return m_new, jnp.exp2(m - m_new) * acc + jnp.dot(va, p, preferred_element_type=F32)

    stats = [[(jnp.full((1, tk), NEG_BIG, F32), jnp.zeros((n_acc, tk), F32)) for _ in range(tiles_per_q)]
             for _ in range(per_pair)]
    for d in range(tiles_per_q):
        keys = tile_keys(qi * tiles_per_q + d)
        kt, va = kaug_ref[keys, :], vaug_ref[:, keys]
        live = [(hh, cb) for hh in range(per_pair) for cb in range(d, tiles_per_q)]
        ss = [lax.dot_general(kt, q_aug[hh][cb * tk:(cb + 1) * tk], _NT, preferred_element_type=F32)
              for hh, cb in live]
        for (hh, cb), s in zip(live, ss):
            if cb == d:
                s = jnp.where(key2 <= qry2, s, NEG_BIG)
            stats[hh][cb] = absorb(stats[hh][cb], s, va, 0.0)

    chains = [(hh, cb) for hh in range(per_pair) for cb in range(tiles_per_q)]

    def body(j, flat):
        keys = tile_keys(j)
        kt, va = kaug_ref[keys, :], vaug_ref[:, keys]
        ss = [lax.dot_general(kt, q_aug[hh][cb * tk:(cb + 1) * tk], _NT, preferred_element_type=F32)
              for hh, cb in chains]
        out = []
        for ci, (hh, cb) in enumerate(chains):
            c = -tile_bias[hh] * (qi - j // tiles_per_q).astype(F32)
            out += absorb(flat[2 * ci:2 * ci + 2], ss[ci], va, c)
        return tuple(out)

    flat = lax.fori_loop(0, qi * tiles_per_q, body, tuple(x for hh, cb in chains for x in stats[hh][cb]))
    dim = lax.broadcasted_iota(jnp.int32, (LANES, tk), 0)
    for cb in range(tiles_per_q):
        out = None
        for hh in range(per_pair):
            acc = flat[2 * chains.index((hh, cb)) + 1]
            o_h = acc[:LANES] / acc[LANES:LANES + 1]
            out = o_h if out is None else jnp.where(dim >= hh * hd, o_h, out)
        o_ref[0, cb * tk:(cb + 1) * tk, :] = out.T


def _moba_consts(t, tq):
    n_blk = t // MOBA_BLOCK
    assert n_blk + _AUG_EXTRA <= LANES
    pos = np.arange(t)
    c = np.zeros((t, LANES), np.float32)
    c[pos, pos // MOBA_BLOCK] = 1.0
    for i in range(_SLOPE_PARTS):
        c[:, n_blk + i] = pos % MOBA_BLOCK
        c[:, n_blk + _SLOPE_PARTS + i] = (pos % tq) // MOBA_BLOCK
    return jnp.asarray(c, dtype=BF16)


_BF16_ROWS = 2 * SUBLANES


def _moba_prompt(qa, kt, vt, kmean, *, hd):
    b, t, w = qa.shape
    blk = MOBA_BLOCK
    n_pair = w // LANES
    n_blk = t // blk
    tk = 2 * blk if t % (2 * blk) == 0 else blk
    tq = 2 * tk if t % (2 * tk) == 0 else tk
    sl2 = _alibi_slopes(w // hd) * LOG2E
    parts, rest = [], sl2
    for _ in range(_SLOPE_PARTS):
        parts.append(rest.astype(ml_dtypes.bfloat16).astype(np.float64))
        rest = rest - parts[-1]
    slopes = jnp.asarray(np.stack([sl2] + parts), dtype=F32)
    kc = _moba_consts(t, tq)
    km = jnp.pad(kmean, ((0, 0), (0, LANES - n_blk), (0, 0)))
    return pl.pallas_call(
        functools.partial(_moba_prompt_kernel, hd=hd, tk=tk),
        grid=(b, n_pair, t // tq),
        in_specs=[
            pl.BlockSpec(memory_space=pltpu.SMEM),
            pl.BlockSpec((1, tq, LANES), lambda bi, hp, i: (bi, i, hp)),
            pl.BlockSpec((1, LANES, t), lambda bi, hp, i: (bi, hp, 0)),
            pl.BlockSpec((1, LANES, t), lambda bi, hp, i: (bi, hp, 0)),
            pl.BlockSpec((t, LANES), lambda bi, hp, i: (0, 0)),
            pl.BlockSpec((1, LANES, LANES), lambda bi, hp, i: (bi, 0, hp)),
        ],
        out_specs=pl.BlockSpec((1, tq, LANES), lambda bi, hp, i: (bi, i, hp)),
        out_shape=jax.ShapeDtypeStruct((b, t, w), F32),
        scratch_shapes=[pltpu.VMEM((t, 2 * LANES), BF16), pltpu.VMEM((LANES + _BF16_ROWS, t), BF16)],
        compiler_params=_params("parallel", "parallel", "arbitrary"),
        name="moba_prompt",
    )(slopes, qa, kt, vt, kc, km)


def _moba_sample_kernel(pt_ref, slopes_ref, q_ref, kn_ref, vn_ref, *refs, n_per_step, pages_per_block, hd):
    k_refs, v_refs = refs[:n_per_step], refs[n_per_step:2 * n_per_step]
    o_ref, qbd_ref, s_ref, p_ref, ksum_ref, l_ref, acc_ref = refs[2 * n_per_step:]
    j = pl.program_id(1)
    n_k_steps = pl.num_programs(1) // 2
    ts, w = q_ref.shape[1:]
    n_head = w // hd
    rows = n_head * ts
    page = k_refs[0].shape[3]
    blk = page * pages_per_block
    past = s_ref.shape[1]
    n_blk = past // blk
    scale = hd ** -0.5

    row_head = lax.broadcasted_iota(jnp.int32, (rows, 1), 0) // ts
    row_q = lax.broadcasted_iota(jnp.int32, (rows, 1), 0) % ts
    col_head = lax.broadcasted_iota(jnp.int32, (rows, w), 1) // hd
    slope = jnp.zeros((rows, 1), F32)
    for h in range(n_head):
        slope = jnp.where(row_head == h, slopes_ref[h], slope)

    @pl.when(j == 0)
    def _():
        qt = jnp.concatenate([q_ref[0]] * n_head, axis=0)
        qbd_ref[...] = jnp.where(col_head == row_head, qt, 0.0)
        ksum_ref[...] = jnp.zeros(ksum_ref.shape, F32)

    @pl.when(j < n_k_steps)
    def _():
        qs = (qbd_ref[...] * scale).astype(BF16)
        lane = lax.broadcasted_iota(jnp.int32, ksum_ref.shape, 1)
        ksum = ksum_ref[...]
        for g in range(n_per_step // pages_per_block):
            total = None
            for p_i in range(g * pages_per_block, (g + 1) * pages_per_block):
                page_idx = j * n_per_step + p_i
                kt = k_refs[p_i][0].reshape(w, page)
                total = kt if total is None else total + kt
                s = jnp.dot(qs, kt.astype(BF16), preferred_element_type=F32)
                s_ref[:, pl.ds(pl.multiple_of(page_idx * page, page), page)] = s
            blk_idx = (j * n_per_step) // pages_per_block + g
            ksum = jnp.where(lane == blk_idx, jnp.sum(total, axis=1, keepdims=True), ksum)
        ksum_ref[...] = ksum

    @pl.when(j == n_k_steps)
    def _():
        qbd = qbd_ref[...]
        gate = _bdot(qbd, ksum_ref[...])
        lane = lax.broadcasted_iota(jnp.int32, gate.shape, 1)
        sel = _top_blocks(gate, lane < n_blk, lane)
        s_own = _bdot_nt(qbd * scale, kn_ref[0])
        kq = lax.broadcasted_iota(jnp.int32, s_own.shape, 1)
        s_own = jnp.where(kq <= row_q, s_own + slope * kq.astype(F32), NEG_BIG)
        key_in_blk = lax.broadcasted_iota(jnp.int32, (1, blk), 1)

        def add_bias(b_i, m_wide):
            cols = pl.ds(pl.multiple_of(b_i * blk, blk), blk)
            picked = jnp.max(jnp.where(lane == b_i, sel, 0.0), axis=-1, keepdims=True)
            rel = (b_i * blk - past + key_in_blk).astype(F32)
            s = s_ref[:, cols] + slope * rel + jnp.where(picked > 0.0, 0.0, NEG_BIG)
            s_ref[:, cols] = s
            return jnp.maximum(m_wide, s)

        unroll = 8 if n_blk % 8 == 0 else 1
        m_wide = lax.fori_loop(0, n_blk, add_bias, jnp.full((rows, blk), NEG_BIG, F32), unroll=unroll)
        m = jnp.maximum(jnp.max(m_wide, axis=-1, keepdims=True), jnp.max(s_own, axis=-1, keepdims=True))

        def probs(b_i, l_wide):
            cols = pl.ds(pl.multiple_of(b_i * blk, blk), blk)
            p = jnp.exp(s_ref[:, cols] - m)
            p_ref[:, cols] = p.astype(BF16)
            return l_wide + p

        l_wide = lax.fori_loop(0, n_blk, probs, jnp.zeros((rows, blk), F32), unroll=unroll)
        p_own = jnp.exp(s_own - m)
        l_ref[...] = jnp.sum(l_wide, axis=-1, keepdims=True) + jnp.sum(p_own, axis=-1, keepdims=True)
        acc_ref[...] = _bdot(p_own, vn_ref[0])

    @pl.when(j >= n_k_steps)
    def _():
        acc = acc_ref[...]
        for p_i in range(n_per_step):
            page_idx = (j - n_k_steps) * n_per_step + p_i
            p = p_ref[:, pl.ds(pl.multiple_of(page_idx * page, page), page)]
            vt = v_refs[p_i][0].reshape(w, page).astype(BF16)
            acc = acc + lax.dot_general(p, vt, _NT, preferred_element_type=F32)
        acc_ref[...] = acc

    @pl.when(j == pl.num_programs(1) - 1)
    def _():
        res = jnp.where(col_head == row_head, acc_ref[...] / l_ref[...], 0.0)
        out = res[0:ts]
        for h in range(1, n_head):
            out = out + res[h * ts:(h + 1) * ts]
        o_ref[0] = out


def _moba_sample(qa, ka, va, cache_kt, cache_vt, page_table, *, hd, n_per_step):
    b, ts, w = qa.shape
    _, n_head, _, page = cache_kt.shape
    n_pages = page_table.shape[1]
    ppb = MOBA_BLOCK // page
    assert n_pages % n_per_step == 0 and n_per_step % ppb == 0 and n_pages // ppb <= LANES
    n_k_steps = n_pages // n_per_step
    rows = n_head * ts
    per_b = lambda a: pl.BlockSpec((1,) + a.shape[1:], lambda bi, j, pt: (bi, 0, 0))

    def page_specs(step_of):
        def spec(p):
            return pl.BlockSpec((1, n_head, hd, page),
                                lambda bi, j, pt: (pt[bi, step_of(j) * n_per_step + p], 0, 0, 0))
        return [spec(p) for p in range(n_per_step)]

    return pl.pallas_call(
        functools.partial(_moba_sample_kernel, n_per_step=n_per_step, pages_per_block=ppb, hd=hd),
        grid_spec=pltpu.PrefetchScalarGridSpec(
            num_scalar_prefetch=1,
            grid=(b, 2 * n_k_steps),
            in_specs=[pl.BlockSpec(memory_space=pltpu.SMEM), per_b(qa), per_b(ka), per_b(va)]
                     + page_specs(lambda j: jnp.minimum(j, n_k_steps - 1))
                     + page_specs(lambda j: jnp.maximum(j - n_k_steps, 0)),
            out_specs=per_b(qa),
            scratch_shapes=[pltpu.VMEM((rows, w), F32),
                            pltpu.VMEM((rows, n_pages * page), F32),
                            pltpu.VMEM((rows, n_pages * page), BF16),
                            pltpu.VMEM((w, LANES), F32),
                            pltpu.VMEM((rows, 1), F32), pltpu.VMEM((rows, w), F32)],
        ),
        out_shape=jax.ShapeDtypeStruct((b, ts, w), F32),
        compiler_params=_params("parallel", "arbitrary"),
        name="moba_sample",
    )(page_table, jnp.asarray(_alibi_slopes(n_head), dtype=F32), qa, ka, va,
      *([cache_kt] * n_per_step), *([cache_vt] * n_per_step))


def _gla_kernel(q_ref, k_ref, v_ref, la_ref, rg_ref, gn_ref, s0_ref, o_ref, sout_ref, st_ref, *, dk, dv):
    c_idx = pl.program_id(1)
    n_chunks = pl.num_programs(1)
    c = q_ref.shape[1]
    n_pair = q_ref.shape[2] // LANES
    per_pair = LANES // dk
    assert dv == LANES and c % SUBLANES == 0

    @pl.when(c_idx == 0)
    def _():
        for p in range(n_pair):
            st_ref[p] = s0_ref[0, p].T

    row = lax.broadcasted_iota(jnp.int32, (c, LANES), 0)
    lane = lax.broadcasted_iota(jnp.int32, (c, LANES), 1)
    row2 = lax.broadcasted_iota(jnp.int32, (c, c), 0)
    col2 = lax.broadcasted_iota(jnp.int32, (c, c), 1)
    tri = jnp.where(row2 >= col2, 1.0, 0.0)
    b_all = jnp.dot(tri, la_ref[0], precision=lax.Precision.HIGHEST, preferred_element_type=F32)
    nb8 = c // SUBLANES
    sub = lax.broadcasted_iota(jnp.int32, (nb8, SUBLANES, LANES), 1)
    lane3 = lax.broadcasted_iota(jnp.int32, (nb8, SUBLANES, LANES), 2)
    lane_st = lax.broadcasted_iota(jnp.int32, (dv, LANES), 1)

    def bcast_row(x3, jj):
        return jnp.broadcast_to(x3[:, jj:jj + 1, :], x3.shape)

    for p in range(n_pair):
        cols = slice(p * LANES, (p + 1) * LANES)
        q = q_ref[0, :, cols] * (dk ** -0.5)
        k = k_ref[0, :, cols]
        b = b_all[:, cols]
        vs = [v_ref[0, :, (p * per_pair + hh) * dv:(p * per_pair + hh + 1) * dv] for hh in range(per_pair)]
        head_lanes = [(lane >= hh * dk) & (lane < (hh + 1) * dk) for hh in range(per_pair)]
        head_lanes3 = [(lane3 >= hh * dk) & (lane3 < (hh + 1) * dk) for hh in range(per_pair)]
        st = st_ref[p]

        qe = q * jnp.exp(b)
        o = [_bdot_nt(jnp.where(head_lanes[hh], qe, 0.0), st) for hh in range(per_pair)]

        q3 = q.reshape(nb8, SUBLANES, LANES)
        k3 = k.reshape(nb8, SUBLANES, LANES)
        b3 = b.reshape(nb8, SUBLANES, LANES)
        v3 = [v.reshape(nb8, SUBLANES, dv) for v in vs]
        o3 = [jnp.zeros((nb8, SUBLANES, dv), F32) for _ in range(per_pair)]
        for jj in range(SUBLANES):
            e = jnp.exp(jnp.minimum(b3 - bcast_row(b3, jj), 0.0))
            term = jnp.where(sub >= jj, q3 * bcast_row(k3, jj) * e, 0.0)
            for hh in range(per_pair):
                a = jnp.sum(jnp.where(head_lanes3[hh], term, 0.0), axis=-1, keepdims=True)
                o3[hh] = o3[hh] + a * bcast_row(v3[hh], jj)
        o = [o[hh] + o3[hh].reshape(c, dv) for hh in range(per_pair)]

        attn = [jnp.zeros((c, c), F32) for _ in range(per_pair)]
        m_half = SUBLANES
        while 2 * m_half <= c:
            span = 2 * m_half
            bnd = jnp.broadcast_to(b.reshape(c // span, span, LANES)[:, m_half - 1:m_half, :],
                                   (c // span, span, LANES)).reshape(c, LANES)
            upper = (row % span) >= m_half
            qm = jnp.where(upper, q * jnp.exp(jnp.minimum(b - bnd, 0.0)), 0.0)
            km = jnp.where(upper, 0.0, k * jnp.exp(jnp.minimum(bnd - b, 0.0)))
            same = (row2 // span) == (col2 // span)
            for hh in range(per_pair):
                a = _bdot_nt(jnp.where(head_lanes[hh], qm, 0.0), km)
                attn[hh] = attn[hh] + jnp.where(same, a, 0.0)
            m_half = span
        if c > SUBLANES:
            o = [o[hh] + _bdot(attn[hh], vs[hh]) for hh in range(per_pair)]

        b_last = b[c - 1:c, :]
        kk = (k * jnp.exp(b_last - b)).astype(BF16)
        upd = lax.dot_general(vs[0].astype(BF16), kk, _TN, preferred_element_type=F32)
        for hh in range(1, per_pair):
            u = lax.dot_general(vs[hh].astype(BF16), kk, _TN, preferred_element_type=F32)
            upd = jnp.where(lane_st >= hh * dk, u, upd)
        st_new = st * jnp.exp(b_last) + upd
        st_ref[p] = st_new

        for hh in range(per_pair):
            h = p * per_pair + hh
            rg = rg_ref[0, :, h * dv:(h + 1) * dv]
            o_ref[0, :, h * dv:(h + 1) * dv] = _rms(o[hh], gn_ref[...]) * (rg * _sigmoid(rg))

        @pl.when(c_idx == n_chunks - 1)
        def _(p=p, st_new=st_new):
            sout_ref[0, p] = st_new.T


def _gla(qg, kg, vg, la, rg, g_norm, s0, *, chunk, dk, dv):
    b, t, wk = qg.shape
    wv = vg.shape[2]
    n_head = wk // dk
    n_pair = wk // LANES
    s0p = s0.reshape(b, n_pair, LANES, dv)
    tok = lambda w: pl.BlockSpec((1, chunk, w), lambda bi, ci: (bi, ci, 0))
    st_spec = pl.BlockSpec((1, n_pair, LANES, dv), lambda bi, ci: (bi, 0, 0, 0))
    og, s_new = pl.pallas_call(
        functools.partial(_gla_kernel, dk=dk, dv=dv),
        grid=(b, t // chunk),
        in_specs=[tok(wk), tok(wk), tok(wv), tok(wk), tok(wv),
                  pl.BlockSpec(g_norm.shape, lambda bi, ci: (0, 0)), st_spec],
        out_specs=[tok(wv), st_spec],
        out_shape=[jax.ShapeDtypeStruct((b, t, wv), F32), jax.ShapeDtypeStruct((b, n_pair, LANES, dv), F32)],
        scratch_shapes=[pltpu.VMEM((n_pair, dv, LANES), F32)],
        compiler_params=_params("parallel", "arbitrary"),
        name="gla",
    )(qg, kg, vg, la, rg, g_norm, s0p)
    return og, s_new.reshape(b, n_head, dk, dv)


def _merge_kernel(x_ref, oa_ref, og_ref, sga_ref, sgb_ref, wa_ref, wb_ref, wo_ref, g_ref, y_ref):
    merged = (sga_ref[...].astype(F32) * _bdot(oa_ref[...], wa_ref[...])
              + sgb_ref[...].astype(F32) * _bdot(og_ref[...], wb_ref[...]))
    y_ref[...] = x_ref[...] + _rms(_bdot(merged, wo_ref[...]), g_ref[...])


def _mlp_kernel(x_ref, wu_ref, wd_ref, g1_ref, g2_ref, y_ref):
    x = x_ref[...]
    u = _bdot(_rms(x, g1_ref[...]), wu_ref[...])
    u = jnp.square(jnp.maximum(u, 0.0))
    y_ref[...] = x + _rms(_bdot(u, wd_ref[...]), g2_ref[...])


def _rowwise_call(kernel, name, row_inputs, const_inputs, out_width, tm):
    n = row_inputs[0].shape[0]
    assert n % tm == 0
    row = lambda a: pl.BlockSpec((tm, a.shape[1]), lambda i: (i, 0))
    full = lambda a: pl.BlockSpec(a.shape, lambda i: (0,) * a.ndim)
    return pl.pallas_call(
        kernel,
        grid=(n // tm,),
        in_specs=[row(a) for a in row_inputs] + [full(a) for a in const_inputs],
        out_specs=pl.BlockSpec((tm, out_width), lambda i: (i, 0)),
        out_shape=jax.ShapeDtypeStruct((n, out_width), F32),
        compiler_params=_params("parallel"),
        name=name,
    )(*row_inputs, *const_inputs)


def _gla_chunk(t):
    c = SUBLANES
    while c * 2 <= min(t, LANES) and t % (c * 2) == 0:
        c *= 2
    return c


def kernel(x_prompt, x_sample, cache_k, cache_v, page_table, state_gla, w_in, w_gla_gate, b_gla_gate, g_gla_norm,
           w_branch_a, w_branch_b, w_out, w_up, w_down, g_pre_mix, g_post_mix, g_pre_mlp, g_post_mlp):
    bp, tp, d_model = x_prompt.shape
    bs, ts, _ = x_sample.shape
    depth, n_phys, page, n_head, hd = cache_k.shape
    _, _, n_head_g, dk, dv = state_gla.shape
    w_a = n_head * hd
    qk_g = n_head_g * dk
    v_g = n_head_g * dv
    rank = w_gla_gate.shape[1]
    n_main = 3 * w_a + 2 * qk_g + 2 * v_g

    hp = x_prompt.reshape(bp * tp, d_model)
    hs = x_sample.reshape(bs * ts, d_model)
    outs = [[] for _ in range(6)]
    for l in range(depth):
        wm = w_in[l, :, :n_main].astype(BF16)
        wlr = jnp.pad(w_in[l, :, n_main:n_main + rank], ((0, 0), (0, LANES - rank))).astype(BF16)
        wgt = w_in[l, :, n_main + rank:].astype(BF16)
        wgg = jnp.pad(w_gla_gate[l], ((0, LANES - rank), (0, 0))).astype(BF16)
        bgg = b_gla_gate[l][None, :]
        wa, wb, wo = (w[l].astype(BF16) for w in (w_branch_a, w_branch_b, w_out))
        wu, wd = w_up[l].astype(BF16), w_down[l].astype(BF16)
        g_mix, g_pm, g_mlp, g_pl, g_gn = (g[l][None, :] for g in
                                          (g_pre_mix, g_post_mix, g_pre_mlp, g_post_mlp, g_gla_norm))

        def layer(x, b, t, moba, s0, kv_transposed):
            qa, ka, va, qg, kg, vg, rg, la, sga, sgb, kmean = _proj(
                x, g_mix, wm, wlr, wgt, wgg, bgg, w_a=w_a, qk_g=qk_g, v_g=v_g,
                kv_seq_len=t if kv_transposed else None)
            r3 = lambda a: a.reshape(b, t, a.shape[-1])
            if kv_transposed:
                oa = moba(r3(qa), ka, va, kmean)
                as_out = lambda a: jnp.transpose(a.reshape(b, n_head, hd, t), (0, 3, 1, 2))
            else:
                oa = moba(r3(qa), r3(ka), r3(va), kmean)
                as_out = lambda a: a.reshape(b, t, n_head, hd)
            og, s_new = _gla(r3(qg), r3(kg), r3(vg), r3(la), r3(rg), g_gn, s0, chunk=_gla_chunk(t), dk=dk, dv=dv)
            n = b * t
            x = _rowwise_call(_merge_kernel, "merge", [x, oa.reshape(n, w_a), og.reshape(n, v_g), sga, sgb],
                              [wa, wb, wo, g_pm], d_model, 2 * MOBA_BLOCK if n % (2 * MOBA_BLOCK) == 0 else MOBA_BLOCK)
            x = _rowwise_call(_mlp_kernel, "mlp", [x], [wu, wd, g_mlp, g_pl], d_model, MOBA_BLOCK)
            return x, as_out(ka), as_out(va), s_new

        def moba_prompt(qa, kt, vt, kmean):
            return _moba_prompt(qa, kt, vt, kmean.reshape(bp, tp // MOBA_BLOCK, w_a), hd=hd)

        def moba_sample(qa, ka, va, kmean_unused):
            n_pages = page_table.shape[1]
            n_per_step = 32 if n_pages % 32 == 0 else MOBA_BLOCK // page
            to_stored = lambda c: jnp.transpose(c[l], (0, 2, 3, 1))
            return _moba_sample(qa, ka, va, to_stored(cache_k), to_stored(cache_v), page_table,
                                hd=hd, n_per_step=n_per_step)

        hp, kp, vp, sp = layer(hp, bp, tp, moba_prompt, jnp.zeros((bp, n_head_g, dk, dv), state_gla.dtype), True)
        hs, ksn, vsn, ssn = layer(hs, bs, ts, moba_sample, state_gla[l], False)
        for lst, val in zip(outs, (kp, vp, sp, ksn, vsn, ssn)):
            lst.append(val)
    return (hp.reshape(bp, tp, d_model), hs.reshape(bs, ts, d_model)) + tuple(jnp.stack(o) for o in outs)
```

```python
import functools

import ml_dtypes
import numpy as np
import jax
import jax.numpy as jnp
from jax import lax
from jax.experimental import pallas as pl
from jax.experimental.pallas import tpu as pltpu

F32 = jnp.float32
BF16 = jnp.bfloat16

LANES = 128
SUBLANES = 8
VMEM_LIMIT_BYTES = 56 * 1024 * 1024

EPS = 1e-6
MOBA_BLOCK = 256
MOBA_TOPK = 3
GLA_TAU = 16.0
NEG_BIG = -1e30
PENALTY = -30000.0
LOG2E = float(np.log2(np.e))

_NT = (((1,), (1,)), ((), ()))
_TN = (((0,), (0,)), ((), ()))


def _params(*sem):
    return pltpu.CompilerParams(dimension_semantics=sem, vmem_limit_bytes=VMEM_LIMIT_BYTES)


def _sigmoid(x):
    return 1.0 / (1.0 + jnp.exp(-x))


def _rms(x, g):
    return x * lax.rsqrt(jnp.mean(x * x, axis=-1, keepdims=True) + EPS) * g


def _bdot(a, b):
    return jnp.dot(a.astype(BF16), b.astype(BF16), preferred_element_type=F32)


def _bdot_nt(a, b):
    return lax.dot_general(a.astype(BF16), b.astype(BF16), _NT, preferred_element_type=F32)


def _proj_kernel(x_ref, g_ref, wm_ref, wlr_ref, wgt_ref, wgg_ref, bgg_ref,
                 qa_ref, ka_ref, va_ref, qg_ref, kg_ref, vg_ref, rg_ref, la_ref, sga_ref, sgb_ref, kmean_ref,
                 *attn_refs, w_a, qk_g, v_g, d_model, kv_transposed):
    hb = _rms(x_ref[...], g_ref[...]).astype(BF16)

    def proj(lo, n):
        return jnp.dot(hb, wm_ref[:, lo:lo + n], preferred_element_type=F32)

    qa_ref[...] = proj(0, w_a)
    ka = proj(w_a, w_a)
    va = proj(2 * w_a, w_a)
    kmean_ref[0] = jnp.sum(ka, axis=0, keepdims=True) * (1.0 / ka.shape[0])
    if kv_transposed:
        k_bf_ref, vt_bf_ref = attn_refs
        vt = va.T
        ka_ref[0] = ka.T
        va_ref[0] = vt
        k_bf_ref[...] = ka.astype(BF16)
        vt_bf_ref[0] = vt.astype(BF16)
    else:
        ka_ref[...] = ka
        va_ref[...] = va
    lo = 3 * w_a
    qg_ref[...] = proj(lo, qk_g)
    kg_ref[...] = proj(lo + qk_g, qk_g)
    vg_ref[...] = proj(lo + 2 * qk_g, v_g)
    rg_ref[...] = proj(lo + 2 * qk_g + v_g, v_g)
    lr = jnp.dot(hb, wlr_ref[...], preferred_element_type=F32)
    xg = jnp.dot(lr.astype(BF16), wgg_ref[...], preferred_element_type=F32) + bgg_ref[...]
    log_sig = jnp.minimum(xg, 0.0) - jnp.log1p(jnp.exp(-jnp.abs(xg)))
    la_ref[...] = log_sig * (1.0 / GLA_TAU)
    ga = jnp.dot(hb, wgt_ref[:, :d_model], preferred_element_type=F32)
    sga_ref[...] = _sigmoid(ga).astype(BF16)
    gb = jnp.dot(hb, wgt_ref[:, d_model:], preferred_element_type=F32)
    sgb_ref[...] = _sigmoid(gb).astype(BF16)


def _proj(x, g, wm, wlr, wgt, wgg, bgg, *, w_a, qk_g, v_g, kv_seq_len=None):
    n, d_model = x.shape
    tm = MOBA_BLOCK
    assert n % tm == 0
    nt = n // tm
    row = lambda w: pl.BlockSpec((tm, w), lambda i: (i, 0))
    full = lambda a: pl.BlockSpec(a.shape, lambda i: (0,) * a.ndim)
    widths = (w_a, w_a, w_a, qk_g, qk_g, v_g, v_g, qk_g)
    out_shape = [jax.ShapeDtypeStruct((n, w), F32) for w in widths]
    out_shape += [jax.ShapeDtypeStruct((n, d_model), BF16)] * 2
    out_shape += [jax.ShapeDtypeStruct((nt, 1, w_a), F32)]
    out_specs = [row(w) for w in widths] + [row(d_model)] * 2
    out_specs += [pl.BlockSpec((1, 1, w_a), lambda i: (i, 0, 0))]
    if kv_seq_len is not None:
        assert kv_seq_len % tm == 0
        seq_tiles = kv_seq_len // tm
        transposed = pl.BlockSpec((1, w_a, tm), lambda i: (i // seq_tiles, 0, i % seq_tiles))
        for i in (1, 2):
            out_shape[i] = jax.ShapeDtypeStruct((n // kv_seq_len, w_a, kv_seq_len), F32)
            out_specs[i] = transposed
        out_shape += [jax.ShapeDtypeStruct((n, w_a), BF16),
                      jax.ShapeDtypeStruct((n // kv_seq_len, w_a, kv_seq_len), BF16)]
        out_specs += [row(w_a), transposed]
    return pl.pallas_call(
        functools.partial(_proj_kernel, w_a=w_a, qk_g=qk_g, v_g=v_g, d_model=d_model,
                          kv_transposed=kv_seq_len is not None),
        grid=(nt,),
        in_specs=[row(d_model), full(g), full(wm), full(wlr), full(wgt), full(wgg), full(bgg)],
        out_specs=out_specs,
        out_shape=out_shape,
        compiler_params=_params("parallel"),
        name="proj",
    )(x, g, wm, wlr, wgt, wgg, bgg)


def _top_blocks(gate, candidate, lane):
    lane_f = lane.astype(F32)
    no_lane = float(gate.shape[-1])
    avail = jnp.where(candidate, 1.0, 0.0)
    sel = jnp.zeros(gate.shape, F32)
    for _ in range(MOBA_TOPK):
        cur = jnp.where(avail > 0.0, gate, -jnp.inf)
        mx = jnp.max(cur, axis=-1, keepdims=True)
        cand = jnp.where(avail > 0.0, jnp.where(cur == mx, lane_f, no_lane), no_lane)
        first = jnp.min(cand, axis=-1, keepdims=True)
        pick = lane_f == first
        sel = jnp.where(pick, 1.0, sel)
        avail = jnp.where(pick, 0.0, avail)
    return sel


def _alibi_slopes(n_head):
    return 2.0 ** (-8.0 * (np.arange(n_head) + 1) / n_head)


_SLOPE_PARTS = 3
_AUG_EXTRA = 2 * _SLOPE_PARTS


def _moba_prompt_step(r, n_sub, hp, qi, sl_ref, q_ref, k_ref, vt_ref, kc_ref, km_ref, o_ref,
                      qaug_ref, m_ref, acc_ref, *, hd, tk):
    blk = MOBA_BLOCK
    t = k_ref.shape[1]
    tq = q_ref.shape[1]
    n_blk = t // blk
    per_pair = LANES // hd
    n_acc = acc_ref.shape[1]
    tiles_per_q = tq // tk
    chains = [(hh, cb) for hh in range(per_pair) for cb in range(tiles_per_q)]
    ones = jnp.ones((n_acc - LANES, tk), BF16)

    def tiles(j):
        keys = pl.ds(pl.multiple_of(j * tk, tk), tk)
        kt = jnp.concatenate([k_ref[0, keys, :], kc_ref[keys, :]], axis=1)
        return kt, jnp.concatenate([vt_ref[0, :, keys], ones], axis=0)

    def logits(kt, hh, cb):
        return lax.dot_general(kt, qaug_ref[hh, cb * tk:(cb + 1) * tk, :], _NT, preferred_element_type=F32)

    def absorb(stat, s, va, c):
        m, acc = stat
        m_new = jnp.maximum(m, jnp.max(s, axis=0, keepdims=True) + c)
        p = jnp.exp2(s - (m_new - c)).astype(BF16)
        return m_new, jnp.exp2(m - m_new) * acc + jnp.dot(va, p, preferred_element_type=F32)

    @pl.when(r == 0)
    def _():
        lane = lax.broadcasted_iota(jnp.int32, (tq, LANES), 1)
        own = qi * (tq // blk) + lax.broadcasted_iota(jnp.int32, (tq, LANES), 0) // blk
        q = q_ref[0]
        for hh in range(per_pair):
            h = hp * per_pair + hh
            head_lanes = (lane >= hh * hd) & (lane < (hh + 1) * hd)
            qh = jnp.where(head_lanes, q, 0.0)
            gate = _bdot_nt(qh, km_ref[0])
            sel = _top_blocks(gate, lane < own, lane)
            parts = [sl_ref[1 + i, h] for i in range(_SLOPE_PARTS)]
            qx = jnp.where(lane < n_blk, jnp.where((sel > 0.0) | (lane == own), 0.0, PENALTY), 0.0)
            for off, val in enumerate(parts + [part * blk for part in parts]):
                qx = jnp.where(lane == n_blk + off, val, qx)
            qaug_ref[hh, :, :LANES] = (qh * (hd ** -0.5 * LOG2E)).astype(BF16)
            qaug_ref[hh, :, LANES:] = qx.astype(BF16)

        key2 = lax.broadcasted_iota(jnp.int32, (tk, tk), 0)
        qry2 = lax.broadcasted_iota(jnp.int32, (tk, tk), 1)
        stats = {c: (jnp.full((1, tk), NEG_BIG, F32), jnp.zeros((n_acc, tk), F32)) for c in chains}
        for d in range(tiles_per_q):
            kt, va = tiles(qi * tiles_per_q + d)
            live = [(hh, cb) for hh, cb in chains if cb >= d]
            ss = [logits(kt, hh, cb) for hh, cb in live]
            for (hh, cb), s in zip(live, ss):
                if cb == d:
                    s = jnp.where(key2 <= qry2, s, NEG_BIG)
                stats[hh, cb] = absorb(stats[hh, cb], s, va, 0.0)
        for ci, c in enumerate(chains):
            m_ref[ci], acc_ref[ci] = stats[c]

    def body(i, flat):
        j = r + i * n_sub
        kt, va = tiles(j)
        ss = [logits(kt, hh, cb) for hh, cb in chains]
        out = []
        for ci, (hh, cb) in enumerate(chains):
            c = -(sl_ref[0, hp * per_pair + hh] * tq) * (qi - j // tiles_per_q).astype(F32)
            out += absorb(flat[2 * ci:2 * ci + 2], ss[ci], va, c)
        return tuple(out)

    n_mine = (qi * tiles_per_q - r + n_sub - 1) // n_sub
    flat = lax.fori_loop(0, n_mine, body, tuple(x for ci in range(len(chains)) for x in (m_ref[ci], acc_ref[ci])))
    for ci in range(len(chains)):
        m_ref[ci], acc_ref[ci] = flat[2 * ci], flat[2 * ci + 1]

    @pl.when(r == n_sub - 1)
    def _():
        dim = lax.broadcasted_iota(jnp.int32, (LANES, tk), 0)
        for cb in range(tiles_per_q):
            out = None
            for hh in range(per_pair):
                acc = flat[2 * chains.index((hh, cb)) + 1]
                o_h = acc[:LANES] / acc[LANES:LANES + 1]
                out = o_h if out is None else jnp.where(dim >= hh * hd, o_h, out)
            o_ref[0, cb * tk:(cb + 1) * tk, :] = out.T


def _moba_consts(t, tq):
    n_blk = t // MOBA_BLOCK
    assert n_blk + _AUG_EXTRA <= LANES
    pos = np.arange(t)
    c = np.zeros((t, LANES), np.float32)
    c[pos, pos // MOBA_BLOCK] = 1.0
    for i in range(_SLOPE_PARTS):
        c[:, n_blk + i] = pos % MOBA_BLOCK
        c[:, n_blk + _SLOPE_PARTS + i] = (pos % tq) // MOBA_BLOCK
    return jnp.asarray(c, dtype=BF16)


_BF16_ROWS = 2 * SUBLANES


def _moba_sample_step(j, n_k_steps, slopes_ref, q_ref, kn_ref, vn_ref, k_refs, v_refs,
                      o_ref, qbd_ref, s_ref, p_ref, ksum_ref, l_ref, acc_ref, *, pages_per_block, hd):
    n_per_step = len(k_refs)
    ts, w = q_ref.shape[1:]
    n_head = w // hd
    rows = n_head * ts
    page = k_refs[0].shape[3]
    blk = page * pages_per_block
    past = s_ref.shape[1]
    n_blk = past // blk
    scale = hd ** -0.5

    row_head = lax.broadcasted_iota(jnp.int32, (rows, 1), 0) // ts
    row_q = lax.broadcasted_iota(jnp.int32, (rows, 1), 0) % ts
    col_head = lax.broadcasted_iota(jnp.int32, (rows, w), 1) // hd
    slope = jnp.zeros((rows, 1), F32)
    for h in range(n_head):
        slope = jnp.where(row_head == h, slopes_ref[h], slope)

    @pl.when(j == 0)
    def _():
        qt = jnp.concatenate([q_ref[0]] * n_head, axis=0)
        qbd_ref[...] = jnp.where(col_head == row_head, qt, 0.0)
        ksum_ref[...] = jnp.zeros(ksum_ref.shape, F32)

    @pl.when(j < n_k_steps)
    def _():
        qs = (qbd_ref[...] * scale).astype(BF16)
        lane = lax.broadcasted_iota(jnp.int32, ksum_ref.shape, 1)
        ksum = ksum_ref[...]
        for g in range(n_per_step // pages_per_block):
            total = None
            for p_i in range(g * pages_per_block, (g + 1) * pages_per_block):
                page_idx = j * n_per_step + p_i
                kt = k_refs[p_i][0].reshape(w, page)
                total = kt if total is None else total + kt
                s = jnp.dot(qs, kt.astype(BF16), preferred_element_type=F32)
                s_ref[:, pl.ds(pl.multiple_of(page_idx * page, page), page)] = s
            blk_idx = (j * n_per_step) // pages_per_block + g
            ksum = jnp.where(lane == blk_idx, jnp.sum(total, axis=1, keepdims=True), ksum)
        ksum_ref[...] = ksum

    @pl.when(j == n_k_steps)
    def _():
        qbd = qbd_ref[...]
        gate = _bdot(qbd, ksum_ref[...])
        lane = lax.broadcasted_iota(jnp.int32, gate.shape, 1)
        sel = _top_blocks(gate, lane < n_blk, lane)
        s_own = _bdot_nt(qbd * scale, kn_ref[0])
        kq = lax.broadcasted_iota(jnp.int32, s_own.shape, 1)
        s_own = jnp.where(kq <= row_q, s_own + slope * kq.astype(F32), NEG_BIG)
        key_in_blk = lax.broadcasted_iota(jnp.int32, (1, blk), 1)

        def add_bias(b_i, m_wide):
            cols = pl.ds(pl.multiple_of(b_i * blk, blk), blk)
            picked = jnp.max(jnp.where(lane == b_i, sel, 0.0), axis=-1, keepdims=True)
            rel = (b_i * blk - past + key_in_blk).astype(F32)
            s = s_ref[:, cols] + slope * rel + jnp.where(picked > 0.0, 0.0, NEG_BIG)
            s_ref[:, cols] = s
            return jnp.maximum(m_wide, s)

        unroll = 8 if n_blk % 8 == 0 else 1
        m_wide = lax.fori_loop(0, n_blk, add_bias, jnp.full((rows, blk), NEG_BIG, F32), unroll=unroll)
        m = jnp.maximum(jnp.max(m_wide, axis=-1, keepdims=True), jnp.max(s_own, axis=-1, keepdims=True))

        def probs(b_i, l_wide):
            cols = pl.ds(pl.multiple_of(b_i * blk, blk), blk)
            p = jnp.exp(s_ref[:, cols] - m)
            p_ref[:, cols] = p.astype(BF16)
            return l_wide + p

        l_wide = lax.fori_loop(0, n_blk, probs, jnp.zeros((rows, blk), F32), unroll=unroll)
        p_own = jnp.exp(s_own - m)
        l_ref[...] = jnp.sum(l_wide, axis=-1, keepdims=True) + jnp.sum(p_own, axis=-1, keepdims=True)
        acc_ref[...] = _bdot(p_own, vn_ref[0])

    @pl.when(j >= n_k_steps)
    def _():
        acc = acc_ref[...]
        for p_i in range(n_per_step):
            page_idx = (j - n_k_steps) * n_per_step + p_i
            p = p_ref[:, pl.ds(pl.multiple_of(page_idx * page, page), page)]
            vt = v_refs[p_i][0].reshape(w, page).astype(BF16)
            acc = acc + lax.dot_general(p, vt, _NT, preferred_element_type=F32)
        acc_ref[...] = acc

    @pl.when(j == 2 * n_k_steps - 1)
    def _():
        res = jnp.where(col_head == row_head, acc_ref[...] / l_ref[...], 0.0)
        out = res[0:ts]
        for h in range(1, n_head):
            out = out + res[h * ts:(h + 1) * ts]
        o_ref[0] = out


def _moba_kernel(pt_ref, sl_ref, slopes_ref, q_ref, k_ref, vt_ref, kc_ref, km_ref, qs_ref, kn_ref, vn_ref,
                 *refs, n_per_step, pages_per_block, hd, tk, n_k_steps):
    k_refs, v_refs = refs[:n_per_step], refs[n_per_step:2 * n_per_step]
    o_ref, os_ref, qaug_ref, m_ref, acc_ref = refs[2 * n_per_step:2 * n_per_step + 5]
    sample_scratch = refs[2 * n_per_step + 5:]
    hp, qi, r = pl.program_id(1), pl.program_id(2), pl.program_id(3)
    n_sub = pl.num_programs(3)
    _moba_prompt_step(r, n_sub, hp, qi, sl_ref, q_ref, k_ref, vt_ref, kc_ref, km_ref, o_ref,
                      qaug_ref, m_ref, acc_ref, hd=hd, tk=tk)
    step = ((pl.program_id(0) * pl.num_programs(1) + hp) * pl.num_programs(2) + qi) * n_sub + r
    _moba_sample_step(step % (2 * n_k_steps), n_k_steps, slopes_ref, qs_ref, kn_ref, vn_ref, k_refs, v_refs,
                      os_ref, *sample_scratch, pages_per_block=pages_per_block, hd=hd)


def _moba(qa, k_bf, vt_bf, kmean, qa_s, ka_s, va_s, cache_kt, cache_vt, page_table, *, hd):
    b, t, w = qa.shape
    bs, ts, _ = qa_s.shape
    _, n_head, _, page = cache_kt.shape
    n_pages = page_table.shape[1]
    blk = MOBA_BLOCK
    ppb = blk // page
    n_pair = w // LANES
    n_blk = t // blk
    tk = 2 * blk if t % (2 * blk) == 0 else blk
    tq = 2 * tk if t % (2 * tk) == 0 else tk
    n_q = t // tq
    n_per_step = next(p for p in (32, 16, 8, 4, 2) if n_pages % p == 0 and p % ppb == 0
                      and (bs * 2 * (n_pages // p)) % (b * n_pair * n_q) == 0)
    n_k_steps = n_pages // n_per_step
    n_sub = bs * 2 * n_k_steps // (b * n_pair * n_q)
    assert n_pages // ppb <= LANES
    rows = n_head * ts

    sl2 = _alibi_slopes(n_head) * LOG2E
    parts, rest = [], sl2
    for _ in range(_SLOPE_PARTS):
        parts.append(rest.astype(ml_dtypes.bfloat16).astype(np.float64))
        rest = rest - parts[-1]
    sl = jnp.asarray(np.stack([sl2] + parts), dtype=F32)
    slopes = jnp.asarray(_alibi_slopes(n_head), dtype=F32)
    kc = _moba_consts(t, tq)
    km = jnp.pad(kmean, ((0, 0), (0, LANES - n_blk), (0, 0)))

    def sample_step(bi, hp, qi, r):
        return ((bi * n_pair + hp) * n_q + qi) * n_sub + r

    def per_seq(a):
        return pl.BlockSpec((1,) + a.shape[1:], lambda bi, hp, qi, r, pt: (sample_step(bi, hp, qi, r) // (2 * n_k_steps), 0, 0))

    def page_specs(group_of):
        def spec(p):
            def index(bi, hp, qi, r, pt):
                step = sample_step(bi, hp, qi, r)
                return pt[step // (2 * n_k_steps), group_of(step % (2 * n_k_steps)) * n_per_step + p], 0, 0, 0
            return pl.BlockSpec((1, n_head, hd, page), index)
        return [spec(p) for p in range(n_per_step)]

    smem = pl.BlockSpec(memory_space=pltpu.SMEM)
    n_chain = (LANES // hd) * (tq // tk)
    return pl.pallas_call(
        functools.partial(_moba_kernel, n_per_step=n_per_step, pages_per_block=ppb, hd=hd, tk=tk, n_k_steps=n_k_steps),
        grid_spec=pltpu.PrefetchScalarGridSpec(
            num_scalar_prefetch=1,
            grid=(b, n_pair, n_q, n_sub),
            in_specs=[smem, smem,
                      pl.BlockSpec((1, tq, LANES), lambda bi, hp, qi, r, pt: (bi, qi, hp)),
                      pl.BlockSpec((1, t, LANES), lambda bi, hp, qi, r, pt: (bi, 0, hp), pipeline_mode=pl.Buffered(1)),
                      pl.BlockSpec((1, LANES, t), lambda bi, hp, qi, r, pt: (bi, hp, 0), pipeline_mode=pl.Buffered(1)),
                      pl.BlockSpec((t, LANES), lambda bi, hp, qi, r, pt: (0, 0), pipeline_mode=pl.Buffered(1)),
                      pl.BlockSpec((1, LANES, LANES), lambda bi, hp, qi, r, pt: (bi, 0, hp)),
                      per_seq(qa_s), per_seq(ka_s), per_seq(va_s)]
                     + page_specs(lambda j: jnp.minimum(j, n_k_steps - 1))
                     + page_specs(lambda j: jnp.maximum(j - n_k_steps, 0)),
            out_specs=[pl.BlockSpec((1, tq, LANES), lambda bi, hp, qi, r, pt: (bi, qi, hp)), per_seq(qa_s)],
            scratch_shapes=[pltpu.VMEM((LANES // hd, tq, 2 * LANES), BF16),
                            pltpu.VMEM((n_chain, 1, tk), F32),
                            pltpu.VMEM((n_chain, LANES + _BF16_ROWS, tk), F32),
                            pltpu.VMEM((rows, w), F32),
                            pltpu.VMEM((rows, n_pages * page), F32),
                            pltpu.VMEM((rows, n_pages * page), BF16),
                            pltpu.VMEM((w, LANES), F32),
                            pltpu.VMEM((rows, 1), F32), pltpu.VMEM((rows, w), F32)],
        ),
        out_shape=[jax.ShapeDtypeStruct((b, t, w), F32), jax.ShapeDtypeStruct((bs, ts, w), F32)],
        compiler_params=_params("arbitrary", "arbitrary", "arbitrary", "arbitrary"),
        name="moba",
    )(page_table, sl, slopes, qa, k_bf, vt_bf, kc, km, qa_s, ka_s, va_s,
      *([cache_kt] * n_per_step), *([cache_vt] * n_per_step))


def _gla_kernel(q_ref, k_ref, v_ref, la_ref, rg_ref, gn_ref, s0_ref, o_ref, sout_ref, st_ref, *, dk, dv):
    c_idx = pl.program_id(1)
    n_chunks = pl.num_programs(1)
    c = q_ref.shape[1]
    n_pair = q_ref.shape[2] // LANES
    per_pair = LANES // dk
    assert dv == LANES and c % SUBLANES == 0

    @pl.when(c_idx == 0)
    def _():
        for p in range(n_pair):
            st_ref[p] = s0_ref[0, p].T

    row = lax.broadcasted_iota(jnp.int32, (c, LANES), 0)
    lane = lax.broadcasted_iota(jnp.int32, (c, LANES), 1)
    row2 = lax.broadcasted_iota(jnp.int32, (c, c), 0)
    col2 = lax.broadcasted_iota(jnp.int32, (c, c), 1)
    tri = jnp.where(row2 >= col2, 1.0, 0.0)
    b_all = jnp.dot(tri, la_ref[0], precision=lax.Precision.HIGHEST, preferred_element_type=F32)
    nb8 = c // SUBLANES
    sub = lax.broadcasted_iota(jnp.int32, (nb8, SUBLANES, LANES), 1)
    lane3 = lax.broadcasted_iota(jnp.int32, (nb8, SUBLANES, LANES), 2)
    lane_st = lax.broadcasted_iota(jnp.int32, (dv, LANES), 1)

    def bcast_row(x3, jj):
        return jnp.broadcast_to(x3[:, jj:jj + 1, :], x3.shape)

    for p in range(n_pair):
        cols = slice(p * LANES, (p + 1) * LANES)
        q = q_ref[0, :, cols] * (dk ** -0.5)
        k = k_ref[0, :, cols]
        b = b_all[:, cols]
        vs = [v_ref[0, :, (p * per_pair + hh) * dv:(p * per_pair + hh + 1) * dv] for hh in range(per_pair)]
        head_lanes = [(lane >= hh * dk) & (lane < (hh + 1) * dk) for hh in range(per_pair)]
        head_lanes3 = [(lane3 >= hh * dk) & (lane3 < (hh + 1) * dk) for hh in range(per_pair)]
        st = st_ref[p]

        qe = q * jnp.exp(b)
        o = [_bdot_nt(jnp.where(head_lanes[hh], qe, 0.0), st) for hh in range(per_pair)]

        q3 = q.reshape(nb8, SUBLANES, LANES)
        k3 = k.reshape(nb8, SUBLANES, LANES)
        b3 = b.reshape(nb8, SUBLANES, LANES)
        v3 = [v.reshape(nb8, SUBLANES, dv) for v in vs]
        o3 = [jnp.zeros((nb8, SUBLANES, dv), F32) for _ in range(per_pair)]
        for jj in range(SUBLANES):
            e = jnp.exp(jnp.minimum(b3 - bcast_row(b3, jj), 0.0))
            term = jnp.where(sub >= jj, q3 * bcast_row(k3, jj) * e, 0.0)
            for hh in range(per_pair):
                a = jnp.sum(jnp.where(head_lanes3[hh], term, 0.0), axis=-1, keepdims=True)
                o3[hh] = o3[hh] + a * bcast_row(v3[hh], jj)
        o = [o[hh] + o3[hh].reshape(c, dv) for hh in range(per_pair)]

        attn = [jnp.zeros((c, c), F32) for _ in range(per_pair)]
        m_half = SUBLANES
        while 2 * m_half <= c:
            span = 2 * m_half
            bnd = jnp.broadcast_to(b.reshape(c // span, span, LANES)[:, m_half - 1:m_half, :],
                                   (c // span, span, LANES)).reshape(c, LANES)
            upper = (row % span) >= m_half
            qm = jnp.where(upper, q * jnp.exp(jnp.minimum(b - bnd, 0.0)), 0.0)
            km = jnp.where(upper, 0.0, k * jnp.exp(jnp.minimum(bnd - b, 0.0)))
            same = (row2 // span) == (col2 // span)
            for hh in range(per_pair):
                a = _bdot_nt(jnp.where(head_lanes[hh], qm, 0.0), km)
                attn[hh] = attn[hh] + jnp.where(same, a, 0.0)
            m_half = span
        if c > SUBLANES:
            o = [o[hh] + _bdot(attn[hh], vs[hh]) for hh in range(per_pair)]

        b_last = b[c - 1:c, :]
        kk = (k * jnp.exp(b_last - b)).astype(BF16)
        upd = lax.dot_general(vs[0].astype(BF16), kk, _TN, preferred_element_type=F32)
        for hh in range(1, per_pair):
            u = lax.dot_general(vs[hh].astype(BF16), kk, _TN, preferred_element_type=F32)
            upd = jnp.where(lane_st >= hh * dk, u, upd)
        st_new = st * jnp.exp(b_last) + upd
        st_ref[p] = st_new

        for hh in range(per_pair):
            h = p * per_pair + hh
            rg = rg_ref[0, :, h * dv:(h + 1) * dv]
            o_ref[0, :, h * dv:(h + 1) * dv] = _rms(o[hh], gn_ref[...]) * (rg * _sigmoid(rg))

        @pl.when(c_idx == n_chunks - 1)
        def _(p=p, st_new=st_new):
            sout_ref[0, p] = st_new.T


def _gla(qg, kg, vg, la, rg, g_norm, s0, *, chunk, dk, dv):
    b, t, wk = qg.shape
    wv = vg.shape[2]
    n_head = wk // dk
    n_pair = wk // LANES
    s0p = s0.reshape(b, n_pair, LANES, dv)
    tok = lambda w: pl.BlockSpec((1, chunk, w), lambda bi, ci: (bi, ci, 0))
    st_spec = pl.BlockSpec((1, n_pair, LANES, dv), lambda bi, ci: (bi, 0, 0, 0))
    og, s_new = pl.pallas_call(
        functools.partial(_gla_kernel, dk=dk, dv=dv),
        grid=(b, t // chunk),
        in_specs=[tok(wk), tok(wk), tok(wv), tok(wk), tok(wv),
                  pl.BlockSpec(g_norm.shape, lambda bi, ci: (0, 0)), st_spec],
        out_specs=[tok(wv), st_spec],
        out_shape=[jax.ShapeDtypeStruct((b, t, wv), F32), jax.ShapeDtypeStruct((b, n_pair, LANES, dv), F32)],
        scratch_shapes=[pltpu.VMEM((n_pair, dv, LANES), F32)],
        compiler_params=_params("parallel", "arbitrary"),
        name="gla",
    )(qg, kg, vg, la, rg, g_norm, s0p)
    return og, s_new.reshape(b, n_head, dk, dv)


def _merge_kernel(x_ref, oa_ref, og_ref, sga_ref, sgb_ref, wa_ref, wb_ref, wo_ref, g_ref, y_ref):
    merged = (sga_ref[...].astype(F32) * _bdot(oa_ref[...], wa_ref[...])
              + sgb_ref[...].astype(F32) * _bdot(og_ref[...], wb_ref[...]))
    y_ref[...] = x_ref[...] + _rms(_bdot(merged, wo_ref[...]), g_ref[...])


def _mlp_kernel(x_ref, wu_ref, wd_ref, g1_ref, g2_ref, y_ref):
    x = x_ref[...]
    u = _bdot(_rms(x, g1_ref[...]), wu_ref[...])
    u = jnp.square(jnp.maximum(u, 0.0))
    y_ref[...] = x + _rms(_bdot(u, wd_ref[...]), g2_ref[...])


def _rowwise_call(kernel, name, row_inputs, const_inputs, out_width, tm):
    n = row_inputs[0].shape[0]
    assert n % tm == 0
    row = lambda a: pl.BlockSpec((tm, a.shape[1]), lambda i: (i, 0))
    full = lambda a: pl.BlockSpec(a.shape, lambda i: (0,) * a.ndim)
    return pl.pallas_call(
        kernel,
        grid=(n // tm,),
        in_specs=[row(a) for a in row_inputs] + [full(a) for a in const_inputs],
        out_specs=pl.BlockSpec((tm, out_width), lambda i: (i, 0)),
        out_shape=jax.ShapeDtypeStruct((n, out_width), F32),
        compiler_params=_params("parallel"),
        name=name,
    )(*row_inputs, *const_inputs)


def _gla_chunk(t):
    c = SUBLANES
    while c * 2 <= min(t, LANES) and t % (c * 2) == 0:
        c *= 2
    return c


def kernel(x_prompt, x_sample, cache_k, cache_v, page_table, state_gla, w_in, w_gla_gate, b_gla_gate, g_gla_norm,
           w_branch_a, w_branch_b, w_out, w_up, w_down, g_pre_mix, g_post_mix, g_pre_mlp, g_post_mlp):
    bp, tp, d_model = x_prompt.shape
    bs, ts, _ = x_sample.shape
    depth, n_phys, page, n_head, hd = cache_k.shape
    _, _, n_head_g, dk, dv = state_gla.shape
    w_a = n_head * hd
    qk_g = n_head_g * dk
    v_g = n_head_g * dv
    rank = w_gla_gate.shape[1]
    n_main = 3 * w_a + 2 * qk_g + 2 * v_g

    hp = x_prompt.reshape(bp * tp, d_model)
    hs = x_sample.reshape(bs * ts, d_model)
    outs = [[] for _ in range(6)]
    for l in range(depth):
        wm = w_in[l, :, :n_main].astype(BF16)
        wlr = jnp.pad(w_in[l, :, n_main:n_main + rank], ((0, 0), (0, LANES - rank))).astype(BF16)
        wgt = w_in[l, :, n_main + rank:].astype(BF16)
        wgg = jnp.pad(w_gla_gate[l], ((0, LANES - rank), (0, 0))).astype(BF16)
        bgg = b_gla_gate[l][None, :]
        wa, wb, wo = (w[l].astype(BF16) for w in (w_branch_a, w_branch_b, w_out))
        wu, wd = w_up[l].astype(BF16), w_down[l].astype(BF16)
        g_mix, g_pm, g_mlp, g_pl, g_gn = (g[l][None, :] for g in
                                          (g_pre_mix, g_post_mix, g_pre_mlp, g_post_mlp, g_gla_norm))

        def project(x, kv_seq_len):
            return _proj(x, g_mix, wm, wlr, wgt, wgg, bgg, w_a=w_a, qk_g=qk_g, v_g=v_g, kv_seq_len=kv_seq_len)

        def mix(x, b, t, oa, proj_out, s0):
            qg, kg, vg, rg, la, sga, sgb = proj_out[3:10]
            r3 = lambda a: a.reshape(b, t, a.shape[-1])
            og, s_new = _gla(r3(qg), r3(kg), r3(vg), r3(la), r3(rg), g_gn, s0, chunk=_gla_chunk(t), dk=dk, dv=dv)
            n = b * t
            x = _rowwise_call(_merge_kernel, "merge", [x, oa.reshape(n, w_a), og.reshape(n, v_g), sga, sgb],
                              [wa, wb, wo, g_pm], d_model, 2 * MOBA_BLOCK if n % (2 * MOBA_BLOCK) == 0 else MOBA_BLOCK)
            return _rowwise_call(_mlp_kernel, "mlp", [x], [wu, wd, g_mlp, g_pl], d_model, MOBA_BLOCK), s_new

        pp = project(hp, tp)
        ps = project(hs, None)
        qa_p, kt_p, vt_p, kmean_p, k_bf, vt_bf = pp[0], pp[1], pp[2], pp[10], pp[11], pp[12]
        qa_s, ka_s, va_s = (a.reshape(bs, ts, w_a) for a in ps[:3])
        to_stored = lambda c: jnp.transpose(c[l], (0, 2, 3, 1))
        oa_p, oa_s = _moba(qa_p.reshape(bp, tp, w_a), k_bf.reshape(bp, tp, w_a), vt_bf,
                           kmean_p.reshape(bp, tp // MOBA_BLOCK, w_a), qa_s, ka_s, va_s,
                           to_stored(cache_k), to_stored(cache_v), page_table, hd=hd)
        hp, sp = mix(hp, bp, tp, oa_p, pp, jnp.zeros((bp, n_head_g, dk, dv), state_gla.dtype))
        hs, ssn = mix(hs, bs, ts, oa_s, ps, state_gla[l])
        kp, vp = (jnp.transpose(a.reshape(bp, n_head, hd, tp), (0, 3, 1, 2)) for a in (kt_p, vt_p))
        ksn, vsn = (a.reshape(bs, ts, n_head, hd) for a in (ka_s, va_s))
        for lst, val in zip(outs, (kp, vp, sp, ksn, vsn, ssn)):
            lst.append(val)
    return (hp.reshape(bp, tp, d_model), hs.reshape(bs, ts, d_model)) + tuple(jnp.stack(o) for o in outs)
```

```python
import functools

import ml_dtypes
import numpy as np
import jax
import jax.numpy as jnp
from jax import lax
from jax.experimental import pallas as pl
from jax.experimental.pallas import tpu as pltpu

F32 = jnp.float32
BF16 = jnp.bfloat16

LANES = 128
SUBLANES = 8
VMEM_LIMIT_BYTES = 56 * 1024 * 1024

EPS = 1e-6
MOBA_BLOCK = 256
MOBA_TOPK = 3
GLA_TAU = 16.0
NEG_BIG = -1e30
PENALTY = -30000.0
LOG2E = float(np.log2(np.e))

_NT = (((1,), (1,)), ((), ()))
_TN = (((0,), (0,)), ((), ()))


def _params(*sem):
    return pltpu.CompilerParams(dimension_semantics=sem, vmem_limit_bytes=VMEM_LIMIT_BYTES)


def _sigmoid(x):
    return 1.0 / (1.0 + jnp.exp(-x))


def _rms(x, g):
    return x * lax.rsqrt(jnp.mean(x * x, axis=-1, keepdims=True) + EPS) * g


def _bdot(a, b):
    return jnp.dot(a.astype(BF16), b.astype(BF16), preferred_element_type=F32)


def _bdot_nt(a, b):
    return lax.dot_general(a.astype(BF16), b.astype(BF16), _NT, preferred_element_type=F32)


def _proj_kernel(x_ref, g_ref, wm_ref, wlr_ref, wgt_ref, wgg_ref, bgg_ref,
                 qa_ref, ka_ref, va_ref, qg_ref, kg_ref, vg_ref, rg_ref, la_ref, sga_ref, sgb_ref, kmean_ref,
                 *attn_refs, w_a, qk_g, v_g, d_model, kv_transposed):
    hb = _rms(x_ref[...], g_ref[...]).astype(BF16)

    def proj(lo, n):
        return jnp.dot(hb, wm_ref[:, lo:lo + n], preferred_element_type=F32)

    qa_ref[...] = proj(0, w_a)
    ka = proj(w_a, w_a)
    va = proj(2 * w_a, w_a)
    kmean_ref[0] = jnp.sum(ka, axis=0, keepdims=True) * (1.0 / ka.shape[0])
    if kv_transposed:
        k_bf_ref, vt_bf_ref = attn_refs
        vt = va.T
        ka_ref[0] = ka.T
        va_ref[0] = vt
        k_bf_ref[...] = ka.astype(BF16)
        vt_bf_ref[0] = vt.astype(BF16)
    else:
        ka_ref[...] = ka
        va_ref[...] = va
    lo = 3 * w_a
    qg_ref[...] = proj(lo, qk_g)
    kg_ref[...] = proj(lo + qk_g, qk_g)
    vg_ref[...] = proj(lo + 2 * qk_g, v_g)
    rg_ref[...] = proj(lo + 2 * qk_g + v_g, v_g)
    lr = jnp.dot(hb, wlr_ref[...], preferred_element_type=F32)
    xg = jnp.dot(lr.astype(BF16), wgg_ref[...], preferred_element_type=F32) + bgg_ref[...]
    log_sig = jnp.minimum(xg, 0.0) - jnp.log1p(jnp.exp(-jnp.abs(xg)))
    la_ref[...] = log_sig * (1.0 / GLA_TAU)
    ga = jnp.dot(hb, wgt_ref[:, :d_model], preferred_element_type=F32)
    sga_ref[...] = _sigmoid(ga).astype(BF16)
    gb = jnp.dot(hb, wgt_ref[:, d_model:], preferred_element_type=F32)
    sgb_ref[...] = _sigmoid(gb).astype(BF16)


def _proj(x, g, wm, wlr, wgt, wgg, bgg, *, w_a, qk_g, v_g, kv_seq_len=None):
    n, d_model = x.shape
    tm = MOBA_BLOCK
    assert n % tm == 0
    nt = n // tm
    row = lambda w: pl.BlockSpec((tm, w), lambda i: (i, 0))
    full = lambda a: pl.BlockSpec(a.shape, lambda i: (0,) * a.ndim)
    widths = (w_a, w_a, w_a, qk_g, qk_g, v_g, v_g, qk_g)
    out_shape = [jax.ShapeDtypeStruct((n, w), F32) for w in widths]
    out_shape += [jax.ShapeDtypeStruct((n, d_model), BF16)] * 2
    out_shape += [jax.ShapeDtypeStruct((nt, 1, w_a), F32)]
    out_specs = [row(w) for w in widths] + [row(d_model)] * 2
    out_specs += [pl.BlockSpec((1, 1, w_a), lambda i: (i, 0, 0))]
    if kv_seq_len is not None:
        assert kv_seq_len % tm == 0
        seq_tiles = kv_seq_len // tm
        transposed = pl.BlockSpec((1, w_a, tm), lambda i: (i // seq_tiles, 0, i % seq_tiles))
        for i in (1, 2):
            out_shape[i] = jax.ShapeDtypeStruct((n // kv_seq_len, w_a, kv_seq_len), F32)
            out_specs[i] = transposed
        out_shape += [jax.ShapeDtypeStruct((n, w_a), BF16),
                      jax.ShapeDtypeStruct((n // kv_seq_len, w_a, kv_seq_len), BF16)]
        out_specs += [row(w_a), transposed]
    return pl.pallas_call(
        functools.partial(_proj_kernel, w_a=w_a, qk_g=qk_g, v_g=v_g, d_model=d_model,
                          kv_transposed=kv_seq_len is not None),
        grid=(nt,),
        in_specs=[row(d_model), full(g), full(wm), full(wlr), full(wgt), full(wgg), full(bgg)],
        out_specs=out_specs,
        out_shape=out_shape,
        compiler_params=_params("parallel"),
        name="proj",
    )(x, g, wm, wlr, wgt, wgg, bgg)


def _top_blocks(gate, candidate, lane):
    lane_f = lane.astype(F32)
    no_lane = float(gate.shape[-1])
    avail = jnp.where(candidate, 1.0, 0.0)
    sel = jnp.zeros(gate.shape, F32)
    for _ in range(MOBA_TOPK):
        cur = jnp.where(avail > 0.0, gate, -jnp.inf)
        mx = jnp.max(cur, axis=-1, keepdims=True)
        cand = jnp.where(avail > 0.0, jnp.where(cur == mx, lane_f, no_lane), no_lane)
        first = jnp.min(cand, axis=-1, keepdims=True)
        pick = lane_f == first
        sel = jnp.where(pick, 1.0, sel)
        avail = jnp.where(pick, 0.0, avail)
    return sel


def _alibi_slopes(n_head):
    return 2.0 ** (-8.0 * (np.arange(n_head) + 1) / n_head)


_SLOPE_PARTS = 3
_AUG_EXTRA = 2 * _SLOPE_PARTS


def _moba_prompt_step(r, n_sub, hp, qi, sl_ref, q_ref, k_ref, vt_ref, kc_ref, km_ref, o_ref,
                      qaug_ref, m_ref, acc_ref, *, hd, tk):
    blk = MOBA_BLOCK
    t = k_ref.shape[1]
    tq = q_ref.shape[1]
    n_blk = t // blk
    per_pair = LANES // hd
    n_acc = acc_ref.shape[1]
    tiles_per_q = tq // tk
    chains = [(hh, cb) for hh in range(per_pair) for cb in range(tiles_per_q)]
    ones = jnp.ones((n_acc - LANES, tk), BF16)

    def tiles(j):
        keys = pl.ds(pl.multiple_of(j * tk, tk), tk)
        kt = jnp.concatenate([k_ref[0, keys, :], kc_ref[keys, :]], axis=1)
        return kt, jnp.concatenate([vt_ref[0, :, keys], ones], axis=0)

    def logits(kt, hh, cb):
        return lax.dot_general(kt, qaug_ref[hh, cb * tk:(cb + 1) * tk, :], _NT, preferred_element_type=F32)

    def absorb(stat, s, va, c):
        m, acc = stat
        m_new = jnp.maximum(m, jnp.max(s, axis=0, keepdims=True) + c)
        p = jnp.exp2(s - (m_new - c)).astype(BF16)
        return m_new, jnp.exp2(m - m_new) * acc + jnp.dot(va, p, preferred_element_type=F32)

    @pl.when(r == 0)
    def _():
        lane = lax.broadcasted_iota(jnp.int32, (tq, LANES), 1)
        own = qi * (tq // blk) + lax.broadcasted_iota(jnp.int32, (tq, LANES), 0) // blk
        q = q_ref[0]
        for hh in range(per_pair):
            h = hp * per_pair + hh
            head_lanes = (lane >= hh * hd) & (lane < (hh + 1) * hd)
            qh = jnp.where(head_lanes, q, 0.0)
            gate = _bdot_nt(qh, km_ref[0])
            sel = _top_blocks(gate, lane < own, lane)
            parts = [sl_ref[1 + i, h] for i in range(_SLOPE_PARTS)]
            qx = jnp.where(lane < n_blk, jnp.where((sel > 0.0) | (lane == own), 0.0, PENALTY), 0.0)
            for off, val in enumerate(parts + [part * blk for part in parts]):
                qx = jnp.where(lane == n_blk + off, val, qx)
            qaug_ref[hh, :, :LANES] = (qh * (hd ** -0.5 * LOG2E)).astype(BF16)
            qaug_ref[hh, :, LANES:] = qx.astype(BF16)

        key2 = lax.broadcasted_iota(jnp.int32, (tk, tk), 0)
        qry2 = lax.broadcasted_iota(jnp.int32, (tk, tk), 1)
        stats = {c: (jnp.full((1, tk), NEG_BIG, F32), jnp.zeros((n_acc, tk), F32)) for c in chains}
        for d in range(tiles_per_q):
            kt, va = tiles(qi * tiles_per_q + d)
            live = [(hh, cb) for hh, cb in chains if cb >= d]
            ss = [logits(kt, hh, cb) for hh, cb in live]
            for (hh, cb), s in zip(live, ss):
                if cb == d:
                    s = jnp.where(key2 <= qry2, s, NEG_BIG)
                stats[hh, cb] = absorb(stats[hh, cb], s, va, 0.0)
        for ci, c in enumerate(chains):
            m_ref[ci], acc_ref[ci] = stats[c]

    def body(i, flat):
        j = r + i * n_sub
        kt, va = tiles(j)
        ss = [logits(kt, hh, cb) for hh, cb in chains]
        out = []
        for ci, (hh, cb) in enumerate(chains):
            c = -(sl_ref[0, hp * per_pair + hh] * tq) * (qi - j // tiles_per_q).astype(F32)
            out += absorb(flat[2 * ci:2 * ci + 2], ss[ci], va, c)
        return tuple(out)

    n_mine = (qi * tiles_per_q - r + n_sub - 1) // n_sub
    flat = lax.fori_loop(0, n_mine, body, tuple(x for ci in range(len(chains)) for x in (m_ref[ci], acc_ref[ci])))
    for ci in range(len(chains)):
        m_ref[ci], acc_ref[ci] = flat[2 * ci], flat[2 * ci + 1]

    @pl.when(r == n_sub - 1)
    def _():
        dim = lax.broadcasted_iota(jnp.int32, (LANES, tk), 0)
        for cb in range(tiles_per_q):
            out = None
            for hh in range(per_pair):
                acc = flat[2 * chains.index((hh, cb)) + 1]
                o_h = acc[:LANES] / acc[LANES:LANES + 1]
                out = o_h if out is None else jnp.where(dim >= hh * hd, o_h, out)
            o_ref[0, cb * tk:(cb + 1) * tk, :] = out.T


def _moba_consts(t, tq):
    n_blk = t // MOBA_BLOCK
    assert n_blk + _AUG_EXTRA <= LANES
    pos = np.arange(t)
    c = np.zeros((t, LANES), np.float32)
    c[pos, pos // MOBA_BLOCK] = 1.0
    for i in range(_SLOPE_PARTS):
        c[:, n_blk + i] = pos % MOBA_BLOCK
        c[:, n_blk + _SLOPE_PARTS + i] = (pos % tq) // MOBA_BLOCK
    return jnp.asarray(c, dtype=BF16)


_BF16_ROWS = 2 * SUBLANES


def _moba_sample_step(j, n_k_steps, slopes_ref, q_ref, kn_ref, vn_ref, pages_ref,
                      o_ref, qbd_ref, s_ref, p_ref, ksum_ref, l_ref, acc_ref, *, pages_per_block, hd):
    n_per_step = pages_ref.shape[0]
    ts, w = q_ref.shape[1:]
    n_head = w // hd
    rows = n_head * ts
    page = pages_ref.shape[3]
    blk = page * pages_per_block
    past = s_ref.shape[1]
    n_blk = past // blk
    scale = hd ** -0.5

    row_head = lax.broadcasted_iota(jnp.int32, (rows, 1), 0) // ts
    row_q = lax.broadcasted_iota(jnp.int32, (rows, 1), 0) % ts
    col_head = lax.broadcasted_iota(jnp.int32, (rows, w), 1) // hd
    slope = jnp.zeros((rows, 1), F32)
    for h in range(n_head):
        slope = jnp.where(row_head == h, slopes_ref[h], slope)

    @pl.when(j == 0)
    def _():
        qt = jnp.concatenate([q_ref[0]] * n_head, axis=0)
        qbd_ref[...] = jnp.where(col_head == row_head, qt, 0.0)
        ksum_ref[...] = jnp.zeros(ksum_ref.shape, F32)

    @pl.when(j < n_k_steps)
    def _():
        qs = (qbd_ref[...] * scale).astype(BF16)
        lane = lax.broadcasted_iota(jnp.int32, ksum_ref.shape, 1)
        ksum = ksum_ref[...]
        for g in range(n_per_step // pages_per_block):
            total = None
            for p_i in range(g * pages_per_block, (g + 1) * pages_per_block):
                page_idx = j * n_per_step + p_i
                kt = pages_ref[p_i].reshape(w, page)
                total = kt if total is None else total + kt
                s = jnp.dot(qs, kt.astype(BF16), preferred_element_type=F32)
                s_ref[:, pl.ds(pl.multiple_of(page_idx * page, page), page)] = s
            blk_idx = (j * n_per_step) // pages_per_block + g
            ksum = jnp.where(lane == blk_idx, jnp.sum(total, axis=1, keepdims=True), ksum)
        ksum_ref[...] = ksum

    @pl.when(j == n_k_steps)
    def _():
        qbd = qbd_ref[...]
        gate = _bdot(qbd, ksum_ref[...])
        lane = lax.broadcasted_iota(jnp.int32, gate.shape, 1)
        sel = _top_blocks(gate, lane < n_blk, lane)
        s_own = _bdot_nt(qbd * scale, kn_ref[0])
        kq = lax.broadcasted_iota(jnp.int32, s_own.shape, 1)
        s_own = jnp.where(kq <= row_q, s_own + slope * kq.astype(F32), NEG_BIG)
        key_in_blk = lax.broadcasted_iota(jnp.int32, (1, blk), 1)

        def add_bias(b_i, m_wide):
            cols = pl.ds(pl.multiple_of(b_i * blk, blk), blk)
            picked = jnp.max(jnp.where(lane == b_i, sel, 0.0), axis=-1, keepdims=True)
            rel = (b_i * blk - past + key_in_blk).astype(F32)
            s = s_ref[:, cols] + slope * rel + jnp.where(picked > 0.0, 0.0, NEG_BIG)
            s_ref[:, cols] = s
            return jnp.maximum(m_wide, s)

        unroll = 8 if n_blk % 8 == 0 else 1
        m_wide = lax.fori_loop(0, n_blk, add_bias, jnp.full((rows, blk), NEG_BIG, F32), unroll=unroll)
        m = jnp.maximum(jnp.max(m_wide, axis=-1, keepdims=True), jnp.max(s_own, axis=-1, keepdims=True))

        def probs(b_i, l_wide):
            cols = pl.ds(pl.multiple_of(b_i * blk, blk), blk)
            p = jnp.exp(s_ref[:, cols] - m)
            p_ref[:, cols] = p.astype(BF16)
            return l_wide + p

        l_wide = lax.fori_loop(0, n_blk, probs, jnp.zeros((rows, blk), F32), unroll=unroll)
        p_own = jnp.exp(s_own - m)
        l_ref[...] = jnp.sum(l_wide, axis=-1, keepdims=True) + jnp.sum(p_own, axis=-1, keepdims=True)
        acc_ref[...] = _bdot(p_own, vn_ref[0])

    @pl.when(j >= n_k_steps)
    def _():
        acc = acc_ref[...]
        for p_i in range(n_per_step):
            page_idx = (j - n_k_steps) * n_per_step + p_i
            p = p_ref[:, pl.ds(pl.multiple_of(page_idx * page, page), page)]
            vt = pages_ref[p_i].reshape(w, page).astype(BF16)
            acc = acc + lax.dot_general(p, vt, _NT, preferred_element_type=F32)
        acc_ref[...] = acc

    @pl.when(j == 2 * n_k_steps - 1)
    def _():
        res = jnp.where(col_head == row_head, acc_ref[...] / l_ref[...], 0.0)
        out = res[0:ts]
        for h in range(1, n_head):
            out = out + res[h * ts:(h + 1) * ts]
        o_ref[0] = out


def _moba_kernel(pt_ref, sl_ref, slopes_ref, q_ref, k_ref, vt_ref, kc_ref, km_ref, qs_ref, kn_ref, vn_ref,
                 cache_k_hbm, cache_v_hbm, o_ref, os_ref, qaug_ref, m_ref, acc_ref, pages_ref, sem, *sample_scratch,
                 pages_per_block, hd, tk, n_k_steps):
    hp, qi, r = pl.program_id(1), pl.program_id(2), pl.program_id(3)
    n_sub = pl.num_programs(3)
    step = ((pl.program_id(0) * pl.num_programs(1) + hp) * pl.num_programs(2) + qi) * n_sub + r
    n_steps = pl.num_programs(0) * pl.num_programs(1) * pl.num_programs(2) * n_sub
    n_per_step = pages_ref.shape[1]
    per_seq = 2 * n_k_steps

    def page_copies(s):
        seq, j, slot = s // per_seq, s % per_seq, s % 2

        def copies(cache_hbm, group):
            return [pltpu.make_async_copy(cache_hbm.at[pt_ref[seq, group * n_per_step + p]],
                                          pages_ref.at[slot, p], sem.at[slot]) for p in range(n_per_step)]
        return (j < n_k_steps, lambda: copies(cache_k_hbm, j)), (j >= n_k_steps, lambda: copies(cache_v_hbm, j - n_k_steps))

    def for_pages(s, action):
        for cond, make in page_copies(s):
            @pl.when(cond)
            def _(make=make):
                for copy in make():
                    action(copy)

    @pl.when(step == 0)
    def _():
        for_pages(step, lambda copy: copy.start())

    @pl.when(step + 1 < n_steps)
    def _():
        for_pages(step + 1, lambda copy: copy.start())

    _moba_prompt_step(r, n_sub, hp, qi, sl_ref, q_ref, k_ref, vt_ref, kc_ref, km_ref, o_ref,
                      qaug_ref, m_ref, acc_ref, hd=hd, tk=tk)
    for_pages(step, lambda copy: copy.wait())
    _moba_sample_step(step % per_seq, n_k_steps, slopes_ref, qs_ref, kn_ref, vn_ref, pages_ref.at[step % 2],
                      os_ref, *sample_scratch, pages_per_block=pages_per_block, hd=hd)


def _moba(qa, k_bf, vt_bf, kmean, qa_s, ka_s, va_s, cache_kt, cache_vt, page_table, *, hd):
    b, t, w = qa.shape
    bs, ts, _ = qa_s.shape
    _, n_head, _, page = cache_kt.shape
    n_pages = page_table.shape[1]
    blk = MOBA_BLOCK
    ppb = blk // page
    n_pair = w // LANES
    n_blk = t // blk
    tk = 2 * blk if t % (2 * blk) == 0 else blk
    tq = 2 * tk if t % (2 * tk) == 0 else tk
    n_q = t // tq
    n_per_step = next(p for p in (32, 16, 8, 4, 2) if n_pages % p == 0 and p % ppb == 0
                      and (bs * 2 * (n_pages // p)) % (b * n_pair * n_q) == 0)
    n_k_steps = n_pages // n_per_step
    n_sub = bs * 2 * n_k_steps // (b * n_pair * n_q)
    assert n_pages // ppb <= LANES
    rows = n_head * ts

    sl2 = _alibi_slopes(n_head) * LOG2E
    parts, rest = [], sl2
    for _ in range(_SLOPE_PARTS):
        parts.append(rest.astype(ml_dtypes.bfloat16).astype(np.float64))
        rest = rest - parts[-1]
    sl = jnp.asarray(np.stack([sl2] + parts), dtype=F32)
    slopes = jnp.asarray(_alibi_slopes(n_head), dtype=F32)
    kc = _moba_consts(t, tq)
    km = jnp.pad(kmean, ((0, 0), (0, LANES - n_blk), (0, 0)))

    def sample_step(bi, hp, qi, r):
        return ((bi * n_pair + hp) * n_q + qi) * n_sub + r

    def per_seq(a):
        return pl.BlockSpec((1,) + a.shape[1:], lambda bi, hp, qi, r, pt: (sample_step(bi, hp, qi, r) // (2 * n_k_steps), 0, 0))

    smem = pl.BlockSpec(memory_space=pltpu.SMEM)
    hbm = pl.BlockSpec(memory_space=pl.ANY)
    n_chain = (LANES // hd) * (tq // tk)
    return pl.pallas_call(
        functools.partial(_moba_kernel, pages_per_block=ppb, hd=hd, tk=tk, n_k_steps=n_k_steps),
        grid_spec=pltpu.PrefetchScalarGridSpec(
            num_scalar_prefetch=1,
            grid=(b, n_pair, n_q, n_sub),
            in_specs=[smem, smem,
                      pl.BlockSpec((1, tq, LANES), lambda bi, hp, qi, r, pt: (bi, qi, hp)),
                      pl.BlockSpec((1, t, LANES), lambda bi, hp, qi, r, pt: (bi, 0, hp), pipeline_mode=pl.Buffered(1)),
                      pl.BlockSpec((1, LANES, t), lambda bi, hp, qi, r, pt: (bi, hp, 0), pipeline_mode=pl.Buffered(1)),
                      pl.BlockSpec((t, LANES), lambda bi, hp, qi, r, pt: (0, 0), pipeline_mode=pl.Buffered(1)),
                      pl.BlockSpec((1, LANES, LANES), lambda bi, hp, qi, r, pt: (bi, 0, hp)),
                      per_seq(qa_s), per_seq(ka_s), per_seq(va_s), hbm, hbm],
            out_specs=[pl.BlockSpec((1, tq, LANES), lambda bi, hp, qi, r, pt: (bi, qi, hp)), per_seq(qa_s)],
            scratch_shapes=[pltpu.VMEM((LANES // hd, tq, 2 * LANES), BF16),
                            pltpu.VMEM((n_chain, 1, tk), F32),
                            pltpu.VMEM((n_chain, LANES + _BF16_ROWS, tk), F32),
                            pltpu.VMEM((2, n_per_step, n_head, hd, page), F32),
                            pltpu.SemaphoreType.DMA((2,)),
                            pltpu.VMEM((rows, w), F32),
                            pltpu.VMEM((rows, n_pages * page), F32),
                            pltpu.VMEM((rows, n_pages * page), BF16),
                            pltpu.VMEM((w, LANES), F32),
                            pltpu.VMEM((rows, 1), F32), pltpu.VMEM((rows, w), F32)],
        ),
        out_shape=[jax.ShapeDtypeStruct((b, t, w), F32), jax.ShapeDtypeStruct((bs, ts, w), F32)],
        compiler_params=_params("arbitrary", "arbitrary", "arbitrary", "arbitrary"),
        name="moba",
    )(page_table, sl, slopes, qa, k_bf, vt_bf, kc, km, qa_s, ka_s, va_s, cache_kt, cache_vt)


def _gla_kernel(q_ref, k_ref, v_ref, la_ref, rg_ref, gn_ref, s0_ref, o_ref, sout_ref, st_ref, *, dk, dv):
    c_idx = pl.program_id(1)
    n_chunks = pl.num_programs(1)
    c = q_ref.shape[1]
    n_pair = q_ref.shape[2] // LANES
    per_pair = LANES // dk
    assert dv == LANES and c % SUBLANES == 0

    @pl.when(c_idx == 0)
    def _():
        for p in range(n_pair):
            st_ref[p] = s0_ref[0, p].T

    row = lax.broadcasted_iota(jnp.int32, (c, LANES), 0)
    lane = lax.broadcasted_iota(jnp.int32, (c, LANES), 1)
    row2 = lax.broadcasted_iota(jnp.int32, (c, c), 0)
    col2 = lax.broadcasted_iota(jnp.int32, (c, c), 1)
    tri = jnp.where(row2 >= col2, 1.0, 0.0)
    b_all = jnp.dot(tri, la_ref[0], precision=lax.Precision.HIGHEST, preferred_element_type=F32)
    nb8 = c // SUBLANES
    sub = lax.broadcasted_iota(jnp.int32, (nb8, SUBLANES, LANES), 1)
    lane3 = lax.broadcasted_iota(jnp.int32, (nb8, SUBLANES, LANES), 2)
    lane_st = lax.broadcasted_iota(jnp.int32, (dv, LANES), 1)

    def bcast_row(x3, jj):
        return jnp.broadcast_to(x3[:, jj:jj + 1, :], x3.shape)

    for p in range(n_pair):
        cols = slice(p * LANES, (p + 1) * LANES)
        q = q_ref[0, :, cols] * (dk ** -0.5)
        k = k_ref[0, :, cols]
        b = b_all[:, cols]
        vs = [v_ref[0, :, (p * per_pair + hh) * dv:(p * per_pair + hh + 1) * dv] for hh in range(per_pair)]
        head_lanes = [(lane >= hh * dk) & (lane < (hh + 1) * dk) for hh in range(per_pair)]
        head_lanes3 = [(lane3 >= hh * dk) & (lane3 < (hh + 1) * dk) for hh in range(per_pair)]
        st = st_ref[p]

        qe = q * jnp.exp(b)
        o = [_bdot_nt(jnp.where(head_lanes[hh], qe, 0.0), st) for hh in range(per_pair)]

        q3 = q.reshape(nb8, SUBLANES, LANES)
        k3 = k.reshape(nb8, SUBLANES, LANES)
        b3 = b.reshape(nb8, SUBLANES, LANES)
        v3 = [v.reshape(nb8, SUBLANES, dv) for v in vs]
        o3 = [jnp.zeros((nb8, SUBLANES, dv), F32) for _ in range(per_pair)]
        for jj in range(SUBLANES):
            e = jnp.exp(jnp.minimum(b3 - bcast_row(b3, jj), 0.0))
            term = jnp.where(sub >= jj, q3 * bcast_row(k3, jj) * e, 0.0)
            for hh in range(per_pair):
                a = jnp.sum(jnp.where(head_lanes3[hh], term, 0.0), axis=-1, keepdims=True)
                o3[hh] = o3[hh] + a * bcast_row(v3[hh], jj)
        o = [o[hh] + o3[hh].reshape(c, dv) for hh in range(per_pair)]

        attn = [jnp.zeros((c, c), F32) for _ in range(per_pair)]
        m_half = SUBLANES
        while 2 * m_half <= c:
            span = 2 * m_half
            bnd = jnp.broadcast_to(b.reshape(c // span, span, LANES)[:, m_half - 1:m_half, :],
                                   (c // span, span, LANES)).reshape(c, LANES)
            upper = (row % span) >= m_half
            qm = jnp.where(upper, q * jnp.exp(jnp.minimum(b - bnd, 0.0)), 0.0)
            km = jnp.where(upper, 0.0, k * jnp.exp(jnp.minimum(bnd - b, 0.0)))
            same = (row2 // span) == (col2 // span)
            for hh in range(per_pair):
                a = _bdot_nt(jnp.where(head_lanes[hh], qm, 0.0), km)
                attn[hh] = attn[hh] + jnp.where(same, a, 0.0)
            m_half = span
        if c > SUBLANES:
            o = [o[hh] + _bdot(attn[hh], vs[hh]) for hh in range(per_pair)]

        b_last = b[c - 1:c, :]
        kk = (k * jnp.exp(b_last - b)).astype(BF16)
        upd = lax.dot_general(vs[0].astype(BF16), kk, _TN, preferred_element_type=F32)
        for hh in range(1, per_pair):
            u = lax.dot_general(vs[hh].astype(BF16), kk, _TN, preferred_element_type=F32)
            upd = jnp.where(lane_st >= hh * dk, u, upd)
        st_new = st * jnp.exp(b_last) + upd
        st_ref[p] = st_new

        for hh in range(per_pair):
            h = p * per_pair + hh
            rg = rg_ref[0, :, h * dv:(h + 1) * dv]
            o_ref[0, :, h * dv:(h + 1) * dv] = _rms(o[hh], gn_ref[...]) * (rg * _sigmoid(rg))

        @pl.when(c_idx == n_chunks - 1)
        def _(p=p, st_new=st_new):
            sout_ref[0, p] = st_new.T


def _gla(qg, kg, vg, la, rg, g_norm, s0, *, chunk, dk, dv):
    b, t, wk = qg.shape
    wv = vg.shape[2]
    n_head = wk // dk
    n_pair = wk // LANES
    s0p = s0.reshape(b, n_pair, LANES, dv)
    tok = lambda w: pl.BlockSpec((1, chunk, w), lambda bi, ci: (bi, ci, 0))
    st_spec = pl.BlockSpec((1, n_pair, LANES, dv), lambda bi, ci: (bi, 0, 0, 0))
    og, s_new = pl.pallas_call(
        functools.partial(_gla_kernel, dk=dk, dv=dv),
        grid=(b, t // chunk),
        in_specs=[tok(wk), tok(wk), tok(wv), tok(wk), tok(wv),
                  pl.BlockSpec(g_norm.shape, lambda bi, ci: (0, 0)), st_spec],
        out_specs=[tok(wv), st_spec],
        out_shape=[jax.ShapeDtypeStruct((b, t, wv), F32), jax.ShapeDtypeStruct((b, n_pair, LANES, dv), F32)],
        scratch_shapes=[pltpu.VMEM((n_pair, dv, LANES), F32)],
        compiler_params=_params("parallel", "arbitrary"),
        name="gla",
    )(qg, kg, vg, la, rg, g_norm, s0p)
    return og, s_new.reshape(b, n_head, dk, dv)


def _merge_kernel(x_ref, oa_ref, og_ref, sga_ref, sgb_ref, wa_ref, wb_ref, wo_ref, g_ref, y_ref):
    merged = (sga_ref[...].astype(F32) * _bdot(oa_ref[...], wa_ref[...])
              + sgb_ref[...].astype(F32) * _bdot(og_ref[...], wb_ref[...]))
    y_ref[...] = x_ref[...] + _rms(_bdot(merged, wo_ref[...]), g_ref[...])


def _mlp_kernel(x_ref, wu_ref, wd_ref, g1_ref, g2_ref, y_ref):
    x = x_ref[...]
    u = _bdot(_rms(x, g1_ref[...]), wu_ref[...])
    u = jnp.square(jnp.maximum(u, 0.0))
    y_ref[...] = x + _rms(_bdot(u, wd_ref[...]), g2_ref[...])


def _rowwise_call(kernel, name, row_inputs, const_inputs, out_width, tm):
    n = row_inputs[0].shape[0]
    assert n % tm == 0
    row = lambda a: pl.BlockSpec((tm, a.shape[1]), lambda i: (i, 0))
    full = lambda a: pl.BlockSpec(a.shape, lambda i: (0,) * a.ndim)
    return pl.pallas_call(
        kernel,
        grid=(n // tm,),
        in_specs=[row(a) for a in row_inputs] + [full(a) for a in const_inputs],
        out_specs=pl.BlockSpec((tm, out_width), lambda i: (i, 0)),
        out_shape=jax.ShapeDtypeStruct((n, out_width), F32),
        compiler_params=_params("parallel"),
        name=name,
    )(*row_inputs, *const_inputs)


def _gla_chunk(t):
    c = SUBLANES
    while c * 2 <= min(t, LANES) and t % (c * 2) == 0:
        c *= 2
    return c


def kernel(x_prompt, x_sample, cache_k, cache_v, page_table, state_gla, w_in, w_gla_gate, b_gla_gate, g_gla_norm,
           w_branch_a, w_branch_b, w_out, w_up, w_down, g_pre_mix, g_post_mix, g_pre_mlp, g_post_mlp):
    bp, tp, d_model = x_prompt.shape
    bs, ts, _ = x_sample.shape
    depth, n_phys, page, n_head, hd = cache_k.shape
    _, _, n_head_g, dk, dv = state_gla.shape
    w_a = n_head * hd
    qk_g = n_head_g * dk
    v_g = n_head_g * dv
    rank = w_gla_gate.shape[1]
    n_main = 3 * w_a + 2 * qk_g + 2 * v_g

    hp = x_prompt.reshape(bp * tp, d_model)
    hs = x_sample.reshape(bs * ts, d_model)
    outs = [[] for _ in range(6)]
    for l in range(depth):
        wm = w_in[l, :, :n_main].astype(BF16)
        wlr = jnp.pad(w_in[l, :, n_main:n_main + rank], ((0, 0), (0, LANES - rank))).astype(BF16)
        wgt = w_in[l, :, n_main + rank:].astype(BF16)
        wgg = jnp.pad(w_gla_gate[l], ((0, LANES - rank), (0, 0))).astype(BF16)
        bgg = b_gla_gate[l][None, :]
        wa, wb, wo = (w[l].astype(BF16) for w in (w_branch_a, w_branch_b, w_out))
        wu, wd = w_up[l].astype(BF16), w_down[l].astype(BF16)
        g_mix, g_pm, g_mlp, g_pl, g_gn = (g[l][None, :] for g in
                                          (g_pre_mix, g_post_mix, g_pre_mlp, g_post_mlp, g_gla_norm))

        def project(x, kv_seq_len):
            return _proj(x, g_mix, wm, wlr, wgt, wgg, bgg, w_a=w_a, qk_g=qk_g, v_g=v_g, kv_seq_len=kv_seq_len)

        def mix(x, b, t, oa, proj_out, s0):
            qg, kg, vg, rg, la, sga, sgb = proj_out[3:10]
            r3 = lambda a: a.reshape(b, t, a.shape[-1])
            og, s_new = _gla(r3(qg), r3(kg), r3(vg), r3(la), r3(rg), g_gn, s0, chunk=_gla_chunk(t), dk=dk, dv=dv)
            n = b * t
            x = _rowwise_call(_merge_kernel, "merge", [x, oa.reshape(n, w_a), og.reshape(n, v_g), sga, sgb],
                              [wa, wb, wo, g_pm], d_model, 2 * MOBA_BLOCK if n % (2 * MOBA_BLOCK) == 0 else MOBA_BLOCK)
            return _rowwise_call(_mlp_kernel, "mlp", [x], [wu, wd, g_mlp, g_pl], d_model, MOBA_BLOCK), s_new

        pp = project(hp, tp)
        ps = project(hs, None)
        qa_p, kt_p, vt_p, kmean_p, k_bf, vt_bf = pp[0], pp[1], pp[2], pp[10], pp[11], pp[12]
        qa_s, ka_s, va_s = (a.reshape(bs, ts, w_a) for a in ps[:3])
        to_stored = lambda c: jnp.transpose(c[l], (0, 2, 3, 1))
        oa_p, oa_s = _moba(qa_p.reshape(bp, tp, w_a), k_bf.reshape(bp, tp, w_a), vt_bf,
                           kmean_p.reshape(bp, tp // MOBA_BLOCK, w_a), qa_s, ka_s, va_s,
                           to_stored(cache_k), to_stored(cache_v), page_table, hd=hd)
        hp, sp = mix(hp, bp, tp, oa_p, pp, jnp.zeros((bp, n_head_g, dk, dv), state_gla.dtype))
        hs, ssn = mix(hs, bs, ts, oa_s, ps, state_gla[l])
        kp, vp = (jnp.transpose(a.reshape(bp, n_head, hd, tp), (0, 3, 1, 2)) for a in (kt_p, vt_p))
        ksn, vsn = (a.reshape(bs, ts, n_head, hd) for a in (ka_s, va_s))
        for lst, val in zip(outs, (kp, vp, sp, ksn, vsn, ssn)):
            lst.append(val)
    return (hp.reshape(bp, tp, d_model), hs.reshape(bs, ts, d_model)) + tuple(jnp.stack(o) for o in outs)
```

```python
import functools

import ml_dtypes
import numpy as np
import jax
import jax.numpy as jnp
from jax import lax
from jax.experimental import pallas as pl
from jax.experimental.pallas import tpu as pltpu

F32 = jnp.float32
BF16 = jnp.bfloat16

LANES = 128
SUBLANES = 8
VMEM_LIMIT_BYTES = 56 * 1024 * 1024

EPS = 1e-6
MOBA_BLOCK = 256
MOBA_TOPK = 3
GLA_TAU = 16.0
NEG_BIG = -1e30
PENALTY = -30000.0
LOG2E = float(np.log2(np.e))

_NT = (((1,), (1,)), ((), ()))
_TN = (((0,), (0,)), ((), ()))


def _params(*sem):
    return pltpu.CompilerParams(dimension_semantics=sem, vmem_limit_bytes=VMEM_LIMIT_BYTES)


def _sigmoid(x):
    return 1.0 / (1.0 + jnp.exp(-x))


def _rms(x, g):
    return x * lax.rsqrt(jnp.mean(x * x, axis=-1, keepdims=True) + EPS) * g


def _bdot(a, b):
    return jnp.dot(a.astype(BF16), b.astype(BF16), preferred_element_type=F32)


def _bdot_nt(a, b):
    return lax.dot_general(a.astype(BF16), b.astype(BF16), _NT, preferred_element_type=F32)


def _proj_kernel(x_ref, g_ref, wm_ref, wlr_ref, wgt_ref, wgg_ref, bgg_ref,
                 qa_ref, ka_ref, va_ref, qg_ref, kg_ref, vg_ref, rg_ref, la_ref, sga_ref, sgb_ref, kmean_ref,
                 *attn_refs, w_a, qk_g, v_g, d_model, kv_transposed):
    hb = _rms(x_ref[...], g_ref[...]).astype(BF16)

    def proj(lo, n):
        return jnp.dot(hb, wm_ref[:, lo:lo + n], preferred_element_type=F32)

    qa_ref[...] = proj(0, w_a)
    ka = proj(w_a, w_a)
    va = proj(2 * w_a, w_a)
    for i in range(kmean_ref.shape[0]):
        kmean_ref[i] = jnp.sum(ka[i * MOBA_BLOCK:(i + 1) * MOBA_BLOCK], axis=0, keepdims=True) * (1.0 / MOBA_BLOCK)
    if kv_transposed:
        k_bf_ref, vt_bf_ref = attn_refs
        vt = va.T
        ka_ref[0] = ka.T
        va_ref[0] = vt
        k_bf_ref[...] = ka.astype(BF16)
        vt_bf_ref[0] = vt.astype(BF16)
    else:
        ka_ref[...] = ka
        va_ref[...] = va
    lo = 3 * w_a
    qg_ref[...] = proj(lo, qk_g)
    kg_ref[...] = proj(lo + qk_g, qk_g)
    vg_ref[...] = proj(lo + 2 * qk_g, v_g)
    rg_ref[...] = proj(lo + 2 * qk_g + v_g, v_g)
    lr = jnp.dot(hb, wlr_ref[...], preferred_element_type=F32)
    xg = jnp.dot(lr.astype(BF16), wgg_ref[...], preferred_element_type=F32) + bgg_ref[...]
    log_sig = jnp.minimum(xg, 0.0) - jnp.log1p(jnp.exp(-jnp.abs(xg)))
    la_ref[...] = log_sig * (1.0 / GLA_TAU)
    ga = jnp.dot(hb, wgt_ref[:, :d_model], preferred_element_type=F32)
    sga_ref[...] = _sigmoid(ga).astype(BF16)
    gb = jnp.dot(hb, wgt_ref[:, d_model:], preferred_element_type=F32)
    sgb_ref[...] = _sigmoid(gb).astype(BF16)


def _proj(x, g, wm, wlr, wgt, wgg, bgg, *, w_a, qk_g, v_g, kv_seq_len=None):
    n, d_model = x.shape
    tm = _row_tile(n)
    assert n % tm == 0
    nt = n // tm
    blocks_per_tile = tm // MOBA_BLOCK
    row = lambda w: pl.BlockSpec((tm, w), lambda i: (i, 0))
    full = lambda a: pl.BlockSpec(a.shape, lambda i: (0,) * a.ndim, pipeline_mode=pl.Buffered(1))
    widths = (w_a, w_a, w_a, qk_g, qk_g, v_g, v_g, qk_g)
    out_shape = [jax.ShapeDtypeStruct((n, w), F32) for w in widths]
    out_shape += [jax.ShapeDtypeStruct((n, d_model), BF16)] * 2
    out_shape += [jax.ShapeDtypeStruct((n // MOBA_BLOCK, 1, w_a), F32)]
    out_specs = [row(w) for w in widths] + [row(d_model)] * 2
    out_specs += [pl.BlockSpec((blocks_per_tile, 1, w_a), lambda i: (i, 0, 0))]
    if kv_seq_len is not None:
        assert kv_seq_len % tm == 0
        seq_tiles = kv_seq_len // tm
        transposed = pl.BlockSpec((1, w_a, tm), lambda i: (i // seq_tiles, 0, i % seq_tiles))
        for i in (1, 2):
            out_shape[i] = jax.ShapeDtypeStruct((n // kv_seq_len, w_a, kv_seq_len), F32)
            out_specs[i] = transposed
        out_shape += [jax.ShapeDtypeStruct((n, w_a), BF16),
                      jax.ShapeDtypeStruct((n // kv_seq_len, w_a, kv_seq_len), BF16)]
        out_specs += [row(w_a), transposed]
    return pl.pallas_call(
        functools.partial(_proj_kernel, w_a=w_a, qk_g=qk_g, v_g=v_g, d_model=d_model,
                          kv_transposed=kv_seq_len is not None),
        grid=(nt,),
        in_specs=[row(d_model), full(g), full(wm), full(wlr), full(wgt), full(wgg), full(bgg)],
        out_specs=out_specs,
        out_shape=out_shape,
        compiler_params=_params("parallel"),
        name="proj",
    )(x, g, wm, wlr, wgt, wgg, bgg)


def _top_blocks(gate, candidate, lane):
    lane_f = lane.astype(F32)
    no_lane = float(gate.shape[-1])
    avail = jnp.where(candidate, 1.0, 0.0)
    sel = jnp.zeros(gate.shape, F32)
    for _ in range(MOBA_TOPK):
        cur = jnp.where(avail > 0.0, gate, -jnp.inf)
        mx = jnp.max(cur, axis=-1, keepdims=True)
        cand = jnp.where(avail > 0.0, jnp.where(cur == mx, lane_f, no_lane), no_lane)
        first = jnp.min(cand, axis=-1, keepdims=True)
        pick = lane_f == first
        sel = jnp.where(pick, 1.0, sel)
        avail = jnp.where(pick, 0.0, avail)
    return sel


def _alibi_slopes(n_head):
    return 2.0 ** (-8.0 * (np.arange(n_head) + 1) / n_head)


_SLOPE_PARTS = 3
_AUG_EXTRA = 2 * _SLOPE_PARTS


def _moba_prompt_step(r, n_sub, hp, qi, sl_ref, q_ref, k_ref, vt_ref, kc_ref, km_ref, o_ref,
                      qaug_ref, m_ref, acc_ref, *, hd, tk):
    blk = MOBA_BLOCK
    t = k_ref.shape[1]
    tq = q_ref.shape[1]
    n_blk = t // blk
    per_pair = LANES // hd
    n_acc = acc_ref.shape[1]
    tiles_per_q = tq // tk
    chains = [(hh, cb) for hh in range(per_pair) for cb in range(tiles_per_q)]
    ones = jnp.ones((n_acc - LANES, tk), BF16)

    def tiles(j):
        keys = pl.ds(pl.multiple_of(j * tk, tk), tk)
        kt = jnp.concatenate([k_ref[0, keys, :], kc_ref[keys, :]], axis=1)
        return kt, jnp.concatenate([vt_ref[0, :, keys], ones], axis=0)

    def logits(kt, hh, cb):
        return lax.dot_general(kt, qaug_ref[hh, cb * tk:(cb + 1) * tk, :], _NT, preferred_element_type=F32)

    def absorb(stat, s, va, c):
        m, acc = stat
        m_new = jnp.maximum(m, jnp.max(s, axis=0, keepdims=True) + c)
        p = jnp.exp2(s - (m_new - c)).astype(BF16)
        return m_new, jnp.exp2(m - m_new) * acc + jnp.dot(va, p, preferred_element_type=F32)

    @pl.when(r == 0)
    def _():
        lane = lax.broadcasted_iota(jnp.int32, (tq, LANES), 1)
        own = qi * (tq // blk) + lax.broadcasted_iota(jnp.int32, (tq, LANES), 0) // blk
        q = q_ref[0]
        for hh in range(per_pair):
            h = hp * per_pair + hh
            head_lanes = (lane >= hh * hd) & (lane < (hh + 1) * hd)
            qh = jnp.where(head_lanes, q, 0.0)
            gate = _bdot_nt(qh, km_ref[0])
            sel = _top_blocks(gate, lane < own, lane)
            parts = [sl_ref[1 + i, h] for i in range(_SLOPE_PARTS)]
            qx = jnp.where(lane < n_blk, jnp.where((sel > 0.0) | (lane == own), 0.0, PENALTY), 0.0)
            for off, val in enumerate(parts + [part * blk for part in parts]):
                qx = jnp.where(lane == n_blk + off, val, qx)
            qaug_ref[hh, :, :LANES] = (qh * (hd ** -0.5 * LOG2E)).astype(BF16)
            qaug_ref[hh, :, LANES:] = qx.astype(BF16)

        key2 = lax.broadcasted_iota(jnp.int32, (tk, tk), 0)
        qry2 = lax.broadcasted_iota(jnp.int32, (tk, tk), 1)
        stats = {c: (jnp.full((1, tk), NEG_BIG, F32), jnp.zeros((n_acc, tk), F32)) for c in chains}
        for d in range(tiles_per_q):
            kt, va = tiles(qi * tiles_per_q + d)
            live = [(hh, cb) for hh, cb in chains if cb >= d]
            ss = [logits(kt, hh, cb) for hh, cb in live]
            for (hh, cb), s in zip(live, ss):
                if cb == d:
                    s = jnp.where(key2 <= qry2, s, NEG_BIG)
                stats[hh, cb] = absorb(stats[hh, cb], s, va, 0.0)
        for ci, c in enumerate(chains):
            m_ref[ci], acc_ref[ci] = stats[c]

    def body(i, flat):
        j = r + i * n_sub
        kt, va = tiles(j)
        ss = [logits(kt, hh, cb) for hh, cb in chains]
        out = []
        for ci, (hh, cb) in enumerate(chains):
            c = -(sl_ref[0, hp * per_pair + hh] * tq) * (qi - j // tiles_per_q).astype(F32)
            out += absorb(flat[2 * ci:2 * ci + 2], ss[ci], va, c)
        return tuple(out)

    n_mine = (qi * tiles_per_q - r + n_sub - 1) // n_sub
    flat = lax.fori_loop(0, n_mine, body, tuple(x for ci in range(len(chains)) for x in (m_ref[ci], acc_ref[ci])))
    for ci in range(len(chains)):
        m_ref[ci], acc_ref[ci] = flat[2 * ci], flat[2 * ci + 1]

    @pl.when(r == n_sub - 1)
    def _():
        dim = lax.broadcasted_iota(jnp.int32, (LANES, tk), 0)
        for cb in range(tiles_per_q):
            out = None
            for hh in range(per_pair):
                acc = flat[2 * chains.index((hh, cb)) + 1]
                o_h = acc[:LANES] / acc[LANES:LANES + 1]
                out = o_h if out is None else jnp.where(dim >= hh * hd, o_h, out)
            o_ref[0, cb * tk:(cb + 1) * tk, :] = out.T


def _moba_consts(t, tq):
    n_blk = t // MOBA_BLOCK
    assert n_blk + _AUG_EXTRA <= LANES
    pos = np.arange(t)
    c = np.zeros((t, LANES), np.float32)
    c[pos, pos // MOBA_BLOCK] = 1.0
    for i in range(_SLOPE_PARTS):
        c[:, n_blk + i] = pos % MOBA_BLOCK
        c[:, n_blk + _SLOPE_PARTS + i] = (pos % tq) // MOBA_BLOCK
    return jnp.asarray(c, dtype=BF16)


_BF16_ROWS = 2 * SUBLANES


def _moba_sample_step(j, n_k_steps, slopes_ref, q_ref, kn_ref, vn_ref, pages_ref,
                      o_ref, qbd_ref, s_ref, p_ref, ksum_ref, l_ref, acc_ref, *, pages_per_block, hd):
    n_per_step = pages_ref.shape[0]
    qk_pages, pv_pages = 4, 2
    ts, w = q_ref.shape[1:]
    n_head = w // hd
    rows = n_head * ts
    page = pages_ref.shape[3]
    blk = page * pages_per_block
    past = s_ref.shape[1]
    n_blk = past // blk
    scale = hd ** -0.5

    row_head = lax.broadcasted_iota(jnp.int32, (rows, 1), 0) // ts
    row_q = lax.broadcasted_iota(jnp.int32, (rows, 1), 0) % ts
    col_head = lax.broadcasted_iota(jnp.int32, (rows, w), 1) // hd
    slope = jnp.zeros((rows, 1), F32)
    for h in range(n_head):
        slope = jnp.where(row_head == h, slopes_ref[h], slope)

    @pl.when(j == 0)
    def _():
        qt = jnp.concatenate([q_ref[0]] * n_head, axis=0)
        qbd_ref[...] = jnp.where(col_head == row_head, qt, 0.0)
        ksum_ref[...] = jnp.zeros(ksum_ref.shape, F32)

    @pl.when(j < n_k_steps)
    def _():
        qs = (qbd_ref[...] * scale).astype(BF16)
        lane = lax.broadcasted_iota(jnp.int32, ksum_ref.shape, 1)
        ksum = ksum_ref[...]
        group = qk_pages if n_per_step % qk_pages == 0 and qk_pages % pages_per_block == 0 else pages_per_block
        for g0 in range(0, n_per_step, group):
            kts = [pages_ref[p_i].reshape(w, page) for p_i in range(g0, g0 + group)]
            first = j * n_per_step + g0
            s = jnp.dot(qs, jnp.concatenate([kt.astype(BF16) for kt in kts], axis=1), preferred_element_type=F32)
            s_ref[:, pl.ds(pl.multiple_of(first * page, group * page), group * page)] = s
            for b0 in range(0, group, pages_per_block):
                total = kts[b0]
                for kt in kts[b0 + 1:b0 + pages_per_block]:
                    total = total + kt
                ksum = jnp.where(lane == (first + b0) // pages_per_block, jnp.sum(total, axis=1, keepdims=True), ksum)
        ksum_ref[...] = ksum

    @pl.when(j == n_k_steps)
    def _():
        qbd = qbd_ref[...]
        gate = _bdot(qbd, ksum_ref[...])
        lane = lax.broadcasted_iota(jnp.int32, gate.shape, 1)
        sel = _top_blocks(gate, lane < n_blk, lane)
        s_own = _bdot_nt(qbd * scale, kn_ref[0])
        kq = lax.broadcasted_iota(jnp.int32, s_own.shape, 1)
        s_own = jnp.where(kq <= row_q, s_own + slope * kq.astype(F32), NEG_BIG)
        key_in_blk = lax.broadcasted_iota(jnp.int32, (1, blk), 1)

        def add_bias(b_i, m_wide):
            cols = pl.ds(pl.multiple_of(b_i * blk, blk), blk)
            picked = jnp.max(jnp.where(lane == b_i, sel, 0.0), axis=-1, keepdims=True)
            rel = (b_i * blk - past + key_in_blk).astype(F32)
            s = s_ref[:, cols] + slope * rel + jnp.where(picked > 0.0, 0.0, NEG_BIG)
            s_ref[:, cols] = s
            return jnp.maximum(m_wide, s)

        unroll = 8 if n_blk % 8 == 0 else 1
        m_wide = lax.fori_loop(0, n_blk, add_bias, jnp.full((rows, blk), NEG_BIG, F32), unroll=unroll)
        m = jnp.maximum(jnp.max(m_wide, axis=-1, keepdims=True), jnp.max(s_own, axis=-1, keepdims=True))

        def probs(b_i, l_wide):
            cols = pl.ds(pl.multiple_of(b_i * blk, blk), blk)
            p = jnp.exp(s_ref[:, cols] - m)
            p_ref[:, cols] = p.astype(BF16)
            return l_wide + p

        l_wide = lax.fori_loop(0, n_blk, probs, jnp.zeros((rows, blk), F32), unroll=unroll)
        p_own = jnp.exp(s_own - m)
        l_ref[...] = jnp.sum(l_wide, axis=-1, keepdims=True) + jnp.sum(p_own, axis=-1, keepdims=True)
        acc_ref[...] = _bdot(p_own, vn_ref[0])

    @pl.when(j >= n_k_steps)
    def _():
        acc = acc_ref[...]
        group = pv_pages if n_per_step % pv_pages == 0 else 1
        for g0 in range(0, n_per_step, group):
            first = (j - n_k_steps) * n_per_step + g0
            p = p_ref[:, pl.ds(pl.multiple_of(first * page, group * page), group * page)]
            vt = jnp.concatenate([pages_ref[p_i].reshape(w, page).astype(BF16) for p_i in range(g0, g0 + group)], axis=1)
            acc = acc + lax.dot_general(p, vt, _NT, preferred_element_type=F32)
        acc_ref[...] = acc

    @pl.when(j == 2 * n_k_steps - 1)
    def _():
        res = jnp.where(col_head == row_head, acc_ref[...] / l_ref[...], 0.0)
        out = res[0:ts]
        for h in range(1, n_head):
            out = out + res[h * ts:(h + 1) * ts]
        o_ref[0] = out


def _moba_kernel(pt_ref, sl_ref, slopes_ref, q_ref, k_ref, vt_ref, kc_ref, km_ref, qs_ref, kn_ref, vn_ref,
                 cache_k_hbm, cache_v_hbm, o_ref, os_ref, qaug_ref, m_ref, acc_ref, pages_ref, sem, *sample_scratch,
                 pages_per_block, hd, tk, n_k_steps):
    hp, qi, r = pl.program_id(1), pl.program_id(2), pl.program_id(3)
    n_sub = pl.num_programs(3)
    step = ((pl.program_id(0) * pl.num_programs(1) + hp) * pl.num_programs(2) + qi) * n_sub + r
    n_steps = pl.num_programs(0) * pl.num_programs(1) * pl.num_programs(2) * n_sub
    n_per_step = pages_ref.shape[1]
    per_seq = 2 * n_k_steps

    def page_copies(s):
        seq, j, slot = s // per_seq, s % per_seq, s % 2

        def copies(cache_hbm, group):
            return [pltpu.make_async_copy(cache_hbm.at[pt_ref[seq, group * n_per_step + p]],
                                          pages_ref.at[slot, p], sem.at[slot]) for p in range(n_per_step)]
        return (j < n_k_steps, lambda: copies(cache_k_hbm, j)), (j >= n_k_steps, lambda: copies(cache_v_hbm, j - n_k_steps))

    def for_pages(s, action):
        for cond, make in page_copies(s):
            @pl.when(cond)
            def _(make=make):
                for copy in make():
                    action(copy)

    @pl.when(step == 0)
    def _():
        for_pages(step, lambda copy: copy.start())

    @pl.when(step + 1 < n_steps)
    def _():
        for_pages(step + 1, lambda copy: copy.start())

    _moba_prompt_step(r, n_sub, hp, qi, sl_ref, q_ref, k_ref, vt_ref, kc_ref, km_ref, o_ref,
                      qaug_ref, m_ref, acc_ref, hd=hd, tk=tk)
    pltpu.make_async_copy(cache_k_hbm.at[pl.ds(0, n_per_step)], pages_ref.at[step % 2], sem.at[step % 2]).wait()
    _moba_sample_step(step % per_seq, n_k_steps, slopes_ref, qs_ref, kn_ref, vn_ref, pages_ref.at[step % 2],
                      os_ref, *sample_scratch, pages_per_block=pages_per_block, hd=hd)


def _moba(qa, k_bf, vt_bf, kmean, qa_s, ka_s, va_s, cache_kt, cache_vt, page_table, *, hd):
    b, t, w = qa.shape
    bs, ts, _ = qa_s.shape
    _, n_head, _, page = cache_kt.shape
    n_pages = page_table.shape[1]
    blk = MOBA_BLOCK
    ppb = blk // page
    n_pair = w // LANES
    n_blk = t // blk
    tk = 2 * blk if t % (2 * blk) == 0 else blk
    tq = 2 * tk if t % (2 * tk) == 0 else tk
    n_q = t // tq
    n_per_step = next(p for p in (32, 16, 8, 4, 2) if n_pages % p == 0 and p % ppb == 0
                      and (bs * 2 * (n_pages // p)) % (b * n_pair * n_q) == 0)
    n_k_steps = n_pages // n_per_step
    n_sub = bs * 2 * n_k_steps // (b * n_pair * n_q)
    assert n_pages // ppb <= LANES
    rows = n_head * ts

    sl2 = _alibi_slopes(n_head) * LOG2E
    parts, rest = [], sl2
    for _ in range(_SLOPE_PARTS):
        parts.append(rest.astype(ml_dtypes.bfloat16).astype(np.float64))
        rest = rest - parts[-1]
    sl = jnp.asarray(np.stack([sl2] + parts), dtype=F32)
    slopes = jnp.asarray(_alibi_slopes(n_head), dtype=F32)
    kc = _moba_consts(t, tq)
    km = jnp.pad(kmean, ((0, 0), (0, LANES - n_blk), (0, 0)))

    def sample_step(bi, hp, qi, r):
        return ((bi * n_pair + hp) * n_q + qi) * n_sub + r

    def per_seq(a):
        return pl.BlockSpec((1,) + a.shape[1:], lambda bi, hp, qi, r, pt: (sample_step(bi, hp, qi, r) // (2 * n_k_steps), 0, 0))

    smem = pl.BlockSpec(memory_space=pltpu.SMEM)
    hbm = pl.BlockSpec(memory_space=pl.ANY)
    n_chain = (LANES // hd) * (tq // tk)
    return pl.pallas_call(
        functools.partial(_moba_kernel, pages_per_block=ppb, hd=hd, tk=tk, n_k_steps=n_k_steps),
        grid_spec=pltpu.PrefetchScalarGridSpec(
            num_scalar_prefetch=1,
            grid=(b, n_pair, n_q, n_sub),
            in_specs=[smem, smem,
                      pl.BlockSpec((1, tq, LANES), lambda bi, hp, qi, r, pt: (bi, qi, hp)),
                      pl.BlockSpec((1, t, LANES), lambda bi, hp, qi, r, pt: (bi, 0, hp), pipeline_mode=pl.Buffered(1)),
                      pl.BlockSpec((1, LANES, t), lambda bi, hp, qi, r, pt: (bi, hp, 0), pipeline_mode=pl.Buffered(1)),
                      pl.BlockSpec((t, LANES), lambda bi, hp, qi, r, pt: (0, 0), pipeline_mode=pl.Buffered(1)),
                      pl.BlockSpec((1, LANES, LANES), lambda bi, hp, qi, r, pt: (bi, 0, hp)),
                      per_seq(qa_s), per_seq(ka_s), per_seq(va_s), hbm, hbm],
            out_specs=[pl.BlockSpec((1, tq, LANES), lambda bi, hp, qi, r, pt: (bi, qi, hp)), per_seq(qa_s)],
            scratch_shapes=[pltpu.VMEM((LANES // hd, tq, 2 * LANES), BF16),
                            pltpu.VMEM((n_chain, 1, tk), F32),
                            pltpu.VMEM((n_chain, LANES + _BF16_ROWS, tk), F32),
                            pltpu.VMEM((2, n_per_step, n_head, hd, page), F32),
                            pltpu.SemaphoreType.DMA((2,)),
                            pltpu.VMEM((rows, w), F32),
                            pltpu.VMEM((rows, n_pages * page), F32),
                            pltpu.VMEM((rows, n_pages * page), BF16),
                            pltpu.VMEM((w, LANES), F32),
                            pltpu.VMEM((rows, 1), F32), pltpu.VMEM((rows, w), F32)],
        ),
        out_shape=[jax.ShapeDtypeStruct((b, t, w), F32), jax.ShapeDtypeStruct((bs, ts, w), F32)],
        compiler_params=_params("arbitrary", "arbitrary", "arbitrary", "arbitrary"),
        name="moba",
    )(page_table, sl, slopes, qa, k_bf, vt_bf, kc, km, qa_s, ka_s, va_s, cache_kt, cache_vt)


def _gla_kernel(q_ref, k_ref, v_ref, la_ref, rg_ref, gn_ref, s0_ref, o_ref, sout_ref, st_ref, *, dk, dv):
    c_idx = pl.program_id(1)
    n_chunks = pl.num_programs(1)
    c = q_ref.shape[1]
    n_pair = q_ref.shape[2] // LANES
    per_pair = LANES // dk
    assert dv == LANES and c % SUBLANES == 0

    @pl.when(c_idx == 0)
    def _():
        for p in range(n_pair):
            st_ref[p] = s0_ref[0, p].T

    row = lax.broadcasted_iota(jnp.int32, (c, LANES), 0)
    lane = lax.broadcasted_iota(jnp.int32, (c, LANES), 1)
    row2 = lax.broadcasted_iota(jnp.int32, (c, c), 0)
    col2 = lax.broadcasted_iota(jnp.int32, (c, c), 1)
    tri = jnp.where(row2 >= col2, 1.0, 0.0)
    b_all = jnp.dot(tri, la_ref[0], precision=lax.Precision.HIGHEST, preferred_element_type=F32)
    nb8 = c // SUBLANES
    sub = lax.broadcasted_iota(jnp.int32, (nb8, SUBLANES, LANES), 1)
    lane3 = lax.broadcasted_iota(jnp.int32, (nb8, SUBLANES, LANES), 2)
    lane_st = lax.broadcasted_iota(jnp.int32, (dv, LANES), 1)

    def bcast_row(x3, jj):
        return jnp.broadcast_to(x3[:, jj:jj + 1, :], x3.shape)

    for p in range(n_pair):
        cols = slice(p * LANES, (p + 1) * LANES)
        q = q_ref[0, :, cols] * (dk ** -0.5)
        k = k_ref[0, :, cols]
        b = b_all[:, cols]
        vs = [v_ref[0, :, (p * per_pair + hh) * dv:(p * per_pair + hh + 1) * dv] for hh in range(per_pair)]
        head_lanes = [(lane >= hh * dk) & (lane < (hh + 1) * dk) for hh in range(per_pair)]
        head_lanes3 = [(lane3 >= hh * dk) & (lane3 < (hh + 1) * dk) for hh in range(per_pair)]
        st = st_ref[p]

        qe = q * jnp.exp(b)
        o = [_bdot_nt(jnp.where(head_lanes[hh], qe, 0.0), st) for hh in range(per_pair)]

        q3 = q.reshape(nb8, SUBLANES, LANES)
        k3 = k.reshape(nb8, SUBLANES, LANES)
        b3 = b.reshape(nb8, SUBLANES, LANES)
        v3 = [v.reshape(nb8, SUBLANES, dv) for v in vs]
        o3 = [jnp.zeros((nb8, SUBLANES, dv), F32) for _ in range(per_pair)]
        for jj in range(SUBLANES):
            e = jnp.exp(jnp.minimum(b3 - bcast_row(b3, jj), 0.0))
            term = jnp.where(sub >= jj, q3 * bcast_row(k3, jj) * e, 0.0)
            for hh in range(per_pair):
                a = jnp.sum(jnp.where(head_lanes3[hh], term, 0.0), axis=-1, keepdims=True)
                o3[hh] = o3[hh] + a * bcast_row(v3[hh], jj)
        o = [o[hh] + o3[hh].reshape(c, dv) for hh in range(per_pair)]

        attn = [jnp.zeros((c, c), F32) for _ in range(per_pair)]
        m_half = SUBLANES
        while 2 * m_half <= c:
            span = 2 * m_half
            bnd = jnp.broadcast_to(b.reshape(c // span, span, LANES)[:, m_half - 1:m_half, :],
                                   (c // span, span, LANES)).reshape(c, LANES)
            upper = (row % span) >= m_half
            qm = jnp.where(upper, q * jnp.exp(jnp.minimum(b - bnd, 0.0)), 0.0)
            km = jnp.where(upper, 0.0, k * jnp.exp(jnp.minimum(bnd - b, 0.0)))
            same = (row2 // span) == (col2 // span)
            for hh in range(per_pair):
                a = _bdot_nt(jnp.where(head_lanes[hh], qm, 0.0), km)
                attn[hh] = attn[hh] + jnp.where(same, a, 0.0)
            m_half = span
        if c > SUBLANES:
            o = [o[hh] + _bdot(attn[hh], vs[hh]) for hh in range(per_pair)]

        b_last = b[c - 1:c, :]
        kk = (k * jnp.exp(b_last - b)).astype(BF16)
        upd = lax.dot_general(vs[0].astype(BF16), kk, _TN, preferred_element_type=F32)
        for hh in range(1, per_pair):
            u = lax.dot_general(vs[hh].astype(BF16), kk, _TN, preferred_element_type=F32)
            upd = jnp.where(lane_st >= hh * dk, u, upd)
        st_new = st * jnp.exp(b_last) + upd
        st_ref[p] = st_new

        for hh in range(per_pair):
            h = p * per_pair + hh
            rg = rg_ref[0, :, h * dv:(h + 1) * dv]
            o_ref[0, :, h * dv:(h + 1) * dv] = _rms(o[hh], gn_ref[...]) * (rg * _sigmoid(rg))

        @pl.when(c_idx == n_chunks - 1)
        def _(p=p, st_new=st_new):
            sout_ref[0, p] = st_new.T


def _gla(qg, kg, vg, la, rg, g_norm, s0, *, chunk, dk, dv):
    b, t, wk = qg.shape
    wv = vg.shape[2]
    n_head = wk // dk
    n_pair = wk // LANES
    s0p = s0.reshape(b, n_pair, LANES, dv)
    tok = lambda w: pl.BlockSpec((1, chunk, w), lambda bi, ci: (bi, ci, 0))
    st_spec = pl.BlockSpec((1, n_pair, LANES, dv), lambda bi, ci: (bi, 0, 0, 0))
    og, s_new = pl.pallas_call(
        functools.partial(_gla_kernel, dk=dk, dv=dv),
        grid=(b, t // chunk),
        in_specs=[tok(wk), tok(wk), tok(wv), tok(wk), tok(wv),
                  pl.BlockSpec(g_norm.shape, lambda bi, ci: (0, 0)), st_spec],
        out_specs=[tok(wv), st_spec],
        out_shape=[jax.ShapeDtypeStruct((b, t, wv), F32), jax.ShapeDtypeStruct((b, n_pair, LANES, dv), F32)],
        scratch_shapes=[pltpu.VMEM((n_pair, dv, LANES), F32)],
        compiler_params=_params("parallel", "arbitrary"),
        name="gla",
    )(qg, kg, vg, la, rg, g_norm, s0p)
    return og, s_new.reshape(b, n_head, dk, dv)


def _merge_kernel(x_ref, oa_ref, og_ref, sga_ref, sgb_ref, wa_ref, wb_ref, wo_ref, g_ref, y_ref):
    merged = (sga_ref[...].astype(F32) * _bdot(oa_ref[...], wa_ref[...])
              + sgb_ref[...].astype(F32) * _bdot(og_ref[...], wb_ref[...]))
    y_ref[...] = x_ref[...] + _rms(_bdot(merged, wo_ref[...]), g_ref[...])


def _mlp_kernel(x_ref, wu_ref, wd_ref, g1_ref, g2_ref, y_ref):
    x = x_ref[...]
    u = _bdot(_rms(x, g1_ref[...]), wu_ref[...])
    u = jnp.square(jnp.maximum(u, 0.0))
    y_ref[...] = x + _rms(_bdot(u, wd_ref[...]), g2_ref[...])


def _row_tile(n):
    return 2 * MOBA_BLOCK if n % (2 * MOBA_BLOCK) == 0 else MOBA_BLOCK


def _rowwise_call(kernel, name, row_inputs, const_inputs, out_width):
    n = row_inputs[0].shape[0]
    tm = _row_tile(n)
    assert n % tm == 0
    row = lambda a: pl.BlockSpec((tm, a.shape[1]), lambda i: (i, 0))
    full = lambda a: pl.BlockSpec(a.shape, lambda i: (0,) * a.ndim, pipeline_mode=pl.Buffered(1))
    return pl.pallas_call(
        kernel,
        grid=(n // tm,),
        in_specs=[row(a) for a in row_inputs] + [full(a) for a in const_inputs],
        out_specs=pl.BlockSpec((tm, out_width), lambda i: (i, 0)),
        out_shape=jax.ShapeDtypeStruct((n, out_width), F32),
        compiler_params=_params("parallel"),
        name=name,
    )(*row_inputs, *const_inputs)


def _gla_chunk(t):
    c = SUBLANES
    while c * 2 <= min(t, LANES) and t % (c * 2) == 0:
        c *= 2
    return c


def kernel(x_prompt, x_sample, cache_k, cache_v, page_table, state_gla, w_in, w_gla_gate, b_gla_gate, g_gla_norm,
           w_branch_a, w_branch_b, w_out, w_up, w_down, g_pre_mix, g_post_mix, g_pre_mlp, g_post_mlp):
    bp, tp, d_model = x_prompt.shape
    bs, ts, _ = x_sample.shape
    depth, n_phys, page, n_head, hd = cache_k.shape
    _, _, n_head_g, dk, dv = state_gla.shape
    w_a = n_head * hd
    qk_g = n_head_g * dk
    v_g = n_head_g * dv
    rank = w_gla_gate.shape[1]
    n_main = 3 * w_a + 2 * qk_g + 2 * v_g

    hp = x_prompt.reshape(bp * tp, d_model)
    hs = x_sample.reshape(bs * ts, d_model)
    outs = [[] for _ in range(6)]
    for l in range(depth):
        wm = w_in[l, :, :n_main].astype(BF16)
        wlr = jnp.pad(w_in[l, :, n_main:n_main + rank], ((0, 0), (0, LANES - rank))).astype(BF16)
        wgt = w_in[l, :, n_main + rank:].astype(BF16)
        wgg = jnp.pad(w_gla_gate[l], ((0, LANES - rank), (0, 0))).astype(BF16)
        bgg = b_gla_gate[l][None, :]
        wa, wb, wo = (w[l].astype(BF16) for w in (w_branch_a, w_branch_b, w_out))
        wu, wd = w_up[l].astype(BF16), w_down[l].astype(BF16)
        g_mix, g_pm, g_mlp, g_pl, g_gn = (g[l][None, :] for g in
                                          (g_pre_mix, g_post_mix, g_pre_mlp, g_post_mlp, g_gla_norm))

        def project(x, kv_seq_len):
            return _proj(x, g_mix, wm, wlr, wgt, wgg, bgg, w_a=w_a, qk_g=qk_g, v_g=v_g, kv_seq_len=kv_seq_len)

        def mix(x, b, t, oa, proj_out, s0):
            qg, kg, vg, rg, la, sga, sgb = proj_out[3:10]
            r3 = lambda a: a.reshape(b, t, a.shape[-1])
            og, s_new = _gla(r3(qg), r3(kg), r3(vg), r3(la), r3(rg), g_gn, s0, chunk=_gla_chunk(t), dk=dk, dv=dv)
            n = b * t
            x = _rowwise_call(_merge_kernel, "merge", [x, oa.reshape(n, w_a), og.reshape(n, v_g), sga, sgb],
                              [wa, wb, wo, g_pm], d_model)
            return _rowwise_call(_mlp_kernel, "mlp", [x], [wu, wd, g_mlp, g_pl], d_model), s_new

        pp = project(hp, tp)
        ps = project(hs, None)
        qa_p, kt_p, vt_p, kmean_p, k_bf, vt_bf = pp[0], pp[1], pp[2], pp[10], pp[11], pp[12]
        qa_s, ka_s, va_s = (a.reshape(bs, ts, w_a) for a in ps[:3])
        to_stored = lambda c: jnp.transpose(c[l], (0, 2, 3, 1))
        oa_p, oa_s = _moba(qa_p.reshape(bp, tp, w_a), k_bf.reshape(bp, tp, w_a), vt_bf,
                           kmean_p.reshape(bp, tp // MOBA_BLOCK, w_a), qa_s, ka_s, va_s,
                           to_stored(cache_k), to_stored(cache_v), page_table, hd=hd)
        hp, sp = mix(hp, bp, tp, oa_p, pp, jnp.zeros((bp, n_head_g, dk, dv), state_gla.dtype))
        hs, ssn = mix(hs, bs, ts, oa_s, ps, state_gla[l])
        kp, vp = (jnp.transpose(a.reshape(bp, n_head, hd, tp), (0, 3, 1, 2)) for a in (kt_p, vt_p))
        ksn, vsn = (a.reshape(bs, ts, n_head, hd) for a in (ka_s, va_s))
        for lst, val in zip(outs, (kp, vp, sp, ksn, vsn, ssn)):
            lst.append(val)
    return (hp.reshape(bp, tp, d_model), hs.reshape(bs, ts, d_model)) + tuple(jnp.stack(o) for o in outs)
```

```python
import functools

import ml_dtypes
import numpy as np
import jax
import jax.numpy as jnp
from jax import lax
from jax.experimental import pallas as pl
from jax.experimental.pallas import tpu as pltpu

F32 = jnp.float32
BF16 = jnp.bfloat16

LANES = 128
SUBLANES = 8
VMEM_LIMIT_BYTES = 56 * 1024 * 1024

EPS = 1e-6
MOBA_BLOCK = 256
MOBA_TOPK = 3
GLA_TAU = 16.0
NEG_BIG = -1e30
PENALTY = -30000.0
LOG2E = float(np.log2(np.e))

_NT = (((1,), (1,)), ((), ()))
_TN = (((0,), (0,)), ((), ()))


def _params(*sem):
    return pltpu.CompilerParams(dimension_semantics=sem, vmem_limit_bytes=VMEM_LIMIT_BYTES)


def _sigmoid(x):
    return 1.0 / (1.0 + jnp.exp(-x))


def _rms(x, g):
    return x * lax.rsqrt(jnp.mean(x * x, axis=-1, keepdims=True) + EPS) * g


def _bdot(a, b):
    return jnp.dot(a.astype(BF16), b.astype(BF16), preferred_element_type=F32)


def _bdot_nt(a, b):
    return lax.dot_general(a.astype(BF16), b.astype(BF16), _NT, preferred_element_type=F32)


def _proj_kernel(x_ref, g_ref, wm_ref, wlr_ref, wgt_ref, wgg_ref, bgg_ref,
                 qa_ref, ka_ref, va_ref, qg_ref, kg_ref, vg_ref, rg_ref, la_ref, sga_ref, sgb_ref, kmean_ref,
                 *attn_refs, w_a, qk_g, v_g, d_model, kv_transposed):
    hb = _rms(x_ref[...], g_ref[...]).astype(BF16)

    def proj(lo, n):
        return jnp.dot(hb, wm_ref[:, lo:lo + n], preferred_element_type=F32)

    qa_ref[...] = proj(0, w_a)
    ka = proj(w_a, w_a)
    va = proj(2 * w_a, w_a)
    for i in range(kmean_ref.shape[0]):
        kmean_ref[i] = jnp.sum(ka[i * MOBA_BLOCK:(i + 1) * MOBA_BLOCK], axis=0, keepdims=True) * (1.0 / MOBA_BLOCK)
    if kv_transposed:
        k_bf_ref, vt_bf_ref = attn_refs
        vt = va.T
        ka_ref[0] = ka.T
        va_ref[0] = vt
        k_bf_ref[...] = ka.astype(BF16)
        vt_bf_ref[0] = vt.astype(BF16)
    else:
        ka_ref[...] = ka
        va_ref[...] = va
    lo = 3 * w_a
    qg_ref[...] = proj(lo, qk_g)
    kg_ref[...] = proj(lo + qk_g, qk_g)
    vg_ref[...] = proj(lo + 2 * qk_g, v_g)
    rg_ref[...] = proj(lo + 2 * qk_g + v_g, v_g)
    lr = jnp.dot(hb, wlr_ref[...], preferred_element_type=F32)
    xg = jnp.dot(lr.astype(BF16), wgg_ref[...], preferred_element_type=F32) + bgg_ref[...]
    log_sig = jnp.minimum(xg, 0.0) - jnp.log1p(jnp.exp(-jnp.abs(xg)))
    la_ref[...] = log_sig * (1.0 / GLA_TAU)
    ga = jnp.dot(hb, wgt_ref[:, :d_model], preferred_element_type=F32)
    sga_ref[...] = _sigmoid(ga).astype(BF16)
    gb = jnp.dot(hb, wgt_ref[:, d_model:], preferred_element_type=F32)
    sgb_ref[...] = _sigmoid(gb).astype(BF16)


def _proj(x, g, wm, wlr, wgt, wgg, bgg, *, w_a, qk_g, v_g, kv_seq_len=None):
    n, d_model = x.shape
    tm = _row_tile(n)
    assert n % tm == 0
    nt = n // tm
    blocks_per_tile = tm // MOBA_BLOCK
    row = lambda w: pl.BlockSpec((tm, w), lambda i: (i, 0))
    full = lambda a: pl.BlockSpec(a.shape, lambda i: (0,) * a.ndim, pipeline_mode=pl.Buffered(1))
    widths = (w_a, w_a, w_a, qk_g, qk_g, v_g, v_g, qk_g)
    out_shape = [jax.ShapeDtypeStruct((n, w), F32) for w in widths]
    out_shape += [jax.ShapeDtypeStruct((n, d_model), BF16)] * 2
    out_shape += [jax.ShapeDtypeStruct((n // MOBA_BLOCK, 1, w_a), F32)]
    out_specs = [row(w) for w in widths] + [row(d_model)] * 2
    out_specs += [pl.BlockSpec((blocks_per_tile, 1, w_a), lambda i: (i, 0, 0))]
    if kv_seq_len is not None:
        assert kv_seq_len % tm == 0
        seq_tiles = kv_seq_len // tm
        transposed = pl.BlockSpec((1, w_a, tm), lambda i: (i // seq_tiles, 0, i % seq_tiles))
        for i in (1, 2):
            out_shape[i] = jax.ShapeDtypeStruct((n // kv_seq_len, w_a, kv_seq_len), F32)
            out_specs[i] = transposed
        out_shape += [jax.ShapeDtypeStruct((n, w_a), BF16),
                      jax.ShapeDtypeStruct((n // kv_seq_len, w_a, kv_seq_len), BF16)]
        out_specs += [row(w_a), transposed]
    return pl.pallas_call(
        functools.partial(_proj_kernel, w_a=w_a, qk_g=qk_g, v_g=v_g, d_model=d_model,
                          kv_transposed=kv_seq_len is not None),
        grid=(nt,),
        in_specs=[row(d_model), full(g), full(wm), full(wlr), full(wgt), full(wgg), full(bgg)],
        out_specs=out_specs,
        out_shape=out_shape,
        compiler_params=_params("parallel"),
        name="proj",
    )(x, g, wm, wlr, wgt, wgg, bgg)


def _top_blocks(gate, candidate, block, axis=-1):
    block_f = block.astype(F32)
    no_block = float(gate.shape[axis])
    avail = jnp.where(candidate, 1.0, 0.0)
    sel = jnp.zeros(gate.shape, F32)
    for _ in range(MOBA_TOPK):
        cur = jnp.where(avail > 0.0, gate, -jnp.inf)
        mx = jnp.max(cur, axis=axis, keepdims=True)
        cand = jnp.where(avail > 0.0, jnp.where(cur == mx, block_f, no_block), no_block)
        first = jnp.min(cand, axis=axis, keepdims=True)
        pick = block_f == first
        sel = jnp.where(pick, 1.0, sel)
        avail = jnp.where(pick, 0.0, avail)
    return sel


def _alibi_slopes(n_head):
    return 2.0 ** (-8.0 * (np.arange(n_head) + 1) / n_head)


_SLOPE_PARTS = 3
_AUG_EXTRA = 2 * _SLOPE_PARTS


def _moba_prompt_step(r, n_sub, hp, qi, sl_ref, q_ref, k_ref, vt_ref, kc_ref, km_ref, o_ref,
                      qaug_ref, m_ref, acc_ref, *, hd, tk):
    blk = MOBA_BLOCK
    t = k_ref.shape[1]
    tq = q_ref.shape[1]
    n_blk = t // blk
    per_pair = LANES // hd
    n_acc = acc_ref.shape[1]
    tiles_per_q = tq // tk
    chains = [(hh, cb) for hh in range(per_pair) for cb in range(tiles_per_q)]
    ones = jnp.ones((n_acc - LANES, tk), BF16)

    def tiles(j):
        keys = pl.ds(pl.multiple_of(j * tk, tk), tk)
        kt = jnp.concatenate([k_ref[0, keys, :], kc_ref[keys, :]], axis=1)
        return kt, jnp.concatenate([vt_ref[0, :, keys], ones], axis=0)

    def logits(kt, hh, cb):
        return lax.dot_general(kt, qaug_ref[hh, cb * tk:(cb + 1) * tk, :], _NT, preferred_element_type=F32)

    def absorb(stat, s, va, c):
        m, acc = stat
        m_new = jnp.maximum(m, jnp.max(s, axis=0, keepdims=True) + c)
        p = jnp.exp2(s - (m_new - c)).astype(BF16)
        return m_new, jnp.exp2(m - m_new) * acc + jnp.dot(va, p, preferred_element_type=F32)

    @pl.when(r == 0)
    def _():
        lane = lax.broadcasted_iota(jnp.int32, (tq, LANES), 1)
        q = q_ref[0]
        n_rows = -(-n_blk // SUBLANES) * SUBLANES
        block = lax.broadcasted_iota(jnp.int32, (n_rows, tq), 0)
        own = qi * (tq // blk) + lax.broadcasted_iota(jnp.int32, (n_rows, tq), 1) // blk
        extra = lax.broadcasted_iota(jnp.int32, (SUBLANES, tq), 0)
        assert _AUG_EXTRA <= SUBLANES and n_rows + SUBLANES <= LANES
        for hh in range(per_pair):
            h = hp * per_pair + hh
            head_lanes = (lane >= hh * hd) & (lane < (hh + 1) * hd)
            qh = jnp.where(head_lanes, q, 0.0)
            gate = _bdot_nt(km_ref[0, :n_rows, :], qh)
            sel = _top_blocks(gate, block < own, block, axis=0)
            penalty = jnp.where((sel > 0.0) | (block == own) | (block >= n_blk), 0.0, PENALTY)
            parts = [sl_ref[1 + i, h] for i in range(_SLOPE_PARTS)]
            slope_rows = jnp.zeros((SUBLANES, tq), F32)
            for off, val in enumerate(parts + [part * blk for part in parts]):
                slope_rows = jnp.where(extra == off, val, slope_rows)
            qx_t = jnp.concatenate([penalty, slope_rows, jnp.zeros((LANES - n_rows - SUBLANES, tq), F32)], axis=0)
            qaug_ref[hh, :, :LANES] = (qh * (hd ** -0.5 * LOG2E)).astype(BF16)
            qaug_ref[hh, :, LANES:] = qx_t.T.astype(BF16)

        key2 = lax.broadcasted_iota(jnp.int32, (tk, tk), 0)
        qry2 = lax.broadcasted_iota(jnp.int32, (tk, tk), 1)
        stats = {c: (jnp.full((1, tk), NEG_BIG, F32), jnp.zeros((n_acc, tk), F32)) for c in chains}
        for d in range(tiles_per_q):
            kt, va = tiles(qi * tiles_per_q + d)
            live = [(hh, cb) for hh, cb in chains if cb >= d]
            ss = [logits(kt, hh, cb) for hh, cb in live]
            for (hh, cb), s in zip(live, ss):
                if cb == d:
                    s = jnp.where(key2 <= qry2, s, NEG_BIG)
                stats[hh, cb] = absorb(stats[hh, cb], s, va, 0.0)
        for ci, c in enumerate(chains):
            m_ref[ci], acc_ref[ci] = stats[c]

    def body(i, flat):
        j = r + i * n_sub
        kt, va = tiles(j)
        ss = [logits(kt, hh, cb) for hh, cb in chains]
        out = []
        for ci, (hh, cb) in enumerate(chains):
            c = -(sl_ref[0, hp * per_pair + hh] * tq) * (qi - j // tiles_per_q).astype(F32)
            out += absorb(flat[2 * ci:2 * ci + 2], ss[ci], va, c)
        return tuple(out)

    n_mine = (qi * tiles_per_q - r + n_sub - 1) // n_sub
    flat = lax.fori_loop(0, n_mine, body, tuple(x for ci in range(len(chains)) for x in (m_ref[ci], acc_ref[ci])))
    for ci in range(len(chains)):
        m_ref[ci], acc_ref[ci] = flat[2 * ci], flat[2 * ci + 1]

    @pl.when(r == n_sub - 1)
    def _():
        dim = lax.broadcasted_iota(jnp.int32, (LANES, tk), 0)
        for cb in range(tiles_per_q):
            out = None
            for hh in range(per_pair):
                acc = flat[2 * chains.index((hh, cb)) + 1]
                o_h = acc[:LANES] / acc[LANES:LANES + 1]
                out = o_h if out is None else jnp.where(dim >= hh * hd, o_h, out)
            o_ref[0, cb * tk:(cb + 1) * tk, :] = out.T


def _moba_consts(t, tq):
    n_blk = t // MOBA_BLOCK
    base = -(-n_blk // SUBLANES) * SUBLANES
    assert base + _AUG_EXTRA <= LANES
    pos = np.arange(t)
    c = np.zeros((t, LANES), np.float32)
    c[pos, pos // MOBA_BLOCK] = 1.0
    for i in range(_SLOPE_PARTS):
        c[:, base + i] = pos % MOBA_BLOCK
        c[:, base + _SLOPE_PARTS + i] = (pos % tq) // MOBA_BLOCK
    return jnp.asarray(c, dtype=BF16)


_BF16_ROWS = 2 * SUBLANES


def _moba_sample_step(j, n_k_steps, slopes_ref, q_ref, kn_ref, vn_ref, pages_ref,
                      o_ref, qbd_ref, s_ref, p_ref, ksum_ref, l_ref, acc_ref, *, pages_per_block, hd):
    n_per_step = pages_ref.shape[0]
    qk_pages, pv_pages = 4, 2
    ts, w = q_ref.shape[1:]
    n_head = w // hd
    rows = n_head * ts
    page = pages_ref.shape[3]
    blk = page * pages_per_block
    past = s_ref.shape[1]
    n_blk = past // blk
    scale = hd ** -0.5

    row_head = lax.broadcasted_iota(jnp.int32, (rows, 1), 0) // ts
    row_q = lax.broadcasted_iota(jnp.int32, (rows, 1), 0) % ts
    col_head = lax.broadcasted_iota(jnp.int32, (rows, w), 1) // hd
    slope = jnp.zeros((rows, 1), F32)
    for h in range(n_head):
        slope = jnp.where(row_head == h, slopes_ref[h], slope)

    @pl.when(j == 0)
    def _():
        qt = jnp.concatenate([q_ref[0]] * n_head, axis=0)
        qbd_ref[...] = jnp.where(col_head == row_head, qt, 0.0)
        ksum_ref[...] = jnp.zeros(ksum_ref.shape, F32)

    @pl.when(j < n_k_steps)
    def _():
        qs = (qbd_ref[...] * scale).astype(BF16)
        lane = lax.broadcasted_iota(jnp.int32, ksum_ref.shape, 1)
        ksum = ksum_ref[...]
        group = qk_pages if n_per_step % qk_pages == 0 and qk_pages % pages_per_block == 0 else pages_per_block
        for g0 in range(0, n_per_step, group):
            kts = [pages_ref[p_i].reshape(w, page) for p_i in range(g0, g0 + group)]
            first = j * n_per_step + g0
            s = jnp.dot(qs, jnp.concatenate([kt.astype(BF16) for kt in kts], axis=1), preferred_element_type=F32)
            s_ref[:, pl.ds(pl.multiple_of(first * page, group * page), group * page)] = s
            for b0 in range(0, group, pages_per_block):
                total = kts[b0]
                for kt in kts[b0 + 1:b0 + pages_per_block]:
                    total = total + kt
                ksum = jnp.where(lane == (first + b0) // pages_per_block, jnp.sum(total, axis=1, keepdims=True), ksum)
        ksum_ref[...] = ksum

    @pl.when(j == n_k_steps)
    def _():
        qbd = qbd_ref[...]
        gate = _bdot(qbd, ksum_ref[...])
        lane = lax.broadcasted_iota(jnp.int32, gate.shape, 1)
        sel = _top_blocks(gate, lane < n_blk, lane)
        s_own = _bdot_nt(qbd * scale, kn_ref[0])
        kq = lax.broadcasted_iota(jnp.int32, s_own.shape, 1)
        s_own = jnp.where(kq <= row_q, s_own + slope * kq.astype(F32), NEG_BIG)
        key_in_blk = lax.broadcasted_iota(jnp.int32, (1, blk), 1)

        def add_bias(b_i, m_wide):
            cols = pl.ds(pl.multiple_of(b_i * blk, blk), blk)
            picked = jnp.max(jnp.where(lane == b_i, sel, 0.0), axis=-1, keepdims=True)
            rel = (b_i * blk - past + key_in_blk).astype(F32)
            s = s_ref[:, cols] + slope * rel + jnp.where(picked > 0.0, 0.0, NEG_BIG)
            s_ref[:, cols] = s
            return jnp.maximum(m_wide, s)

        unroll = 8 if n_blk % 8 == 0 else 1
        m_wide = lax.fori_loop(0, n_blk, add_bias, jnp.full((rows, blk), NEG_BIG, F32), unroll=unroll)
        m = jnp.maximum(jnp.max(m_wide, axis=-1, keepdims=True), jnp.max(s_own, axis=-1, keepdims=True))

        def probs(b_i, l_wide):
            cols = pl.ds(pl.multiple_of(b_i * blk, blk), blk)
            p = jnp.exp(s_ref[:, cols] - m)
            p_ref[:, cols] = p.astype(BF16)
            return l_wide + p

        l_wide = lax.fori_loop(0, n_blk, probs, jnp.zeros((rows, blk), F32), unroll=unroll)
        p_own = jnp.exp(s_own - m)
        l_ref[...] = jnp.sum(l_wide, axis=-1, keepdims=True) + jnp.sum(p_own, axis=-1, keepdims=True)
        acc_ref[...] = _bdot(p_own, vn_ref[0])

    @pl.when(j >= n_k_steps)
    def _():
        acc = acc_ref[...]
        group = pv_pages if n_per_step % pv_pages == 0 else 1
        for g0 in range(0, n_per_step, group):
            first = (j - n_k_steps) * n_per_step + g0
            p = p_ref[:, pl.ds(pl.multiple_of(first * page, group * page), group * page)]
            vt = jnp.concatenate([pages_ref[p_i].reshape(w, page).astype(BF16) for p_i in range(g0, g0 + group)], axis=1)
            acc = acc + lax.dot_general(p, vt, _NT, preferred_element_type=F32)
        acc_ref[...] = acc

    @pl.when(j == 2 * n_k_steps - 1)
    def _():
        res = jnp.where(col_head == row_head, acc_ref[...] / l_ref[...], 0.0)
        out = res[0:ts]
        for h in range(1, n_head):
            out = out + res[h * ts:(h + 1) * ts]
        o_ref[0] = out


def _moba_kernel(pt_ref, sl_ref, slopes_ref, q_ref, k_ref, vt_ref, kc_ref, km_ref, qs_ref, kn_ref, vn_ref,
                 cache_k_hbm, cache_v_hbm, o_ref, os_ref, qaug_ref, m_ref, acc_ref, pages_ref, sem, *sample_scratch,
                 pages_per_block, hd, tk, n_k_steps):
    hp, qi, r = pl.program_id(1), pl.program_id(2), pl.program_id(3)
    n_sub = pl.num_programs(3)
    step = ((pl.program_id(0) * pl.num_programs(1) + hp) * pl.num_programs(2) + qi) * n_sub + r
    n_steps = pl.num_programs(0) * pl.num_programs(1) * pl.num_programs(2) * n_sub
    n_per_step = pages_ref.shape[1]
    per_seq = 2 * n_k_steps

    def page_copies(s):
        seq, j, slot = s // per_seq, s % per_seq, s % 2

        def copies(cache_hbm, group):
            return [pltpu.make_async_copy(cache_hbm.at[pt_ref[seq, group * n_per_step + p]],
                                          pages_ref.at[slot, p], sem.at[slot]) for p in range(n_per_step)]
        return (j < n_k_steps, lambda: copies(cache_k_hbm, j)), (j >= n_k_steps, lambda: copies(cache_v_hbm, j - n_k_steps))

    def for_pages(s, action):
        for cond, make in page_copies(s):
            @pl.when(cond)
            def _(make=make):
                for copy in make():
                    action(copy)

    @pl.when(step == 0)
    def _():
        for_pages(step, lambda copy: copy.start())

    @pl.when(step + 1 < n_steps)
    def _():
        for_pages(step + 1, lambda copy: copy.start())

    _moba_prompt_step(r, n_sub, hp, qi, sl_ref, q_ref, k_ref, vt_ref, kc_ref, km_ref, o_ref,
                      qaug_ref, m_ref, acc_ref, hd=hd, tk=tk)
    pltpu.make_async_copy(cache_k_hbm.at[pl.ds(0, n_per_step)], pages_ref.at[step % 2], sem.at[step % 2]).wait()
    _moba_sample_step(step % per_seq, n_k_steps, slopes_ref, qs_ref, kn_ref, vn_ref, pages_ref.at[step % 2],
                      os_ref, *sample_scratch, pages_per_block=pages_per_block, hd=hd)


def _moba(qa, k_bf, vt_bf, kmean, qa_s, ka_s, va_s, cache_kt, cache_vt, page_table, *, hd):
    b, t, w = qa.shape
    bs, ts, _ = qa_s.shape
    _, n_head, _, page = cache_kt.shape
    n_pages = page_table.shape[1]
    blk = MOBA_BLOCK
    ppb = blk // page
    n_pair = w // LANES
    n_blk = t // blk
    tk = 2 * blk if t % (2 * blk) == 0 else blk
    tq = 2 * tk if t % (2 * tk) == 0 else tk
    n_q = t // tq
    n_per_step = next(p for p in (32, 16, 8, 4, 2) if n_pages % p == 0 and p % ppb == 0
                      and (bs * 2 * (n_pages // p)) % (b * n_pair * n_q) == 0)
    n_k_steps = n_pages // n_per_step
    n_sub = bs * 2 * n_k_steps // (b * n_pair * n_q)
    assert n_pages // ppb <= LANES
    rows = n_head * ts

    sl2 = _alibi_slopes(n_head) * LOG2E
    parts, rest = [], sl2
    for _ in range(_SLOPE_PARTS):
        parts.append(rest.astype(ml_dtypes.bfloat16).astype(np.float64))
        rest = rest - parts[-1]
    sl = jnp.asarray(np.stack([sl2] + parts), dtype=F32)
    slopes = jnp.asarray(_alibi_slopes(n_head), dtype=F32)
    kc = _moba_consts(t, tq)
    km = jnp.pad(kmean, ((0, 0), (0, LANES - n_blk), (0, 0)))

    def sample_step(bi, hp, qi, r):
        return ((bi * n_pair + hp) * n_q + qi) * n_sub + r

    def per_seq(a):
        return pl.BlockSpec((1,) + a.shape[1:], lambda bi, hp, qi, r, pt: (sample_step(bi, hp, qi, r) // (2 * n_k_steps), 0, 0))

    smem = pl.BlockSpec(memory_space=pltpu.SMEM)
    hbm = pl.BlockSpec(memory_space=pl.ANY)
    n_chain = (LANES // hd) * (tq // tk)
    return pl.pallas_call(
        functools.partial(_moba_kernel, pages_per_block=ppb, hd=hd, tk=tk, n_k_steps=n_k_steps),
        grid_spec=pltpu.PrefetchScalarGridSpec(
            num_scalar_prefetch=1,
            grid=(b, n_pair, n_q, n_sub),
            in_specs=[smem, smem,
                      pl.BlockSpec((1, tq, LANES), lambda bi, hp, qi, r, pt: (bi, qi, hp)),
                      pl.BlockSpec((1, t, LANES), lambda bi, hp, qi, r, pt: (bi, 0, hp), pipeline_mode=pl.Buffered(1)),
                      pl.BlockSpec((1, LANES, t), lambda bi, hp, qi, r, pt: (bi, hp, 0), pipeline_mode=pl.Buffered(1)),
                      pl.BlockSpec((t, LANES), lambda bi, hp, qi, r, pt: (0, 0), pipeline_mode=pl.Buffered(1)),
                      pl.BlockSpec((1, LANES, LANES), lambda bi, hp, qi, r, pt: (bi, 0, hp)),
                      per_seq(qa_s), per_seq(ka_s), per_seq(va_s), hbm, hbm],
            out_specs=[pl.BlockSpec((1, tq, LANES), lambda bi, hp, qi, r, pt: (bi, qi, hp)), per_seq(qa_s)],
            scratch_shapes=[pltpu.VMEM((LANES // hd, tq, 2 * LANES), BF16),
                            pltpu.VMEM((n_chain, 1, tk), F32),
                            pltpu.VMEM((n_chain, LANES + _BF16_ROWS, tk), F32),
                            pltpu.VMEM((2, n_per_step, n_head, hd, page), F32),
                            pltpu.SemaphoreType.DMA((2,)),
                            pltpu.VMEM((rows, w), F32),
                            pltpu.VMEM((rows, n_pages * page), F32),
                            pltpu.VMEM((rows, n_pages * page), BF16),
                            pltpu.VMEM((w, LANES), F32),
                            pltpu.VMEM((rows, 1), F32), pltpu.VMEM((rows, w), F32)],
        ),
        out_shape=[jax.ShapeDtypeStruct((b, t, w), F32), jax.ShapeDtypeStruct((bs, ts, w), F32)],
        compiler_params=_params("arbitrary", "arbitrary", "arbitrary", "arbitrary"),
        name="moba",
    )(page_table, sl, slopes, qa, k_bf, vt_bf, kc, km, qa_s, ka_s, va_s, cache_kt, cache_vt)


def _gla_kernel(q_ref, k_ref, v_ref, la_ref, rg_ref, gn_ref, s0_ref, o_ref, sout_ref, st_ref, *, dk, dv, chunk):
    c_idx = pl.program_id(1)
    n_chunks = pl.num_programs(1)
    c = chunk
    n_pair = q_ref.shape[2] // LANES
    per_pair = LANES // dk
    assert dv == LANES and c % SUBLANES == 0

    @pl.when(c_idx == 0)
    def _():
        for p in range(n_pair):
            st_ref[p] = s0_ref[0, p].T

    row = lax.broadcasted_iota(jnp.int32, (c, LANES), 0)
    lane = lax.broadcasted_iota(jnp.int32, (c, LANES), 1)
    row2 = lax.broadcasted_iota(jnp.int32, (c, c), 0)
    col2 = lax.broadcasted_iota(jnp.int32, (c, c), 1)
    tri = jnp.where(row2 >= col2, 1.0, 0.0)
    nb8 = c // SUBLANES
    sub = lax.broadcasted_iota(jnp.int32, (nb8, SUBLANES, LANES), 1)
    lane3 = lax.broadcasted_iota(jnp.int32, (nb8, SUBLANES, LANES), 2)
    lane_st = lax.broadcasted_iota(jnp.int32, (dv, LANES), 1)

    def bcast_row(x3, jj):
        return jnp.broadcast_to(x3[:, jj:jj + 1, :], x3.shape)

    def one_chunk(p, rows, b_all, st):
        cols = slice(p * LANES, (p + 1) * LANES)
        q = q_ref[0, rows, cols] * (dk ** -0.5)
        k = k_ref[0, rows, cols]
        b = b_all[:, cols]
        vs = [v_ref[0, rows, (p * per_pair + hh) * dv:(p * per_pair + hh + 1) * dv] for hh in range(per_pair)]
        head_lanes = [(lane >= hh * dk) & (lane < (hh + 1) * dk) for hh in range(per_pair)]
        head_lanes3 = [(lane3 >= hh * dk) & (lane3 < (hh + 1) * dk) for hh in range(per_pair)]

        qe = q * jnp.exp(b)
        o = [_bdot_nt(jnp.where(head_lanes[hh], qe, 0.0), st) for hh in range(per_pair)]

        q3 = q.reshape(nb8, SUBLANES, LANES)
        k3 = k.reshape(nb8, SUBLANES, LANES)
        b3 = b.reshape(nb8, SUBLANES, LANES)
        v3 = [v.reshape(nb8, SUBLANES, dv) for v in vs]
        o3 = [jnp.zeros((nb8, SUBLANES, dv), F32) for _ in range(per_pair)]
        for jj in range(SUBLANES):
            e = jnp.exp(jnp.minimum(b3 - bcast_row(b3, jj), 0.0))
            term = jnp.where(sub >= jj, q3 * bcast_row(k3, jj) * e, 0.0)
            for hh in range(per_pair):
                a = jnp.sum(jnp.where(head_lanes3[hh], term, 0.0), axis=-1, keepdims=True)
                o3[hh] = o3[hh] + a * bcast_row(v3[hh], jj)
        o = [o[hh] + o3[hh].reshape(c, dv) for hh in range(per_pair)]

        attn = [jnp.zeros((c, c), F32) for _ in range(per_pair)]
        m_half = SUBLANES
        while 2 * m_half <= c:
            span = 2 * m_half
            bnd = jnp.broadcast_to(b.reshape(c // span, span, LANES)[:, m_half - 1:m_half, :],
                                   (c // span, span, LANES)).reshape(c, LANES)
            upper = (row % span) >= m_half
            qm = jnp.where(upper, q * jnp.exp(jnp.minimum(b - bnd, 0.0)), 0.0)
            km = jnp.where(upper, 0.0, k * jnp.exp(jnp.minimum(bnd - b, 0.0)))
            same = (row2 // span) == (col2 // span)
            for hh in range(per_pair):
                a = _bdot_nt(jnp.where(head_lanes[hh], qm, 0.0), km)
                attn[hh] = attn[hh] + jnp.where(same, a, 0.0)
            m_half = span
        if c > SUBLANES:
            o = [o[hh] + _bdot(attn[hh], vs[hh]) for hh in range(per_pair)]

        b_last = b[c - 1:c, :]
        kk = (k * jnp.exp(b_last - b)).astype(BF16)
        upd = lax.dot_general(vs[0].astype(BF16), kk, _TN, preferred_element_type=F32)
        for hh in range(1, per_pair):
            u = lax.dot_general(vs[hh].astype(BF16), kk, _TN, preferred_element_type=F32)
            upd = jnp.where(lane_st >= hh * dk, u, upd)
        for hh in range(per_pair):
            h = p * per_pair + hh
            rg = rg_ref[0, rows, h * dv:(h + 1) * dv]
            o_ref[0, rows, h * dv:(h + 1) * dv] = _rms(o[hh], gn_ref[...]) * (rg * _sigmoid(rg))
        return st * jnp.exp(b_last) + upd

    states = [st_ref[p] for p in range(n_pair)]
    for ch in range(q_ref.shape[1] // c):
        rows = slice(ch * c, (ch + 1) * c)
        b_all = jnp.dot(tri, la_ref[0, rows, :], precision=lax.Precision.HIGHEST, preferred_element_type=F32)
        states = [one_chunk(p, rows, b_all, states[p]) for p in range(n_pair)]
    for p in range(n_pair):
        st_ref[p] = states[p]

    @pl.when(c_idx == n_chunks - 1)
    def _():
        for p in range(n_pair):
            sout_ref[0, p] = states[p].T


def _gla(qg, kg, vg, la, rg, g_norm, s0, *, chunk, dk, dv):
    b, t, wk = qg.shape
    wv = vg.shape[2]
    n_head = wk // dk
    n_pair = wk // LANES
    s0p = s0.reshape(b, n_pair, LANES, dv)
    step = 2 * chunk if t % (2 * chunk) == 0 else chunk
    tok = lambda w: pl.BlockSpec((1, step, w), lambda bi, ci: (bi, ci, 0))
    st_spec = pl.BlockSpec((1, n_pair, LANES, dv), lambda bi, ci: (bi, 0, 0, 0))
    og, s_new = pl.pallas_call(
        functools.partial(_gla_kernel, dk=dk, dv=dv, chunk=chunk),
        grid=(b, t // step),
        in_specs=[tok(wk), tok(wk), tok(wv), tok(wk), tok(wv),
                  pl.BlockSpec(g_norm.shape, lambda bi, ci: (0, 0)), st_spec],
        out_specs=[tok(wv), st_spec],
        out_shape=[jax.ShapeDtypeStruct((b, t, wv), F32), jax.ShapeDtypeStruct((b, n_pair, LANES, dv), F32)],
        scratch_shapes=[pltpu.VMEM((n_pair, dv, LANES), F32)],
        compiler_params=_params("parallel", "arbitrary"),
        name="gla",
    )(qg, kg, vg, la, rg, g_norm, s0p)
    return og, s_new.reshape(b, n_head, dk, dv)


def _merge_kernel(x_ref, oa_ref, og_ref, sga_ref, sgb_ref, wa_ref, wb_ref, wo_ref, g_ref, y_ref):
    merged = (sga_ref[...].astype(F32) * _bdot(oa_ref[...], wa_ref[...])
              + sgb_ref[...].astype(F32) * _bdot(og_ref[...], wb_ref[...]))
    y_ref[...] = x_ref[...] + _rms(_bdot(merged, wo_ref[...]), g_ref[...])


def _mlp_kernel(x_ref, wu_ref, wd_ref, g1_ref, g2_ref, y_ref):
    x = x_ref[...]
    u = _bdot(_rms(x, g1_ref[...]), wu_ref[...])
    u = jnp.square(jnp.maximum(u, 0.0))
    y_ref[...] = x + _rms(_bdot(u, wd_ref[...]), g2_ref[...])


def _row_tile(n):
    return 2 * MOBA_BLOCK if n % (2 * MOBA_BLOCK) == 0 else MOBA_BLOCK


def _rowwise_call(kernel, name, row_inputs, const_inputs, out_width):
    n = row_inputs[0].shape[0]
    tm = _row_tile(n)
    assert n % tm == 0
    row = lambda a: pl.BlockSpec((tm, a.shape[1]), lambda i: (i, 0))
    full = lambda a: pl.BlockSpec(a.shape, lambda i: (0,) * a.ndim, pipeline_mode=pl.Buffered(1))
    return pl.pallas_call(
        kernel,
        grid=(n // tm,),
        in_specs=[row(a) for a in row_inputs] + [full(a) for a in const_inputs],
        out_specs=pl.BlockSpec((tm, out_width), lambda i: (i, 0)),
        out_shape=jax.ShapeDtypeStruct((n, out_width), F32),
        compiler_params=_params("parallel"),
        name=name,
    )(*row_inputs, *const_inputs)


def _gla_chunk(t):
    c = SUBLANES
    while c * 2 <= min(t, LANES) and t % (c * 2) == 0:
        c *= 2
    return c


def kernel(x_prompt, x_sample, cache_k, cache_v, page_table, state_gla, w_in, w_gla_gate, b_gla_gate, g_gla_norm,
           w_branch_a, w_branch_b, w_out, w_up, w_down, g_pre_mix, g_post_mix, g_pre_mlp, g_post_mlp):
    bp, tp, d_model = x_prompt.shape
    bs, ts, _ = x_sample.shape
    depth, n_phys, page, n_head, hd = cache_k.shape
    _, _, n_head_g, dk, dv = state_gla.shape
    w_a = n_head * hd
    qk_g = n_head_g * dk
    v_g = n_head_g * dv
    rank = w_gla_gate.shape[1]
    n_main = 3 * w_a + 2 * qk_g + 2 * v_g

    hp = x_prompt.reshape(bp * tp, d_model)
    hs = x_sample.reshape(bs * ts, d_model)
    outs = [[] for _ in range(6)]
    for l in range(depth):
        wm = w_in[l, :, :n_main].astype(BF16)
        wlr = jnp.pad(w_in[l, :, n_main:n_main + rank], ((0, 0), (0, LANES - rank))).astype(BF16)
        wgt = w_in[l, :, n_main + rank:].astype(BF16)
        wgg = jnp.pad(w_gla_gate[l], ((0, LANES - rank), (0, 0))).astype(BF16)
        bgg = b_gla_gate[l][None, :]
        wa, wb, wo = (w[l].astype(BF16) for w in (w_branch_a, w_branch_b, w_out))
        wu, wd = w_up[l].astype(BF16), w_down[l].astype(BF16)
        g_mix, g_pm, g_mlp, g_pl, g_gn = (g[l][None, :] for g in
                                          (g_pre_mix, g_post_mix, g_pre_mlp, g_post_mlp, g_gla_norm))

        def project(x, kv_seq_len):
            return _proj(x, g_mix, wm, wlr, wgt, wgg, bgg, w_a=w_a, qk_g=qk_g, v_g=v_g, kv_seq_len=kv_seq_len)

        def mix(x, b, t, oa, proj_out, s0):
            qg, kg, vg, rg, la, sga, sgb = proj_out[3:10]
            r3 = lambda a: a.reshape(b, t, a.shape[-1])
            og, s_new = _gla(r3(qg), r3(kg), r3(vg), r3(la), r3(rg), g_gn, s0, chunk=_gla_chunk(t), dk=dk, dv=dv)
            n = b * t
            x = _rowwise_call(_merge_kernel, "merge", [x, oa.reshape(n, w_a), og.reshape(n, v_g), sga, sgb],
                              [wa, wb, wo, g_pm], d_model)
            return _rowwise_call(_mlp_kernel, "mlp", [x], [wu, wd, g_mlp, g_pl], d_model), s_new

        pp = project(hp, tp)
        ps = project(hs, None)
        qa_p, kt_p, vt_p, kmean_p, k_bf, vt_bf = pp[0], pp[1], pp[2], pp[10], pp[11], pp[12]
        qa_s, ka_s, va_s = (a.reshape(bs, ts, w_a) for a in ps[:3])
        to_stored = lambda c: jnp.transpose(c[l], (0, 2, 3, 1))
        oa_p, oa_s = _moba(qa_p.reshape(bp, tp, w_a), k_bf.reshape(bp, tp, w_a), vt_bf,
                           kmean_p.reshape(bp, tp // MOBA_BLOCK, w_a), qa_s, ka_s, va_s,
                           to_stored(cache_k), to_stored(cache_v), page_table, hd=hd)
        hp, sp = mix(hp, bp, tp, oa_p, pp, jnp.zeros((bp, n_head_g, dk, dv), state_gla.dtype))
        hs, ssn = mix(hs, bs, ts, oa_s, ps, state_gla[l])
        kp, vp = (jnp.transpose(a.reshape(bp, n_head, hd, tp), (0, 3, 1, 2)) for a in (kt_p, vt_p))
        ksn, vsn = (a.reshape(bs, ts, n_head, hd) for a in (ka_s, va_s))
        for lst, val in zip(outs, (kp, vp, sp, ksn, vsn, ssn)):
            lst.append(val)
    return (hp.reshape(bp, tp, d_model), hs.reshape(bs, ts, d_model)) + tuple(jnp.stack(o) for o in outs)
```

```python
import functools

import ml_dtypes
import numpy as np
import jax
import jax.numpy as jnp
from jax import lax
from jax.experimental import pallas as pl
from jax.experimental.pallas import tpu as pltpu

F32 = jnp.float32
BF16 = jnp.bfloat16

LANES = 128
SUBLANES = 8
VMEM_LIMIT_BYTES = 56 * 1024 * 1024

EPS = 1e-6
MOBA_BLOCK = 256
MOBA_TOPK = 3
GLA_TAU = 16.0
NEG_BIG = -1e30
PENALTY = -30000.0
LOG2E = float(np.log2(np.e))

_NT = (((1,), (1,)), ((), ()))
_TN = (((0,), (0,)), ((), ()))


def _params(*sem):
    return pltpu.CompilerParams(dimension_semantics=sem, vmem_limit_bytes=VMEM_LIMIT_BYTES)


def _sigmoid(x):
    return 1.0 / (1.0 + jnp.exp(-x))


def _rms(x, g):
    return x * lax.rsqrt(jnp.mean(x * x, axis=-1, keepdims=True) + EPS) * g


def _bdot(a, b):
    return jnp.dot(a.astype(BF16), b.astype(BF16), preferred_element_type=F32)


def _bdot_nt(a, b):
    return lax.dot_general(a.astype(BF16), b.astype(BF16), _NT, preferred_element_type=F32)


def _proj_kernel(x_ref, g_ref, wm_ref, wlr_ref, wgt_ref, wgg_ref, bgg_ref,
                 qa_ref, ka_ref, va_ref, qg_ref, kg_ref, vg_ref, rg_ref, la_ref, sga_ref, sgb_ref, kmean_ref,
                 *attn_refs, w_a, qk_g, v_g, d_model, kv_transposed):
    hb = _rms(x_ref[...], g_ref[...]).astype(BF16)

    def proj(lo, n):
        return jnp.dot(hb, wm_ref[:, lo:lo + n], preferred_element_type=F32)

    qa_ref[...] = proj(0, w_a)
    ka = proj(w_a, w_a)
    va = proj(2 * w_a, w_a)
    for i in range(kmean_ref.shape[0]):
        kmean_ref[i] = jnp.sum(ka[i * MOBA_BLOCK:(i + 1) * MOBA_BLOCK], axis=0, keepdims=True) * (1.0 / MOBA_BLOCK)
    if kv_transposed:
        k_bf_ref, vt_bf_ref = attn_refs
        vt = va.T
        ka_ref[0] = ka.T
        va_ref[0] = vt
        k_bf_ref[...] = ka.astype(BF16)
        vt_bf_ref[0] = vt.astype(BF16)
    else:
        ka_ref[...] = ka
        va_ref[...] = va
    lo = 3 * w_a
    qg_ref[...] = proj(lo, qk_g)
    kg_ref[...] = proj(lo + qk_g, qk_g)
    vg_ref[...] = proj(lo + 2 * qk_g, v_g)
    rg_ref[...] = proj(lo + 2 * qk_g + v_g, v_g)
    lr = jnp.dot(hb, wlr_ref[...], preferred_element_type=F32)
    xg = jnp.dot(lr.astype(BF16), wgg_ref[...], preferred_element_type=F32) + bgg_ref[...]
    log_sig = jnp.minimum(xg, 0.0) - jnp.log1p(jnp.exp(-jnp.abs(xg)))
    la_ref[...] = log_sig * (1.0 / GLA_TAU)
    ga = jnp.dot(hb, wgt_ref[:, :d_model], preferred_element_type=F32)
    sga_ref[...] = _sigmoid(ga).astype(BF16)
    gb = jnp.dot(hb, wgt_ref[:, d_model:], preferred_element_type=F32)
    sgb_ref[...] = _sigmoid(gb).astype(BF16)


def _proj(x, g, wm, wlr, wgt, wgg, bgg, *, w_a, qk_g, v_g, kv_seq_len=None):
    n, d_model = x.shape
    tm = _row_tile(n)
    assert n % tm == 0
    nt = n // tm
    blocks_per_tile = tm // MOBA_BLOCK
    row = lambda w: pl.BlockSpec((tm, w), lambda i: (i, 0))
    full = lambda a: pl.BlockSpec(a.shape, lambda i: (0,) * a.ndim, pipeline_mode=pl.Buffered(1))
    widths = (w_a, w_a, w_a, qk_g, qk_g, v_g, v_g, qk_g)
    out_shape = [jax.ShapeDtypeStruct((n, w), F32) for w in widths]
    out_shape += [jax.ShapeDtypeStruct((n, d_model), BF16)] * 2
    out_shape += [jax.ShapeDtypeStruct((n // MOBA_BLOCK, 1, w_a), F32)]
    out_specs = [row(w) for w in widths] + [row(d_model)] * 2
    out_specs += [pl.BlockSpec((blocks_per_tile, 1, w_a), lambda i: (i, 0, 0))]
    if kv_seq_len is not None:
        assert kv_seq_len % tm == 0
        seq_tiles = kv_seq_len // tm
        transposed = pl.BlockSpec((1, w_a, tm), lambda i: (i // seq_tiles, 0, i % seq_tiles))
        for i in (1, 2):
            out_shape[i] = jax.ShapeDtypeStruct((n // kv_seq_len, w_a, kv_seq_len), F32)
            out_specs[i] = transposed
        out_shape += [jax.ShapeDtypeStruct((n, w_a), BF16),
                      jax.ShapeDtypeStruct((n // kv_seq_len, w_a, kv_seq_len), BF16)]
        out_specs += [row(w_a), transposed]
    return pl.pallas_call(
        functools.partial(_proj_kernel, w_a=w_a, qk_g=qk_g, v_g=v_g, d_model=d_model,
                          kv_transposed=kv_seq_len is not None),
        grid=(nt,),
        in_specs=[row(d_model), full(g), full(wm), full(wlr), full(wgt), full(wgg), full(bgg)],
        out_specs=out_specs,
        out_shape=out_shape,
        compiler_params=_params("parallel"),
        name="proj",
    )(x, g, wm, wlr, wgt, wgg, bgg)


def _top_blocks(gate, candidate, block, axis=-1):
    block_f = block.astype(F32)
    no_block = float(gate.shape[axis])
    avail = jnp.where(candidate, 1.0, 0.0)
    sel = jnp.zeros(gate.shape, F32)
    for _ in range(MOBA_TOPK):
        cur = jnp.where(avail > 0.0, gate, -jnp.inf)
        mx = jnp.max(cur, axis=axis, keepdims=True)
        cand = jnp.where(avail > 0.0, jnp.where(cur == mx, block_f, no_block), no_block)
        first = jnp.min(cand, axis=axis, keepdims=True)
        pick = block_f == first
        sel = jnp.where(pick, 1.0, sel)
        avail = jnp.where(pick, 0.0, avail)
    return sel


def _alibi_slopes(n_head):
    return 2.0 ** (-8.0 * (np.arange(n_head) + 1) / n_head)


_SLOPE_PARTS = 3
_AUG_EXTRA = 2 * _SLOPE_PARTS


def _moba_prompt_step(r, n_sub, hp, qi, sl_ref, q_ref, k_ref, vt_ref, kc_ref, km_ref, o_ref,
                      qaug_ref, m_ref, acc_ref, *, hd, tk):
    blk = MOBA_BLOCK
    t = k_ref.shape[1]
    tq = q_ref.shape[1]
    n_blk = t // blk
    per_pair = LANES // hd
    n_acc = acc_ref.shape[1]
    tiles_per_q = tq // tk
    chains = [(hh, cb) for hh in range(per_pair) for cb in range(tiles_per_q)]
    ones = jnp.ones((n_acc - LANES, tk), BF16)

    def tiles(j):
        keys = pl.ds(pl.multiple_of(j * tk, tk), tk)
        kt = jnp.concatenate([k_ref[0, keys, :], kc_ref[keys, :]], axis=1)
        return kt, jnp.concatenate([vt_ref[0, :, keys], ones], axis=0)

    def logits(kt, hh, cb):
        return lax.dot_general(kt, qaug_ref[hh, cb * tk:(cb + 1) * tk, :], _NT, preferred_element_type=F32)

    def absorb(stat, s, va, c):
        m, acc = stat
        m_new = jnp.maximum(m, jnp.max(s, axis=0, keepdims=True) + c)
        p = jnp.exp2(s - (m_new - c)).astype(BF16)
        return m_new, jnp.exp2(m - m_new) * acc + jnp.dot(va, p, preferred_element_type=F32)

    @pl.when(r == 0)
    def _():
        lane = lax.broadcasted_iota(jnp.int32, (tq, LANES), 1)
        q = q_ref[0]
        n_rows = -(-n_blk // SUBLANES) * SUBLANES
        block = lax.broadcasted_iota(jnp.int32, (n_rows, tq), 0)
        own = qi * (tq // blk) + lax.broadcasted_iota(jnp.int32, (n_rows, tq), 1) // blk
        extra = lax.broadcasted_iota(jnp.int32, (SUBLANES, tq), 0)
        assert _AUG_EXTRA <= SUBLANES and n_rows + SUBLANES <= LANES
        for hh in range(per_pair):
            h = hp * per_pair + hh
            head_lanes = (lane >= hh * hd) & (lane < (hh + 1) * hd)
            qh = jnp.where(head_lanes, q, 0.0)
            gate = _bdot_nt(km_ref[0, :n_rows, :], qh)
            sel = _top_blocks(gate, block < own, block, axis=0)
            penalty = jnp.where((sel > 0.0) | (block == own) | (block >= n_blk), 0.0, PENALTY)
            parts = [sl_ref[1 + i, h] for i in range(_SLOPE_PARTS)]
            slope_rows = jnp.zeros((SUBLANES, tq), F32)
            for off, val in enumerate(parts + [part * blk for part in parts]):
                slope_rows = jnp.where(extra == off, val, slope_rows)
            qx_t = jnp.concatenate([penalty, slope_rows, jnp.zeros((LANES - n_rows - SUBLANES, tq), F32)], axis=0)
            qaug_ref[hh, :, :LANES] = (qh * (hd ** -0.5 * LOG2E)).astype(BF16)
            qaug_ref[hh, :, LANES:] = qx_t.T.astype(BF16)

        key2 = lax.broadcasted_iota(jnp.int32, (tk, tk), 0)
        qry2 = lax.broadcasted_iota(jnp.int32, (tk, tk), 1)
        stats = {c: (jnp.full((1, tk), NEG_BIG, F32), jnp.zeros((n_acc, tk), F32)) for c in chains}
        for d in range(tiles_per_q):
            kt, va = tiles(qi * tiles_per_q + d)
            live = [(hh, cb) for hh, cb in chains if cb >= d]
            ss = [logits(kt, hh, cb) for hh, cb in live]
            for (hh, cb), s in zip(live, ss):
                if cb == d:
                    s = jnp.where(key2 <= qry2, s, NEG_BIG)
                stats[hh, cb] = absorb(stats[hh, cb], s, va, 0.0)
        for ci, c in enumerate(chains):
            m_ref[ci], acc_ref[ci] = stats[c]

    def body(i, flat):
        j = r + i * n_sub
        kt, va = tiles(j)
        ss = [logits(kt, hh, cb) for hh, cb in chains]
        out = []
        for ci, (hh, cb) in enumerate(chains):
            c = -(sl_ref[0, hp * per_pair + hh] * tq) * (qi - j // tiles_per_q).astype(F32)
            out += absorb(flat[2 * ci:2 * ci + 2], ss[ci], va, c)
        return tuple(out)

    n_mine = (qi * tiles_per_q - r + n_sub - 1) // n_sub
    flat = lax.fori_loop(0, n_mine, body, tuple(x for ci in range(len(chains)) for x in (m_ref[ci], acc_ref[ci])))
    for ci in range(len(chains)):
        m_ref[ci], acc_ref[ci] = flat[2 * ci], flat[2 * ci + 1]

    @pl.when(r == n_sub - 1)
    def _():
        dim = lax.broadcasted_iota(jnp.int32, (LANES, tk), 0)
        for cb in range(tiles_per_q):
            out = None
            for hh in range(per_pair):
                acc = flat[2 * chains.index((hh, cb)) + 1]
                o_h = acc[:LANES] / acc[LANES:LANES + 1]
                out = o_h if out is None else jnp.where(dim >= hh * hd, o_h, out)
            o_ref[0, cb * tk:(cb + 1) * tk, :] = out.T


def _moba_consts(t, tq):
    n_blk = t // MOBA_BLOCK
    base = -(-n_blk // SUBLANES) * SUBLANES
    assert base + _AUG_EXTRA <= LANES
    pos = np.arange(t)
    c = np.zeros((t, LANES), np.float32)
    c[pos, pos // MOBA_BLOCK] = 1.0
    for i in range(_SLOPE_PARTS):
        c[:, base + i] = pos % MOBA_BLOCK
        c[:, base + _SLOPE_PARTS + i] = (pos % tq) // MOBA_BLOCK
    return jnp.asarray(c, dtype=BF16)


_BF16_ROWS = 2 * SUBLANES


def _moba_sample_step(j, n_k_steps, slopes_ref, q_ref, kn_ref, vn_ref, pages_ref,
                      o_ref, qbd_ref, s_ref, p_ref, ksum_ref, l_ref, acc_ref, *, pages_per_block, hd):
    n_per_step = pages_ref.shape[0]
    qk_pages, pv_pages = 4, 2
    ts, w = q_ref.shape[1:]
    n_head = w // hd
    rows = n_head * ts
    page = pages_ref.shape[3]
    blk = page * pages_per_block
    past = s_ref.shape[1]
    n_blk = past // blk
    scale = hd ** -0.5

    row_head = lax.broadcasted_iota(jnp.int32, (rows, 1), 0) // ts
    row_q = lax.broadcasted_iota(jnp.int32, (rows, 1), 0) % ts
    col_head = lax.broadcasted_iota(jnp.int32, (rows, w), 1) // hd
    slope = jnp.zeros((rows, 1), F32)
    for h in range(n_head):
        slope = jnp.where(row_head == h, slopes_ref[h], slope)

    @pl.when(j == 0)
    def _():
        qt = jnp.concatenate([q_ref[0]] * n_head, axis=0)
        qbd_ref[...] = jnp.where(col_head == row_head, qt, 0.0)
        ksum_ref[...] = jnp.zeros(ksum_ref.shape, F32)

    @pl.when(j < n_k_steps)
    def _():
        qs = (qbd_ref[...] * scale).astype(BF16)
        lane = lax.broadcasted_iota(jnp.int32, ksum_ref.shape, 1)
        ksum = ksum_ref[...]
        group = qk_pages if n_per_step % qk_pages == 0 and qk_pages % pages_per_block == 0 else pages_per_block
        for g0 in range(0, n_per_step, group):
            kts = [pages_ref[p_i].reshape(w, page) for p_i in range(g0, g0 + group)]
            first = j * n_per_step + g0
            s = jnp.dot(qs, jnp.concatenate([kt.astype(BF16) for kt in kts], axis=1), preferred_element_type=F32)
            s_ref[:, pl.ds(pl.multiple_of(first * page, group * page), group * page)] = s
            for b0 in range(0, group, pages_per_block):
                total = kts[b0]
                for kt in kts[b0 + 1:b0 + pages_per_block]:
                    total = total + kt
                ksum = jnp.where(lane == (first + b0) // pages_per_block, jnp.sum(total, axis=1, keepdims=True), ksum)
        ksum_ref[...] = ksum

    @pl.when(j == n_k_steps)
    def _():
        qbd = qbd_ref[...]
        gate = _bdot(qbd, ksum_ref[...])
        lane = lax.broadcasted_iota(jnp.int32, gate.shape, 1)
        sel = _top_blocks(gate, lane < n_blk, lane)
        s_own = _bdot_nt(qbd * scale, kn_ref[0])
        kq = lax.broadcasted_iota(jnp.int32, s_own.shape, 1)
        s_own = jnp.where(kq <= row_q, s_own + slope * kq.astype(F32), NEG_BIG)
        key_in_blk = lax.broadcasted_iota(jnp.int32, (1, blk), 1)

        def add_bias(b_i, m_wide):
            cols = pl.ds(pl.multiple_of(b_i * blk, blk), blk)
            picked = jnp.max(jnp.where(lane == b_i, sel, 0.0), axis=-1, keepdims=True)
            rel = (b_i * blk - past + key_in_blk).astype(F32)
            s = s_ref[:, cols] + slope * rel + jnp.where(picked > 0.0, 0.0, NEG_BIG)
            s_ref[:, cols] = s
            return jnp.maximum(m_wide, s)

        unroll = 8 if n_blk % 8 == 0 else 1
        m_wide = lax.fori_loop(0, n_blk, add_bias, jnp.full((rows, blk), NEG_BIG, F32), unroll=unroll)
        m = jnp.maximum(jnp.max(m_wide, axis=-1, keepdims=True), jnp.max(s_own, axis=-1, keepdims=True))

        def probs(b_i, l_wide):
            cols = pl.ds(pl.multiple_of(b_i * blk, blk), blk)
            p = jnp.exp(s_ref[:, cols] - m)
            p_ref[:, cols] = p.astype(BF16)
            return l_wide + p

        l_wide = lax.fori_loop(0, n_blk, probs, jnp.zeros((rows, blk), F32), unroll=unroll)
        p_own = jnp.exp(s_own - m)
        l_ref[...] = jnp.sum(l_wide, axis=-1, keepdims=True) + jnp.sum(p_own, axis=-1, keepdims=True)
        acc_ref[...] = _bdot(p_own, vn_ref[0])

    @pl.when(j >= n_k_steps)
    def _():
        acc = acc_ref[...]
        group = pv_pages if n_per_step % pv_pages == 0 else 1
        for g0 in range(0, n_per_step, group):
            first = (j - n_k_steps) * n_per_step + g0
            p = p_ref[:, pl.ds(pl.multiple_of(first * page, group * page), group * page)]
            vt = jnp.concatenate([pages_ref[p_i].reshape(w, page).astype(BF16) for p_i in range(g0, g0 + group)], axis=1)
            acc = acc + lax.dot_general(p, vt, _NT, preferred_element_type=F32)
        acc_ref[...] = acc

    @pl.when(j == 2 * n_k_steps - 1)
    def _():
        res = jnp.where(col_head == row_head, acc_ref[...] / l_ref[...], 0.0)
        out = res[0:ts]
        for h in range(1, n_head):
            out = out + res[h * ts:(h + 1) * ts]
        o_ref[0] = out


_SAMPLE_STEPS = 2

def _moba_kernel(pt_ref, sl_ref, slopes_ref, q_ref, k_ref, vt_ref, kc_ref, km_ref, qs_ref, kn_ref, vn_ref,
                 cache_k_hbm, cache_v_hbm, o_ref, os_ref, qaug_ref, m_ref, acc_ref, pages_ref, sem, *sample_scratch,
                 pages_per_block, hd, tk, n_k_steps):
    hp, qi, r = pl.program_id(1), pl.program_id(2), pl.program_id(3)
    n_sub = pl.num_programs(3)
    gstep = ((pl.program_id(0) * pl.num_programs(1) + hp) * pl.num_programs(2) + qi) * n_sub + r
    n_gsteps = pl.num_programs(0) * pl.num_programs(1) * pl.num_programs(2) * n_sub
    n_per_step = pages_ref.shape[1]
    per_seq = 2 * n_k_steps
    first = gstep * _SAMPLE_STEPS

    def start_pages(s, slot):
        seq, j = s // per_seq, s % per_seq

        def start(cache_hbm, group):
            for p in range(n_per_step):
                pltpu.make_async_copy(cache_hbm.at[pt_ref[seq, group * n_per_step + p]],
                                      pages_ref.at[slot, p], sem.at[slot]).start()

        @pl.when(j < n_k_steps)
        def _():
            start(cache_k_hbm, j)

        @pl.when(j >= n_k_steps)
        def _():
            start(cache_v_hbm, j - n_k_steps)

    def sample_step(s, slot):
        pltpu.make_async_copy(cache_k_hbm.at[pl.ds(0, n_per_step)], pages_ref.at[slot], sem.at[slot]).wait()
        _moba_sample_step(s % per_seq, n_k_steps, slopes_ref, qs_ref, kn_ref, vn_ref, pages_ref.at[slot],
                          os_ref, *sample_scratch, pages_per_block=pages_per_block, hd=hd)

    @pl.when(gstep == 0)
    def _():
        start_pages(first, 0)

    start_pages(first + 1, 1)
    _moba_prompt_step(r, n_sub, hp, qi, sl_ref, q_ref, k_ref, vt_ref, kc_ref, km_ref, o_ref,
                      qaug_ref, m_ref, acc_ref, hd=hd, tk=tk)
    sample_step(first, 0)

    @pl.when(gstep + 1 < n_gsteps)
    def _():
        start_pages(first + _SAMPLE_STEPS, 0)

    sample_step(first + 1, 1)


def _moba(qa, k_bf, vt_bf, kmean, qa_s, ka_s, va_s, cache_kt, cache_vt, page_table, *, hd):
    b, t, w = qa.shape
    bs, ts, _ = qa_s.shape
    _, n_head, _, page = cache_kt.shape
    n_pages = page_table.shape[1]
    blk = MOBA_BLOCK
    ppb = blk // page
    n_pair = w // LANES
    n_blk = t // blk
    tk = 2 * blk if t % (2 * blk) == 0 else blk
    tq = 2 * tk if t % (2 * tk) == 0 else tk
    n_q = t // tq
    n_per_step = next(p for p in (32, 16, 8, 4, 2) if n_pages % p == 0 and p % ppb == 0
                      and (bs * 2 * (n_pages // p)) % (b * n_pair * n_q * _SAMPLE_STEPS) == 0)
    n_k_steps = n_pages // n_per_step
    n_sub = bs * 2 * n_k_steps // (b * n_pair * n_q * _SAMPLE_STEPS)
    assert (2 * n_k_steps) % _SAMPLE_STEPS == 0
    assert n_pages // ppb <= LANES
    rows = n_head * ts

    sl2 = _alibi_slopes(n_head) * LOG2E
    parts, rest = [], sl2
    for _ in range(_SLOPE_PARTS):
        parts.append(rest.astype(ml_dtypes.bfloat16).astype(np.float64))
        rest = rest - parts[-1]
    sl = jnp.asarray(np.stack([sl2] + parts), dtype=F32)
    slopes = jnp.asarray(_alibi_slopes(n_head), dtype=F32)
    kc = _moba_consts(t, tq)
    km = jnp.pad(kmean, ((0, 0), (0, LANES - n_blk), (0, 0)))

    def sample_step(bi, hp, qi, r):
        return (((bi * n_pair + hp) * n_q + qi) * n_sub + r) * _SAMPLE_STEPS

    def per_seq(a):
        return pl.BlockSpec((1,) + a.shape[1:], lambda bi, hp, qi, r, pt: (sample_step(bi, hp, qi, r) // (2 * n_k_steps), 0, 0))

    smem = pl.BlockSpec(memory_space=pltpu.SMEM)
    hbm = pl.BlockSpec(memory_space=pl.ANY)
    n_chain = (LANES // hd) * (tq // tk)
    return pl.pallas_call(
        functools.partial(_moba_kernel, pages_per_block=ppb, hd=hd, tk=tk, n_k_steps=n_k_steps),
        grid_spec=pltpu.PrefetchScalarGridSpec(
            num_scalar_prefetch=1,
            grid=(b, n_pair, n_q, n_sub),
            in_specs=[smem, smem,
                      pl.BlockSpec((1, tq, LANES), lambda bi, hp, qi, r, pt: (bi, qi, hp)),
                      pl.BlockSpec((1, t, LANES), lambda bi, hp, qi, r, pt: (bi, 0, hp), pipeline_mode=pl.Buffered(1)),
                      pl.BlockSpec((1, LANES, t), lambda bi, hp, qi, r, pt: (bi, hp, 0), pipeline_mode=pl.Buffered(1)),
                      pl.BlockSpec((t, LANES), lambda bi, hp, qi, r, pt: (0, 0), pipeline_mode=pl.Buffered(1)),
                      pl.BlockSpec((1, LANES, LANES), lambda bi, hp, qi, r, pt: (bi, 0, hp)),
                      per_seq(qa_s), per_seq(ka_s), per_seq(va_s), hbm, hbm],
            out_specs=[pl.BlockSpec((1, tq, LANES), lambda bi, hp, qi, r, pt: (bi, qi, hp)), per_seq(qa_s)],
            scratch_shapes=[pltpu.VMEM((LANES // hd, tq, 2 * LANES), BF16),
                            pltpu.VMEM((n_chain, 1, tk), F32),
                            pltpu.VMEM((n_chain, LANES + _BF16_ROWS, tk), F32),
                            pltpu.VMEM((2, n_per_step, n_head, hd, page), F32),
                            pltpu.SemaphoreType.DMA((2,)),
                            pltpu.VMEM((rows, w), F32),
                            pltpu.VMEM((rows, n_pages * page), F32),
                            pltpu.VMEM((rows, n_pages * page), BF16),
                            pltpu.VMEM((w, LANES), F32),
                            pltpu.VMEM((rows, 1), F32), pltpu.VMEM((rows, w), F32)],
        ),
        out_shape=[jax.ShapeDtypeStruct((b, t, w), F32), jax.ShapeDtypeStruct((bs, ts, w), F32)],
        compiler_params=_params("arbitrary", "arbitrary", "arbitrary", "arbitrary"),
        name="moba",
    )(page_table, sl, slopes, qa, k_bf, vt_bf, kc, km, qa_s, ka_s, va_s, cache_kt, cache_vt)


def _gla_kernel(q_ref, k_ref, v_ref, la_ref, rg_ref, gn_ref, s0_ref, o_ref, sout_ref, st_ref, *, dk, dv, chunk):
    c_idx = pl.program_id(1)
    n_chunks = pl.num_programs(1)
    c = chunk
    n_pair = q_ref.shape[2] // LANES
    per_pair = LANES // dk
    assert dv == LANES and c % SUBLANES == 0

    @pl.when(c_idx == 0)
    def _():
        for p in range(n_pair):
            st_ref[p] = s0_ref[0, p].T

    row = lax.broadcasted_iota(jnp.int32, (c, LANES), 0)
    lane = lax.broadcasted_iota(jnp.int32, (c, LANES), 1)
    row2 = lax.broadcasted_iota(jnp.int32, (c, c), 0)
    col2 = lax.broadcasted_iota(jnp.int32, (c, c), 1)
    tri = jnp.where(row2 >= col2, 1.0, 0.0)
    nb8 = c // SUBLANES
    sub = lax.broadcasted_iota(jnp.int32, (nb8, SUBLANES, LANES), 1)
    lane3 = lax.broadcasted_iota(jnp.int32, (nb8, SUBLANES, LANES), 2)
    lane_st = lax.broadcasted_iota(jnp.int32, (dv, LANES), 1)

    def bcast_row(x3, jj):
        return jnp.broadcast_to(x3[:, jj:jj + 1, :], x3.shape)

    def one_chunk(p, rows, b_all, st):
        cols = slice(p * LANES, (p + 1) * LANES)
        q = q_ref[0, rows, cols] * (dk ** -0.5)
        k = k_ref[0, rows, cols]
        b = b_all[:, cols]
        vs = [v_ref[0, rows, (p * per_pair + hh) * dv:(p * per_pair + hh + 1) * dv] for hh in range(per_pair)]
        head_lanes = [(lane >= hh * dk) & (lane < (hh + 1) * dk) for hh in range(per_pair)]
        head_lanes3 = [(lane3 >= hh * dk) & (lane3 < (hh + 1) * dk) for hh in range(per_pair)]

        qe = q * jnp.exp(b)
        o = [_bdot_nt(jnp.where(head_lanes[hh], qe, 0.0), st) for hh in range(per_pair)]

        q3 = q.reshape(nb8, SUBLANES, LANES)
        k3 = k.reshape(nb8, SUBLANES, LANES)
        b3 = b.reshape(nb8, SUBLANES, LANES)
        v3 = [v.reshape(nb8, SUBLANES, dv) for v in vs]
        o3 = [jnp.zeros((nb8, SUBLANES, dv), F32) for _ in range(per_pair)]
        for jj in range(SUBLANES):
            e = jnp.exp(jnp.minimum(b3 - bcast_row(b3, jj), 0.0))
            term = jnp.where(sub >= jj, q3 * bcast_row(k3, jj) * e, 0.0)
            for hh in range(per_pair):
                a = jnp.sum(jnp.where(head_lanes3[hh], term, 0.0), axis=-1, keepdims=True)
                o3[hh] = o3[hh] + a * bcast_row(v3[hh], jj)
        o = [o[hh] + o3[hh].reshape(c, dv) for hh in range(per_pair)]

        attn = [jnp.zeros((c, c), F32) for _ in range(per_pair)]
        m_half = SUBLANES
        while 2 * m_half <= c:
            span = 2 * m_half
            bnd = jnp.broadcast_to(b.reshape(c // span, span, LANES)[:, m_half - 1:m_half, :],
                                   (c // span, span, LANES)).reshape(c, LANES)
            upper = (row % span) >= m_half
            qm = jnp.where(upper, q * jnp.exp(jnp.minimum(b - bnd, 0.0)), 0.0)
            km = jnp.where(upper, 0.0, k * jnp.exp(jnp.minimum(bnd - b, 0.0)))
            same = (row2 // span) == (col2 // span)
            for hh in range(per_pair):
                a = _bdot_nt(jnp.where(head_lanes[hh], qm, 0.0), km)
                attn[hh] = attn[hh] + jnp.where(same, a, 0.0)
            m_half = span
        if c > SUBLANES:
            o = [o[hh] + _bdot(attn[hh], vs[hh]) for hh in range(per_pair)]

        b_last = b[c - 1:c, :]
        kk = (k * jnp.exp(b_last - b)).astype(BF16)
        upd = lax.dot_general(vs[0].astype(BF16), kk, _TN, preferred_element_type=F32)
        for hh in range(1, per_pair):
            u = lax.dot_general(vs[hh].astype(BF16), kk, _TN, preferred_element_type=F32)
            upd = jnp.where(lane_st >= hh * dk, u, upd)
        for hh in range(per_pair):
            h = p * per_pair + hh
            rg = rg_ref[0, rows, h * dv:(h + 1) * dv]
            o_ref[0, rows, h * dv:(h + 1) * dv] = _rms(o[hh], gn_ref[...]) * (rg * _sigmoid(rg))
        return st * jnp.exp(b_last) + upd

    states = [st_ref[p] for p in range(n_pair)]
    for ch in range(q_ref.shape[1] // c):
        rows = slice(ch * c, (ch + 1) * c)
        b_all = jnp.dot(tri, la_ref[0, rows, :], precision=lax.Precision.HIGHEST, preferred_element_type=F32)
        states = [one_chunk(p, rows, b_all, states[p]) for p in range(n_pair)]
    for p in range(n_pair):
        st_ref[p] = states[p]

    @pl.when(c_idx == n_chunks - 1)
    def _():
        for p in range(n_pair):
            sout_ref[0, p] = states[p].T


def _gla(qg, kg, vg, la, rg, g_norm, s0, *, chunk, dk, dv):
    b, t, wk = qg.shape
    wv = vg.shape[2]
    n_head = wk // dk
    n_pair = wk // LANES
    s0p = s0.reshape(b, n_pair, LANES, dv)
    step = 2 * chunk if t % (2 * chunk) == 0 else chunk
    tok = lambda w: pl.BlockSpec((1, step, w), lambda bi, ci: (bi, ci, 0))
    st_spec = pl.BlockSpec((1, n_pair, LANES, dv), lambda bi, ci: (bi, 0, 0, 0))
    og, s_new = pl.pallas_call(
        functools.partial(_gla_kernel, dk=dk, dv=dv, chunk=chunk),
        grid=(b, t // step),
        in_specs=[tok(wk), tok(wk), tok(wv), tok(wk), tok(wv),
                  pl.BlockSpec(g_norm.shape, lambda bi, ci: (0, 0)), st_spec],
        out_specs=[tok(wv), st_spec],
        out_shape=[jax.ShapeDtypeStruct((b, t, wv), F32), jax.ShapeDtypeStruct((b, n_pair, LANES, dv), F32)],
        scratch_shapes=[pltpu.VMEM((n_pair, dv, LANES), F32)],
        compiler_params=_params("parallel", "arbitrary"),
        name="gla",
    )(qg, kg, vg, la, rg, g_norm, s0p)
    return og, s_new.reshape(b, n_head, dk, dv)


def _merge_kernel(x_ref, oa_ref, og_ref, sga_ref, sgb_ref, wa_ref, wb_ref, wo_ref, g_ref, y_ref):
    merged = (sga_ref[...].astype(F32) * _bdot(oa_ref[...], wa_ref[...])
              + sgb_ref[...].astype(F32) * _bdot(og_ref[...], wb_ref[...]))
    y_ref[...] = x_ref[...] + _rms(_bdot(merged, wo_ref[...]), g_ref[...])


def _mlp_kernel(x_ref, wu_ref, wd_ref, g1_ref, g2_ref, y_ref):
    x = x_ref[...]
    u = _bdot(_rms(x, g1_ref[...]), wu_ref[...])
    u = jnp.square(jnp.maximum(u, 0.0))
    y_ref[...] = x + _rms(_bdot(u, wd_ref[...]), g2_ref[...])


def _row_tile(n):
    return 2 * MOBA_BLOCK if n % (2 * MOBA_BLOCK) == 0 else MOBA_BLOCK


def _rowwise_call(kernel, name, row_inputs, const_inputs, out_width):
    n = row_inputs[0].shape[0]
    tm = _row_tile(n)
    assert n % tm == 0
    row = lambda a: pl.BlockSpec((tm, a.shape[1]), lambda i: (i, 0))
    full = lambda a: pl.BlockSpec(a.shape, lambda i: (0,) * a.ndim, pipeline_mode=pl.Buffered(1))
    return pl.pallas_call(
        kernel,
        grid=(n // tm,),
        in_specs=[row(a) for a in row_inputs] + [full(a) for a in const_inputs],
        out_specs=pl.BlockSpec((tm, out_width), lambda i: (i, 0)),
        out_shape=jax.ShapeDtypeStruct((n, out_width), F32),
        compiler_params=_params("parallel"),
        name=name,
    )(*row_inputs, *const_inputs)


def _gla_chunk(t):
    c = SUBLANES
    while c * 2 <= min(t, LANES) and t % (c * 2) == 0:
        c *= 2
    return c


def kernel(x_prompt, x_sample, cache_k, cache_v, page_table, state_gla, w_in, w_gla_gate, b_gla_gate, g_gla_norm,
           w_branch_a, w_branch_b, w_out, w_up, w_down, g_pre_mix, g_post_mix, g_pre_mlp, g_post_mlp):
    bp, tp, d_model = x_prompt.shape
    bs, ts, _ = x_sample.shape
    depth, n_phys, page, n_head, hd = cache_k.shape
    _, _, n_head_g, dk, dv = state_gla.shape
    w_a = n_head * hd
    qk_g = n_head_g * dk
    v_g = n_head_g * dv
    rank = w_gla_gate.shape[1]
    n_main = 3 * w_a + 2 * qk_g + 2 * v_g

    hp = x_prompt.reshape(bp * tp, d_model)
    hs = x_sample.reshape(bs * ts, d_model)
    outs = [[] for _ in range(6)]
    for l in range(depth):
        wm = w_in[l, :, :n_main].astype(BF16)
        wlr = jnp.pad(w_in[l, :, n_main:n_main + rank], ((0, 0), (0, LANES - rank))).astype(BF16)
        wgt = w_in[l, :, n_main + rank:].astype(BF16)
        wgg = jnp.pad(w_gla_gate[l], ((0, LANES - rank), (0, 0))).astype(BF16)
        bgg = b_gla_gate[l][None, :]
        wa, wb, wo = (w[l].astype(BF16) for w in (w_branch_a, w_branch_b, w_out))
        wu, wd = w_up[l].astype(BF16), w_down[l].astype(BF16)
        g_mix, g_pm, g_mlp, g_pl, g_gn = (g[l][None, :] for g in
                                          (g_pre_mix, g_post_mix, g_pre_mlp, g_post_mlp, g_gla_norm))

        def project(x, kv_seq_len):
            return _proj(x, g_mix, wm, wlr, wgt, wgg, bgg, w_a=w_a, qk_g=qk_g, v_g=v_g, kv_seq_len=kv_seq_len)

        def mix(x, b, t, oa, proj_out, s0):
            qg, kg, vg, rg, la, sga, sgb = proj_out[3:10]
            r3 = lambda a: a.reshape(b, t, a.shape[-1])
            og, s_new = _gla(r3(qg), r3(kg), r3(vg), r3(la), r3(rg), g_gn, s0, chunk=_gla_chunk(t), dk=dk, dv=dv)
            n = b * t
            x = _rowwise_call(_merge_kernel, "merge", [x, oa.reshape(n, w_a), og.reshape(n, v_g), sga, sgb],
                              [wa, wb, wo, g_pm], d_model)
            return _rowwise_call(_mlp_kernel, "mlp", [x], [wu, wd, g_mlp, g_pl], d_model), s_new

        pp = project(hp, tp)
        ps = project(hs, None)
        qa_p, kt_p, vt_p, kmean_p, k_bf, vt_bf = pp[0], pp[1], pp[2], pp[10], pp[11], pp[12]
        qa_s, ka_s, va_s = (a.reshape(bs, ts, w_a) for a in ps[:3])
        to_stored = lambda c: jnp.transpose(c[l], (0, 2, 3, 1))
        oa_p, oa_s = _moba(qa_p.reshape(bp, tp, w_a), k_bf.reshape(bp, tp, w_a), vt_bf,
                           kmean_p.reshape(bp, tp // MOBA_BLOCK, w_a), qa_s, ka_s, va_s,
                           to_stored(cache_k), to_stored(cache_v), page_table, hd=hd)
        hp, sp = mix(hp, bp, tp, oa_p, pp, jnp.zeros((bp, n_head_g, dk, dv), state_gla.dtype))
        hs, ssn = mix(hs, bs, ts, oa_s, ps, state_gla[l])
        kp, vp = (jnp.transpose(a.reshape(bp, n_head, hd, tp), (0, 3, 1, 2)) for a in (kt_p, vt_p))
        ksn, vsn = (a.reshape(bs, ts, n_head, hd) for a in (ka_s, va_s))
        for lst, val in zip(outs, (kp, vp, sp, ksn, vsn, ssn)):
            lst.append(val)
    return (hp.reshape(bp, tp, d_model), hs.reshape(bs, ts, d_model)) + tuple(jnp.stack(o) for o in outs)
```

```python
import functools

import ml_dtypes
import numpy as np
import jax
import jax.numpy as jnp
from jax import lax
from jax.experimental import pallas as pl
from jax.experimental.pallas import tpu as pltpu

F32 = jnp.float32
BF16 = jnp.bfloat16

LANES = 128
SUBLANES = 8
VMEM_LIMIT_BYTES = 56 * 1024 * 1024

EPS = 1e-6
MOBA_BLOCK = 256
MOBA_TOPK = 3
GLA_TAU = 16.0
NEG_BIG = -1e30
PENALTY = -30000.0
LOG2E = float(np.log2(np.e))

_NT = (((1,), (1,)), ((), ()))
_TN = (((0,), (0,)), ((), ()))


def _params(*sem):
    return pltpu.CompilerParams(dimension_semantics=sem, vmem_limit_bytes=VMEM_LIMIT_BYTES)


def _sigmoid(x):
    return 1.0 / (1.0 + jnp.exp(-x))


def _rms(x, g):
    return x * lax.rsqrt(jnp.mean(x * x, axis=-1, keepdims=True) + EPS) * g


def _bdot(a, b):
    return jnp.dot(a.astype(BF16), b.astype(BF16), preferred_element_type=F32)


def _bdot_nt(a, b):
    return lax.dot_general(a.astype(BF16), b.astype(BF16), _NT, preferred_element_type=F32)


def _proj_kernel(x_ref, g_ref, wm_ref, wlr_ref, wgt_ref, wgg_ref, bgg_ref,
                 qa_ref, ka_ref, va_ref, qg_ref, kg_ref, vg_ref, rg_ref, la_ref, sga_ref, sgb_ref, kmean_ref,
                 *attn_refs, w_a, qk_g, v_g, d_model, kv_transposed):
    hb = _rms(x_ref[...], g_ref[...]).astype(BF16)

    def proj(lo, n):
        return jnp.dot(hb, wm_ref[:, lo:lo + n], preferred_element_type=F32)

    qa_ref[...] = proj(0, w_a)
    ka = proj(w_a, w_a)
    va = proj(2 * w_a, w_a)
    for i in range(kmean_ref.shape[0]):
        kmean_ref[i] = jnp.sum(ka[i * MOBA_BLOCK:(i + 1) * MOBA_BLOCK], axis=0, keepdims=True) * (1.0 / MOBA_BLOCK)
    if kv_transposed:
        k_bf_ref, vt_bf_ref = attn_refs
        vt = va.T
        ka_ref[0] = ka.T
        va_ref[0] = vt
        k_bf_ref[...] = ka.astype(BF16)
        vt_bf_ref[0] = vt.astype(BF16)
    else:
        ka_ref[...] = ka
        va_ref[...] = va
    lo = 3 * w_a
    qg_ref[...] = proj(lo, qk_g)
    kg_ref[...] = proj(lo + qk_g, qk_g)
    vg_ref[...] = proj(lo + 2 * qk_g, v_g)
    rg_ref[...] = proj(lo + 2 * qk_g + v_g, v_g)
    lr = jnp.dot(hb, wlr_ref[...], preferred_element_type=F32)
    xg = jnp.dot(lr.astype(BF16), wgg_ref[...], preferred_element_type=F32) + bgg_ref[...]
    log_sig = jnp.minimum(xg, 0.0) - jnp.log1p(jnp.exp(-jnp.abs(xg)))
    la_ref[...] = log_sig * (1.0 / GLA_TAU)
    ga = jnp.dot(hb, wgt_ref[:, :d_model], preferred_element_type=F32)
    sga_ref[...] = _sigmoid(ga).astype(BF16)
    gb = jnp.dot(hb, wgt_ref[:, d_model:], preferred_element_type=F32)
    sgb_ref[...] = _sigmoid(gb).astype(BF16)


def _proj(x, g, wm, wlr, wgt, wgg, bgg, *, w_a, qk_g, v_g, kv_seq_len=None):
    n, d_model = x.shape
    tm = _row_tile(n)
    assert n % tm == 0
    nt = n // tm
    blocks_per_tile = tm // MOBA_BLOCK
    row = lambda w: pl.BlockSpec((tm, w), lambda i: (i, 0))
    full = lambda a: pl.BlockSpec(a.shape, lambda i: (0,) * a.ndim, pipeline_mode=pl.Buffered(1))
    widths = (w_a, w_a, w_a, qk_g, qk_g, v_g, v_g, qk_g)
    out_shape = [jax.ShapeDtypeStruct((n, w), F32) for w in widths]
    out_shape += [jax.ShapeDtypeStruct((n, d_model), BF16)] * 2
    out_shape += [jax.ShapeDtypeStruct((n // MOBA_BLOCK, 1, w_a), F32)]
    out_specs = [row(w) for w in widths] + [row(d_model)] * 2
    out_specs += [pl.BlockSpec((blocks_per_tile, 1, w_a), lambda i: (i, 0, 0))]
    if kv_seq_len is not None:
        assert kv_seq_len % tm == 0
        seq_tiles = kv_seq_len // tm
        transposed = pl.BlockSpec((1, w_a, tm), lambda i: (i // seq_tiles, 0, i % seq_tiles))
        for i in (1, 2):
            out_shape[i] = jax.ShapeDtypeStruct((n // kv_seq_len, w_a, kv_seq_len), F32)
            out_specs[i] = transposed
        out_shape += [jax.ShapeDtypeStruct((n, w_a), BF16),
                      jax.ShapeDtypeStruct((n // kv_seq_len, w_a, kv_seq_len), BF16)]
        out_specs += [row(w_a), transposed]
    return pl.pallas_call(
        functools.partial(_proj_kernel, w_a=w_a, qk_g=qk_g, v_g=v_g, d_model=d_model,
                          kv_transposed=kv_seq_len is not None),
        grid=(nt,),
        in_specs=[row(d_model), full(g), full(wm), full(wlr), full(wgt), full(wgg), full(bgg)],
        out_specs=out_specs,
        out_shape=out_shape,
        compiler_params=_params("parallel"),
        name="proj",
    )(x, g, wm, wlr, wgt, wgg, bgg)


def _top_blocks(gate, candidate, block, axis=-1):
    block_f = block.astype(F32)
    no_block = float(gate.shape[axis])
    avail = jnp.where(candidate, 1.0, 0.0)
    sel = jnp.zeros(gate.shape, F32)
    for _ in range(MOBA_TOPK):
        cur = jnp.where(avail > 0.0, gate, -jnp.inf)
        mx = jnp.max(cur, axis=axis, keepdims=True)
        cand = jnp.where(avail > 0.0, jnp.where(cur == mx, block_f, no_block), no_block)
        first = jnp.min(cand, axis=axis, keepdims=True)
        pick = block_f == first
        sel = jnp.where(pick, 1.0, sel)
        avail = jnp.where(pick, 0.0, avail)
    return sel


def _alibi_slopes(n_head):
    return 2.0 ** (-8.0 * (np.arange(n_head) + 1) / n_head)


_SLOPE_PARTS = 3
_AUG_EXTRA = 2 * _SLOPE_PARTS


def _moba_prompt_step(r, n_sub, hp, qi, sl_ref, q_ref, k_ref, vt_ref, kc_ref, km_ref, o_ref,
                      qaug_ref, m_ref, acc_ref, *, hd, tk):
    blk = MOBA_BLOCK
    t = k_ref.shape[1]
    tq = q_ref.shape[1]
    n_blk = t // blk
    per_pair = LANES // hd
    n_acc = acc_ref.shape[1]
    tiles_per_q = tq // tk
    chains = [(hh, cb) for hh in range(per_pair) for cb in range(tiles_per_q)]
    ones = jnp.ones((n_acc - LANES, tk), BF16)

    def tiles(j):
        keys = pl.ds(pl.multiple_of(j * tk, tk), tk)
        kt = jnp.concatenate([k_ref[0, keys, :], kc_ref[keys, :]], axis=1)
        return kt, jnp.concatenate([vt_ref[0, :, keys], ones], axis=0)

    def logits(kt, hh, cb):
        return jnp.dot(kt, qaug_ref[hh, :, cb * tk:(cb + 1) * tk], preferred_element_type=F32)

    def absorb(stat, s, va, c):
        m, acc = stat
        m_new = jnp.maximum(m, jnp.max(s, axis=0, keepdims=True) + c)
        p = jnp.exp2(s - (m_new - c)).astype(BF16)
        return m_new, jnp.exp2(m - m_new) * acc + jnp.dot(va, p, preferred_element_type=F32)

    @pl.when(r == 0)
    def _():
        lane = lax.broadcasted_iota(jnp.int32, (tq, LANES), 1)
        q = q_ref[0]
        n_rows = -(-n_blk // SUBLANES) * SUBLANES
        block = lax.broadcasted_iota(jnp.int32, (n_rows, tq), 0)
        own = qi * (tq // blk) + lax.broadcasted_iota(jnp.int32, (n_rows, tq), 1) // blk
        extra = lax.broadcasted_iota(jnp.int32, (SUBLANES, tq), 0)
        assert _AUG_EXTRA <= SUBLANES and n_rows + SUBLANES <= LANES
        for hh in range(per_pair):
            h = hp * per_pair + hh
            head_lanes = (lane >= hh * hd) & (lane < (hh + 1) * hd)
            qh = jnp.where(head_lanes, q, 0.0)
            gate = _bdot_nt(km_ref[0, :n_rows, :], qh)
            sel = _top_blocks(gate, block < own, block, axis=0)
            penalty = jnp.where((sel > 0.0) | (block == own) | (block >= n_blk), 0.0, PENALTY)
            parts = [sl_ref[1 + i, h] for i in range(_SLOPE_PARTS)]
            slope_rows = jnp.zeros((SUBLANES, tq), F32)
            for off, val in enumerate(parts + [part * blk for part in parts]):
                slope_rows = jnp.where(extra == off, val, slope_rows)
            qx_t = jnp.concatenate([penalty, slope_rows, jnp.zeros((LANES - n_rows - SUBLANES, tq), F32)], axis=0)
            qaug_ref[hh, :LANES, :] = (qh * (hd ** -0.5 * LOG2E)).T.astype(BF16)
            qaug_ref[hh, LANES:, :] = qx_t.astype(BF16)

        key2 = lax.broadcasted_iota(jnp.int32, (tk, tk), 0)
        qry2 = lax.broadcasted_iota(jnp.int32, (tk, tk), 1)
        stats = {c: (jnp.full((1, tk), NEG_BIG, F32), jnp.zeros((n_acc, tk), F32)) for c in chains}
        for d in range(tiles_per_q):
            kt, va = tiles(qi * tiles_per_q + d)
            live = [(hh, cb) for hh, cb in chains if cb >= d]
            ss = [logits(kt, hh, cb) for hh, cb in live]
            for (hh, cb), s in zip(live, ss):
                if cb == d:
                    s = jnp.where(key2 <= qry2, s, NEG_BIG)
                stats[hh, cb] = absorb(stats[hh, cb], s, va, 0.0)
        for ci, c in enumerate(chains):
            m_ref[ci], acc_ref[ci] = stats[c]

    def body(i, flat):
        j = r + i * n_sub
        kt, va = tiles(j)
        ss = [logits(kt, hh, cb) for hh, cb in chains]
        out = []
        for ci, (hh, cb) in enumerate(chains):
            c = -(sl_ref[0, hp * per_pair + hh] * tq) * (qi - j // tiles_per_q).astype(F32)
            out += absorb(flat[2 * ci:2 * ci + 2], ss[ci], va, c)
        return tuple(out)

    n_mine = (qi * tiles_per_q - r + n_sub - 1) // n_sub
    flat = lax.fori_loop(0, n_mine, body, tuple(x for ci in range(len(chains)) for x in (m_ref[ci], acc_ref[ci])))
    for ci in range(len(chains)):
        m_ref[ci], acc_ref[ci] = flat[2 * ci], flat[2 * ci + 1]

    @pl.when(r == n_sub - 1)
    def _():
        dim = lax.broadcasted_iota(jnp.int32, (LANES, tk), 0)
        for cb in range(tiles_per_q):
            out = None
            for hh in range(per_pair):
                acc = flat[2 * chains.index((hh, cb)) + 1]
                o_h = acc[:LANES] / acc[LANES:LANES + 1]
                out = o_h if out is None else jnp.where(dim >= hh * hd, o_h, out)
            o_ref[0, cb * tk:(cb + 1) * tk, :] = out.T


def _moba_consts(t, tq):
    n_blk = t // MOBA_BLOCK
    base = -(-n_blk // SUBLANES) * SUBLANES
    assert base + _AUG_EXTRA <= LANES
    pos = np.arange(t)
    c = np.zeros((t, LANES), np.float32)
    c[pos, pos // MOBA_BLOCK] = 1.0
    for i in range(_SLOPE_PARTS):
        c[:, base + i] = pos % MOBA_BLOCK
        c[:, base + _SLOPE_PARTS + i] = (pos % tq) // MOBA_BLOCK
    return jnp.asarray(c, dtype=BF16)


_BF16_ROWS = 2 * SUBLANES


def _moba_sample_step(j, n_k_steps, slopes_ref, q_ref, kn_ref, vn_ref, pages_ref,
                      o_ref, qbd_ref, s_ref, p_ref, ksum_ref, l_ref, acc_ref, *, pages_per_block, hd):
    n_per_step = pages_ref.shape[0]
    qk_pages, pv_pages = 4, 2
    ts, w = q_ref.shape[1:]
    n_head = w // hd
    rows = n_head * ts
    page = pages_ref.shape[3]
    blk = page * pages_per_block
    past = s_ref.shape[1]
    n_blk = past // blk
    scale = hd ** -0.5

    row_head = lax.broadcasted_iota(jnp.int32, (rows, 1), 0) // ts
    row_q = lax.broadcasted_iota(jnp.int32, (rows, 1), 0) % ts
    col_head = lax.broadcasted_iota(jnp.int32, (rows, w), 1) // hd
    slope = jnp.zeros((rows, 1), F32)
    for h in range(n_head):
        slope = jnp.where(row_head == h, slopes_ref[h], slope)

    @pl.when(j == 0)
    def _():
        qt = jnp.concatenate([q_ref[0]] * n_head, axis=0)
        qbd_ref[...] = jnp.where(col_head == row_head, qt, 0.0)
        ksum_ref[...] = jnp.zeros(ksum_ref.shape, F32)

    @pl.when(j < n_k_steps)
    def _():
        qs = (qbd_ref[...] * scale).astype(BF16)
        lane = lax.broadcasted_iota(jnp.int32, ksum_ref.shape, 1)
        ksum = ksum_ref[...]
        group = qk_pages if n_per_step % qk_pages == 0 and qk_pages % pages_per_block == 0 else pages_per_block
        for g0 in range(0, n_per_step, group):
            kts = [pages_ref[p_i].reshape(w, page) for p_i in range(g0, g0 + group)]
            first = j * n_per_step + g0
            s = jnp.dot(qs, jnp.concatenate([kt.astype(BF16) for kt in kts], axis=1), preferred_element_type=F32)
            s_ref[:, pl.ds(pl.multiple_of(first * page, group * page), group * page)] = s
            for b0 in range(0, group, pages_per_block):
                total = kts[b0]
                for kt in kts[b0 + 1:b0 + pages_per_block]:
                    total = total + kt
                ksum = jnp.where(lane == (first + b0) // pages_per_block, jnp.sum(total, axis=1, keepdims=True), ksum)
        ksum_ref[...] = ksum

    @pl.when(j == n_k_steps)
    def _():
        qbd = qbd_ref[...]
        gate = _bdot(qbd, ksum_ref[...])
        lane = lax.broadcasted_iota(jnp.int32, gate.shape, 1)
        sel = _top_blocks(gate, lane < n_blk, lane)
        s_own = _bdot_nt(qbd * scale, kn_ref[0])
        kq = lax.broadcasted_iota(jnp.int32, s_own.shape, 1)
        s_own = jnp.where(kq <= row_q, s_own + slope * kq.astype(F32), NEG_BIG)
        key_in_blk = lax.broadcasted_iota(jnp.int32, (1, blk), 1)

        def add_bias(b_i, m_wide):
            cols = pl.ds(pl.multiple_of(b_i * blk, blk), blk)
            picked = jnp.max(jnp.where(lane == b_i, sel, 0.0), axis=-1, keepdims=True)
            rel = (b_i * blk - past + key_in_blk).astype(F32)
            s = s_ref[:, cols] + slope * rel + jnp.where(picked > 0.0, 0.0, NEG_BIG)
            s_ref[:, cols] = s
            return jnp.maximum(m_wide, s)

        unroll = 8 if n_blk % 8 == 0 else 1
        m_wide = lax.fori_loop(0, n_blk, add_bias, jnp.full((rows, blk), NEG_BIG, F32), unroll=unroll)
        m = jnp.maximum(jnp.max(m_wide, axis=-1, keepdims=True), jnp.max(s_own, axis=-1, keepdims=True))

        def probs(b_i, l_wide):
            cols = pl.ds(pl.multiple_of(b_i * blk, blk), blk)
            p = jnp.exp(s_ref[:, cols] - m)
            p_ref[:, cols] = p.astype(BF16)
            return l_wide + p

        l_wide = lax.fori_loop(0, n_blk, probs, jnp.zeros((rows, blk), F32), unroll=unroll)
        p_own = jnp.exp(s_own - m)
        l_ref[...] = jnp.sum(l_wide, axis=-1, keepdims=True) + jnp.sum(p_own, axis=-1, keepdims=True)
        acc_ref[...] = _bdot(p_own, vn_ref[0])

    @pl.when(j >= n_k_steps)
    def _():
        acc = acc_ref[...]
        group = pv_pages if n_per_step % pv_pages == 0 else 1
        for g0 in range(0, n_per_step, group):
            first = (j - n_k_steps) * n_per_step + g0
            p = p_ref[:, pl.ds(pl.multiple_of(first * page, group * page), group * page)]
            vt = jnp.concatenate([pages_ref[p_i].reshape(w, page).astype(BF16) for p_i in range(g0, g0 + group)], axis=1)
            acc = acc + lax.dot_general(p, vt, _NT, preferred_element_type=F32)
        acc_ref[...] = acc

    @pl.when(j == 2 * n_k_steps - 1)
    def _():
        res = jnp.where(col_head == row_head, acc_ref[...] / l_ref[...], 0.0)
        out = res[0:ts]
        for h in range(1, n_head):
            out = out + res[h * ts:(h + 1) * ts]
        o_ref[0] = out


_SAMPLE_STEPS = 2

def _moba_kernel(pt_ref, sl_ref, slopes_ref, q_ref, k_ref, vt_ref, kc_ref, km_ref, qs_ref, kn_ref, vn_ref,
                 cache_k_hbm, cache_v_hbm, o_ref, os_ref, qaug_ref, m_ref, acc_ref, pages_ref, sem, *sample_scratch,
                 pages_per_block, hd, tk, n_k_steps):
    hp, qi, r = pl.program_id(1), pl.program_id(2), pl.program_id(3)
    n_sub = pl.num_programs(3)
    gstep = ((pl.program_id(0) * pl.num_programs(1) + hp) * pl.num_programs(2) + qi) * n_sub + r
    n_gsteps = pl.num_programs(0) * pl.num_programs(1) * pl.num_programs(2) * n_sub
    n_per_step = pages_ref.shape[1]
    per_seq = 2 * n_k_steps
    first = gstep * _SAMPLE_STEPS

    def start_pages(s, slot):
        seq, j = s // per_seq, s % per_seq

        def start(cache_hbm, group):
            for p in range(n_per_step):
                pltpu.make_async_copy(cache_hbm.at[pt_ref[seq, group * n_per_step + p]],
                                      pages_ref.at[slot, p], sem.at[slot]).start()

        @pl.when(j < n_k_steps)
        def _():
            start(cache_k_hbm, j)

        @pl.when(j >= n_k_steps)
        def _():
            start(cache_v_hbm, j - n_k_steps)

    def sample_step(s, slot):
        pltpu.make_async_copy(cache_k_hbm.at[pl.ds(0, n_per_step)], pages_ref.at[slot], sem.at[slot]).wait()
        _moba_sample_step(s % per_seq, n_k_steps, slopes_ref, qs_ref, kn_ref, vn_ref, pages_ref.at[slot],
                          os_ref, *sample_scratch, pages_per_block=pages_per_block, hd=hd)

    @pl.when(gstep == 0)
    def _():
        start_pages(first, 0)

    start_pages(first + 1, 1)
    _moba_prompt_step(r, n_sub, hp, qi, sl_ref, q_ref, k_ref, vt_ref, kc_ref, km_ref, o_ref,
                      qaug_ref, m_ref, acc_ref, hd=hd, tk=tk)
    sample_step(first, 0)

    @pl.when(gstep + 1 < n_gsteps)
    def _():
        start_pages(first + _SAMPLE_STEPS, 0)

    sample_step(first + 1, 1)


def _moba(qa, k_bf, vt_bf, kmean, qa_s, ka_s, va_s, cache_kt, cache_vt, page_table, *, hd):
    b, t, w = qa.shape
    bs, ts, _ = qa_s.shape
    _, n_head, _, page = cache_kt.shape
    n_pages = page_table.shape[1]
    blk = MOBA_BLOCK
    ppb = blk // page
    n_pair = w // LANES
    n_blk = t // blk
    tk = 2 * blk if t % (2 * blk) == 0 else blk
    tq = 2 * tk if t % (2 * tk) == 0 else tk
    n_q = t // tq
    n_per_step = next(p for p in (32, 16, 8, 4, 2) if n_pages % p == 0 and p % ppb == 0
                      and (bs * 2 * (n_pages // p)) % (b * n_pair * n_q * _SAMPLE_STEPS) == 0)
    n_k_steps = n_pages // n_per_step
    n_sub = bs * 2 * n_k_steps // (b * n_pair * n_q * _SAMPLE_STEPS)
    assert (2 * n_k_steps) % _SAMPLE_STEPS == 0
    assert n_pages // ppb <= LANES
    rows = n_head * ts

    sl2 = _alibi_slopes(n_head) * LOG2E
    parts, rest = [], sl2
    for _ in range(_SLOPE_PARTS):
        parts.append(rest.astype(ml_dtypes.bfloat16).astype(np.float64))
        rest = rest - parts[-1]
    sl = jnp.asarray(np.stack([sl2] + parts), dtype=F32)
    slopes = jnp.asarray(_alibi_slopes(n_head), dtype=F32)
    kc = _moba_consts(t, tq)
    km = jnp.pad(kmean, ((0, 0), (0, LANES - n_blk), (0, 0)))

    def sample_step(bi, hp, qi, r):
        return (((bi * n_pair + hp) * n_q + qi) * n_sub + r) * _SAMPLE_STEPS

    def per_seq(a):
        return pl.BlockSpec((1,) + a.shape[1:], lambda bi, hp, qi, r, pt: (sample_step(bi, hp, qi, r) // (2 * n_k_steps), 0, 0))

    smem = pl.BlockSpec(memory_space=pltpu.SMEM)
    hbm = pl.BlockSpec(memory_space=pl.ANY)
    n_chain = (LANES // hd) * (tq // tk)
    return pl.pallas_call(
        functools.partial(_moba_kernel, pages_per_block=ppb, hd=hd, tk=tk, n_k_steps=n_k_steps),
        grid_spec=pltpu.PrefetchScalarGridSpec(
            num_scalar_prefetch=1,
            grid=(b, n_pair, n_q, n_sub),
            in_specs=[smem, smem,
                      pl.BlockSpec((1, tq, LANES), lambda bi, hp, qi, r, pt: (bi, qi, hp)),
                      pl.BlockSpec((1, t, LANES), lambda bi, hp, qi, r, pt: (bi, 0, hp), pipeline_mode=pl.Buffered(1)),
                      pl.BlockSpec((1, LANES, t), lambda bi, hp, qi, r, pt: (bi, hp, 0), pipeline_mode=pl.Buffered(1)),
                      pl.BlockSpec((t, LANES), lambda bi, hp, qi, r, pt: (0, 0), pipeline_mode=pl.Buffered(1)),
                      pl.BlockSpec((1, LANES, LANES), lambda bi, hp, qi, r, pt: (bi, 0, hp)),
                      per_seq(qa_s), per_seq(ka_s), per_seq(va_s), hbm, hbm],
            out_specs=[pl.BlockSpec((1, tq, LANES), lambda bi, hp, qi, r, pt: (bi, qi, hp)), per_seq(qa_s)],
            scratch_shapes=[pltpu.VMEM((LANES // hd, 2 * LANES, tq), BF16),
                            pltpu.VMEM((n_chain, 1, tk), F32),
                            pltpu.VMEM((n_chain, LANES + _BF16_ROWS, tk), F32),
                            pltpu.VMEM((2, n_per_step, n_head, hd, page), F32),
                            pltpu.SemaphoreType.DMA((2,)),
                            pltpu.VMEM((rows, w), F32),
                            pltpu.VMEM((rows, n_pages * page), F32),
                            pltpu.VMEM((rows, n_pages * page), BF16),
                            pltpu.VMEM((w, LANES), F32),
                            pltpu.VMEM((rows, 1), F32), pltpu.VMEM((rows, w), F32)],
        ),
        out_shape=[jax.ShapeDtypeStruct((b, t, w), F32), jax.ShapeDtypeStruct((bs, ts, w), F32)],
        compiler_params=_params("arbitrary", "arbitrary", "arbitrary", "arbitrary"),
        name="moba",
    )(page_table, sl, slopes, qa, k_bf, vt_bf, kc, km, qa_s, ka_s, va_s, cache_kt, cache_vt)


def _gla_kernel(q_ref, k_ref, v_ref, la_ref, rg_ref, gn_ref, s0_ref, o_ref, sout_ref, st_ref, *, dk, dv, chunk):
    c_idx = pl.program_id(1)
    n_chunks = pl.num_programs(1)
    c = chunk
    n_pair = q_ref.shape[2] // LANES
    per_pair = LANES // dk
    assert dv == LANES and c % SUBLANES == 0

    @pl.when(c_idx == 0)
    def _():
        for p in range(n_pair):
            st_ref[p] = s0_ref[0, p].T

    row = lax.broadcasted_iota(jnp.int32, (c, LANES), 0)
    lane = lax.broadcasted_iota(jnp.int32, (c, LANES), 1)
    row2 = lax.broadcasted_iota(jnp.int32, (c, c), 0)
    col2 = lax.broadcasted_iota(jnp.int32, (c, c), 1)
    tri = jnp.where(row2 >= col2, 1.0, 0.0)
    nb8 = c // SUBLANES
    sub = lax.broadcasted_iota(jnp.int32, (nb8, SUBLANES, LANES), 1)
    lane3 = lax.broadcasted_iota(jnp.int32, (nb8, SUBLANES, LANES), 2)
    lane_st = lax.broadcasted_iota(jnp.int32, (dv, LANES), 1)

    def bcast_row(x3, jj):
        return jnp.broadcast_to(x3[:, jj:jj + 1, :], x3.shape)

    def one_chunk(p, rows, b_all, st):
        cols = slice(p * LANES, (p + 1) * LANES)
        q = q_ref[0, rows, cols] * (dk ** -0.5)
        k = k_ref[0, rows, cols]
        b = b_all[:, cols]
        vs = [v_ref[0, rows, (p * per_pair + hh) * dv:(p * per_pair + hh + 1) * dv] for hh in range(per_pair)]
        head_lanes = [(lane >= hh * dk) & (lane < (hh + 1) * dk) for hh in range(per_pair)]
        head_lanes3 = [(lane3 >= hh * dk) & (lane3 < (hh + 1) * dk) for hh in range(per_pair)]

        qe = q * jnp.exp(b)
        o = [_bdot_nt(jnp.where(head_lanes[hh], qe, 0.0), st) for hh in range(per_pair)]

        q3 = q.reshape(nb8, SUBLANES, LANES)
        k3 = k.reshape(nb8, SUBLANES, LANES)
        b3 = b.reshape(nb8, SUBLANES, LANES)
        v3 = [v.reshape(nb8, SUBLANES, dv) for v in vs]
        o3 = [jnp.zeros((nb8, SUBLANES, dv), F32) for _ in range(per_pair)]
        for jj in range(SUBLANES):
            e = jnp.exp(jnp.minimum(b3 - bcast_row(b3, jj), 0.0))
            term = jnp.where(sub >= jj, q3 * bcast_row(k3, jj) * e, 0.0)
            for hh in range(per_pair):
                a = jnp.sum(jnp.where(head_lanes3[hh], term, 0.0), axis=-1, keepdims=True)
                o3[hh] = o3[hh] + a * bcast_row(v3[hh], jj)
        o = [o[hh] + o3[hh].reshape(c, dv) for hh in range(per_pair)]

        attn = [jnp.zeros((c, c), F32) for _ in range(per_pair)]
        m_half = SUBLANES
        while 2 * m_half <= c:
            span = 2 * m_half
            bnd = jnp.broadcast_to(b.reshape(c // span, span, LANES)[:, m_half - 1:m_half, :],
                                   (c // span, span, LANES)).reshape(c, LANES)
            upper = (row % span) >= m_half
            qm = jnp.where(upper, q * jnp.exp(jnp.minimum(b - bnd, 0.0)), 0.0)
            km = jnp.where(upper, 0.0, k * jnp.exp(jnp.minimum(bnd - b, 0.0)))
            same = (row2 // span) == (col2 // span)
            for hh in range(per_pair):
                a = _bdot_nt(jnp.where(head_lanes[hh], qm, 0.0), km)
                attn[hh] = attn[hh] + jnp.where(same, a, 0.0)
            m_half = span
        if c > SUBLANES:
            o = [o[hh] + _bdot(attn[hh], vs[hh]) for hh in range(per_pair)]

        b_last = b[c - 1:c, :]
        kk = (k * jnp.exp(b_last - b)).astype(BF16)
        upd = lax.dot_general(vs[0].astype(BF16), kk, _TN, preferred_element_type=F32)
        for hh in range(1, per_pair):
            u = lax.dot_general(vs[hh].astype(BF16), kk, _TN, preferred_element_type=F32)
            upd = jnp.where(lane_st >= hh * dk, u, upd)
        for hh in range(per_pair):
            h = p * per_pair + hh
            rg = rg_ref[0, rows, h * dv:(h + 1) * dv]
            o_ref[0, rows, h * dv:(h + 1) * dv] = _rms(o[hh], gn_ref[...]) * (rg * _sigmoid(rg))
        return st * jnp.exp(b_last) + upd

    states = [st_ref[p] for p in range(n_pair)]
    for ch in range(q_ref.shape[1] // c):
        rows = slice(ch * c, (ch + 1) * c)
        b_all = jnp.dot(tri, la_ref[0, rows, :], precision=lax.Precision.HIGHEST, preferred_element_type=F32)
        states = [one_chunk(p, rows, b_all, states[p]) for p in range(n_pair)]
    for p in range(n_pair):
        st_ref[p] = states[p]

    @pl.when(c_idx == n_chunks - 1)
    def _():
        for p in range(n_pair):
            sout_ref[0, p] = states[p].T


def _gla(qg, kg, vg, la, rg, g_norm, s0, *, chunk, dk, dv):
    b, t, wk = qg.shape
    wv = vg.shape[2]
    n_head = wk // dk
    n_pair = wk // LANES
    s0p = s0.reshape(b, n_pair, LANES, dv)
    step = next(n * chunk for n in (4, 2, 1) if t % (n * chunk) == 0)
    tok = lambda w: pl.BlockSpec((1, step, w), lambda bi, ci: (bi, ci, 0))
    st_spec = pl.BlockSpec((1, n_pair, LANES, dv), lambda bi, ci: (bi, 0, 0, 0))
    og, s_new = pl.pallas_call(
        functools.partial(_gla_kernel, dk=dk, dv=dv, chunk=chunk),
        grid=(b, t // step),
        in_specs=[tok(wk), tok(wk), tok(wv), tok(wk), tok(wv),
                  pl.BlockSpec(g_norm.shape, lambda bi, ci: (0, 0)), st_spec],
        out_specs=[tok(wv), st_spec],
        out_shape=[jax.ShapeDtypeStruct((b, t, wv), F32), jax.ShapeDtypeStruct((b, n_pair, LANES, dv), F32)],
        scratch_shapes=[pltpu.VMEM((n_pair, dv, LANES), F32)],
        compiler_params=_params("parallel", "arbitrary"),
        name="gla",
    )(qg, kg, vg, la, rg, g_norm, s0p)
    return og, s_new.reshape(b, n_head, dk, dv)


def _merge_kernel(x_ref, oa_ref, og_ref, sga_ref, sgb_ref, wa_ref, wb_ref, wo_ref, g_ref, y_ref):
    merged = (sga_ref[...].astype(F32) * _bdot(oa_ref[...], wa_ref[...])
              + sgb_ref[...].astype(F32) * _bdot(og_ref[...], wb_ref[...]))
    y_ref[...] = x_ref[...] + _rms(_bdot(merged, wo_ref[...]), g_ref[...])


def _mlp_kernel(x_ref, wu_ref, wd_ref, g1_ref, g2_ref, y_ref):
    x = x_ref[...]
    u = _bdot(_rms(x, g1_ref[...]), wu_ref[...])
    u = jnp.square(jnp.maximum(u, 0.0))
    y_ref[...] = x + _rms(_bdot(u, wd_ref[...]), g2_ref[...])


def _row_tile(n):
    return 2 * MOBA_BLOCK if n % (2 * MOBA_BLOCK) == 0 else MOBA_BLOCK


def _rowwise_call(kernel, name, row_inputs, const_inputs, out_width):
    n = row_inputs[0].shape[0]
    tm = _row_tile(n)
    assert n % tm == 0
    row = lambda a: pl.BlockSpec((tm, a.shape[1]), lambda i: (i, 0))
    full = lambda a: pl.BlockSpec(a.shape, lambda i: (0,) * a.ndim, pipeline_mode=pl.Buffered(1))
    return pl.pallas_call(
        kernel,
        grid=(n // tm,),
        in_specs=[row(a) for a in row_inputs] + [full(a) for a in const_inputs],
        out_specs=pl.BlockSpec((tm, out_width), lambda i: (i, 0)),
        out_shape=jax.ShapeDtypeStruct((n, out_width), F32),
        compiler_params=_params("parallel"),
        name=name,
    )(*row_inputs, *const_inputs)


def _gla_chunk(t):
    c = SUBLANES
    while c * 2 <= min(t, LANES) and t % (c * 2) == 0:
        c *= 2
    return c


def kernel(x_prompt, x_sample, cache_k, cache_v, page_table, state_gla, w_in, w_gla_gate, b_gla_gate, g_gla_norm,
           w_branch_a, w_branch_b, w_out, w_up, w_down, g_pre_mix, g_post_mix, g_pre_mlp, g_post_mlp):
    bp, tp, d_model = x_prompt.shape
    bs, ts, _ = x_sample.shape
    depth, n_phys, page, n_head, hd = cache_k.shape
    _, _, n_head_g, dk, dv = state_gla.shape
    w_a = n_head * hd
    qk_g = n_head_g * dk
    v_g = n_head_g * dv
    rank = w_gla_gate.shape[1]
    n_main = 3 * w_a + 2 * qk_g + 2 * v_g

    hp = x_prompt.reshape(bp * tp, d_model)
    hs = x_sample.reshape(bs * ts, d_model)
    outs = [[] for _ in range(6)]
    for l in range(depth):
        wm = w_in[l, :, :n_main].astype(BF16)
        wlr = jnp.pad(w_in[l, :, n_main:n_main + rank], ((0, 0), (0, LANES - rank))).astype(BF16)
        wgt = w_in[l, :, n_main + rank:].astype(BF16)
        wgg = jnp.pad(w_gla_gate[l], ((0, LANES - rank), (0, 0))).astype(BF16)
        bgg = b_gla_gate[l][None, :]
        wa, wb, wo = (w[l].astype(BF16) for w in (w_branch_a, w_branch_b, w_out))
        wu, wd = w_up[l].astype(BF16), w_down[l].astype(BF16)
        g_mix, g_pm, g_mlp, g_pl, g_gn = (g[l][None, :] for g in
                                          (g_pre_mix, g_post_mix, g_pre_mlp, g_post_mlp, g_gla_norm))

        def project(x, kv_seq_len):
            return _proj(x, g_mix, wm, wlr, wgt, wgg, bgg, w_a=w_a, qk_g=qk_g, v_g=v_g, kv_seq_len=kv_seq_len)

        def mix(x, b, t, oa, proj_out, s0):
            qg, kg, vg, rg, la, sga, sgb = proj_out[3:10]
            r3 = lambda a: a.reshape(b, t, a.shape[-1])
            og, s_new = _gla(r3(qg), r3(kg), r3(vg), r3(la), r3(rg), g_gn, s0, chunk=_gla_chunk(t), dk=dk, dv=dv)
            n = b * t
            x = _rowwise_call(_merge_kernel, "merge", [x, oa.reshape(n, w_a), og.reshape(n, v_g), sga, sgb],
                              [wa, wb, wo, g_pm], d_model)
            return _rowwise_call(_mlp_kernel, "mlp", [x], [wu, wd, g_mlp, g_pl], d_model), s_new

        pp = project(hp, tp)
        ps = project(hs, None)
        qa_p, kt_p, vt_p, kmean_p, k_bf, vt_bf = pp[0], pp[1], pp[2], pp[10], pp[11], pp[12]
        qa_s, ka_s, va_s = (a.reshape(bs, ts, w_a) for a in ps[:3])
        to_stored = lambda c: jnp.transpose(c[l], (0, 2, 3, 1))
        oa_p, oa_s = _moba(qa_p.reshape(bp, tp, w_a), k_bf.reshape(bp, tp, w_a), vt_bf,
                           kmean_p.reshape(bp, tp // MOBA_BLOCK, w_a), qa_s, ka_s, va_s,
                           to_stored(cache_k), to_stored(cache_v), page_table, hd=hd)
        hp, sp = mix(hp, bp, tp, oa_p, pp, jnp.zeros((bp, n_head_g, dk, dv), state_gla.dtype))
        hs, ssn = mix(hs, bs, ts, oa_s, ps, state_gla[l])
        kp, vp = (jnp.transpose(a.reshape(bp, n_head, hd, tp), (0, 3, 1, 2)) for a in (kt_p, vt_p))
        ksn, vsn = (a.reshape(bs, ts, n_head, hd) for a in (ka_s, va_s))
        for lst, val in zip(outs, (kp, vp, sp, ksn, vsn, ssn)):
            lst.append(val)
    return (hp.reshape(bp, tp, d_model), hs.reshape(bs, ts, d_model)) + tuple(jnp.stack(o) for o in outs)
```

```python
import functools

import ml_dtypes
import numpy as np
import jax
import jax.numpy as jnp
from jax import lax
from jax.experimental import pallas as pl
from jax.experimental.pallas import tpu as pltpu

F32 = jnp.float32
BF16 = jnp.bfloat16

LANES = 128
SUBLANES = 8
VMEM_LIMIT_BYTES = 56 * 1024 * 1024

EPS = 1e-6
MOBA_BLOCK = 256
MOBA_TOPK = 3
GLA_TAU = 16.0
NEG_BIG = -1e30
PENALTY = -30000.0
LOG2E = float(np.log2(np.e))

_NT = (((1,), (1,)), ((), ()))
_TN = (((0,), (0,)), ((), ()))


def _params(*sem):
    return pltpu.CompilerParams(dimension_semantics=sem, vmem_limit_bytes=VMEM_LIMIT_BYTES)


def _sigmoid(x):
    return 1.0 / (1.0 + jnp.exp(-x))


def _rms(x, g):
    return x * lax.rsqrt(jnp.mean(x * x, axis=-1, keepdims=True) + EPS) * g


def _bdot(a, b):
    return jnp.dot(a.astype(BF16), b.astype(BF16), preferred_element_type=F32)


def _bdot_nt(a, b):
    return lax.dot_general(a.astype(BF16), b.astype(BF16), _NT, preferred_element_type=F32)


def _proj_kernel(x_ref, g_ref, wm_ref, wlr_ref, wgt_ref, wgg_ref, bgg_ref,
                 qa_ref, ka_ref, va_ref, qg_ref, kg_ref, vg_ref, rg_ref, la_ref, sga_ref, sgb_ref, kmean_ref,
                 *attn_refs, w_a, qk_g, v_g, d_model, kv_transposed):
    hb = _rms(x_ref[...], g_ref[...]).astype(BF16)

    def proj(lo, n):
        return jnp.dot(hb, wm_ref[:, lo:lo + n], preferred_element_type=F32)

    qa_ref[...] = proj(0, w_a)
    ka = proj(w_a, w_a)
    va = proj(2 * w_a, w_a)
    for i in range(kmean_ref.shape[0]):
        kmean_ref[i] = jnp.sum(ka[i * MOBA_BLOCK:(i + 1) * MOBA_BLOCK], axis=0, keepdims=True) * (1.0 / MOBA_BLOCK)
    if kv_transposed:
        k_bf_ref, vt_bf_ref = attn_refs
        vt = va.T
        ka_ref[0] = ka.T
        va_ref[0] = vt
        k_bf_ref[...] = ka.astype(BF16)
        vt_bf_ref[0] = vt.astype(BF16)
    else:
        ka_ref[...] = ka
        va_ref[...] = va
    lo = 3 * w_a
    qg_ref[...] = proj(lo, qk_g)
    kg_ref[...] = proj(lo + qk_g, qk_g)
    vg_ref[...] = proj(lo + 2 * qk_g, v_g)
    rg_ref[...] = proj(lo + 2 * qk_g + v_g, v_g)
    lr = jnp.dot(hb, wlr_ref[...], preferred_element_type=F32)
    xg = jnp.dot(lr.astype(BF16), wgg_ref[...], preferred_element_type=F32) + bgg_ref[...]
    log_sig = jnp.minimum(xg, 0.0) - jnp.log1p(jnp.exp(-jnp.abs(xg)))
    la_ref[...] = log_sig * (1.0 / GLA_TAU)
    ga = jnp.dot(hb, wgt_ref[:, :d_model], preferred_element_type=F32)
    sga_ref[...] = _sigmoid(ga).astype(BF16)
    gb = jnp.dot(hb, wgt_ref[:, d_model:], preferred_element_type=F32)
    sgb_ref[...] = _sigmoid(gb).astype(BF16)


def _proj(x, g, wm, wlr, wgt, wgg, bgg, *, w_a, qk_g, v_g, kv_seq_len=None):
    n, d_model = x.shape
    tm = _row_tile(n)
    assert n % tm == 0
    nt = n // tm
    blocks_per_tile = tm // MOBA_BLOCK
    row = lambda w: pl.BlockSpec((tm, w), lambda i: (i, 0))
    full = lambda a: pl.BlockSpec(a.shape, lambda i: (0,) * a.ndim, pipeline_mode=pl.Buffered(1))
    widths = (w_a, w_a, w_a, qk_g, qk_g, v_g, v_g, qk_g)
    out_shape = [jax.ShapeDtypeStruct((n, w), F32) for w in widths]
    out_shape += [jax.ShapeDtypeStruct((n, d_model), BF16)] * 2
    out_shape += [jax.ShapeDtypeStruct((n // MOBA_BLOCK, 1, w_a), F32)]
    out_specs = [row(w) for w in widths] + [row(d_model)] * 2
    out_specs += [pl.BlockSpec((blocks_per_tile, 1, w_a), lambda i: (i, 0, 0))]
    if kv_seq_len is not None:
        assert kv_seq_len % tm == 0
        seq_tiles = kv_seq_len // tm
        transposed = pl.BlockSpec((1, w_a, tm), lambda i: (i // seq_tiles, 0, i % seq_tiles))
        for i in (1, 2):
            out_shape[i] = jax.ShapeDtypeStruct((n // kv_seq_len, w_a, kv_seq_len), F32)
            out_specs[i] = transposed
        out_shape += [jax.ShapeDtypeStruct((n, w_a), BF16),
                      jax.ShapeDtypeStruct((n // kv_seq_len, w_a, kv_seq_len), BF16)]
        out_specs += [row(w_a), transposed]
    return pl.pallas_call(
        functools.partial(_proj_kernel, w_a=w_a, qk_g=qk_g, v_g=v_g, d_model=d_model,
                          kv_transposed=kv_seq_len is not None),
        grid=(nt,),
        in_specs=[row(d_model), full(g), full(wm), full(wlr), full(wgt), full(wgg), full(bgg)],
        out_specs=out_specs,
        out_shape=out_shape,
        compiler_params=_params("parallel"),
        name="proj",
    )(x, g, wm, wlr, wgt, wgg, bgg)


def _top_blocks(gate, candidate, block, axis=-1):
    block_f = block.astype(F32)
    no_block = float(gate.shape[axis])
    avail = jnp.where(candidate, 1.0, 0.0)
    sel = jnp.zeros(gate.shape, F32)
    for _ in range(MOBA_TOPK):
        cur = jnp.where(avail > 0.0, gate, -jnp.inf)
        mx = jnp.max(cur, axis=axis, keepdims=True)
        cand = jnp.where(avail > 0.0, jnp.where(cur == mx, block_f, no_block), no_block)
        first = jnp.min(cand, axis=axis, keepdims=True)
        pick = block_f == first
        sel = jnp.where(pick, 1.0, sel)
        avail = jnp.where(pick, 0.0, avail)
    return sel


def _alibi_slopes(n_head):
    return 2.0 ** (-8.0 * (np.arange(n_head) + 1) / n_head)


_SLOPE_PARTS = 3
_AUG_EXTRA = 2 * _SLOPE_PARTS


def _moba_prompt_step(r, n_sub, hp, qi, sl_ref, q_ref, k_ref, vt_ref, kc_ref, km_ref, o_ref,
                      qaug_ref, m_ref, acc_ref, *, hd, tk):
    blk = MOBA_BLOCK
    t = k_ref.shape[1]
    tq = q_ref.shape[1]
    n_blk = t // blk
    per_pair = LANES // hd
    n_acc = acc_ref.shape[1]
    tiles_per_q = tq // tk
    chains = [(hh, cb) for hh in range(per_pair) for cb in range(tiles_per_q)]
    ones = jnp.ones((n_acc - LANES, tk), BF16)

    def tiles(j):
        keys = pl.ds(pl.multiple_of(j * tk, tk), tk)
        kt = jnp.concatenate([k_ref[0, keys, :], kc_ref[keys, :]], axis=1)
        return kt, jnp.concatenate([vt_ref[0, :, keys], ones], axis=0)

    def logits(kt, hh, cb):
        return jnp.dot(kt, qaug_ref[hh, :, cb * tk:(cb + 1) * tk], preferred_element_type=F32)

    def absorb(stat, s, va, c):
        m, acc = stat
        m_new = jnp.maximum(m, jnp.max(s, axis=0, keepdims=True) + c)
        p = jnp.exp2(s - (m_new - c)).astype(BF16)
        return m_new, jnp.exp2(m - m_new) * acc + jnp.dot(va, p, preferred_element_type=F32)

    @pl.when(r == 0)
    def _():
        lane = lax.broadcasted_iota(jnp.int32, (tq, LANES), 1)
        q = q_ref[0]
        n_rows = -(-n_blk // SUBLANES) * SUBLANES
        block = lax.broadcasted_iota(jnp.int32, (n_rows, tq), 0)
        own = qi * (tq // blk) + lax.broadcasted_iota(jnp.int32, (n_rows, tq), 1) // blk
        extra = lax.broadcasted_iota(jnp.int32, (SUBLANES, tq), 0)
        assert _AUG_EXTRA <= SUBLANES and n_rows + SUBLANES <= LANES
        for hh in range(per_pair):
            h = hp * per_pair + hh
            head_lanes = (lane >= hh * hd) & (lane < (hh + 1) * hd)
            qh = jnp.where(head_lanes, q, 0.0)
            gate = _bdot_nt(km_ref[0, :n_rows, :], qh)
            sel = _top_blocks(gate, block < own, block, axis=0)
            penalty = jnp.where((sel > 0.0) | (block == own) | (block >= n_blk), 0.0, PENALTY)
            parts = [sl_ref[1 + i, h] for i in range(_SLOPE_PARTS)]
            slope_rows = jnp.zeros((SUBLANES, tq), F32)
            for off, val in enumerate(parts + [part * blk for part in parts]):
                slope_rows = jnp.where(extra == off, val, slope_rows)
            qx_t = jnp.concatenate([penalty, slope_rows, jnp.zeros((LANES - n_rows - SUBLANES, tq), F32)], axis=0)
            qaug_ref[hh, :LANES, :] = (qh * (hd ** -0.5 * LOG2E)).T.astype(BF16)
            qaug_ref[hh, LANES:, :] = qx_t.astype(BF16)

        key2 = lax.broadcasted_iota(jnp.int32, (tk, tk), 0)
        qry2 = lax.broadcasted_iota(jnp.int32, (tk, tk), 1)
        stats = {c: (jnp.full((1, tk), NEG_BIG, F32), jnp.zeros((n_acc, tk), F32)) for c in chains}
        for d in range(tiles_per_q):
            kt, va = tiles(qi * tiles_per_q + d)
            live = [(hh, cb) for hh, cb in chains if cb >= d]
            ss = [logits(kt, hh, cb) for hh, cb in live]
            for (hh, cb), s in zip(live, ss):
                if cb == d:
                    s = jnp.where(key2 <= qry2, s, NEG_BIG)
                stats[hh, cb] = absorb(stats[hh, cb], s, va, 0.0)
        for ci, c in enumerate(chains):
            m_ref[ci], acc_ref[ci] = stats[c]

    def body(i, flat):
        j = r + i * n_sub
        kt, va = tiles(j)
        ss = [logits(kt, hh, cb) for hh, cb in chains]
        out = []
        for ci, (hh, cb) in enumerate(chains):
            c = -(sl_ref[0, hp * per_pair + hh] * tq) * (qi - j // tiles_per_q).astype(F32)
            out += absorb(flat[2 * ci:2 * ci + 2], ss[ci], va, c)
        return tuple(out)

    n_mine = (qi * tiles_per_q - r + n_sub - 1) // n_sub
    flat = lax.fori_loop(0, n_mine, body, tuple(x for ci in range(len(chains)) for x in (m_ref[ci], acc_ref[ci])))
    for ci in range(len(chains)):
        m_ref[ci], acc_ref[ci] = flat[2 * ci], flat[2 * ci + 1]

    @pl.when(r == n_sub - 1)
    def _():
        dim = lax.broadcasted_iota(jnp.int32, (LANES, tk), 0)
        for cb in range(tiles_per_q):
            out = None
            for hh in range(per_pair):
                acc = flat[2 * chains.index((hh, cb)) + 1]
                o_h = acc[:LANES] / acc[LANES:LANES + 1]
                out = o_h if out is None else jnp.where(dim >= hh * hd, o_h, out)
            o_ref[0, cb * tk:(cb + 1) * tk, :] = out.T


def _moba_consts(t, tq):
    n_blk = t // MOBA_BLOCK
    base = -(-n_blk // SUBLANES) * SUBLANES
    assert base + _AUG_EXTRA <= LANES
    pos = np.arange(t)
    c = np.zeros((t, LANES), np.float32)
    c[pos, pos // MOBA_BLOCK] = 1.0
    for i in range(_SLOPE_PARTS):
        c[:, base + i] = pos % MOBA_BLOCK
        c[:, base + _SLOPE_PARTS + i] = (pos % tq) // MOBA_BLOCK
    return jnp.asarray(c, dtype=BF16)


_BF16_ROWS = 2 * SUBLANES


def _moba_sample_step(j, n_k_steps, slopes_ref, q_ref, kn_ref, vn_ref, pages_ref,
                      o_ref, qbd_ref, s_ref, p_ref, ksum_ref, l_ref, acc_ref, *, pages_per_block, hd):
    n_per_step = pages_ref.shape[0]
    qk_pages, pv_pages = 4, 2
    ts, w = q_ref.shape[1:]
    n_head = w // hd
    rows = n_head * ts
    page = pages_ref.shape[3]
    blk = page * pages_per_block
    past = s_ref.shape[1]
    n_blk = past // blk
    scale = hd ** -0.5

    row_head = lax.broadcasted_iota(jnp.int32, (rows, 1), 0) // ts
    row_q = lax.broadcasted_iota(jnp.int32, (rows, 1), 0) % ts
    col_head = lax.broadcasted_iota(jnp.int32, (rows, w), 1) // hd
    slope = jnp.zeros((rows, 1), F32)
    for h in range(n_head):
        slope = jnp.where(row_head == h, slopes_ref[h], slope)

    @pl.when(j == 0)
    def _():
        qt = jnp.concatenate([q_ref[0]] * n_head, axis=0)
        qbd_ref[...] = jnp.where(col_head == row_head, qt, 0.0)
        ksum_ref[...] = jnp.zeros(ksum_ref.shape, F32)

    @pl.when(j < n_k_steps)
    def _():
        qs = (qbd_ref[...] * scale).astype(BF16)
        lane = lax.broadcasted_iota(jnp.int32, ksum_ref.shape, 1)
        ksum = ksum_ref[...]
        group = qk_pages if n_per_step % qk_pages == 0 and qk_pages % pages_per_block == 0 else pages_per_block
        for g0 in range(0, n_per_step, group):
            kts = [pages_ref[p_i].reshape(w, page) for p_i in range(g0, g0 + group)]
            first = j * n_per_step + g0
            s = jnp.dot(qs, jnp.concatenate([kt.astype(BF16) for kt in kts], axis=1), preferred_element_type=F32)
            s_ref[:, pl.ds(pl.multiple_of(first * page, group * page), group * page)] = s
            for b0 in range(0, group, pages_per_block):
                total = kts[b0]
                for kt in kts[b0 + 1:b0 + pages_per_block]:
                    total = total + kt
                ksum = jnp.where(lane == (first + b0) // pages_per_block, jnp.sum(total, axis=1, keepdims=True), ksum)
        ksum_ref[...] = ksum

    @pl.when(j == n_k_steps)
    def _():
        qbd = qbd_ref[...]
        gate = _bdot(qbd, ksum_ref[...])
        lane = lax.broadcasted_iota(jnp.int32, gate.shape, 1)
        sel = _top_blocks(gate, lane < n_blk, lane)
        s_own = _bdot_nt(qbd * scale, kn_ref[0])
        kq = lax.broadcasted_iota(jnp.int32, s_own.shape, 1)
        s_own = jnp.where(kq <= row_q, s_own + slope * kq.astype(F32), NEG_BIG)
        key_in_blk = lax.broadcasted_iota(jnp.int32, (1, blk), 1)

        def add_bias(b_i, m_wide):
            cols = pl.ds(pl.multiple_of(b_i * blk, blk), blk)
            picked = jnp.max(jnp.where(lane == b_i, sel, 0.0), axis=-1, keepdims=True)
            rel = (b_i * blk - past + key_in_blk).astype(F32)
            s = s_ref[:, cols] + slope * rel + jnp.where(picked > 0.0, 0.0, NEG_BIG)
            s_ref[:, cols] = s
            return jnp.maximum(m_wide, s)

        unroll = 8 if n_blk % 8 == 0 else 1
        m_wide = lax.fori_loop(0, n_blk, add_bias, jnp.full((rows, blk), NEG_BIG, F32), unroll=unroll)
        m = jnp.maximum(jnp.max(m_wide, axis=-1, keepdims=True), jnp.max(s_own, axis=-1, keepdims=True))

        def probs(b_i, l_wide):
            cols = pl.ds(pl.multiple_of(b_i * blk, blk), blk)
            p = jnp.exp(s_ref[:, cols] - m)
            p_ref[:, cols] = p.astype(BF16)
            return l_wide + p

        l_wide = lax.fori_loop(0, n_blk, probs, jnp.zeros((rows, blk), F32), unroll=unroll)
        p_own = jnp.exp(s_own - m)
        l_ref[...] = jnp.sum(l_wide, axis=-1, keepdims=True) + jnp.sum(p_own, axis=-1, keepdims=True)
        acc_ref[...] = _bdot(p_own, vn_ref[0])

    @pl.when(j >= n_k_steps)
    def _():
        acc = acc_ref[...]
        group = pv_pages if n_per_step % pv_pages == 0 else 1
        for g0 in range(0, n_per_step, group):
            first = (j - n_k_steps) * n_per_step + g0
            p = p_ref[:, pl.ds(pl.multiple_of(first * page, group * page), group * page)]
            vt = jnp.concatenate([pages_ref[p_i].reshape(w, page).astype(BF16) for p_i in range(g0, g0 + group)], axis=1)
            acc = acc + lax.dot_general(p, vt, _NT, preferred_element_type=F32)
        acc_ref[...] = acc

    @pl.when(j == 2 * n_k_steps - 1)
    def _():
        res = jnp.where(col_head == row_head, acc_ref[...] / l_ref[...], 0.0)
        out = res[0:ts]
        for h in range(1, n_head):
            out = out + res[h * ts:(h + 1) * ts]
        o_ref[0] = out


_SAMPLE_STEPS = 2

def _moba_kernel(pt_ref, sl_ref, slopes_ref, q_ref, k_ref, vt_ref, kc_ref, km_ref, qs_ref, kn_ref, vn_ref,
                 cache_k_hbm, cache_v_hbm, o_ref, os_ref, qaug_ref, m_ref, acc_ref, pages_ref, sem, *sample_scratch,
                 pages_per_block, hd, tk, n_k_steps):
    hp, qi, r = pl.program_id(1), pl.program_id(2), pl.program_id(3)
    n_sub = pl.num_programs(3)
    gstep = ((pl.program_id(0) * pl.num_programs(1) + hp) * pl.num_programs(2) + qi) * n_sub + r
    n_gsteps = pl.num_programs(0) * pl.num_programs(1) * pl.num_programs(2) * n_sub
    n_per_step = pages_ref.shape[1]
    per_seq = 2 * n_k_steps
    first = gstep * _SAMPLE_STEPS

    def start_pages(s, slot):
        seq, j = s // per_seq, s % per_seq

        def start(cache_hbm, group):
            for p in range(n_per_step):
                pltpu.make_async_copy(cache_hbm.at[pt_ref[seq, group * n_per_step + p]],
                                      pages_ref.at[slot, p], sem.at[slot]).start()

        @pl.when(j < n_k_steps)
        def _():
            start(cache_k_hbm, j)

        @pl.when(j >= n_k_steps)
        def _():
            start(cache_v_hbm, j - n_k_steps)

    def sample_step(s, slot):
        pltpu.make_async_copy(cache_k_hbm.at[pl.ds(0, n_per_step)], pages_ref.at[slot], sem.at[slot]).wait()
        _moba_sample_step(s % per_seq, n_k_steps, slopes_ref, qs_ref, kn_ref, vn_ref, pages_ref.at[slot],
                          os_ref, *sample_scratch, pages_per_block=pages_per_block, hd=hd)

    @pl.when(gstep == 0)
    def _():
        start_pages(first, 0)

    start_pages(first + 1, 1)
    _moba_prompt_step(r, n_sub, hp, qi, sl_ref, q_ref, k_ref, vt_ref, kc_ref, km_ref, o_ref,
                      qaug_ref, m_ref, acc_ref, hd=hd, tk=tk)
    sample_step(first, 0)

    @pl.when(gstep + 1 < n_gsteps)
    def _():
        start_pages(first + _SAMPLE_STEPS, 0)

    sample_step(first + 1, 1)


def _moba(qa, k_bf, vt_bf, kmean, qa_s, ka_s, va_s, cache_kt, cache_vt, page_table, *, hd):
    b, t, w = qa.shape
    bs, ts, _ = qa_s.shape
    _, n_head, _, page = cache_kt.shape
    n_pages = page_table.shape[1]
    blk = MOBA_BLOCK
    ppb = blk // page
    n_pair = w // LANES
    n_blk = t // blk
    tk = 2 * blk if t % (2 * blk) == 0 else blk
    tq = 2 * tk if t % (2 * tk) == 0 else tk
    n_q = t // tq
    n_per_step = next(p for p in (32, 16, 8, 4, 2) if n_pages % p == 0 and p % ppb == 0
                      and (bs * 2 * (n_pages // p)) % (b * n_pair * n_q * _SAMPLE_STEPS) == 0)
    n_k_steps = n_pages // n_per_step
    n_sub = bs * 2 * n_k_steps // (b * n_pair * n_q * _SAMPLE_STEPS)
    assert (2 * n_k_steps) % _SAMPLE_STEPS == 0
    assert n_pages // ppb <= LANES
    rows = n_head * ts

    sl2 = _alibi_slopes(n_head) * LOG2E
    parts, rest = [], sl2
    for _ in range(_SLOPE_PARTS):
        parts.append(rest.astype(ml_dtypes.bfloat16).astype(np.float64))
        rest = rest - parts[-1]
    sl = jnp.asarray(np.stack([sl2] + parts), dtype=F32)
    slopes = jnp.asarray(_alibi_slopes(n_head), dtype=F32)
    kc = _moba_consts(t, tq)
    km = jnp.pad(kmean, ((0, 0), (0, LANES - n_blk), (0, 0)))

    def sample_step(bi, hp, qi, r):
        return (((bi * n_pair + hp) * n_q + qi) * n_sub + r) * _SAMPLE_STEPS

    def per_seq(a):
        return pl.BlockSpec((1,) + a.shape[1:], lambda bi, hp, qi, r, pt: (sample_step(bi, hp, qi, r) // (2 * n_k_steps), 0, 0))

    smem = pl.BlockSpec(memory_space=pltpu.SMEM)
    hbm = pl.BlockSpec(memory_space=pl.ANY)
    n_chain = (LANES // hd) * (tq // tk)
    return pl.pallas_call(
        functools.partial(_moba_kernel, pages_per_block=ppb, hd=hd, tk=tk, n_k_steps=n_k_steps),
        grid_spec=pltpu.PrefetchScalarGridSpec(
            num_scalar_prefetch=1,
            grid=(b, n_pair, n_q, n_sub),
            in_specs=[smem, smem,
                      pl.BlockSpec((1, tq, LANES), lambda bi, hp, qi, r, pt: (bi, qi, hp)),
                      pl.BlockSpec((1, t, LANES), lambda bi, hp, qi, r, pt: (bi, 0, hp), pipeline_mode=pl.Buffered(1)),
                      pl.BlockSpec((1, LANES, t), lambda bi, hp, qi, r, pt: (bi, hp, 0), pipeline_mode=pl.Buffered(1)),
                      pl.BlockSpec((t, LANES), lambda bi, hp, qi, r, pt: (0, 0), pipeline_mode=pl.Buffered(1)),
                      pl.BlockSpec((1, LANES, LANES), lambda bi, hp, qi, r, pt: (bi, 0, hp)),
                      per_seq(qa_s), per_seq(ka_s), per_seq(va_s), hbm, hbm],
            out_specs=[pl.BlockSpec((1, tq, LANES), lambda bi, hp, qi, r, pt: (bi, qi, hp)), per_seq(qa_s)],
            scratch_shapes=[pltpu.VMEM((LANES // hd, 2 * LANES, tq), BF16),
                            pltpu.VMEM((n_chain, 1, tk), F32),
                            pltpu.VMEM((n_chain, LANES + _BF16_ROWS, tk), F32),
                            pltpu.VMEM((2, n_per_step, n_head, hd, page), F32),
                            pltpu.SemaphoreType.DMA((2,)),
                            pltpu.VMEM((rows, w), F32),
                            pltpu.VMEM((rows, n_pages * page), F32),
                            pltpu.VMEM((rows, n_pages * page), BF16),
                            pltpu.VMEM((w, LANES), F32),
                            pltpu.VMEM((rows, 1), F32), pltpu.VMEM((rows, w), F32)],
        ),
        out_shape=[jax.ShapeDtypeStruct((b, t, w), F32), jax.ShapeDtypeStruct((bs, ts, w), F32)],
        compiler_params=_params("arbitrary", "arbitrary", "arbitrary", "arbitrary"),
        name="moba",
    )(page_table, sl, slopes, qa, k_bf, vt_bf, kc, km, qa_s, ka_s, va_s, cache_kt, cache_vt)


def _gla_kernel(q_ref, k_ref, v_ref, la_ref, rg_ref, gn_ref, s0_ref, o_ref, sout_ref, st_ref, *, dk, dv, chunk):
    c_idx = pl.program_id(1)
    n_chunks = pl.num_programs(1)
    c = chunk
    n_pair = q_ref.shape[2] // LANES
    per_pair = LANES // dk
    assert dv == LANES and c % SUBLANES == 0

    @pl.when(c_idx == 0)
    def _():
        for p in range(n_pair):
            st_ref[p] = s0_ref[0, p].T

    row = lax.broadcasted_iota(jnp.int32, (c, LANES), 0)
    lane = lax.broadcasted_iota(jnp.int32, (c, LANES), 1)
    row2 = lax.broadcasted_iota(jnp.int32, (c, c), 0)
    col2 = lax.broadcasted_iota(jnp.int32, (c, c), 1)
    tri = jnp.where(row2 >= col2, 1.0, 0.0)
    nb8 = c // SUBLANES
    sub = lax.broadcasted_iota(jnp.int32, (nb8, SUBLANES, LANES), 1)
    lane3 = lax.broadcasted_iota(jnp.int32, (nb8, SUBLANES, LANES), 2)
    lane_st = lax.broadcasted_iota(jnp.int32, (dv, LANES), 1)

    def bcast_row(x3, jj):
        return jnp.broadcast_to(x3[:, jj:jj + 1, :], x3.shape)

    def one_chunk(p, rows, b_all, st):
        cols = slice(p * LANES, (p + 1) * LANES)
        q = q_ref[0, rows, cols] * (dk ** -0.5)
        k = k_ref[0, rows, cols]
        b = b_all[:, cols]
        vs = [v_ref[0, rows, (p * per_pair + hh) * dv:(p * per_pair + hh + 1) * dv] for hh in range(per_pair)]
        head_lanes = [(lane >= hh * dk) & (lane < (hh + 1) * dk) for hh in range(per_pair)]
        head_lanes3 = [(lane3 >= hh * dk) & (lane3 < (hh + 1) * dk) for hh in range(per_pair)]

        qe = q * jnp.exp(b)
        o = [_bdot_nt(jnp.where(head_lanes[hh], qe, 0.0), st) for hh in range(per_pair)]

        q3 = q.reshape(nb8, SUBLANES, LANES)
        k3 = k.reshape(nb8, SUBLANES, LANES)
        b3 = b.reshape(nb8, SUBLANES, LANES)
        v3 = [v.reshape(nb8, SUBLANES, dv) for v in vs]
        o3 = [jnp.zeros((nb8, SUBLANES, dv), F32) for _ in range(per_pair)]
        for jj in range(SUBLANES):
            e = jnp.exp(jnp.minimum(b3 - bcast_row(b3, jj), 0.0))
            term = jnp.where(sub >= jj, q3 * bcast_row(k3, jj) * e, 0.0)
            for hh in range(per_pair):
                a = jnp.sum(jnp.where(head_lanes3[hh], term, 0.0), axis=-1, keepdims=True)
                o3[hh] = o3[hh] + a * bcast_row(v3[hh], jj)
        o = [o[hh] + o3[hh].reshape(c, dv) for hh in range(per_pair)]

        attn = [jnp.zeros((c, c), F32) for _ in range(per_pair)]
        m_half = SUBLANES
        while 2 * m_half <= c:
            span = 2 * m_half
            bnd = jnp.broadcast_to(b.reshape(c // span, span, LANES)[:, m_half - 1:m_half, :],
                                   (c // span, span, LANES)).reshape(c, LANES)
            upper = (row % span) >= m_half
            qm = jnp.where(upper, q * jnp.exp(jnp.minimum(b - bnd, 0.0)), 0.0)
            km = jnp.where(upper, 0.0, k * jnp.exp(jnp.minimum(bnd - b, 0.0)))
            same = (row2 // span) == (col2 // span)
            for hh in range(per_pair):
                a = _bdot_nt(jnp.where(head_lanes[hh], qm, 0.0), km)
                attn[hh] = attn[hh] + jnp.where(same, a, 0.0)
            m_half = span
        if c > SUBLANES:
            o = [o[hh] + _bdot(attn[hh], vs[hh]) for hh in range(per_pair)]

        b_last = b[c - 1:c, :]
        kk = (k * jnp.exp(b_last - b)).astype(BF16)
        upd = lax.dot_general(vs[0].astype(BF16), kk, _TN, preferred_element_type=F32)
        for hh in range(1, per_pair):
            u = lax.dot_general(vs[hh].astype(BF16), kk, _TN, preferred_element_type=F32)
            upd = jnp.where(lane_st >= hh * dk, u, upd)
        for hh in range(per_pair):
            h = p * per_pair + hh
            rg = rg_ref[0, rows, h * dv:(h + 1) * dv]
            o_ref[0, rows, h * dv:(h + 1) * dv] = _rms(o[hh], gn_ref[...]) * (rg * _sigmoid(rg))
        return st * jnp.exp(b_last) + upd

    states = [st_ref[p] for p in range(n_pair)]
    for ch in range(q_ref.shape[1] // c):
        rows = slice(ch * c, (ch + 1) * c)
        b_all = jnp.dot(tri, la_ref[0, rows, :], precision=lax.Precision.HIGHEST, preferred_element_type=F32)
        states = [one_chunk(p, rows, b_all, states[p]) for p in range(n_pair)]
    for p in range(n_pair):
        st_ref[p] = states[p]

    @pl.when(c_idx == n_chunks - 1)
    def _():
        for p in range(n_pair):
            sout_ref[0, p] = states[p].T


def _gla(qg, kg, vg, la, rg, g_norm, s0, *, chunk, dk, dv):
    b, t, wk = qg.shape
    wv = vg.shape[2]
    n_head = wk // dk
    n_pair = wk // LANES
    s0p = s0.reshape(b, n_pair, LANES, dv)
    step = next(n * chunk for n in (4, 2, 1) if t % (n * chunk) == 0)
    tok = lambda w: pl.BlockSpec((1, step, w), lambda bi, ci: (bi, ci, 0))
    st_spec = pl.BlockSpec((1, n_pair, LANES, dv), lambda bi, ci: (bi, 0, 0, 0))
    og, s_new = pl.pallas_call(
        functools.partial(_gla_kernel, dk=dk, dv=dv, chunk=chunk),
        grid=(b, t // step),
        in_specs=[tok(wk), tok(wk), tok(wv), tok(wk), tok(wv),
                  pl.BlockSpec(g_norm.shape, lambda bi, ci: (0, 0)), st_spec],
        out_specs=[tok(wv), st_spec],
        out_shape=[jax.ShapeDtypeStruct((b, t, wv), F32), jax.ShapeDtypeStruct((b, n_pair, LANES, dv), F32)],
        scratch_shapes=[pltpu.VMEM((n_pair, dv, LANES), F32)],
        compiler_params=_params("parallel", "arbitrary"),
        name="gla",
    )(qg, kg, vg, la, rg, g_norm, s0p)
    return og, s_new.reshape(b, n_head, dk, dv)


def _mix_kernel(x_ref, oa_ref, og_ref, sga_ref, sgb_ref, wa_ref, wb_ref, wo_ref, wu_ref, wd_ref,
                g_mix_ref, g_pre_ref, g_post_ref, y_ref):
    merged = (sga_ref[...].astype(F32) * _bdot(oa_ref[...], wa_ref[...])
              + sgb_ref[...].astype(F32) * _bdot(og_ref[...], wb_ref[...]))
    x = x_ref[...] + _rms(_bdot(merged, wo_ref[...]), g_mix_ref[...])
    u = _bdot(_rms(x, g_pre_ref[...]), wu_ref[...])
    u = jnp.square(jnp.maximum(u, 0.0))
    y_ref[...] = x + _rms(_bdot(u, wd_ref[...]), g_post_ref[...])


def _row_tile(n):
    return 2 * MOBA_BLOCK if n % (2 * MOBA_BLOCK) == 0 else MOBA_BLOCK


def _rowwise_call(kernel, name, row_inputs, const_inputs, out_width):
    n = row_inputs[0].shape[0]
    tm = _row_tile(n)
    assert n % tm == 0
    row = lambda a: pl.BlockSpec((tm, a.shape[1]), lambda i: (i, 0))
    full = lambda a: pl.BlockSpec(a.shape, lambda i: (0,) * a.ndim, pipeline_mode=pl.Buffered(1))
    return pl.pallas_call(
        kernel,
        grid=(n // tm,),
        in_specs=[row(a) for a in row_inputs] + [full(a) for a in const_inputs],
        out_specs=pl.BlockSpec((tm, out_width), lambda i: (i, 0)),
        out_shape=jax.ShapeDtypeStruct((n, out_width), F32),
        compiler_params=_params("parallel"),
        name=name,
    )(*row_inputs, *const_inputs)


def _gla_chunk(t):
    c = SUBLANES
    while c * 2 <= min(t, LANES) and t % (c * 2) == 0:
        c *= 2
    return c


def kernel(x_prompt, x_sample, cache_k, cache_v, page_table, state_gla, w_in, w_gla_gate, b_gla_gate, g_gla_norm,
           w_branch_a, w_branch_b, w_out, w_up, w_down, g_pre_mix, g_post_mix, g_pre_mlp, g_post_mlp):
    bp, tp, d_model = x_prompt.shape
    bs, ts, _ = x_sample.shape
    depth, n_phys, page, n_head, hd = cache_k.shape
    _, _, n_head_g, dk, dv = state_gla.shape
    w_a = n_head * hd
    qk_g = n_head_g * dk
    v_g = n_head_g * dv
    rank = w_gla_gate.shape[1]
    n_main = 3 * w_a + 2 * qk_g + 2 * v_g

    hp = x_prompt.reshape(bp * tp, d_model)
    hs = x_sample.reshape(bs * ts, d_model)
    outs = [[] for _ in range(6)]
    for l in range(depth):
        wm = w_in[l, :, :n_main].astype(BF16)
        wlr = jnp.pad(w_in[l, :, n_main:n_main + rank], ((0, 0), (0, LANES - rank))).astype(BF16)
        wgt = w_in[l, :, n_main + rank:].astype(BF16)
        wgg = jnp.pad(w_gla_gate[l], ((0, LANES - rank), (0, 0))).astype(BF16)
        bgg = b_gla_gate[l][None, :]
        wa, wb, wo = (w[l].astype(BF16) for w in (w_branch_a, w_branch_b, w_out))
        wu, wd = w_up[l].astype(BF16), w_down[l].astype(BF16)
        g_mix, g_pm, g_mlp, g_pl, g_gn = (g[l][None, :] for g in
                                          (g_pre_mix, g_post_mix, g_pre_mlp, g_post_mlp, g_gla_norm))

        def project(x, kv_seq_len):
            return _proj(x, g_mix, wm, wlr, wgt, wgg, bgg, w_a=w_a, qk_g=qk_g, v_g=v_g, kv_seq_len=kv_seq_len)

        def mix(x, b, t, oa, proj_out, s0):
            qg, kg, vg, rg, la, sga, sgb = proj_out[3:10]
            r3 = lambda a: a.reshape(b, t, a.shape[-1])
            og, s_new = _gla(r3(qg), r3(kg), r3(vg), r3(la), r3(rg), g_gn, s0, chunk=_gla_chunk(t), dk=dk, dv=dv)
            n = b * t
            x = _rowwise_call(_mix_kernel, "mix", [x, oa.reshape(n, w_a), og.reshape(n, v_g), sga, sgb],
                              [wa, wb, wo, wu, wd, g_pm, g_mlp, g_pl], d_model)
            return x, s_new

        pp = project(hp, tp)
        ps = project(hs, None)
        qa_p, kt_p, vt_p, kmean_p, k_bf, vt_bf = pp[0], pp[1], pp[2], pp[10], pp[11], pp[12]
        qa_s, ka_s, va_s = (a.reshape(bs, ts, w_a) for a in ps[:3])
        to_stored = lambda c: jnp.transpose(c[l], (0, 2, 3, 1))
        oa_p, oa_s = _moba(qa_p.reshape(bp, tp, w_a), k_bf.reshape(bp, tp, w_a), vt_bf,
                           kmean_p.reshape(bp, tp // MOBA_BLOCK, w_a), qa_s, ka_s, va_s,
                           to_stored(cache_k), to_stored(cache_v), page_table, hd=hd)
        hp, sp = mix(hp, bp, tp, oa_p, pp, jnp.zeros((bp, n_head_g, dk, dv), state_gla.dtype))
        hs, ssn = mix(hs, bs, ts, oa_s, ps, state_gla[l])
        kp, vp = (jnp.transpose(a.reshape(bp, n_head, hd, tp), (0, 3, 1, 2)) for a in (kt_p, vt_p))
        ksn, vsn = (a.reshape(bs, ts, n_head, hd) for a in (ka_s, va_s))
        for lst, val in zip(outs, (kp, vp, sp, ksn, vsn, ssn)):
            lst.append(val)
    return (hp.reshape(bp, tp, d_model), hs.reshape(bs, ts, d_model)) + tuple(jnp.stack(o) for o in outs)
```

```python
import functools

import ml_dtypes
import numpy as np
import jax
import jax.numpy as jnp
from jax import lax
from jax.experimental import pallas as pl
from jax.experimental.pallas import tpu as pltpu

F32 = jnp.float32
BF16 = jnp.bfloat16

LANES = 128
SUBLANES = 8
VMEM_LIMIT_BYTES = 56 * 1024 * 1024

EPS = 1e-6
MOBA_BLOCK = 256
MOBA_TOPK = 3
GLA_TAU = 16.0
NEG_BIG = -1e30
PENALTY = -30000.0
LOG2E = float(np.log2(np.e))

_NT = (((1,), (1,)), ((), ()))
_TN = (((0,), (0,)), ((), ()))


def _params(*sem):
    return pltpu.CompilerParams(dimension_semantics=sem, vmem_limit_bytes=VMEM_LIMIT_BYTES)


def _sigmoid(x):
    return 1.0 / (1.0 + jnp.exp(-x))


def _rms(x, g):
    return x * lax.rsqrt(jnp.mean(x * x, axis=-1, keepdims=True) + EPS) * g


def _bdot(a, b):
    return jnp.dot(a.astype(BF16), b.astype(BF16), preferred_element_type=F32)


def _bdot_nt(a, b):
    return lax.dot_general(a.astype(BF16), b.astype(BF16), _NT, preferred_element_type=F32)


def _proj_kernel(x_ref, g_ref, wm_ref, wlr_ref, wgt_ref, wgg_ref, bgg_ref,
                 qa_ref, ka_ref, va_ref, qg_ref, kg_ref, vg_ref, rg_ref, la_ref, sga_ref, sgb_ref, kmean_ref,
                 *attn_refs, w_a, qk_g, v_g, d_model, kv_transposed):
    hb = _rms(x_ref[...], g_ref[...]).astype(BF16)

    def proj(lo, n):
        return jnp.dot(hb, wm_ref[:, lo:lo + n], preferred_element_type=F32)

    qa_ref[...] = proj(0, w_a)
    ka = proj(w_a, w_a)
    va = proj(2 * w_a, w_a)
    for i in range(kmean_ref.shape[0]):
        kmean_ref[i] = jnp.sum(ka[i * MOBA_BLOCK:(i + 1) * MOBA_BLOCK], axis=0, keepdims=True) * (1.0 / MOBA_BLOCK)
    if kv_transposed:
        k_bf_ref, vt_bf_ref = attn_refs
        vt = va.T
        ka_ref[0] = ka.T
        va_ref[0] = vt
        k_bf_ref[...] = ka.astype(BF16)
        vt_bf_ref[0] = vt.astype(BF16)
    else:
        ka_ref[...] = ka
        va_ref[...] = va
    lo = 3 * w_a
    qg_ref[...] = proj(lo, qk_g)
    kg_ref[...] = proj(lo + qk_g, qk_g)
    vg_ref[...] = proj(lo + 2 * qk_g, v_g)
    rg_ref[...] = proj(lo + 2 * qk_g + v_g, v_g)
    lr = jnp.dot(hb, wlr_ref[...], preferred_element_type=F32)
    xg = jnp.dot(lr.astype(BF16), wgg_ref[...], preferred_element_type=F32) + bgg_ref[...]
    log_sig = jnp.minimum(xg, 0.0) - jnp.log1p(jnp.exp(-jnp.abs(xg)))
    la_ref[...] = log_sig * (1.0 / GLA_TAU)
    ga = jnp.dot(hb, wgt_ref[:, :d_model], preferred_element_type=F32)
    sga_ref[...] = _sigmoid(ga).astype(BF16)
    gb = jnp.dot(hb, wgt_ref[:, d_model:], preferred_element_type=F32)
    sgb_ref[...] = _sigmoid(gb).astype(BF16)


def _proj(x, g, wm, wlr, wgt, wgg, bgg, *, w_a, qk_g, v_g, kv_seq_len=None):
    n, d_model = x.shape
    tm = _row_tile(n)
    assert n % tm == 0
    nt = n // tm
    blocks_per_tile = tm // MOBA_BLOCK
    row = lambda w: pl.BlockSpec((tm, w), lambda i: (i, 0))
    full = lambda a: pl.BlockSpec(a.shape, lambda i: (0,) * a.ndim, pipeline_mode=pl.Buffered(1))
    widths = (w_a, w_a, w_a, qk_g, qk_g, v_g, v_g, qk_g)
    out_shape = [jax.ShapeDtypeStruct((n, w), F32) for w in widths]
    out_shape += [jax.ShapeDtypeStruct((n, d_model), BF16)] * 2
    out_shape += [jax.ShapeDtypeStruct((n // MOBA_BLOCK, 1, w_a), F32)]
    out_specs = [row(w) for w in widths] + [row(d_model)] * 2
    out_specs += [pl.BlockSpec((blocks_per_tile, 1, w_a), lambda i: (i, 0, 0))]
    if kv_seq_len is not None:
        assert kv_seq_len % tm == 0
        seq_tiles = kv_seq_len // tm
        transposed = pl.BlockSpec((1, w_a, tm), lambda i: (i // seq_tiles, 0, i % seq_tiles))
        for i in (1, 2):
            out_shape[i] = jax.ShapeDtypeStruct((n // kv_seq_len, w_a, kv_seq_len), F32)
            out_specs[i] = transposed
        out_shape += [jax.ShapeDtypeStruct((n, w_a), BF16),
                      jax.ShapeDtypeStruct((n // kv_seq_len, w_a, kv_seq_len), BF16)]
        out_specs += [row(w_a), transposed]
    return pl.pallas_call(
        functools.partial(_proj_kernel, w_a=w_a, qk_g=qk_g, v_g=v_g, d_model=d_model,
                          kv_transposed=kv_seq_len is not None),
        grid=(nt,),
        in_specs=[row(d_model), full(g), full(wm), full(wlr), full(wgt), full(wgg), full(bgg)],
        out_specs=out_specs,
        out_shape=out_shape,
        compiler_params=_params("parallel"),
        name="proj",
    )(x, g, wm, wlr, wgt, wgg, bgg)


def _top_blocks(gate, candidate, block, axis=-1):
    block_f = block.astype(F32)
    no_block = float(gate.shape[axis])
    avail = jnp.where(candidate, 1.0, 0.0)
    sel = jnp.zeros(gate.shape, F32)
    for _ in range(MOBA_TOPK):
        cur = jnp.where(avail > 0.0, gate, -jnp.inf)
        mx = jnp.max(cur, axis=axis, keepdims=True)
        cand = jnp.where(avail > 0.0, jnp.where(cur == mx, block_f, no_block), no_block)
        first = jnp.min(cand, axis=axis, keepdims=True)
        pick = block_f == first
        sel = jnp.where(pick, 1.0, sel)
        avail = jnp.where(pick, 0.0, avail)
    return sel


def _alibi_slopes(n_head):
    return 2.0 ** (-8.0 * (np.arange(n_head) + 1) / n_head)


_SLOPE_PARTS = 3
_AUG_EXTRA = 2 * _SLOPE_PARTS


def _moba_prompt_step(r, n_sub, hp, qi, sl_ref, q_ref, k_ref, vt_ref, kc_ref, km_ref, o_ref,
                      qaug_ref, m_ref, acc_ref, *, hd, tk):
    blk = MOBA_BLOCK
    t = k_ref.shape[1]
    tq = q_ref.shape[1]
    n_blk = t // blk
    per_pair = LANES // hd
    n_acc = acc_ref.shape[1]
    tiles_per_q = tq // tk
    chains = [(hh, cb) for hh in range(per_pair) for cb in range(tiles_per_q)]
    ones = jnp.ones((n_acc - hd, tk), BF16)

    def tiles(j):
        keys = pl.ds(pl.multiple_of(j * tk, tk), tk)
        kt = jnp.concatenate([k_ref[0, keys, :], kc_ref[keys, :]], axis=1)
        return kt, [jnp.concatenate([vt_ref[0, hh * hd:(hh + 1) * hd, keys], ones], axis=0) for hh in range(per_pair)]

    def logits(kt, hh, cb):
        return jnp.dot(kt, qaug_ref[hh, :, cb * tk:(cb + 1) * tk], preferred_element_type=F32)

    def absorb(stat, s, va, c):
        m, acc = stat
        m_new = jnp.maximum(m, jnp.max(s, axis=0, keepdims=True) + c)
        p = jnp.exp2(s - (m_new - c)).astype(BF16)
        return m_new, jnp.exp2(m - m_new) * acc + jnp.dot(va, p, preferred_element_type=F32)

    @pl.when(r == 0)
    def _():
        lane = lax.broadcasted_iota(jnp.int32, (tq, LANES), 1)
        q = q_ref[0]
        n_rows = -(-n_blk // SUBLANES) * SUBLANES
        block = lax.broadcasted_iota(jnp.int32, (n_rows, tq), 0)
        own = qi * (tq // blk) + lax.broadcasted_iota(jnp.int32, (n_rows, tq), 1) // blk
        extra = lax.broadcasted_iota(jnp.int32, (SUBLANES, tq), 0)
        assert _AUG_EXTRA <= SUBLANES and n_rows + SUBLANES <= LANES
        for hh in range(per_pair):
            h = hp * per_pair + hh
            head_lanes = (lane >= hh * hd) & (lane < (hh + 1) * hd)
            qh = jnp.where(head_lanes, q, 0.0)
            gate = _bdot_nt(km_ref[0, :n_rows, :], qh)
            sel = _top_blocks(gate, block < own, block, axis=0)
            penalty = jnp.where((sel > 0.0) | (block == own) | (block >= n_blk), 0.0, PENALTY)
            parts = [sl_ref[1 + i, h] for i in range(_SLOPE_PARTS)]
            slope_rows = jnp.zeros((SUBLANES, tq), F32)
            for off, val in enumerate(parts + [part * blk for part in parts]):
                slope_rows = jnp.where(extra == off, val, slope_rows)
            qx_t = jnp.concatenate([penalty, slope_rows, jnp.zeros((LANES - n_rows - SUBLANES, tq), F32)], axis=0)
            qaug_ref[hh, :LANES, :] = (qh * (hd ** -0.5 * LOG2E)).T.astype(BF16)
            qaug_ref[hh, LANES:, :] = qx_t.astype(BF16)

        key2 = lax.broadcasted_iota(jnp.int32, (tk, tk), 0)
        qry2 = lax.broadcasted_iota(jnp.int32, (tk, tk), 1)
        stats = {c: (jnp.full((1, tk), NEG_BIG, F32), jnp.zeros((n_acc, tk), F32)) for c in chains}
        for d in range(tiles_per_q):
            kt, va = tiles(qi * tiles_per_q + d)
            live = [(hh, cb) for hh, cb in chains if cb >= d]
            ss = [logits(kt, hh, cb) for hh, cb in live]
            for (hh, cb), s in zip(live, ss):
                if cb == d:
                    s = jnp.where(key2 <= qry2, s, NEG_BIG)
                stats[hh, cb] = absorb(stats[hh, cb], s, va[hh], 0.0)
        for ci, c in enumerate(chains):
            m_ref[ci], acc_ref[ci] = stats[c]

    def absorb_tiles(js, flat):
        loaded = [tiles(j) for j in js]
        ss = [[logits(kt, hh, cb) for hh, cb in chains] for kt, _ in loaded]
        out = list(flat)
        for j, (_, va), s_tile in zip(js, loaded, ss):
            for ci, (hh, cb) in enumerate(chains):
                c = -(sl_ref[0, hp * per_pair + hh] * tq) * (qi - j // tiles_per_q).astype(F32)
                out[2 * ci:2 * ci + 2] = absorb(out[2 * ci:2 * ci + 2], s_tile[ci], va[hh], c)
        return tuple(out)

    n_mine = (qi * tiles_per_q - r + n_sub - 1) // n_sub
    flat = tuple(x for ci in range(len(chains)) for x in (m_ref[ci], acc_ref[ci]))
    flat = lax.fori_loop(0, n_mine // 2, lambda i, f: absorb_tiles([r + 2 * i * n_sub, r + (2 * i + 1) * n_sub], f), flat)
    flat = lax.fori_loop(0, n_mine % 2, lambda i, f: absorb_tiles([r + (n_mine - 1) * n_sub], f), flat)
    for ci in range(len(chains)):
        m_ref[ci], acc_ref[ci] = flat[2 * ci], flat[2 * ci + 1]

    @pl.when(r == n_sub - 1)
    def _():
        for cb in range(tiles_per_q):
            accs = [flat[2 * chains.index((hh, cb)) + 1] for hh in range(per_pair)]
            out = jnp.concatenate([acc[:hd] / acc[hd:hd + 1] for acc in accs], axis=0)
            o_ref[0, cb * tk:(cb + 1) * tk, :] = out.T


def _moba_consts(t, tq):
    n_blk = t // MOBA_BLOCK
    base = -(-n_blk // SUBLANES) * SUBLANES
    assert base + _AUG_EXTRA <= LANES
    pos = np.arange(t)
    c = np.zeros((t, LANES), np.float32)
    c[pos, pos // MOBA_BLOCK] = 1.0
    for i in range(_SLOPE_PARTS):
        c[:, base + i] = pos % MOBA_BLOCK
        c[:, base + _SLOPE_PARTS + i] = (pos % tq) // MOBA_BLOCK
    return jnp.asarray(c, dtype=BF16)


_BF16_ROWS = 2 * SUBLANES


def _moba_sample_step(j, n_k_steps, slopes_ref, q_ref, kn_ref, vn_ref, pages_ref,
                      o_ref, qbd_ref, s_ref, p_ref, ksum_ref, l_ref, acc_ref, *, pages_per_block, hd):
    n_per_step = pages_ref.shape[0]
    qk_pages, pv_pages = 4, 2
    ts, w = q_ref.shape[1:]
    n_head = w // hd
    rows = n_head * ts
    page = pages_ref.shape[3]
    blk = page * pages_per_block
    past = s_ref.shape[1]
    n_blk = past // blk
    scale = hd ** -0.5

    row_head = lax.broadcasted_iota(jnp.int32, (rows, 1), 0) // ts
    row_q = lax.broadcasted_iota(jnp.int32, (rows, 1), 0) % ts
    col_head = lax.broadcasted_iota(jnp.int32, (rows, w), 1) // hd
    slope = jnp.zeros((rows, 1), F32)
    for h in range(n_head):
        slope = jnp.where(row_head == h, slopes_ref[h], slope)

    @pl.when(j == 0)
    def _():
        qt = jnp.concatenate([q_ref[0]] * n_head, axis=0)
        qbd_ref[...] = jnp.where(col_head == row_head, qt, 0.0)
        ksum_ref[...] = jnp.zeros(ksum_ref.shape, F32)

    @pl.when(j < n_k_steps)
    def _():
        qs = (qbd_ref[...] * scale).astype(BF16)
        lane = lax.broadcasted_iota(jnp.int32, ksum_ref.shape, 1)
        ksum = ksum_ref[...]
        group = qk_pages if n_per_step % qk_pages == 0 and qk_pages % pages_per_block == 0 else pages_per_block
        for g0 in range(0, n_per_step, group):
            kts = [pages_ref[p_i].reshape(w, page) for p_i in range(g0, g0 + group)]
            first = j * n_per_step + g0
            s = jnp.dot(qs, jnp.concatenate([kt.astype(BF16) for kt in kts], axis=1), preferred_element_type=F32)
            s_ref[:, pl.ds(pl.multiple_of(first * page, group * page), group * page)] = s
            for b0 in range(0, group, pages_per_block):
                total = kts[b0]
                for kt in kts[b0 + 1:b0 + pages_per_block]:
                    total = total + kt
                ksum = jnp.where(lane == (first + b0) // pages_per_block, jnp.sum(total, axis=1, keepdims=True), ksum)
        ksum_ref[...] = ksum

    @pl.when(j == n_k_steps)
    def _():
        qbd = qbd_ref[...]
        gate = _bdot(qbd, ksum_ref[...])
        lane = lax.broadcasted_iota(jnp.int32, gate.shape, 1)
        sel = _top_blocks(gate, lane < n_blk, lane)
        s_own = _bdot_nt(qbd * scale, kn_ref[0])
        kq = lax.broadcasted_iota(jnp.int32, s_own.shape, 1)
        s_own = jnp.where(kq <= row_q, s_own + slope * kq.astype(F32), NEG_BIG)
        key_in_blk = lax.broadcasted_iota(jnp.int32, (1, blk), 1)

        def add_bias(b_i, m_wide):
            cols = pl.ds(pl.multiple_of(b_i * blk, blk), blk)
            picked = jnp.max(jnp.where(lane == b_i, sel, 0.0), axis=-1, keepdims=True)
            rel = (b_i * blk - past + key_in_blk).astype(F32)
            s = s_ref[:, cols] + slope * rel + jnp.where(picked > 0.0, 0.0, NEG_BIG)
            s_ref[:, cols] = s
            return jnp.maximum(m_wide, s)

        unroll = 8 if n_blk % 8 == 0 else 1
        m_wide = lax.fori_loop(0, n_blk, add_bias, jnp.full((rows, blk), NEG_BIG, F32), unroll=unroll)
        m = jnp.maximum(jnp.max(m_wide, axis=-1, keepdims=True), jnp.max(s_own, axis=-1, keepdims=True))

        def probs(b_i, l_wide):
            cols = pl.ds(pl.multiple_of(b_i * blk, blk), blk)
            p = jnp.exp(s_ref[:, cols] - m)
            p_ref[:, cols] = p.astype(BF16)
            return l_wide + p

        l_wide = lax.fori_loop(0, n_blk, probs, jnp.zeros((rows, blk), F32), unroll=unroll)
        p_own = jnp.exp(s_own - m)
        l_ref[...] = jnp.sum(l_wide, axis=-1, keepdims=True) + jnp.sum(p_own, axis=-1, keepdims=True)
        acc_ref[...] = _bdot(p_own, vn_ref[0])

    @pl.when(j >= n_k_steps)
    def _():
        acc = acc_ref[...]
        group = pv_pages if n_per_step % pv_pages == 0 else 1
        for g0 in range(0, n_per_step, group):
            first = (j - n_k_steps) * n_per_step + g0
            p = p_ref[:, pl.ds(pl.multiple_of(first * page, group * page), group * page)]
            vt = jnp.concatenate([pages_ref[p_i].reshape(w, page).astype(BF16) for p_i in range(g0, g0 + group)], axis=1)
            acc = acc + lax.dot_general(p, vt, _NT, preferred_element_type=F32)
        acc_ref[...] = acc

    @pl.when(j == 2 * n_k_steps - 1)
    def _():
        res = jnp.where(col_head == row_head, acc_ref[...] / l_ref[...], 0.0)
        out = res[0:ts]
        for h in range(1, n_head):
            out = out + res[h * ts:(h + 1) * ts]
        o_ref[0] = out


_SAMPLE_STEPS = 2

def _moba_kernel(pt_ref, sl_ref, slopes_ref, q_ref, k_ref, vt_ref, kc_ref, km_ref, qs_ref, kn_ref, vn_ref,
                 cache_k_hbm, cache_v_hbm, o_ref, os_ref, qaug_ref, m_ref, acc_ref, pages_ref, sem, *sample_scratch,
                 pages_per_block, hd, tk, n_k_steps):
    hp, qi, r = pl.program_id(1), pl.program_id(2), pl.program_id(3)
    n_sub = pl.num_programs(3)
    gstep = ((pl.program_id(0) * pl.num_programs(1) + hp) * pl.num_programs(2) + qi) * n_sub + r
    n_gsteps = pl.num_programs(0) * pl.num_programs(1) * pl.num_programs(2) * n_sub
    n_per_step = pages_ref.shape[1]
    per_seq = 2 * n_k_steps
    first = gstep * _SAMPLE_STEPS

    def start_pages(s, slot):
        seq, j = s // per_seq, s % per_seq

        def start(cache_hbm, group):
            for p in range(n_per_step):
                pltpu.make_async_copy(cache_hbm.at[pt_ref[seq, group * n_per_step + p]],
                                      pages_ref.at[slot, p], sem.at[slot]).start()

        @pl.when(j < n_k_steps)
        def _():
            start(cache_k_hbm, j)

        @pl.when(j >= n_k_steps)
        def _():
            start(cache_v_hbm, j - n_k_steps)

    def sample_step(s, slot):
        pltpu.make_async_copy(cache_k_hbm.at[pl.ds(0, n_per_step)], pages_ref.at[slot], sem.at[slot]).wait()
        _moba_sample_step(s % per_seq, n_k_steps, slopes_ref, qs_ref, kn_ref, vn_ref, pages_ref.at[slot],
                          os_ref, *sample_scratch, pages_per_block=pages_per_block, hd=hd)

    @pl.when(gstep == 0)
    def _():
        start_pages(first, 0)

    start_pages(first + 1, 1)
    _moba_prompt_step(r, n_sub, hp, qi, sl_ref, q_ref, k_ref, vt_ref, kc_ref, km_ref, o_ref,
                      qaug_ref, m_ref, acc_ref, hd=hd, tk=tk)
    sample_step(first, 0)

    @pl.when(gstep + 1 < n_gsteps)
    def _():
        start_pages(first + _SAMPLE_STEPS, 0)

    sample_step(first + 1, 1)


def _moba(qa, k_bf, vt_bf, kmean, qa_s, ka_s, va_s, cache_kt, cache_vt, page_table, *, hd):
    b, t, w = qa.shape
    bs, ts, _ = qa_s.shape
    _, n_head, _, page = cache_kt.shape
    n_pages = page_table.shape[1]
    blk = MOBA_BLOCK
    ppb = blk // page
    n_pair = w // LANES
    n_blk = t // blk
    tk = 2 * blk if t % (2 * blk) == 0 else blk
    tq = 2 * tk if t % (2 * tk) == 0 else tk
    n_q = t // tq
    n_per_step = next(p for p in (32, 16, 8, 4, 2) if n_pages % p == 0 and p % ppb == 0
                      and (bs * 2 * (n_pages // p)) % (b * n_pair * n_q * _SAMPLE_STEPS) == 0)
    n_k_steps = n_pages // n_per_step
    n_sub = bs * 2 * n_k_steps // (b * n_pair * n_q * _SAMPLE_STEPS)
    assert (2 * n_k_steps) % _SAMPLE_STEPS == 0
    assert n_pages // ppb <= LANES
    rows = n_head * ts

    sl2 = _alibi_slopes(n_head) * LOG2E
    parts, rest = [], sl2
    for _ in range(_SLOPE_PARTS):
        parts.append(rest.astype(ml_dtypes.bfloat16).astype(np.float64))
        rest = rest - parts[-1]
    sl = jnp.asarray(np.stack([sl2] + parts), dtype=F32)
    slopes = jnp.asarray(_alibi_slopes(n_head), dtype=F32)
    kc = _moba_consts(t, tq)
    km = jnp.pad(kmean, ((0, 0), (0, LANES - n_blk), (0, 0)))

    def sample_step(bi, hp, qi, r):
        return (((bi * n_pair + hp) * n_q + qi) * n_sub + r) * _SAMPLE_STEPS

    def per_seq(a):
        return pl.BlockSpec((1,) + a.shape[1:], lambda bi, hp, qi, r, pt: (sample_step(bi, hp, qi, r) // (2 * n_k_steps), 0, 0))

    smem = pl.BlockSpec(memory_space=pltpu.SMEM)
    hbm = pl.BlockSpec(memory_space=pl.ANY)
    n_chain = (LANES // hd) * (tq // tk)
    return pl.pallas_call(
        functools.partial(_moba_kernel, pages_per_block=ppb, hd=hd, tk=tk, n_k_steps=n_k_steps),
        grid_spec=pltpu.PrefetchScalarGridSpec(
            num_scalar_prefetch=1,
            grid=(b, n_pair, n_q, n_sub),
            in_specs=[smem, smem,
                      pl.BlockSpec((1, tq, LANES), lambda bi, hp, qi, r, pt: (bi, qi, hp)),
                      pl.BlockSpec((1, t, LANES), lambda bi, hp, qi, r, pt: (bi, 0, hp), pipeline_mode=pl.Buffered(1)),
                      pl.BlockSpec((1, LANES, t), lambda bi, hp, qi, r, pt: (bi, hp, 0), pipeline_mode=pl.Buffered(1)),
                      pl.BlockSpec((t, LANES), lambda bi, hp, qi, r, pt: (0, 0), pipeline_mode=pl.Buffered(1)),
                      pl.BlockSpec((1, LANES, LANES), lambda bi, hp, qi, r, pt: (bi, 0, hp)),
                      per_seq(qa_s), per_seq(ka_s), per_seq(va_s), hbm, hbm],
            out_specs=[pl.BlockSpec((1, tq, LANES), lambda bi, hp, qi, r, pt: (bi, qi, hp)), per_seq(qa_s)],
            scratch_shapes=[pltpu.VMEM((LANES // hd, 2 * LANES, tq), BF16),
                            pltpu.VMEM((n_chain, 1, tk), F32),
                            pltpu.VMEM((n_chain, hd + _BF16_ROWS, tk), F32),
                            pltpu.VMEM((2, n_per_step, n_head, hd, page), F32),
                            pltpu.SemaphoreType.DMA((2,)),
                            pltpu.VMEM((rows, w), F32),
                            pltpu.VMEM((rows, n_pages * page), F32),
                            pltpu.VMEM((rows, n_pages * page), BF16),
                            pltpu.VMEM((w, LANES), F32),
                            pltpu.VMEM((rows, 1), F32), pltpu.VMEM((rows, w), F32)],
        ),
        out_shape=[jax.ShapeDtypeStruct((b, t, w), F32), jax.ShapeDtypeStruct((bs, ts, w), F32)],
        compiler_params=_params("arbitrary", "arbitrary", "arbitrary", "arbitrary"),
        name="moba",
    )(page_table, sl, slopes, qa, k_bf, vt_bf, kc, km, qa_s, ka_s, va_s, cache_kt, cache_vt)


def _gla_kernel(q_ref, k_ref, v_ref, la_ref, rg_ref, gn_ref, s0_ref, o_ref, sout_ref, st_ref, *, dk, dv, chunk):
    c_idx = pl.program_id(1)
    n_chunks = pl.num_programs(1)
    c = chunk
    n_pair = q_ref.shape[2] // LANES
    per_pair = LANES // dk
    assert dv == LANES and c % SUBLANES == 0

    @pl.when(c_idx == 0)
    def _():
        for p in range(n_pair):
            st_ref[p] = s0_ref[0, p].T

    row = lax.broadcasted_iota(jnp.int32, (c, LANES), 0)
    lane = lax.broadcasted_iota(jnp.int32, (c, LANES), 1)
    row2 = lax.broadcasted_iota(jnp.int32, (c, c), 0)
    col2 = lax.broadcasted_iota(jnp.int32, (c, c), 1)
    tri = jnp.where(row2 >= col2, 1.0, 0.0)
    nb8 = c // SUBLANES
    sub = lax.broadcasted_iota(jnp.int32, (nb8, SUBLANES, LANES), 1)
    lane3 = lax.broadcasted_iota(jnp.int32, (nb8, SUBLANES, LANES), 2)
    lane_st = lax.broadcasted_iota(jnp.int32, (dv, LANES), 1)

    def bcast_row(x3, jj):
        return jnp.broadcast_to(x3[:, jj:jj + 1, :], x3.shape)

    def one_chunk(p, rows, b_all, st):
        cols = slice(p * LANES, (p + 1) * LANES)
        q = q_ref[0, rows, cols] * (dk ** -0.5)
        k = k_ref[0, rows, cols]
        b = b_all[:, cols]
        vs = [v_ref[0, rows, (p * per_pair + hh) * dv:(p * per_pair + hh + 1) * dv] for hh in range(per_pair)]
        head_lanes = [(lane >= hh * dk) & (lane < (hh + 1) * dk) for hh in range(per_pair)]
        head_lanes3 = [(lane3 >= hh * dk) & (lane3 < (hh + 1) * dk) for hh in range(per_pair)]

        qe = q * jnp.exp(b)
        o = [_bdot_nt(jnp.where(head_lanes[hh], qe, 0.0), st) for hh in range(per_pair)]

        q3 = q.reshape(nb8, SUBLANES, LANES)
        k3 = k.reshape(nb8, SUBLANES, LANES)
        b3 = b.reshape(nb8, SUBLANES, LANES)
        v3 = [v.reshape(nb8, SUBLANES, dv) for v in vs]
        o3 = [jnp.zeros((nb8, SUBLANES, dv), F32) for _ in range(per_pair)]
        for jj in range(SUBLANES):
            e = jnp.exp(jnp.minimum(b3 - bcast_row(b3, jj), 0.0))
            term = jnp.where(sub >= jj, q3 * bcast_row(k3, jj) * e, 0.0)
            for hh in range(per_pair):
                a = jnp.sum(jnp.where(head_lanes3[hh], term, 0.0), axis=-1, keepdims=True)
                o3[hh] = o3[hh] + a * bcast_row(v3[hh], jj)
        o = [o[hh] + o3[hh].reshape(c, dv) for hh in range(per_pair)]

        attn = [jnp.zeros((c, c), F32) for _ in range(per_pair)]
        m_half = SUBLANES
        while 2 * m_half <= c:
            span = 2 * m_half
            bnd = jnp.broadcast_to(b.reshape(c // span, span, LANES)[:, m_half - 1:m_half, :],
                                   (c // span, span, LANES)).reshape(c, LANES)
            upper = (row % span) >= m_half
            qm = jnp.where(upper, q * jnp.exp(jnp.minimum(b - bnd, 0.0)), 0.0)
            km = jnp.where(upper, 0.0, k * jnp.exp(jnp.minimum(bnd - b, 0.0)))
            same = (row2 // span) == (col2 // span)
            for hh in range(per_pair):
                a = _bdot_nt(jnp.where(head_lanes[hh], qm, 0.0), km)
                attn[hh] = attn[hh] + jnp.where(same, a, 0.0)
            m_half = span
        if c > SUBLANES:
            o = [o[hh] + _bdot(attn[hh], vs[hh]) for hh in range(per_pair)]

        b_last = b[c - 1:c, :]
        kk = (k * jnp.exp(b_last - b)).astype(BF16)
        upd = lax.dot_general(vs[0].astype(BF16), kk, _TN, preferred_element_type=F32)
        for hh in range(1, per_pair):
            u = lax.dot_general(vs[hh].astype(BF16), kk, _TN, preferred_element_type=F32)
            upd = jnp.where(lane_st >= hh * dk, u, upd)
        for hh in range(per_pair):
            h = p * per_pair + hh
            rg = rg_ref[0, rows, h * dv:(h + 1) * dv]
            o_ref[0, rows, h * dv:(h + 1) * dv] = _rms(o[hh], gn_ref[...]) * (rg * _sigmoid(rg))
        return st * jnp.exp(b_last) + upd

    states = [st_ref[p] for p in range(n_pair)]
    for ch in range(q_ref.shape[1] // c):
        rows = slice(ch * c, (ch + 1) * c)
        b_all = jnp.dot(tri, la_ref[0, rows, :], precision=lax.Precision.HIGHEST, preferred_element_type=F32)
        states = [one_chunk(p, rows, b_all, states[p]) for p in range(n_pair)]
    for p in range(n_pair):
        st_ref[p] = states[p]

    @pl.when(c_idx == n_chunks - 1)
    def _():
        for p in range(n_pair):
            sout_ref[0, p] = states[p].T


def _gla(qg, kg, vg, la, rg, g_norm, s0, *, chunk, dk, dv):
    b, t, wk = qg.shape
    wv = vg.shape[2]
    n_head = wk // dk
    n_pair = wk // LANES
    s0p = s0.reshape(b, n_pair, LANES, dv)
    step = next(n * chunk for n in (4, 2, 1) if t % (n * chunk) == 0)
    tok = lambda w: pl.BlockSpec((1, step, w), lambda bi, ci: (bi, ci, 0))
    st_spec = pl.BlockSpec((1, n_pair, LANES, dv), lambda bi, ci: (bi, 0, 0, 0))
    og, s_new = pl.pallas_call(
        functools.partial(_gla_kernel, dk=dk, dv=dv, chunk=chunk),
        grid=(b, t // step),
        in_specs=[tok(wk), tok(wk), tok(wv), tok(wk), tok(wv),
                  pl.BlockSpec(g_norm.shape, lambda bi, ci: (0, 0)), st_spec],
        out_specs=[tok(wv), st_spec],
        out_shape=[jax.ShapeDtypeStruct((b, t, wv), F32), jax.ShapeDtypeStruct((b, n_pair, LANES, dv), F32)],
        scratch_shapes=[pltpu.VMEM((n_pair, dv, LANES), F32)],
        compiler_params=_params("parallel", "arbitrary"),
        name="gla",
    )(qg, kg, vg, la, rg, g_norm, s0p)
    return og, s_new.reshape(b, n_head, dk, dv)


def _mix_kernel(x_ref, oa_ref, og_ref, sga_ref, sgb_ref, wa_ref, wb_ref, wo_ref, wu_ref, wd_ref,
                g_mix_ref, g_pre_ref, g_post_ref, y_ref):
    merged = (sga_ref[...].astype(F32) * _bdot(oa_ref[...], wa_ref[...])
              + sgb_ref[...].astype(F32) * _bdot(og_ref[...], wb_ref[...]))
    x = x_ref[...] + _rms(_bdot(merged, wo_ref[...]), g_mix_ref[...])
    u = _bdot(_rms(x, g_pre_ref[...]), wu_ref[...])
    u = jnp.square(jnp.maximum(u, 0.0))
    y_ref[...] = x + _rms(_bdot(u, wd_ref[...]), g_post_ref[...])


def _row_tile(n):
    return 2 * MOBA_BLOCK if n % (2 * MOBA_BLOCK) == 0 else MOBA_BLOCK


def _rowwise_call(kernel, name, row_inputs, const_inputs, out_width):
    n = row_inputs[0].shape[0]
    tm = _row_tile(n)
    assert n % tm == 0
    row = lambda a: pl.BlockSpec((tm, a.shape[1]), lambda i: (i, 0))
    full = lambda a: pl.BlockSpec(a.shape, lambda i: (0,) * a.ndim, pipeline_mode=pl.Buffered(1))
    return pl.pallas_call(
        kernel,
        grid=(n // tm,),
        in_specs=[row(a) for a in row_inputs] + [full(a) for a in const_inputs],
        out_specs=pl.BlockSpec((tm, out_width), lambda i: (i, 0)),
        out_shape=jax.ShapeDtypeStruct((n, out_width), F32),
        compiler_params=_params("parallel"),
        name=name,
    )(*row_inputs, *const_inputs)


def _gla_chunk(t):
    c = SUBLANES
    while c * 2 <= min(t, LANES) and t % (c * 2) == 0:
        c *= 2
    return c


def kernel(x_prompt, x_sample, cache_k, cache_v, page_table, state_gla, w_in, w_gla_gate, b_gla_gate, g_gla_norm,
           w_branch_a, w_branch_b, w_out, w_up, w_down, g_pre_mix, g_post_mix, g_pre_mlp, g_post_mlp):
    bp, tp, d_model = x_prompt.shape
    bs, ts, _ = x_sample.shape
    depth, n_phys, page, n_head, hd = cache_k.shape
    _, _, n_head_g, dk, dv = state_gla.shape
    w_a = n_head * hd
    qk_g = n_head_g * dk
    v_g = n_head_g * dv
    rank = w_gla_gate.shape[1]
    n_main = 3 * w_a + 2 * qk_g + 2 * v_g

    hp = x_prompt.reshape(bp * tp, d_model)
    hs = x_sample.reshape(bs * ts, d_model)
    outs = [[] for _ in range(6)]
    for l in range(depth):
        wm = w_in[l, :, :n_main].astype(BF16)
        wlr = jnp.pad(w_in[l, :, n_main:n_main + rank], ((0, 0), (0, LANES - rank))).astype(BF16)
        wgt = w_in[l, :, n_main + rank:].astype(BF16)
        wgg = jnp.pad(w_gla_gate[l], ((0, LANES - rank), (0, 0))).astype(BF16)
        bgg = b_gla_gate[l][None, :]
        wa, wb, wo = (w[l].astype(BF16) for w in (w_branch_a, w_branch_b, w_out))
        wu, wd = w_up[l].astype(BF16), w_down[l].astype(BF16)
        g_mix, g_pm, g_mlp, g_pl, g_gn = (g[l][None, :] for g in
                                          (g_pre_mix, g_post_mix, g_pre_mlp, g_post_mlp, g_gla_norm))

        def project(x, kv_seq_len):
            return _proj(x, g_mix, wm, wlr, wgt, wgg, bgg, w_a=w_a, qk_g=qk_g, v_g=v_g, kv_seq_len=kv_seq_len)

        def mix(x, b, t, oa, proj_out, s0):
            qg, kg, vg, rg, la, sga, sgb = proj_out[3:10]
            r3 = lambda a: a.reshape(b, t, a.shape[-1])
            og, s_new = _gla(r3(qg), r3(kg), r3(vg), r3(la), r3(rg), g_gn, s0, chunk=_gla_chunk(t), dk=dk, dv=dv)
            n = b * t
            x = _rowwise_call(_mix_kernel, "mix", [x, oa.reshape(n, w_a), og.reshape(n, v_g), sga, sgb],
                              [wa, wb, wo, wu, wd, g_pm, g_mlp, g_pl], d_model)
            return x, s_new

        pp = project(hp, tp)
        ps = project(hs, None)
        qa_p, kt_p, vt_p, kmean_p, k_bf, vt_bf = pp[0], pp[1], pp[2], pp[10], pp[11], pp[12]
        qa_s, ka_s, va_s = (a.reshape(bs, ts, w_a) for a in ps[:3])
        to_stored = lambda c: jnp.transpose(c[l], (0, 2, 3, 1))
        oa_p, oa_s = _moba(qa_p.reshape(bp, tp, w_a), k_bf.reshape(bp, tp, w_a), vt_bf,
                           kmean_p.reshape(bp, tp // MOBA_BLOCK, w_a), qa_s, ka_s, va_s,
                           to_stored(cache_k), to_stored(cache_v), page_table, hd=hd)
        hp, sp = mix(hp, bp, tp, oa_p, pp, jnp.zeros((bp, n_head_g, dk, dv), state_gla.dtype))
        hs, ssn = mix(hs, bs, ts, oa_s, ps, state_gla[l])
        kp, vp = (jnp.transpose(a.reshape(bp, n_head, hd, tp), (0, 3, 1, 2)) for a in (kt_p, vt_p))
        ksn, vsn = (a.reshape(bs, ts, n_head, hd) for a in (ka_s, va_s))
        for lst, val in zip(outs, (kp, vp, sp, ksn, vsn, ssn)):
            lst.append(val)
    return (hp.reshape(bp, tp, d_model), hs.reshape(bs, ts, d_model)) + tuple(jnp.stack(o) for o in outs)
```

```python
import functools

import ml_dtypes
import numpy as np
import jax
import jax.numpy as jnp
from jax import lax
from jax.experimental import pallas as pl
from jax.experimental.pallas import tpu as pltpu

F32 = jnp.float32
BF16 = jnp.bfloat16

LANES = 128
SUBLANES = 8
VMEM_LIMIT_BYTES = 56 * 1024 * 1024

EPS = 1e-6
MOBA_BLOCK = 256
MOBA_TOPK = 3
GLA_TAU = 16.0
NEG_BIG = -1e30
PENALTY = -30000.0
LOG2E = float(np.log2(np.e))

_NT = (((1,), (1,)), ((), ()))
_TN = (((0,), (0,)), ((), ()))


def _params(*sem):
    return pltpu.CompilerParams(dimension_semantics=sem, vmem_limit_bytes=VMEM_LIMIT_BYTES)


def _sigmoid(x):
    return 1.0 / (1.0 + jnp.exp(-x))


def _rms(x, g):
    return x * lax.rsqrt(jnp.mean(x * x, axis=-1, keepdims=True) + EPS) * g


def _bdot(a, b):
    return jnp.dot(a.astype(BF16), b.astype(BF16), preferred_element_type=F32)


def _bdot_nt(a, b):
    return lax.dot_general(a.astype(BF16), b.astype(BF16), _NT, preferred_element_type=F32)


def _proj_kernel(x_ref, g_ref, wm_ref, wlr_ref, wgt_ref, wgg_ref, bgg_ref,
                 qa_ref, ka_ref, va_ref, qg_ref, kg_ref, vg_ref, rg_ref, la_ref, sga_ref, sgb_ref, kmean_ref,
                 *attn_refs, w_a, qk_g, v_g, d_model, kv_transposed):
    hb = _rms(x_ref[...], g_ref[...]).astype(BF16)

    def proj(lo, n):
        return jnp.dot(hb, wm_ref[:, lo:lo + n], preferred_element_type=F32)

    qa_ref[...] = proj(0, w_a)
    ka = proj(w_a, w_a)
    va = proj(2 * w_a, w_a)
    for i in range(kmean_ref.shape[0]):
        kmean_ref[i] = jnp.sum(ka[i * MOBA_BLOCK:(i + 1) * MOBA_BLOCK], axis=0, keepdims=True) * (1.0 / MOBA_BLOCK)
    if kv_transposed:
        k_bf_ref, vt_bf_ref = attn_refs
        vt = va.T
        ka_ref[0] = ka.T
        va_ref[0] = vt
        k_bf_ref[...] = ka.astype(BF16)
        vt_bf_ref[0] = vt.astype(BF16)
    else:
        ka_ref[...] = ka
        va_ref[...] = va
    lo = 3 * w_a
    qg_ref[...] = proj(lo, qk_g)
    kg_ref[...] = proj(lo + qk_g, qk_g)
    vg_ref[...] = proj(lo + 2 * qk_g, v_g)
    rg_ref[...] = proj(lo + 2 * qk_g + v_g, v_g)
    lr = jnp.dot(hb, wlr_ref[...], preferred_element_type=F32)
    xg = jnp.dot(lr.astype(BF16), wgg_ref[...], preferred_element_type=F32) + bgg_ref[...]
    log_sig = jnp.minimum(xg, 0.0) - jnp.log1p(jnp.exp(-jnp.abs(xg)))
    la_ref[...] = log_sig * (1.0 / GLA_TAU)
    ga = jnp.dot(hb, wgt_ref[:, :d_model], preferred_element_type=F32)
    sga_ref[...] = _sigmoid(ga).astype(BF16)
    gb = jnp.dot(hb, wgt_ref[:, d_model:], preferred_element_type=F32)
    sgb_ref[...] = _sigmoid(gb).astype(BF16)


def _proj(x, g, wm, wlr, wgt, wgg, bgg, *, w_a, qk_g, v_g, kv_seq_len=None):
    n, d_model = x.shape
    tm = _row_tile(n)
    assert n % tm == 0
    nt = n // tm
    blocks_per_tile = tm // MOBA_BLOCK
    row = lambda w: pl.BlockSpec((tm, w), lambda i: (i, 0))
    full = lambda a: pl.BlockSpec(a.shape, lambda i: (0,) * a.ndim, pipeline_mode=pl.Buffered(1))
    widths = (w_a, w_a, w_a, qk_g, qk_g, v_g, v_g, qk_g)
    out_shape = [jax.ShapeDtypeStruct((n, w), F32) for w in widths]
    out_shape += [jax.ShapeDtypeStruct((n, d_model), BF16)] * 2
    out_shape += [jax.ShapeDtypeStruct((n // MOBA_BLOCK, 1, w_a), F32)]
    out_specs = [row(w) for w in widths] + [row(d_model)] * 2
    out_specs += [pl.BlockSpec((blocks_per_tile, 1, w_a), lambda i: (i, 0, 0))]
    if kv_seq_len is not None:
        assert kv_seq_len % tm == 0
        seq_tiles = kv_seq_len // tm
        transposed = pl.BlockSpec((1, w_a, tm), lambda i: (i // seq_tiles, 0, i % seq_tiles))
        for i in (1, 2):
            out_shape[i] = jax.ShapeDtypeStruct((n // kv_seq_len, w_a, kv_seq_len), F32)
            out_specs[i] = transposed
        out_shape += [jax.ShapeDtypeStruct((n, w_a), BF16),
                      jax.ShapeDtypeStruct((n // kv_seq_len, w_a, kv_seq_len), BF16)]
        out_specs += [row(w_a), transposed]
    return pl.pallas_call(
        functools.partial(_proj_kernel, w_a=w_a, qk_g=qk_g, v_g=v_g, d_model=d_model,
                          kv_transposed=kv_seq_len is not None),
        grid=(nt,),
        in_specs=[row(d_model), full(g), full(wm), full(wlr), full(wgt), full(wgg), full(bgg)],
        out_specs=out_specs,
        out_shape=out_shape,
        compiler_params=_params("parallel"),
        name="proj",
    )(x, g, wm, wlr, wgt, wgg, bgg)


def _top_blocks(gate, candidate, block, axis=-1):
    block_f = block.astype(F32)
    no_block = float(gate.shape[axis])
    avail = jnp.where(candidate, 1.0, 0.0)
    sel = jnp.zeros(gate.shape, F32)
    for _ in range(MOBA_TOPK):
        cur = jnp.where(avail > 0.0, gate, -jnp.inf)
        mx = jnp.max(cur, axis=axis, keepdims=True)
        cand = jnp.where(avail > 0.0, jnp.where(cur == mx, block_f, no_block), no_block)
        first = jnp.min(cand, axis=axis, keepdims=True)
        pick = block_f == first
        sel = jnp.where(pick, 1.0, sel)
        avail = jnp.where(pick, 0.0, avail)
    return sel


def _alibi_slopes(n_head):
    return 2.0 ** (-8.0 * (np.arange(n_head) + 1) / n_head)


_SLOPE_PARTS = 3
_AUG_EXTRA = 2 * _SLOPE_PARTS


def _moba_prompt_step(r, n_sub, hp, qi, sl_ref, q_ref, k_ref, vt_ref, kc_ref, km_ref, o_ref,
                      qaug_ref, m_ref, acc_ref, *, hd, tk):
    blk = MOBA_BLOCK
    t = k_ref.shape[1]
    tq = q_ref.shape[1]
    n_blk = t // blk
    per_pair = LANES // hd
    n_acc = acc_ref.shape[1]
    tiles_per_q = tq // tk
    chains = [(hh, cb) for hh in range(per_pair) for cb in range(tiles_per_q)]
    ones = jnp.ones((n_acc - hd, tk), BF16)

    def tiles(j):
        keys = pl.ds(pl.multiple_of(j * tk, tk), tk)
        kt = jnp.concatenate([k_ref[0, keys, :], kc_ref[keys, :]], axis=1)
        return kt, [jnp.concatenate([vt_ref[0, hh * hd:(hh + 1) * hd, keys], ones], axis=0) for hh in range(per_pair)]

    def logits(kt, hh, cb):
        return jnp.dot(kt, qaug_ref[hh, :, cb * tk:(cb + 1) * tk], preferred_element_type=F32)

    def absorb(stat, s, va, c):
        m, acc = stat
        m_new = jnp.maximum(m, jnp.max(s, axis=0, keepdims=True) + c)
        p = jnp.exp2(s - (m_new - c)).astype(BF16)
        return m_new, jnp.exp2(m - m_new) * acc + jnp.dot(va, p, preferred_element_type=F32)

    @pl.when(r == 0)
    def _():
        lane = lax.broadcasted_iota(jnp.int32, (tq, LANES), 1)
        q = q_ref[0]
        n_rows = -(-n_blk // SUBLANES) * SUBLANES
        block = lax.broadcasted_iota(jnp.int32, (n_rows, tq), 0)
        own = qi * (tq // blk) + lax.broadcasted_iota(jnp.int32, (n_rows, tq), 1) // blk
        extra = lax.broadcasted_iota(jnp.int32, (SUBLANES, tq), 0)
        assert _AUG_EXTRA <= SUBLANES and n_rows + SUBLANES <= LANES
        for hh in range(per_pair):
            h = hp * per_pair + hh
            head_lanes = (lane >= hh * hd) & (lane < (hh + 1) * hd)
            qh = jnp.where(head_lanes, q, 0.0)
            gate = _bdot_nt(km_ref[0, :n_rows, :], qh)
            sel = _top_blocks(gate, block < own, block, axis=0)
            penalty = jnp.where((sel > 0.0) | (block == own) | (block >= n_blk), 0.0, PENALTY)
            parts = [sl_ref[1 + i, h] for i in range(_SLOPE_PARTS)]
            slope_rows = jnp.zeros((SUBLANES, tq), F32)
            for off, val in enumerate(parts + [part * blk for part in parts]):
                slope_rows = jnp.where(extra == off, val, slope_rows)
            qx_t = jnp.concatenate([penalty, slope_rows, jnp.zeros((LANES - n_rows - SUBLANES, tq), F32)], axis=0)
            qaug_ref[hh, :LANES, :] = (qh * (hd ** -0.5 * LOG2E)).T.astype(BF16)
            qaug_ref[hh, LANES:, :] = qx_t.astype(BF16)

        key2 = lax.broadcasted_iota(jnp.int32, (tk, tk), 0)
        qry2 = lax.broadcasted_iota(jnp.int32, (tk, tk), 1)
        stats = {c: (jnp.full((1, tk), NEG_BIG, F32), jnp.zeros((n_acc, tk), F32)) for c in chains}
        for d in range(tiles_per_q):
            kt, va = tiles(qi * tiles_per_q + d)
            live = [(hh, cb) for hh, cb in chains if cb >= d]
            ss = [logits(kt, hh, cb) for hh, cb in live]
            for (hh, cb), s in zip(live, ss):
                if cb == d:
                    s = jnp.where(key2 <= qry2, s, NEG_BIG)
                stats[hh, cb] = absorb(stats[hh, cb], s, va[hh], 0.0)
        for ci, c in enumerate(chains):
            m_ref[ci], acc_ref[ci] = stats[c]

    def absorb_tiles(js, flat):
        loaded = [tiles(j) for j in js]
        ss = [[logits(kt, hh, cb) for hh, cb in chains] for kt, _ in loaded]
        out = list(flat)
        for j, (_, va), s_tile in zip(js, loaded, ss):
            for ci, (hh, cb) in enumerate(chains):
                c = -(sl_ref[0, hp * per_pair + hh] * tq) * (qi - j // tiles_per_q).astype(F32)
                out[2 * ci:2 * ci + 2] = absorb(out[2 * ci:2 * ci + 2], s_tile[ci], va[hh], c)
        return tuple(out)

    n_mine = (qi * tiles_per_q - r + n_sub - 1) // n_sub
    flat = tuple(x for ci in range(len(chains)) for x in (m_ref[ci], acc_ref[ci]))
    flat = lax.fori_loop(0, n_mine // 2, lambda i, f: absorb_tiles([r + 2 * i * n_sub, r + (2 * i + 1) * n_sub], f), flat)
    flat = lax.fori_loop(0, n_mine % 2, lambda i, f: absorb_tiles([r + (n_mine - 1) * n_sub], f), flat)
    for ci in range(len(chains)):
        m_ref[ci], acc_ref[ci] = flat[2 * ci], flat[2 * ci + 1]

    @pl.when(r == n_sub - 1)
    def _():
        for cb in range(tiles_per_q):
            accs = [flat[2 * chains.index((hh, cb)) + 1] for hh in range(per_pair)]
            out = jnp.concatenate([acc[:hd] / acc[hd:hd + 1] for acc in accs], axis=0)
            o_ref[0, cb * tk:(cb + 1) * tk, :] = out.T


def _moba_consts(t, tq):
    n_blk = t // MOBA_BLOCK
    base = -(-n_blk // SUBLANES) * SUBLANES
    assert base + _AUG_EXTRA <= LANES
    pos = np.arange(t)
    c = np.zeros((t, LANES), np.float32)
    c[pos, pos // MOBA_BLOCK] = 1.0
    for i in range(_SLOPE_PARTS):
        c[:, base + i] = pos % MOBA_BLOCK
        c[:, base + _SLOPE_PARTS + i] = (pos % tq) // MOBA_BLOCK
    return jnp.asarray(c, dtype=BF16)


_BF16_ROWS = 2 * SUBLANES


def _moba_sample_ops(slopes_ref, qbd_ref, s_ref, p_ref, ksum_ref, l_ref, acc_ref, *,
                     ts, w, page, n_per_step, pages_per_block, hd):
    qk_pages, pv_pages = 4, 2
    n_head = w // hd
    rows = n_head * ts
    blk = page * pages_per_block
    past = s_ref.shape[1]
    n_blk = past // blk
    scale = hd ** -0.5

    row_head = lax.broadcasted_iota(jnp.int32, (rows, 1), 0) // ts
    row_q = lax.broadcasted_iota(jnp.int32, (rows, 1), 0) % ts
    col_head = lax.broadcasted_iota(jnp.int32, (rows, w), 1) // hd
    slope = jnp.zeros((rows, 1), F32)
    for h in range(n_head):
        slope = jnp.where(row_head == h, slopes_ref[h], slope)

    def new_sequence(q):
        qt = jnp.concatenate([q] * n_head, axis=0)
        qbd_ref[...] = jnp.where(col_head == row_head, qt, 0.0)
        ksum_ref[...] = jnp.zeros(ksum_ref.shape, F32)

    def k_group(pages_ref, j):
        qs = (qbd_ref[...] * scale).astype(BF16)
        lane = lax.broadcasted_iota(jnp.int32, ksum_ref.shape, 1)
        ksum = ksum_ref[...]
        group = qk_pages if n_per_step % qk_pages == 0 and qk_pages % pages_per_block == 0 else pages_per_block
        for g0 in range(0, n_per_step, group):
            kts = [pages_ref[p_i].reshape(w, page) for p_i in range(g0, g0 + group)]
            first = j * n_per_step + g0
            s = jnp.dot(qs, jnp.concatenate([kt.astype(BF16) for kt in kts], axis=1), preferred_element_type=F32)
            s_ref[:, pl.ds(pl.multiple_of(first * page, group * page), group * page)] = s
            for b0 in range(0, group, pages_per_block):
                total = kts[b0]
                for kt in kts[b0 + 1:b0 + pages_per_block]:
                    total = total + kt
                ksum = jnp.where(lane == (first + b0) // pages_per_block, jnp.sum(total, axis=1, keepdims=True), ksum)
        ksum_ref[...] = ksum

    def boundary(kn, vn):
        qbd = qbd_ref[...]
        gate = _bdot(qbd, ksum_ref[...])
        lane = lax.broadcasted_iota(jnp.int32, gate.shape, 1)
        sel = _top_blocks(gate, lane < n_blk, lane)
        s_own = _bdot_nt(qbd * scale, kn)
        kq = lax.broadcasted_iota(jnp.int32, s_own.shape, 1)
        s_own = jnp.where(kq <= row_q, s_own + slope * kq.astype(F32), NEG_BIG)
        key_in_blk = lax.broadcasted_iota(jnp.int32, (1, blk), 1)

        def add_bias(b_i, m_wide):
            cols = pl.ds(pl.multiple_of(b_i * blk, blk), blk)
            picked = jnp.max(jnp.where(lane == b_i, sel, 0.0), axis=-1, keepdims=True)
            rel = (b_i * blk - past + key_in_blk).astype(F32)
            s = s_ref[:, cols] + slope * rel + jnp.where(picked > 0.0, 0.0, NEG_BIG)
            s_ref[:, cols] = s
            return jnp.maximum(m_wide, s)

        unroll = 8 if n_blk % 8 == 0 else 1
        m_wide = lax.fori_loop(0, n_blk, add_bias, jnp.full((rows, blk), NEG_BIG, F32), unroll=unroll)
        m = jnp.maximum(jnp.max(m_wide, axis=-1, keepdims=True), jnp.max(s_own, axis=-1, keepdims=True))

        def probs(b_i, l_wide):
            cols = pl.ds(pl.multiple_of(b_i * blk, blk), blk)
            p = jnp.exp(s_ref[:, cols] - m)
            p_ref[:, cols] = p.astype(BF16)
            return l_wide + p

        l_wide = lax.fori_loop(0, n_blk, probs, jnp.zeros((rows, blk), F32), unroll=unroll)
        p_own = jnp.exp(s_own - m)
        l_ref[...] = jnp.sum(l_wide, axis=-1, keepdims=True) + jnp.sum(p_own, axis=-1, keepdims=True)
        acc_ref[...] = _bdot(p_own, vn)

    def v_group(pages_ref, j):
        acc = acc_ref[...]
        group = pv_pages if n_per_step % pv_pages == 0 else 1
        for g0 in range(0, n_per_step, group):
            first = j * n_per_step + g0
            p = p_ref[:, pl.ds(pl.multiple_of(first * page, group * page), group * page)]
            vt = jnp.concatenate([pages_ref[p_i].reshape(w, page).astype(BF16) for p_i in range(g0, g0 + group)], axis=1)
            acc = acc + lax.dot_general(p, vt, _NT, preferred_element_type=F32)
        acc_ref[...] = acc

    def result():
        res = jnp.where(col_head == row_head, acc_ref[...] / l_ref[...], 0.0)
        out = res[0:ts]
        for h in range(1, n_head):
            out = out + res[h * ts:(h + 1) * ts]
        return out

    return new_sequence, k_group, boundary, v_group, result


_K_SLOT, _V_SLOT = 0, 1


def _moba_kernel(pt_ref, sl_ref, slopes_ref, q_ref, k_ref, vt_ref, kc_ref, km_ref, qs_ref, kn_ref, vn_ref,
                 cache_k_hbm, cache_v_hbm, o_ref, os_ref, qaug_ref, m_ref, acc_ref, pages_ref, sem, *sample_scratch,
                 pages_per_block, hd, tk, n_k_steps):
    hp, qi, r = pl.program_id(1), pl.program_id(2), pl.program_id(3)
    n_sub = pl.num_programs(3)
    u = ((pl.program_id(0) * pl.num_programs(1) + hp) * pl.num_programs(2) + qi) * n_sub + r
    n_steps = pl.num_programs(0) * pl.num_programs(1) * pl.num_programs(2) * n_sub
    n_seq, ts, w = qs_ref.shape
    _, n_per_step, _, _, page = pages_ref.shape
    new_sequence, k_group, boundary, v_group, result = _moba_sample_ops(
        slopes_ref, *sample_scratch, ts=ts, w=w, page=page, n_per_step=n_per_step,
        pages_per_block=pages_per_block, hd=hd)

    def start_pages(cache_hbm, seq, group, slot):
        for p in range(n_per_step):
            pltpu.make_async_copy(cache_hbm.at[pt_ref[seq, group * n_per_step + p]],
                                  pages_ref.at[slot, p], sem.at[slot]).start()

    def wait_pages(slot):
        pltpu.make_async_copy(cache_k_hbm.at[pl.ds(0, n_per_step)], pages_ref.at[slot], sem.at[slot]).wait()

    def start_step(step):
        seq, group = step // n_k_steps, step % n_k_steps
        start_pages(cache_k_hbm, seq, group, _K_SLOT)
        start_pages(cache_v_hbm, jnp.maximum(seq - 1, 0), group, _V_SLOT)

    @pl.when(u == 0)
    def _():
        start_step(u)
        p_ref, l_ref, sacc_ref = sample_scratch[2], sample_scratch[4], sample_scratch[5]
        p_ref[...] = jnp.zeros(p_ref.shape, p_ref.dtype)
        l_ref[...] = jnp.ones(l_ref.shape, F32)
        sacc_ref[...] = jnp.zeros(sacc_ref.shape, F32)

    _moba_prompt_step(r, n_sub, hp, qi, sl_ref, q_ref, k_ref, vt_ref, kc_ref, km_ref, o_ref,
                      qaug_ref, m_ref, acc_ref, hd=hd, tk=tk)

    seq, group = u // n_k_steps, u % n_k_steps

    @pl.when((group == 0) & (u > 0))
    def _():
        boundary(kn_ref[seq - 1], vn_ref[seq - 1])

    @pl.when(group == 0)
    def _():
        new_sequence(qs_ref[seq])

    wait_pages(_K_SLOT)
    wait_pages(_V_SLOT)
    k_group(pages_ref.at[_K_SLOT], group)
    v_group(pages_ref.at[_V_SLOT], group)

    @pl.when((group == n_k_steps - 1) & (seq > 0))
    def _():
        os_ref[seq - 1] = result()

    @pl.when(u + 1 < n_steps)
    def _():
        start_step(u + 1)

    @pl.when(u == n_steps - 1)
    def _():
        boundary(kn_ref[n_seq - 1], vn_ref[n_seq - 1])
        for g in range(n_k_steps):
            slot = g % 2
            if g == 0:
                start_pages(cache_v_hbm, n_seq - 1, 0, slot)
            if g + 1 < n_k_steps:
                start_pages(cache_v_hbm, n_seq - 1, g + 1, 1 - slot)
            wait_pages(slot)
            v_group(pages_ref.at[slot], g)
        os_ref[n_seq - 1] = result()


def _moba(qa, k_bf, vt_bf, kmean, qa_s, ka_s, va_s, cache_kt, cache_vt, page_table, *, hd):
    b, t, w = qa.shape
    bs, ts, _ = qa_s.shape
    _, n_head, _, page = cache_kt.shape
    n_pages = page_table.shape[1]
    blk = MOBA_BLOCK
    ppb = blk // page
    n_pair = w // LANES
    n_blk = t // blk
    tk = 2 * blk if t % (2 * blk) == 0 else blk
    tq = 2 * tk if t % (2 * tk) == 0 else tk
    n_q = t // tq
    n_per_step = next(p for p in (32, 16, 8, 4, 2) if n_pages % p == 0 and p % ppb == 0
                      and (bs * (n_pages // p)) % (b * n_pair * n_q) == 0)
    n_k_steps = n_pages // n_per_step
    n_sub = bs * n_k_steps // (b * n_pair * n_q)
    assert n_pages // ppb <= LANES
    rows = n_head * ts

    sl2 = _alibi_slopes(n_head) * LOG2E
    parts, rest = [], sl2
    for _ in range(_SLOPE_PARTS):
        parts.append(rest.astype(ml_dtypes.bfloat16).astype(np.float64))
        rest = rest - parts[-1]
    sl = jnp.asarray(np.stack([sl2] + parts), dtype=F32)
    slopes = jnp.asarray(_alibi_slopes(n_head), dtype=F32)
    kc = _moba_consts(t, tq)
    km = jnp.pad(kmean, ((0, 0), (0, LANES - n_blk), (0, 0)))

    def whole(a):
        return pl.BlockSpec(a.shape, lambda bi, hp, qi, r, pt: (0,) * a.ndim)

    smem = pl.BlockSpec(memory_space=pltpu.SMEM)
    hbm = pl.BlockSpec(memory_space=pl.ANY)
    n_chain = (LANES // hd) * (tq // tk)
    return pl.pallas_call(
        functools.partial(_moba_kernel, pages_per_block=ppb, hd=hd, tk=tk, n_k_steps=n_k_steps),
        grid_spec=pltpu.PrefetchScalarGridSpec(
            num_scalar_prefetch=1,
            grid=(b, n_pair, n_q, n_sub),
            in_specs=[smem, smem,
                      pl.BlockSpec((1, tq, LANES), lambda bi, hp, qi, r, pt: (bi, qi, hp)),
                      pl.BlockSpec((1, t, LANES), lambda bi, hp, qi, r, pt: (bi, 0, hp), pipeline_mode=pl.Buffered(1)),
                      pl.BlockSpec((1, LANES, t), lambda bi, hp, qi, r, pt: (bi, hp, 0), pipeline_mode=pl.Buffered(1)),
                      pl.BlockSpec((t, LANES), lambda bi, hp, qi, r, pt: (0, 0), pipeline_mode=pl.Buffered(1)),
                      pl.BlockSpec((1, LANES, LANES), lambda bi, hp, qi, r, pt: (bi, 0, hp)),
                      whole(qa_s), whole(ka_s), whole(va_s), hbm, hbm],
            out_specs=[pl.BlockSpec((1, tq, LANES), lambda bi, hp, qi, r, pt: (bi, qi, hp)), whole(qa_s)],
            scratch_shapes=[pltpu.VMEM((LANES // hd, 2 * LANES, tq), BF16),
                            pltpu.VMEM((n_chain, 1, tk), F32),
                            pltpu.VMEM((n_chain, hd + _BF16_ROWS, tk), F32),
                            pltpu.VMEM((2, n_per_step, n_head, hd, page), F32),
                            pltpu.SemaphoreType.DMA((2,)),
                            pltpu.VMEM((rows, w), F32),
                            pltpu.VMEM((rows, n_pages * page), F32),
                            pltpu.VMEM((rows, n_pages * page), BF16),
                            pltpu.VMEM((w, LANES), F32),
                            pltpu.VMEM((rows, 1), F32), pltpu.VMEM((rows, w), F32)],
        ),
        out_shape=[jax.ShapeDtypeStruct((b, t, w), F32), jax.ShapeDtypeStruct((bs, ts, w), F32)],
        compiler_params=_params("arbitrary", "arbitrary", "arbitrary", "arbitrary"),
        name="moba",
    )(page_table, sl, slopes, qa, k_bf, vt_bf, kc, km, qa_s, ka_s, va_s, cache_kt, cache_vt)


def _gla_kernel(q_ref, k_ref, v_ref, la_ref, rg_ref, gn_ref, s0_ref, o_ref, sout_ref, st_ref, *, dk, dv, chunk):
    c_idx = pl.program_id(1)
    n_chunks = pl.num_programs(1)
    c = chunk
    n_pair = q_ref.shape[2] // LANES
    per_pair = LANES // dk
    assert dv == LANES and c % SUBLANES == 0

    @pl.when(c_idx == 0)
    def _():
        for p in range(n_pair):
            st_ref[p] = s0_ref[0, p].T

    row = lax.broadcasted_iota(jnp.int32, (c, LANES), 0)
    lane = lax.broadcasted_iota(jnp.int32, (c, LANES), 1)
    row2 = lax.broadcasted_iota(jnp.int32, (c, c), 0)
    col2 = lax.broadcasted_iota(jnp.int32, (c, c), 1)
    tri = jnp.where(row2 >= col2, 1.0, 0.0)
    nb8 = c // SUBLANES
    sub = lax.broadcasted_iota(jnp.int32, (nb8, SUBLANES, LANES), 1)
    lane3 = lax.broadcasted_iota(jnp.int32, (nb8, SUBLANES, LANES), 2)
    lane_st = lax.broadcasted_iota(jnp.int32, (dv, LANES), 1)

    def bcast_row(x3, jj):
        return jnp.broadcast_to(x3[:, jj:jj + 1, :], x3.shape)

    def one_chunk(p, rows, b_all, st):
        cols = slice(p * LANES, (p + 1) * LANES)
        q = q_ref[0, rows, cols] * (dk ** -0.5)
        k = k_ref[0, rows, cols]
        b = b_all[:, cols]
        vs = [v_ref[0, rows, (p * per_pair + hh) * dv:(p * per_pair + hh + 1) * dv] for hh in range(per_pair)]
        head_lanes = [(lane >= hh * dk) & (lane < (hh + 1) * dk) for hh in range(per_pair)]
        head_lanes3 = [(lane3 >= hh * dk) & (lane3 < (hh + 1) * dk) for hh in range(per_pair)]

        qe = q * jnp.exp(b)
        o = [_bdot_nt(jnp.where(head_lanes[hh], qe, 0.0), st) for hh in range(per_pair)]

        q3 = q.reshape(nb8, SUBLANES, LANES)
        k3 = k.reshape(nb8, SUBLANES, LANES)
        b3 = b.reshape(nb8, SUBLANES, LANES)
        v3 = [v.reshape(nb8, SUBLANES, dv) for v in vs]
        o3 = [jnp.zeros((nb8, SUBLANES, dv), F32) for _ in range(per_pair)]
        for jj in range(SUBLANES):
            e = jnp.exp(jnp.minimum(b3 - bcast_row(b3, jj), 0.0))
            term = jnp.where(sub >= jj, q3 * bcast_row(k3, jj) * e, 0.0)
            for hh in range(per_pair):
                a = jnp.sum(jnp.where(head_lanes3[hh], term, 0.0), axis=-1, keepdims=True)
                o3[hh] = o3[hh] + a * bcast_row(v3[hh], jj)
        o = [o[hh] + o3[hh].reshape(c, dv) for hh in range(per_pair)]

        attn = [jnp.zeros((c, c), F32) for _ in range(per_pair)]
        m_half = SUBLANES
        while 2 * m_half <= c:
            span = 2 * m_half
            bnd = jnp.broadcast_to(b.reshape(c // span, span, LANES)[:, m_half - 1:m_half, :],
                                   (c // span, span, LANES)).reshape(c, LANES)
            upper = (row % span) >= m_half
            qm = jnp.where(upper, q * jnp.exp(jnp.minimum(b - bnd, 0.0)), 0.0)
            km = jnp.where(upper, 0.0, k * jnp.exp(jnp.minimum(bnd - b, 0.0)))
            same = (row2 // span) == (col2 // span)
            for hh in range(per_pair):
                a = _bdot_nt(jnp.where(head_lanes[hh], qm, 0.0), km)
                attn[hh] = attn[hh] + jnp.where(same, a, 0.0)
            m_half = span
        if c > SUBLANES:
            o = [o[hh] + _bdot(attn[hh], vs[hh]) for hh in range(per_pair)]

        b_last = b[c - 1:c, :]
        kk = (k * jnp.exp(b_last - b)).astype(BF16)
        upd = lax.dot_general(vs[0].astype(BF16), kk, _TN, preferred_element_type=F32)
        for hh in range(1, per_pair):
            u = lax.dot_general(vs[hh].astype(BF16), kk, _TN, preferred_element_type=F32)
            upd = jnp.where(lane_st >= hh * dk, u, upd)
        for hh in range(per_pair):
            h = p * per_pair + hh
            rg = rg_ref[0, rows, h * dv:(h + 1) * dv]
            o_ref[0, rows, h * dv:(h + 1) * dv] = _rms(o[hh], gn_ref[...]) * (rg * _sigmoid(rg))
        return st * jnp.exp(b_last) + upd

    states = [st_ref[p] for p in range(n_pair)]
    for ch in range(q_ref.shape[1] // c):
        rows = slice(ch * c, (ch + 1) * c)
        b_all = jnp.dot(tri, la_ref[0, rows, :], precision=lax.Precision.HIGHEST, preferred_element_type=F32)
        states = [one_chunk(p, rows, b_all, states[p]) for p in range(n_pair)]
    for p in range(n_pair):
        st_ref[p] = states[p]

    @pl.when(c_idx == n_chunks - 1)
    def _():
        for p in range(n_pair):
            sout_ref[0, p] = states[p].T


def _gla(qg, kg, vg, la, rg, g_norm, s0, *, chunk, dk, dv):
    b, t, wk = qg.shape
    wv = vg.shape[2]
    n_head = wk // dk
    n_pair = wk // LANES
    s0p = s0.reshape(b, n_pair, LANES, dv)
    step = next(n * chunk for n in (4, 2, 1) if t % (n * chunk) == 0)
    tok = lambda w: pl.BlockSpec((1, step, w), lambda bi, ci: (bi, ci, 0))
    st_spec = pl.BlockSpec((1, n_pair, LANES, dv), lambda bi, ci: (bi, 0, 0, 0))
    og, s_new = pl.pallas_call(
        functools.partial(_gla_kernel, dk=dk, dv=dv, chunk=chunk),
        grid=(b, t // step),
        in_specs=[tok(wk), tok(wk), tok(wv), tok(wk), tok(wv),
                  pl.BlockSpec(g_norm.shape, lambda bi, ci: (0, 0)), st_spec],
        out_specs=[tok(wv), st_spec],
        out_shape=[jax.ShapeDtypeStruct((b, t, wv), F32), jax.ShapeDtypeStruct((b, n_pair, LANES, dv), F32)],
        scratch_shapes=[pltpu.VMEM((n_pair, dv, LANES), F32)],
        compiler_params=_params("parallel", "arbitrary"),
        name="gla",
    )(qg, kg, vg, la, rg, g_norm, s0p)
    return og, s_new.reshape(b, n_head, dk, dv)


def _mix_kernel(x_ref, oa_ref, og_ref, sga_ref, sgb_ref, wa_ref, wb_ref, wo_ref, wu_ref, wd_ref,
                g_mix_ref, g_pre_ref, g_post_ref, y_ref):
    merged = (sga_ref[...].astype(F32) * _bdot(oa_ref[...], wa_ref[...])
              + sgb_ref[...].astype(F32) * _bdot(og_ref[...], wb_ref[...]))
    x = x_ref[...] + _rms(_bdot(merged, wo_ref[...]), g_mix_ref[...])
    u = _bdot(_rms(x, g_pre_ref[...]), wu_ref[...])
    u = jnp.square(jnp.maximum(u, 0.0))
    y_ref[...] = x + _rms(_bdot(u, wd_ref[...]), g_post_ref[...])


def _row_tile(n):
    return 2 * MOBA_BLOCK if n % (2 * MOBA_BLOCK) == 0 else MOBA_BLOCK


def _rowwise_call(kernel, name, row_inputs, const_inputs, out_width):
    n = row_inputs[0].shape[0]
    tm = _row_tile(n)
    assert n % tm == 0
    row = lambda a: pl.BlockSpec((tm, a.shape[1]), lambda i: (i, 0))
    full = lambda a: pl.BlockSpec(a.shape, lambda i: (0,) * a.ndim, pipeline_mode=pl.Buffered(1))
    return pl.pallas_call(
        kernel,
        grid=(n // tm,),
        in_specs=[row(a) for a in row_inputs] + [full(a) for a in const_inputs],
        out_specs=pl.BlockSpec((tm, out_width), lambda i: (i, 0)),
        out_shape=jax.ShapeDtypeStruct((n, out_width), F32),
        compiler_params=_params("parallel"),
        name=name,
    )(*row_inputs, *const_inputs)


def _gla_chunk(t):
    c = SUBLANES
    while c * 2 <= min(t, LANES) and t % (c * 2) == 0:
        c *= 2
    return c


def kernel(x_prompt, x_sample, cache_k, cache_v, page_table, state_gla, w_in, w_gla_gate, b_gla_gate, g_gla_norm,
           w_branch_a, w_branch_b, w_out, w_up, w_down, g_pre_mix, g_post_mix, g_pre_mlp, g_post_mlp):
    bp, tp, d_model = x_prompt.shape
    bs, ts, _ = x_sample.shape
    depth, n_phys, page, n_head, hd = cache_k.shape
    _, _, n_head_g, dk, dv = state_gla.shape
    w_a = n_head * hd
    qk_g = n_head_g * dk
    v_g = n_head_g * dv
    rank = w_gla_gate.shape[1]
    n_main = 3 * w_a + 2 * qk_g + 2 * v_g

    hp = x_prompt.reshape(bp * tp, d_model)
    hs = x_sample.reshape(bs * ts, d_model)
    outs = [[] for _ in range(6)]
    for l in range(depth):
        wm = w_in[l, :, :n_main].astype(BF16)
        wlr = jnp.pad(w_in[l, :, n_main:n_main + rank], ((0, 0), (0, LANES - rank))).astype(BF16)
        wgt = w_in[l, :, n_main + rank:].astype(BF16)
        wgg = jnp.pad(w_gla_gate[l], ((0, LANES - rank), (0, 0))).astype(BF16)
        bgg = b_gla_gate[l][None, :]
        wa, wb, wo = (w[l].astype(BF16) for w in (w_branch_a, w_branch_b, w_out))
        wu, wd = w_up[l].astype(BF16), w_down[l].astype(BF16)
        g_mix, g_pm, g_mlp, g_pl, g_gn = (g[l][None, :] for g in
                                          (g_pre_mix, g_post_mix, g_pre_mlp, g_post_mlp, g_gla_norm))

        def project(x, kv_seq_len):
            return _proj(x, g_mix, wm, wlr, wgt, wgg, bgg, w_a=w_a, qk_g=qk_g, v_g=v_g, kv_seq_len=kv_seq_len)

        def mix(x, b, t, oa, proj_out, s0):
            qg, kg, vg, rg, la, sga, sgb = proj_out[3:10]
            r3 = lambda a: a.reshape(b, t, a.shape[-1])
            og, s_new = _gla(r3(qg), r3(kg), r3(vg), r3(la), r3(rg), g_gn, s0, chunk=_gla_chunk(t), dk=dk, dv=dv)
            n = b * t
            x = _rowwise_call(_mix_kernel, "mix", [x, oa.reshape(n, w_a), og.reshape(n, v_g), sga, sgb],
                              [wa, wb, wo, wu, wd, g_pm, g_mlp, g_pl], d_model)
            return x, s_new

        pp = project(hp, tp)
        ps = project(hs, None)
        qa_p, kt_p, vt_p, kmean_p, k_bf, vt_bf = pp[0], pp[1], pp[2], pp[10], pp[11], pp[12]
        qa_s, ka_s, va_s = (a.reshape(bs, ts, w_a) for a in ps[:3])
        to_stored = lambda c: jnp.transpose(c[l], (0, 2, 3, 1))
        oa_p, oa_s = _moba(qa_p.reshape(bp, tp, w_a), k_bf.reshape(bp, tp, w_a), vt_bf,
                           kmean_p.reshape(bp, tp // MOBA_BLOCK, w_a), qa_s, ka_s, va_s,
                           to_stored(cache_k), to_stored(cache_v), page_table, hd=hd)
        hp, sp = mix(hp, bp, tp, oa_p, pp, jnp.zeros((bp, n_head_g, dk, dv), state_gla.dtype))
        hs, ssn = mix(hs, bs, ts, oa_s, ps, state_gla[l])
        kp, vp = (jnp.transpose(a.reshape(bp, n_head, hd, tp), (0, 3, 1, 2)) for a in (kt_p, vt_p))
        ksn, vsn = (a.reshape(bs, ts, n_head, hd) for a in (ka_s, va_s))
        for lst, val in zip(outs, (kp, vp, sp, ksn, vsn, ssn)):
            lst.append(val)
    return (hp.reshape(bp, tp, d_model), hs.reshape(bs, ts, d_model)) + tuple(jnp.stack(o) for o in outs)
```

```python
import functools

import ml_dtypes
import numpy as np
import jax
import jax.numpy as jnp
from jax import lax
from jax.experimental import pallas as pl
from jax.experimental.pallas import tpu as pltpu

F32 = jnp.float32
BF16 = jnp.bfloat16

LANES = 128
SUBLANES = 8
VMEM_LIMIT_BYTES = 56 * 1024 * 1024

EPS = 1e-6
MOBA_BLOCK = 256
MOBA_TOPK = 3
GLA_TAU = 16.0
NEG_BIG = -1e30
PENALTY = -30000.0
LOG2E = float(np.log2(np.e))

_NT = (((1,), (1,)), ((), ()))
_TN = (((0,), (0,)), ((), ()))


def _params(*sem):
    return pltpu.CompilerParams(dimension_semantics=sem, vmem_limit_bytes=VMEM_LIMIT_BYTES)


def _sigmoid(x):
    return 1.0 / (1.0 + jnp.exp(-x))


def _rms(x, g):
    return x * lax.rsqrt(jnp.mean(x * x, axis=-1, keepdims=True) + EPS) * g


def _bdot(a, b):
    return jnp.dot(a.astype(BF16), b.astype(BF16), preferred_element_type=F32)


def _bdot_nt(a, b):
    return lax.dot_general(a.astype(BF16), b.astype(BF16), _NT, preferred_element_type=F32)


def _proj_kernel(x_ref, g_ref, wm_ref, wlr_ref, wgt_ref, wgg_ref, bgg_ref,
                 qa_ref, ka_ref, va_ref, qg_ref, kg_ref, vg_ref, rg_ref, la_ref, sga_ref, sgb_ref, kmean_ref,
                 *attn_refs, w_a, qk_g, v_g, d_model, kv_transposed):
    hb = _rms(x_ref[...], g_ref[...]).astype(BF16)

    def proj(lo, n):
        return jnp.dot(hb, wm_ref[:, lo:lo + n], preferred_element_type=F32)

    qa_ref[...] = proj(0, w_a)
    ka = proj(w_a, w_a)
    va = proj(2 * w_a, w_a)
    for i in range(kmean_ref.shape[0]):
        kmean_ref[i] = jnp.sum(ka[i * MOBA_BLOCK:(i + 1) * MOBA_BLOCK], axis=0, keepdims=True) * (1.0 / MOBA_BLOCK)
    if kv_transposed:
        k_bf_ref, vt_bf_ref = attn_refs
        vt = va.T
        ka_ref[0] = ka.T
        va_ref[0] = vt
        k_bf_ref[...] = ka.astype(BF16)
        vt_bf_ref[0] = vt.astype(BF16)
    else:
        ka_ref[...] = ka
        va_ref[...] = va
    lo = 3 * w_a
    qg_ref[...] = proj(lo, qk_g)
    kg_ref[...] = proj(lo + qk_g, qk_g)
    vg_ref[...] = proj(lo + 2 * qk_g, v_g)
    rg_ref[...] = proj(lo + 2 * qk_g + v_g, v_g)
    lr = jnp.dot(hb, wlr_ref[...], preferred_element_type=F32)
    xg = jnp.dot(lr.astype(BF16), wgg_ref[...], preferred_element_type=F32) + bgg_ref[...]
    log_sig = jnp.minimum(xg, 0.0) - jnp.log1p(jnp.exp(-jnp.abs(xg)))
    la_ref[...] = log_sig * (1.0 / GLA_TAU)
    ga = jnp.dot(hb, wgt_ref[:, :d_model], preferred_element_type=F32)
    sga_ref[...] = _sigmoid(ga).astype(BF16)
    gb = jnp.dot(hb, wgt_ref[:, d_model:], preferred_element_type=F32)
    sgb_ref[...] = _sigmoid(gb).astype(BF16)


def _proj(x, g, wm, wlr, wgt, wgg, bgg, *, w_a, qk_g, v_g, kv_seq_len=None):
    n, d_model = x.shape
    tm = _row_tile(n)
    assert n % tm == 0
    nt = n // tm
    blocks_per_tile = tm // MOBA_BLOCK
    row = lambda w: pl.BlockSpec((tm, w), lambda i: (i, 0))
    full = lambda a: pl.BlockSpec(a.shape, lambda i: (0,) * a.ndim, pipeline_mode=pl.Buffered(1))
    widths = (w_a, w_a, w_a, qk_g, qk_g, v_g, v_g, qk_g)
    out_shape = [jax.ShapeDtypeStruct((n, w), F32) for w in widths]
    out_shape += [jax.ShapeDtypeStruct((n, d_model), BF16)] * 2
    out_shape += [jax.ShapeDtypeStruct((n // MOBA_BLOCK, 1, w_a), F32)]
    out_specs = [row(w) for w in widths] + [row(d_model)] * 2
    out_specs += [pl.BlockSpec((blocks_per_tile, 1, w_a), lambda i: (i, 0, 0))]
    if kv_seq_len is not None:
        assert kv_seq_len % tm == 0
        seq_tiles = kv_seq_len // tm
        transposed = pl.BlockSpec((1, w_a, tm), lambda i: (i // seq_tiles, 0, i % seq_tiles))
        for i in (1, 2):
            out_shape[i] = jax.ShapeDtypeStruct((n // kv_seq_len, w_a, kv_seq_len), F32)
            out_specs[i] = transposed
        out_shape += [jax.ShapeDtypeStruct((n, w_a), BF16),
                      jax.ShapeDtypeStruct((n // kv_seq_len, w_a, kv_seq_len), BF16)]
        out_specs += [row(w_a), transposed]
    return pl.pallas_call(
        functools.partial(_proj_kernel, w_a=w_a, qk_g=qk_g, v_g=v_g, d_model=d_model,
                          kv_transposed=kv_seq_len is not None),
        grid=(nt,),
        in_specs=[row(d_model), full(g), full(wm), full(wlr), full(wgt), full(wgg), full(bgg)],
        out_specs=out_specs,
        out_shape=out_shape,
        compiler_params=_params("parallel"),
        name="proj",
    )(x, g, wm, wlr, wgt, wgg, bgg)


def _top_blocks(gate, candidate, block, axis=-1):
    block_f = block.astype(F32)
    no_block = float(gate.shape[axis])
    avail = jnp.where(candidate, 1.0, 0.0)
    sel = jnp.zeros(gate.shape, F32)
    for _ in range(MOBA_TOPK):
        cur = jnp.where(avail > 0.0, gate, -jnp.inf)
        mx = jnp.max(cur, axis=axis, keepdims=True)
        cand = jnp.where(avail > 0.0, jnp.where(cur == mx, block_f, no_block), no_block)
        first = jnp.min(cand, axis=axis, keepdims=True)
        pick = block_f == first
        sel = jnp.where(pick, 1.0, sel)
        avail = jnp.where(pick, 0.0, avail)
    return sel


def _alibi_slopes(n_head):
    return 2.0 ** (-8.0 * (np.arange(n_head) + 1) / n_head)


_SLOPE_PARTS = 3
_AUG_EXTRA = 2 * _SLOPE_PARTS


def _moba_prompt_step(r, n_sub, hp, qi, sl_ref, q_ref, k_ref, vt_ref, kc_ref, km_ref, o_ref,
                      qaug_ref, m_ref, acc_ref, *, hd, tk):
    blk = MOBA_BLOCK
    t = k_ref.shape[1]
    tq = q_ref.shape[1]
    n_blk = t // blk
    per_pair = LANES // hd
    n_acc = acc_ref.shape[1]
    tiles_per_q = tq // tk
    chains = [(hh, cb) for hh in range(per_pair) for cb in range(tiles_per_q)]
    ones = jnp.ones((n_acc - hd, tk), BF16)

    def tiles(j):
        keys = pl.ds(pl.multiple_of(j * tk, tk), tk)
        kt = jnp.concatenate([k_ref[0, keys, :], kc_ref[keys, :]], axis=1)
        return kt, [jnp.concatenate([vt_ref[0, hh * hd:(hh + 1) * hd, keys], ones], axis=0) for hh in range(per_pair)]

    def logits(kt, hh, cb):
        return jnp.dot(kt, qaug_ref[hh, :, cb * tk:(cb + 1) * tk], preferred_element_type=F32)

    def absorb(stat, s, va, c):
        m, acc = stat
        m_new = jnp.maximum(m, jnp.max(s, axis=0, keepdims=True) + c)
        p = jnp.exp2(s - (m_new - c)).astype(BF16)
        return m_new, jnp.exp2(m - m_new) * acc + jnp.dot(va, p, preferred_element_type=F32)

    @pl.when(r == 0)
    def _():
        lane = lax.broadcasted_iota(jnp.int32, (tq, LANES), 1)
        q = q_ref[0]
        n_rows = -(-n_blk // SUBLANES) * SUBLANES
        block = lax.broadcasted_iota(jnp.int32, (n_rows, tq), 0)
        own = qi * (tq // blk) + lax.broadcasted_iota(jnp.int32, (n_rows, tq), 1) // blk
        extra = lax.broadcasted_iota(jnp.int32, (SUBLANES, tq), 0)
        assert _AUG_EXTRA <= SUBLANES and n_rows + SUBLANES <= LANES
        for hh in range(per_pair):
            h = hp * per_pair + hh
            head_lanes = (lane >= hh * hd) & (lane < (hh + 1) * hd)
            qh = jnp.where(head_lanes, q, 0.0)
            gate = _bdot_nt(km_ref[0, :n_rows, :], qh)
            sel = _top_blocks(gate, block < own, block, axis=0)
            penalty = jnp.where((sel > 0.0) | (block == own) | (block >= n_blk), 0.0, PENALTY)
            parts = [sl_ref[1 + i, h] for i in range(_SLOPE_PARTS)]
            slope_rows = jnp.zeros((SUBLANES, tq), F32)
            for off, val in enumerate(parts + [part * blk for part in parts]):
                slope_rows = jnp.where(extra == off, val, slope_rows)
            qx_t = jnp.concatenate([penalty, slope_rows, jnp.zeros((LANES - n_rows - SUBLANES, tq), F32)], axis=0)
            qaug_ref[hh, :LANES, :] = (qh * (hd ** -0.5 * LOG2E)).T.astype(BF16)
            qaug_ref[hh, LANES:, :] = qx_t.astype(BF16)

        key2 = lax.broadcasted_iota(jnp.int32, (tk, tk), 0)
        qry2 = lax.broadcasted_iota(jnp.int32, (tk, tk), 1)
        stats = {c: (jnp.full((1, tk), NEG_BIG, F32), jnp.zeros((n_acc, tk), F32)) for c in chains}
        for d in range(tiles_per_q):
            kt, va = tiles(qi * tiles_per_q + d)
            live = [(hh, cb) for hh, cb in chains if cb >= d]
            ss = [logits(kt, hh, cb) for hh, cb in live]
            for (hh, cb), s in zip(live, ss):
                if cb == d:
                    s = jnp.where(key2 <= qry2, s, NEG_BIG)
                stats[hh, cb] = absorb(stats[hh, cb], s, va[hh], 0.0)
        for ci, c in enumerate(chains):
            m_ref[ci], acc_ref[ci] = stats[c]

    def absorb_tiles(js, flat):
        loaded = [tiles(j) for j in js]
        ss = [[logits(kt, hh, cb) for hh, cb in chains] for kt, _ in loaded]
        out = list(flat)
        for j, (_, va), s_tile in zip(js, loaded, ss):
            for ci, (hh, cb) in enumerate(chains):
                c = -(sl_ref[0, hp * per_pair + hh] * tq) * (qi - j // tiles_per_q).astype(F32)
                out[2 * ci:2 * ci + 2] = absorb(out[2 * ci:2 * ci + 2], s_tile[ci], va[hh], c)
        return tuple(out)

    n_mine = (qi * tiles_per_q - r + n_sub - 1) // n_sub
    flat = tuple(x for ci in range(len(chains)) for x in (m_ref[ci], acc_ref[ci]))
    flat = lax.fori_loop(0, n_mine // 2, lambda i, f: absorb_tiles([r + 2 * i * n_sub, r + (2 * i + 1) * n_sub], f), flat)
    flat = lax.fori_loop(0, n_mine % 2, lambda i, f: absorb_tiles([r + (n_mine - 1) * n_sub], f), flat)
    for ci in range(len(chains)):
        m_ref[ci], acc_ref[ci] = flat[2 * ci], flat[2 * ci + 1]

    @pl.when(r == n_sub - 1)
    def _():
        for cb in range(tiles_per_q):
            accs = [flat[2 * chains.index((hh, cb)) + 1] for hh in range(per_pair)]
            out = jnp.concatenate([acc[:hd] / acc[hd:hd + 1] for acc in accs], axis=0)
            o_ref[0, cb * tk:(cb + 1) * tk, :] = out.T


def _moba_consts(t, tq):
    n_blk = t // MOBA_BLOCK
    base = -(-n_blk // SUBLANES) * SUBLANES
    assert base + _AUG_EXTRA <= LANES
    pos = np.arange(t)
    c = np.zeros((t, LANES), np.float32)
    c[pos, pos // MOBA_BLOCK] = 1.0
    for i in range(_SLOPE_PARTS):
        c[:, base + i] = pos % MOBA_BLOCK
        c[:, base + _SLOPE_PARTS + i] = (pos % tq) // MOBA_BLOCK
    return jnp.asarray(c, dtype=BF16)


_BF16_ROWS = 2 * SUBLANES


def _moba_sample_ops(slopes_ref, qbd_ref, s_ref, p_ref, ksum_ref, l_ref, acc_ref, *,
                     ts, w, page, n_per_step, pages_per_block, hd):
    qk_pages, pv_pages = 4, 2
    n_head = w // hd
    rows = n_head * ts
    blk = page * pages_per_block
    past = s_ref.shape[1]
    n_blk = past // blk
    scale = hd ** -0.5

    row_head = lax.broadcasted_iota(jnp.int32, (rows, 1), 0) // ts
    row_q = lax.broadcasted_iota(jnp.int32, (rows, 1), 0) % ts
    col_head = lax.broadcasted_iota(jnp.int32, (rows, w), 1) // hd
    slope = jnp.zeros((rows, 1), F32)
    for h in range(n_head):
        slope = jnp.where(row_head == h, slopes_ref[h], slope)

    def new_sequence(q):
        qt = jnp.concatenate([q] * n_head, axis=0)
        qbd_ref[...] = jnp.where(col_head == row_head, qt, 0.0)
        ksum_ref[...] = jnp.zeros(ksum_ref.shape, F32)

    def k_group(pages_ref, j):
        qs = (qbd_ref[...] * scale).astype(BF16)
        lane = lax.broadcasted_iota(jnp.int32, ksum_ref.shape, 1)
        ksum = ksum_ref[...]
        group = qk_pages if n_per_step % qk_pages == 0 and qk_pages % pages_per_block == 0 else pages_per_block
        for g0 in range(0, n_per_step, group):
            kts = [pages_ref[p_i].reshape(w, page) for p_i in range(g0, g0 + group)]
            first = j * n_per_step + g0
            s = jnp.dot(qs, jnp.concatenate([kt.astype(BF16) for kt in kts], axis=1), preferred_element_type=F32)
            s_ref[:, pl.ds(pl.multiple_of(first * page, group * page), group * page)] = s
            for b0 in range(0, group, pages_per_block):
                total = kts[b0]
                for kt in kts[b0 + 1:b0 + pages_per_block]:
                    total = total + kt
                ksum = jnp.where(lane == (first + b0) // pages_per_block, jnp.sum(total, axis=1, keepdims=True), ksum)
        ksum_ref[...] = ksum

    def boundary(kn, vn):
        qbd = qbd_ref[...]
        gate = _bdot(qbd, ksum_ref[...])
        lane = lax.broadcasted_iota(jnp.int32, gate.shape, 1)
        sel = _top_blocks(gate, lane < n_blk, lane)
        s_own = _bdot_nt(qbd * scale, kn)
        kq = lax.broadcasted_iota(jnp.int32, s_own.shape, 1)
        s_own = jnp.where(kq <= row_q, s_own + slope * kq.astype(F32), NEG_BIG)
        key_in_blk = lax.broadcasted_iota(jnp.int32, (1, blk), 1)

        def add_bias(b_i, m_wide):
            cols = pl.ds(pl.multiple_of(b_i * blk, blk), blk)
            picked = jnp.max(jnp.where(lane == b_i, sel, 0.0), axis=-1, keepdims=True)
            rel = (b_i * blk - past + key_in_blk).astype(F32)
            s = s_ref[:, cols] + slope * rel + jnp.where(picked > 0.0, 0.0, NEG_BIG)
            s_ref[:, cols] = s
            return jnp.maximum(m_wide, s)

        unroll = 8 if n_blk % 8 == 0 else 1
        m_wide = lax.fori_loop(0, n_blk, add_bias, jnp.full((rows, blk), NEG_BIG, F32), unroll=unroll)
        m = jnp.maximum(jnp.max(m_wide, axis=-1, keepdims=True), jnp.max(s_own, axis=-1, keepdims=True))

        def probs(b_i, l_wide):
            cols = pl.ds(pl.multiple_of(b_i * blk, blk), blk)
            p = jnp.exp(s_ref[:, cols] - m)
            p_ref[:, cols] = p.astype(BF16)
            return l_wide + p

        l_wide = lax.fori_loop(0, n_blk, probs, jnp.zeros((rows, blk), F32), unroll=unroll)
        p_own = jnp.exp(s_own - m)
        l_ref[...] = jnp.sum(l_wide, axis=-1, keepdims=True) + jnp.sum(p_own, axis=-1, keepdims=True)
        acc_ref[...] = _bdot(p_own, vn)

    def v_group(pages_ref, j):
        acc = acc_ref[...]
        group = pv_pages if n_per_step % pv_pages == 0 else 1
        for g0 in range(0, n_per_step, group):
            first = j * n_per_step + g0
            p = p_ref[:, pl.ds(pl.multiple_of(first * page, group * page), group * page)]
            vt = jnp.concatenate([pages_ref[p_i].reshape(w, page).astype(BF16) for p_i in range(g0, g0 + group)], axis=1)
            acc = acc + lax.dot_general(p, vt, _NT, preferred_element_type=F32)
        acc_ref[...] = acc

    def result():
        res = jnp.where(col_head == row_head, acc_ref[...] / l_ref[...], 0.0)
        out = res[0:ts]
        for h in range(1, n_head):
            out = out + res[h * ts:(h + 1) * ts]
        return out

    return new_sequence, k_group, boundary, v_group, result


def _moba_kernel(pt_ref, sl_ref, slopes_ref, q_ref, k_ref, vt_ref, kc_ref, km_ref, qs_ref, kn_ref, vn_ref,
                 cache_k_hbm, cache_v_hbm, o_ref, os_ref, qaug_ref, m_ref, acc_ref, pages_ref, sem, *sample_scratch,
                 pages_per_block, hd, tk, n_k_steps):
    hp, qi, r = pl.program_id(1), pl.program_id(2), pl.program_id(3)
    n_sub = pl.num_programs(3)
    u = ((pl.program_id(0) * pl.num_programs(1) + hp) * pl.num_programs(2) + qi) * n_sub + r
    n_steps = pl.num_programs(0) * pl.num_programs(1) * pl.num_programs(2) * n_sub
    n_seq, ts, w = qs_ref.shape
    _, n_per_step, _, _, page = pages_ref.shape
    new_sequence, k_group, boundary, v_group, result = _moba_sample_ops(
        slopes_ref, *sample_scratch, ts=ts, w=w, page=page, n_per_step=n_per_step,
        pages_per_block=pages_per_block, hd=hd)

    def start_pages(cache_hbm, seq, group, slot):
        for p in range(n_per_step):
            pltpu.make_async_copy(cache_hbm.at[pt_ref[seq, group * n_per_step + p]],
                                  pages_ref.at[slot, p], sem.at[slot]).start()

    def wait_pages(slot):
        pltpu.make_async_copy(cache_k_hbm.at[pl.ds(0, n_per_step)], pages_ref.at[slot], sem.at[slot]).wait()

    v_slot = 2

    def start_k(step):
        start_pages(cache_k_hbm, step // n_k_steps, step % n_k_steps, step % 2)

    def start_v(step):
        start_pages(cache_v_hbm, jnp.maximum(step // n_k_steps - 1, 0), step % n_k_steps, v_slot)

    @pl.when(u == 0)
    def _():
        start_k(u)
        start_v(u)
        p_ref, l_ref, sacc_ref = sample_scratch[2], sample_scratch[4], sample_scratch[5]
        p_ref[...] = jnp.zeros(p_ref.shape, p_ref.dtype)
        l_ref[...] = jnp.ones(l_ref.shape, F32)
        sacc_ref[...] = jnp.zeros(sacc_ref.shape, F32)

    @pl.when(u + 1 < n_steps)
    def _():
        start_k(u + 1)

    _moba_prompt_step(r, n_sub, hp, qi, sl_ref, q_ref, k_ref, vt_ref, kc_ref, km_ref, o_ref,
                      qaug_ref, m_ref, acc_ref, hd=hd, tk=tk)

    seq, group = u // n_k_steps, u % n_k_steps
    k_slot = u % 2

    @pl.when((group == 0) & (u > 0))
    def _():
        boundary(kn_ref[seq - 1], vn_ref[seq - 1])

    @pl.when(group == 0)
    def _():
        new_sequence(qs_ref[seq])

    wait_pages(k_slot)
    wait_pages(v_slot)
    k_group(pages_ref.at[k_slot], group)
    v_group(pages_ref.at[v_slot], group)

    @pl.when((group == n_k_steps - 1) & (seq > 0))
    def _():
        os_ref[seq - 1] = result()

    @pl.when(u + 1 < n_steps)
    def _():
        start_v(u + 1)

    @pl.when(u == n_steps - 1)
    def _():
        boundary(kn_ref[n_seq - 1], vn_ref[n_seq - 1])
        for g in range(n_k_steps):
            slot = g % 2
            if g == 0:
                start_pages(cache_v_hbm, n_seq - 1, 0, slot)
            if g + 1 < n_k_steps:
                start_pages(cache_v_hbm, n_seq - 1, g + 1, 1 - slot)
            wait_pages(slot)
            v_group(pages_ref.at[slot], g)
        os_ref[n_seq - 1] = result()


def _moba(qa, k_bf, vt_bf, kmean, qa_s, ka_s, va_s, cache_kt, cache_vt, page_table, *, hd):
    b, t, w = qa.shape
    bs, ts, _ = qa_s.shape
    _, n_head, _, page = cache_kt.shape
    n_pages = page_table.shape[1]
    blk = MOBA_BLOCK
    ppb = blk // page
    n_pair = w // LANES
    n_blk = t // blk
    tk = 2 * blk if t % (2 * blk) == 0 else blk
    tq = 2 * tk if t % (2 * tk) == 0 else tk
    n_q = t // tq
    n_per_step = next(p for p in (32, 16, 8, 4, 2) if n_pages % p == 0 and p % ppb == 0
                      and (bs * (n_pages // p)) % (b * n_pair * n_q) == 0)
    n_k_steps = n_pages // n_per_step
    n_sub = bs * n_k_steps // (b * n_pair * n_q)
    assert n_pages // ppb <= LANES
    rows = n_head * ts

    sl2 = _alibi_slopes(n_head) * LOG2E
    parts, rest = [], sl2
    for _ in range(_SLOPE_PARTS):
        parts.append(rest.astype(ml_dtypes.bfloat16).astype(np.float64))
        rest = rest - parts[-1]
    sl = jnp.asarray(np.stack([sl2] + parts), dtype=F32)
    slopes = jnp.asarray(_alibi_slopes(n_head), dtype=F32)
    kc = _moba_consts(t, tq)
    km = jnp.pad(kmean, ((0, 0), (0, LANES - n_blk), (0, 0)))

    def whole(a):
        return pl.BlockSpec(a.shape, lambda bi, hp, qi, r, pt: (0,) * a.ndim)

    smem = pl.BlockSpec(memory_space=pltpu.SMEM)
    hbm = pl.BlockSpec(memory_space=pl.ANY)
    n_chain = (LANES // hd) * (tq // tk)
    return pl.pallas_call(
        functools.partial(_moba_kernel, pages_per_block=ppb, hd=hd, tk=tk, n_k_steps=n_k_steps),
        grid_spec=pltpu.PrefetchScalarGridSpec(
            num_scalar_prefetch=1,
            grid=(b, n_pair, n_q, n_sub),
            in_specs=[smem, smem,
                      pl.BlockSpec((1, tq, LANES), lambda bi, hp, qi, r, pt: (bi, qi, hp)),
                      pl.BlockSpec((1, t, LANES), lambda bi, hp, qi, r, pt: (bi, 0, hp), pipeline_mode=pl.Buffered(1)),
                      pl.BlockSpec((1, LANES, t), lambda bi, hp, qi, r, pt: (bi, hp, 0), pipeline_mode=pl.Buffered(1)),
                      pl.BlockSpec((t, LANES), lambda bi, hp, qi, r, pt: (0, 0), pipeline_mode=pl.Buffered(1)),
                      pl.BlockSpec((1, LANES, LANES), lambda bi, hp, qi, r, pt: (bi, 0, hp)),
                      whole(qa_s), whole(ka_s), whole(va_s), hbm, hbm],
            out_specs=[pl.BlockSpec((1, tq, LANES), lambda bi, hp, qi, r, pt: (bi, qi, hp)), whole(qa_s)],
            scratch_shapes=[pltpu.VMEM((LANES // hd, 2 * LANES, tq), BF16),
                            pltpu.VMEM((n_chain, 1, tk), F32),
                            pltpu.VMEM((n_chain, hd + _BF16_ROWS, tk), F32),
                            pltpu.VMEM((3, n_per_step, n_head, hd, page), F32),
                            pltpu.SemaphoreType.DMA((3,)),
                            pltpu.VMEM((rows, w), F32),
                            pltpu.VMEM((rows, n_pages * page), F32),
                            pltpu.VMEM((rows, n_pages * page), BF16),
                            pltpu.VMEM((w, LANES), F32),
                            pltpu.VMEM((rows, 1), F32), pltpu.VMEM((rows, w), F32)],
        ),
        out_shape=[jax.ShapeDtypeStruct((b, t, w), F32), jax.ShapeDtypeStruct((bs, ts, w), F32)],
        compiler_params=_params("arbitrary", "arbitrary", "arbitrary", "arbitrary"),
        name="moba",
    )(page_table, sl, slopes, qa, k_bf, vt_bf, kc, km, qa_s, ka_s, va_s, cache_kt, cache_vt)


def _gla_kernel(q_ref, k_ref, v_ref, la_ref, rg_ref, gn_ref, s0_ref, o_ref, sout_ref, st_ref, *, dk, dv, chunk):
    c_idx = pl.program_id(1)
    n_chunks = pl.num_programs(1)
    c = chunk
    n_pair = q_ref.shape[2] // LANES
    per_pair = LANES // dk
    assert dv == LANES and c % SUBLANES == 0

    @pl.when(c_idx == 0)
    def _():
        for p in range(n_pair):
            st_ref[p] = s0_ref[0, p].T

    row = lax.broadcasted_iota(jnp.int32, (c, LANES), 0)
    lane = lax.broadcasted_iota(jnp.int32, (c, LANES), 1)
    row2 = lax.broadcasted_iota(jnp.int32, (c, c), 0)
    col2 = lax.broadcasted_iota(jnp.int32, (c, c), 1)
    tri = jnp.where(row2 >= col2, 1.0, 0.0)
    nb8 = c // SUBLANES
    sub = lax.broadcasted_iota(jnp.int32, (nb8, SUBLANES, LANES), 1)
    lane3 = lax.broadcasted_iota(jnp.int32, (nb8, SUBLANES, LANES), 2)
    lane_st = lax.broadcasted_iota(jnp.int32, (dv, LANES), 1)

    def bcast_row(x3, jj):
        return jnp.broadcast_to(x3[:, jj:jj + 1, :], x3.shape)

    def one_chunk(p, rows, b_all, st):
        cols = slice(p * LANES, (p + 1) * LANES)
        q = q_ref[0, rows, cols] * (dk ** -0.5)
        k = k_ref[0, rows, cols]
        b = b_all[:, cols]
        vs = [v_ref[0, rows, (p * per_pair + hh) * dv:(p * per_pair + hh + 1) * dv] for hh in range(per_pair)]
        head_lanes = [(lane >= hh * dk) & (lane < (hh + 1) * dk) for hh in range(per_pair)]
        head_lanes3 = [(lane3 >= hh * dk) & (lane3 < (hh + 1) * dk) for hh in range(per_pair)]

        qe = q * jnp.exp(b)
        o = [_bdot_nt(jnp.where(head_lanes[hh], qe, 0.0), st) for hh in range(per_pair)]

        q3 = q.reshape(nb8, SUBLANES, LANES)
        k3 = k.reshape(nb8, SUBLANES, LANES)
        b3 = b.reshape(nb8, SUBLANES, LANES)
        v3 = [v.reshape(nb8, SUBLANES, dv) for v in vs]
        o3 = [jnp.zeros((nb8, SUBLANES, dv), F32) for _ in range(per_pair)]
        for jj in range(SUBLANES):
            e = jnp.exp(jnp.minimum(b3 - bcast_row(b3, jj), 0.0))
            term = jnp.where(sub >= jj, q3 * bcast_row(k3, jj) * e, 0.0)
            for hh in range(per_pair):
                a = jnp.sum(jnp.where(head_lanes3[hh], term, 0.0), axis=-1, keepdims=True)
                o3[hh] = o3[hh] + a * bcast_row(v3[hh], jj)
        o = [o[hh] + o3[hh].reshape(c, dv) for hh in range(per_pair)]

        attn = [jnp.zeros((c, c), F32) for _ in range(per_pair)]
        m_half = SUBLANES
        while 2 * m_half <= c:
            span = 2 * m_half
            bnd = jnp.broadcast_to(b.reshape(c // span, span, LANES)[:, m_half - 1:m_half, :],
                                   (c // span, span, LANES)).reshape(c, LANES)
            upper = (row % span) >= m_half
            qm = jnp.where(upper, q * jnp.exp(jnp.minimum(b - bnd, 0.0)), 0.0)
            km = jnp.where(upper, 0.0, k * jnp.exp(jnp.minimum(bnd - b, 0.0)))
            same = (row2 // span) == (col2 // span)
            for hh in range(per_pair):
                a = _bdot_nt(jnp.where(head_lanes[hh], qm, 0.0), km)
                attn[hh] = attn[hh] + jnp.where(same, a, 0.0)
            m_half = span
        if c > SUBLANES:
            o = [o[hh] + _bdot(attn[hh], vs[hh]) for hh in range(per_pair)]

        b_last = b[c - 1:c, :]
        kk = (k * jnp.exp(b_last - b)).astype(BF16)
        upd = lax.dot_general(vs[0].astype(BF16), kk, _TN, preferred_element_type=F32)
        for hh in range(1, per_pair):
            u = lax.dot_general(vs[hh].astype(BF16), kk, _TN, preferred_element_type=F32)
            upd = jnp.where(lane_st >= hh * dk, u, upd)
        for hh in range(per_pair):
            h = p * per_pair + hh
            rg = rg_ref[0, rows, h * dv:(h + 1) * dv]
            o_ref[0, rows, h * dv:(h + 1) * dv] = _rms(o[hh], gn_ref[...]) * (rg * _sigmoid(rg))
        return st * jnp.exp(b_last) + upd

    states = [st_ref[p] for p in range(n_pair)]
    for ch in range(q_ref.shape[1] // c):
        rows = slice(ch * c, (ch + 1) * c)
        b_all = jnp.dot(tri, la_ref[0, rows, :], precision=lax.Precision.HIGHEST, preferred_element_type=F32)
        states = [one_chunk(p, rows, b_all, states[p]) for p in range(n_pair)]
    for p in range(n_pair):
        st_ref[p] = states[p]

    @pl.when(c_idx == n_chunks - 1)
    def _():
        for p in range(n_pair):
            sout_ref[0, p] = states[p].T


def _gla(qg, kg, vg, la, rg, g_norm, s0, *, chunk, dk, dv):
    b, t, wk = qg.shape
    wv = vg.shape[2]
    n_head = wk // dk
    n_pair = wk // LANES
    s0p = s0.reshape(b, n_pair, LANES, dv)
    step = next(n * chunk for n in (4, 2, 1) if t % (n * chunk) == 0)
    tok = lambda w: pl.BlockSpec((1, step, w), lambda bi, ci: (bi, ci, 0))
    st_spec = pl.BlockSpec((1, n_pair, LANES, dv), lambda bi, ci: (bi, 0, 0, 0))
    og, s_new = pl.pallas_call(
        functools.partial(_gla_kernel, dk=dk, dv=dv, chunk=chunk),
        grid=(b, t // step),
        in_specs=[tok(wk), tok(wk), tok(wv), tok(wk), tok(wv),
                  pl.BlockSpec(g_norm.shape, lambda bi, ci: (0, 0)), st_spec],
        out_specs=[tok(wv), st_spec],
        out_shape=[jax.ShapeDtypeStruct((b, t, wv), F32), jax.ShapeDtypeStruct((b, n_pair, LANES, dv), F32)],
        scratch_shapes=[pltpu.VMEM((n_pair, dv, LANES), F32)],
        compiler_params=_params("parallel", "arbitrary"),
        name="gla",
    )(qg, kg, vg, la, rg, g_norm, s0p)
    return og, s_new.reshape(b, n_head, dk, dv)


def _mix_kernel(x_ref, oa_ref, og_ref, sga_ref, sgb_ref, wa_ref, wb_ref, wo_ref, wu_ref, wd_ref,
                g_mix_ref, g_pre_ref, g_post_ref, y_ref):
    merged = (sga_ref[...].astype(F32) * _bdot(oa_ref[...], wa_ref[...])
              + sgb_ref[...].astype(F32) * _bdot(og_ref[...], wb_ref[...]))
    x = x_ref[...] + _rms(_bdot(merged, wo_ref[...]), g_mix_ref[...])
    u = _bdot(_rms(x, g_pre_ref[...]), wu_ref[...])
    u = jnp.square(jnp.maximum(u, 0.0))
    y_ref[...] = x + _rms(_bdot(u, wd_ref[...]), g_post_ref[...])


def _row_tile(n):
    return 2 * MOBA_BLOCK if n % (2 * MOBA_BLOCK) == 0 else MOBA_BLOCK


def _rowwise_call(kernel, name, row_inputs, const_inputs, out_width):
    n = row_inputs[0].shape[0]
    tm = _row_tile(n)
    assert n % tm == 0
    row = lambda a: pl.BlockSpec((tm, a.shape[1]), lambda i: (i, 0))
    full = lambda a: pl.BlockSpec(a.shape, lambda i: (0,) * a.ndim, pipeline_mode=pl.Buffered(1))
    return pl.pallas_call(
        kernel,
        grid=(n // tm,),
        in_specs=[row(a) for a in row_inputs] + [full(a) for a in const_inputs],
        out_specs=pl.BlockSpec((tm, out_width), lambda i: (i, 0)),
        out_shape=jax.ShapeDtypeStruct((n, out_width), F32),
        compiler_params=_params("parallel"),
        name=name,
    )(*row_inputs, *const_inputs)


def _gla_chunk(t):
    c = SUBLANES
    while c * 2 <= min(t, LANES) and t % (c * 2) == 0:
        c *= 2
    return c


def kernel(x_prompt, x_sample, cache_k, cache_v, page_table, state_gla, w_in, w_gla_gate, b_gla_gate, g_gla_norm,
           w_branch_a, w_branch_b, w_out, w_up, w_down, g_pre_mix, g_post_mix, g_pre_mlp, g_post_mlp):
    bp, tp, d_model = x_prompt.shape
    bs, ts, _ = x_sample.shape
    depth, n_phys, page, n_head, hd = cache_k.shape
    _, _, n_head_g, dk, dv = state_gla.shape
    w_a = n_head * hd
    qk_g = n_head_g * dk
    v_g = n_head_g * dv
    rank = w_gla_gate.shape[1]
    n_main = 3 * w_a + 2 * qk_g + 2 * v_g

    hp = x_prompt.reshape(bp * tp, d_model)
    hs = x_sample.reshape(bs * ts, d_model)
    outs = [[] for _ in range(6)]
    for l in range(depth):
        wm = w_in[l, :, :n_main].astype(BF16)
        wlr = jnp.pad(w_in[l, :, n_main:n_main + rank], ((0, 0), (0, LANES - rank))).astype(BF16)
        wgt = w_in[l, :, n_main + rank:].astype(BF16)
        wgg = jnp.pad(w_gla_gate[l], ((0, LANES - rank), (0, 0))).astype(BF16)
        bgg = b_gla_gate[l][None, :]
        wa, wb, wo = (w[l].astype(BF16) for w in (w_branch_a, w_branch_b, w_out))
        wu, wd = w_up[l].astype(BF16), w_down[l].astype(BF16)
        g_mix, g_pm, g_mlp, g_pl, g_gn = (g[l][None, :] for g in
                                          (g_pre_mix, g_post_mix, g_pre_mlp, g_post_mlp, g_gla_norm))

        def project(x, kv_seq_len):
            return _proj(x, g_mix, wm, wlr, wgt, wgg, bgg, w_a=w_a, qk_g=qk_g, v_g=v_g, kv_seq_len=kv_seq_len)

        def mix(x, b, t, oa, proj_out, s0):
            qg, kg, vg, rg, la, sga, sgb = proj_out[3:10]
            r3 = lambda a: a.reshape(b, t, a.shape[-1])
            og, s_new = _gla(r3(qg), r3(kg), r3(vg), r3(la), r3(rg), g_gn, s0, chunk=_gla_chunk(t), dk=dk, dv=dv)
            n = b * t
            x = _rowwise_call(_mix_kernel, "mix", [x, oa.reshape(n, w_a), og.reshape(n, v_g), sga, sgb],
                              [wa, wb, wo, wu, wd, g_pm, g_mlp, g_pl], d_model)
            return x, s_new

        pp = project(hp, tp)
        ps = project(hs, None)
        qa_p, kt_p, vt_p, kmean_p, k_bf, vt_bf = pp[0], pp[1], pp[2], pp[10], pp[11], pp[12]
        qa_s, ka_s, va_s = (a.reshape(bs, ts, w_a) for a in ps[:3])
        to_stored = lambda c: jnp.transpose(c[l], (0, 2, 3, 1))
        oa_p, oa_s = _moba(qa_p.reshape(bp, tp, w_a), k_bf.reshape(bp, tp, w_a), vt_bf,
                           kmean_p.reshape(bp, tp // MOBA_BLOCK, w_a), qa_s, ka_s, va_s,
                           to_stored(cache_k), to_stored(cache_v), page_table, hd=hd)
        hp, sp = mix(hp, bp, tp, oa_p, pp, jnp.zeros((bp, n_head_g, dk, dv), state_gla.dtype))
        hs, ssn = mix(hs, bs, ts, oa_s, ps, state_gla[l])
        kp, vp = (jnp.transpose(a.reshape(bp, n_head, hd, tp), (0, 3, 1, 2)) for a in (kt_p, vt_p))
        ksn, vsn = (a.reshape(bs, ts, n_head, hd) for a in (ka_s, va_s))
        for lst, val in zip(outs, (kp, vp, sp, ksn, vsn, ssn)):
            lst.append(val)
    return (hp.reshape(bp, tp, d_model), hs.reshape(bs, ts, d_model)) + tuple(jnp.stack(o) for o in outs)
```

```python
import functools

import ml_dtypes
import numpy as np
import jax
import jax.numpy as jnp
from jax import lax
from jax.experimental import pallas as pl
from jax.experimental.pallas import tpu as pltpu

F32 = jnp.float32
BF16 = jnp.bfloat16

LANES = 128
SUBLANES = 8
VMEM_LIMIT_BYTES = 56 * 1024 * 1024

EPS = 1e-6
MOBA_BLOCK = 256
MOBA_TOPK = 3
GLA_TAU = 16.0
NEG_BIG = -1e30
PENALTY = -30000.0
LOG2E = float(np.log2(np.e))

_NT = (((1,), (1,)), ((), ()))
_TN = (((0,), (0,)), ((), ()))


def _params(*sem, vmem_limit_bytes=VMEM_LIMIT_BYTES):
    return pltpu.CompilerParams(dimension_semantics=sem, vmem_limit_bytes=vmem_limit_bytes)


def _sigmoid(x):
    return 1.0 / (1.0 + jnp.exp(-x))


def _rms(x, g):
    return x * lax.rsqrt(jnp.mean(x * x, axis=-1, keepdims=True) + EPS) * g


def _bdot(a, b):
    return jnp.dot(a.astype(BF16), b.astype(BF16), preferred_element_type=F32)


def _bdot_nt(a, b):
    return lax.dot_general(a.astype(BF16), b.astype(BF16), _NT, preferred_element_type=F32)


def _proj_kernel(x_ref, g_ref, wm_ref, wlr_ref, wgt_ref, wgg_ref, bgg_ref,
                 qa_ref, ka_ref, va_ref, qg_ref, kg_ref, vg_ref, rg_ref, la_ref, sga_ref, sgb_ref, kmean_ref,
                 *attn_refs, w_a, qk_g, v_g, d_model, kv_transposed):
    hb = _rms(x_ref[...], g_ref[...]).astype(BF16)

    def proj(lo, n):
        return jnp.dot(hb, wm_ref[:, lo:lo + n], preferred_element_type=F32)

    qa_ref[...] = proj(0, w_a)
    ka = proj(w_a, w_a)
    va = proj(2 * w_a, w_a)
    for i in range(kmean_ref.shape[0]):
        kmean_ref[i] = jnp.sum(ka[i * MOBA_BLOCK:(i + 1) * MOBA_BLOCK], axis=0, keepdims=True) * (1.0 / MOBA_BLOCK)
    if kv_transposed:
        k_bf_ref, vt_bf_ref = attn_refs
        vt = va.T
        ka_ref[0] = ka.T
        va_ref[0] = vt
        k_bf_ref[...] = ka.astype(BF16)
        vt_bf_ref[0] = vt.astype(BF16)
    else:
        ka_ref[...] = ka
        va_ref[...] = va
    lo = 3 * w_a
    qg_ref[...] = proj(lo, qk_g)
    kg_ref[...] = proj(lo + qk_g, qk_g)
    vg_ref[...] = proj(lo + 2 * qk_g, v_g)
    rg_ref[...] = proj(lo + 2 * qk_g + v_g, v_g)
    lr = jnp.dot(hb, wlr_ref[...], preferred_element_type=F32)
    xg = jnp.dot(lr.astype(BF16), wgg_ref[...], preferred_element_type=F32) + bgg_ref[...]
    log_sig = jnp.minimum(xg, 0.0) - jnp.log1p(jnp.exp(-jnp.abs(xg)))
    la_ref[...] = log_sig * (1.0 / GLA_TAU)
    ga = jnp.dot(hb, wgt_ref[:, :d_model], preferred_element_type=F32)
    sga_ref[...] = _sigmoid(ga).astype(BF16)
    gb = jnp.dot(hb, wgt_ref[:, d_model:], preferred_element_type=F32)
    sgb_ref[...] = _sigmoid(gb).astype(BF16)


def _proj(x, g, wm, wlr, wgt, wgg, bgg, *, w_a, qk_g, v_g, kv_seq_len=None):
    n, d_model = x.shape
    tm = _row_tile(n)
    assert n % tm == 0
    nt = n // tm
    blocks_per_tile = tm // MOBA_BLOCK
    row = lambda w: pl.BlockSpec((tm, w), lambda i: (i, 0))
    full = lambda a: pl.BlockSpec(a.shape, lambda i: (0,) * a.ndim, pipeline_mode=pl.Buffered(1))
    widths = (w_a, w_a, w_a, qk_g, qk_g, v_g, v_g, qk_g)
    out_shape = [jax.ShapeDtypeStruct((n, w), F32) for w in widths]
    out_shape += [jax.ShapeDtypeStruct((n, d_model), BF16)] * 2
    out_shape += [jax.ShapeDtypeStruct((n // MOBA_BLOCK, 1, w_a), F32)]
    out_specs = [row(w) for w in widths] + [row(d_model)] * 2
    out_specs += [pl.BlockSpec((blocks_per_tile, 1, w_a), lambda i: (i, 0, 0))]
    if kv_seq_len is not None:
        assert kv_seq_len % tm == 0
        seq_tiles = kv_seq_len // tm
        transposed = pl.BlockSpec((1, w_a, tm), lambda i: (i // seq_tiles, 0, i % seq_tiles))
        for i in (1, 2):
            out_shape[i] = jax.ShapeDtypeStruct((n // kv_seq_len, w_a, kv_seq_len), F32)
            out_specs[i] = transposed
        out_shape += [jax.ShapeDtypeStruct((n, w_a), BF16),
                      jax.ShapeDtypeStruct((n // kv_seq_len, w_a, kv_seq_len), BF16)]
        out_specs += [row(w_a), transposed]
    return pl.pallas_call(
        functools.partial(_proj_kernel, w_a=w_a, qk_g=qk_g, v_g=v_g, d_model=d_model,
                          kv_transposed=kv_seq_len is not None),
        grid=(nt,),
        in_specs=[row(d_model), full(g), full(wm), full(wlr), full(wgt), full(wgg), full(bgg)],
        out_specs=out_specs,
        out_shape=out_shape,
        compiler_params=_params("parallel"),
        name="proj",
    )(x, g, wm, wlr, wgt, wgg, bgg)


def _top_blocks(gate, candidate, block, axis=-1):
    block_f = block.astype(F32)
    no_block = float(gate.shape[axis])
    avail = jnp.where(candidate, 1.0, 0.0)
    sel = jnp.zeros(gate.shape, F32)
    for _ in range(MOBA_TOPK):
        cur = jnp.where(avail > 0.0, gate, -jnp.inf)
        mx = jnp.max(cur, axis=axis, keepdims=True)
        cand = jnp.where(avail > 0.0, jnp.where(cur == mx, block_f, no_block), no_block)
        first = jnp.min(cand, axis=axis, keepdims=True)
        pick = block_f == first
        sel = jnp.where(pick, 1.0, sel)
        avail = jnp.where(pick, 0.0, avail)
    return sel


def _alibi_slopes(n_head):
    return 2.0 ** (-8.0 * (np.arange(n_head) + 1) / n_head)


_SLOPE_PARTS = 3
_AUG_EXTRA = 2 * _SLOPE_PARTS


def _moba_prompt_step(r, n_sub, hp, qi, sl_ref, q_ref, k_ref, vt_ref, kc_ref, km_ref, o_ref,
                      qaug_ref, m_ref, acc_ref, *, hd, tk):
    blk = MOBA_BLOCK
    t = k_ref.shape[1]
    tq = q_ref.shape[1]
    n_blk = t // blk
    per_pair = LANES // hd
    n_acc = acc_ref.shape[1]
    tiles_per_q = tq // tk
    chains = [(hh, cb) for hh in range(per_pair) for cb in range(tiles_per_q)]
    ones = jnp.ones((n_acc - hd, tk), BF16)

    def tiles(j):
        keys = pl.ds(pl.multiple_of(j * tk, tk), tk)
        kt = jnp.concatenate([k_ref[0, keys, :], kc_ref[keys, :]], axis=1)
        return kt, [jnp.concatenate([vt_ref[0, hh * hd:(hh + 1) * hd, keys], ones], axis=0) for hh in range(per_pair)]

    def logits(kt, hh, cb):
        return jnp.dot(kt, qaug_ref[hh, :, cb * tk:(cb + 1) * tk], preferred_element_type=F32)

    def absorb(stat, s, va, c):
        m, acc = stat
        m_new = jnp.maximum(m, jnp.max(s, axis=0, keepdims=True) + c)
        p = jnp.exp2(s - (m_new - c)).astype(BF16)
        return m_new, jnp.exp2(m - m_new) * acc + jnp.dot(va, p, preferred_element_type=F32)

    @pl.when(r == 0)
    def _():
        lane = lax.broadcasted_iota(jnp.int32, (tq, LANES), 1)
        q = q_ref[0]
        n_rows = -(-n_blk // SUBLANES) * SUBLANES
        block = lax.broadcasted_iota(jnp.int32, (n_rows, tq), 0)
        own = qi * (tq // blk) + lax.broadcasted_iota(jnp.int32, (n_rows, tq), 1) // blk
        extra = lax.broadcasted_iota(jnp.int32, (SUBLANES, tq), 0)
        assert _AUG_EXTRA <= SUBLANES and n_rows + SUBLANES <= LANES
        for hh in range(per_pair):
            h = hp * per_pair + hh
            head_lanes = (lane >= hh * hd) & (lane < (hh + 1) * hd)
            qh = jnp.where(head_lanes, q, 0.0)
            gate = _bdot_nt(km_ref[0, :n_rows, :], qh)
            sel = _top_blocks(gate, block < own, block, axis=0)
            penalty = jnp.where((sel > 0.0) | (block == own) | (block >= n_blk), 0.0, PENALTY)
            parts = [sl_ref[1 + i, h] for i in range(_SLOPE_PARTS)]
            slope_rows = jnp.zeros((SUBLANES, tq), F32)
            for off, val in enumerate(parts + [part * blk for part in parts]):
                slope_rows = jnp.where(extra == off, val, slope_rows)
            qx_t = jnp.concatenate([penalty, slope_rows, jnp.zeros((LANES - n_rows - SUBLANES, tq), F32)], axis=0)
            qaug_ref[hh, :LANES, :] = (qh * (hd ** -0.5 * LOG2E)).T.astype(BF16)
            qaug_ref[hh, LANES:, :] = qx_t.astype(BF16)

        key2 = lax.broadcasted_iota(jnp.int32, (tk, tk), 0)
        qry2 = lax.broadcasted_iota(jnp.int32, (tk, tk), 1)
        stats = {c: (jnp.full((1, tk), NEG_BIG, F32), jnp.zeros((n_acc, tk), F32)) for c in chains}
        for d in range(tiles_per_q):
            kt, va = tiles(qi * tiles_per_q + d)
            live = [(hh, cb) for hh, cb in chains if cb >= d]
            ss = [logits(kt, hh, cb) for hh, cb in live]
            for (hh, cb), s in zip(live, ss):
                if cb == d:
                    s = jnp.where(key2 <= qry2, s, NEG_BIG)
                stats[hh, cb] = absorb(stats[hh, cb], s, va[hh], 0.0)
        for ci, c in enumerate(chains):
            m_ref[ci], acc_ref[ci] = stats[c]

    def absorb_tiles(js, flat):
        loaded = [tiles(j) for j in js]
        ss = [[logits(kt, hh, cb) for hh, cb in chains] for kt, _ in loaded]
        out = list(flat)
        for j, (_, va), s_tile in zip(js, loaded, ss):
            for ci, (hh, cb) in enumerate(chains):
                c = -(sl_ref[0, hp * per_pair + hh] * tq) * (qi - j // tiles_per_q).astype(F32)
                out[2 * ci:2 * ci + 2] = absorb(out[2 * ci:2 * ci + 2], s_tile[ci], va[hh], c)
        return tuple(out)

    n_mine = (qi * tiles_per_q - r + n_sub - 1) // n_sub
    flat = tuple(x for ci in range(len(chains)) for x in (m_ref[ci], acc_ref[ci]))
    flat = lax.fori_loop(0, n_mine // 2, lambda i, f: absorb_tiles([r + 2 * i * n_sub, r + (2 * i + 1) * n_sub], f), flat)
    flat = lax.fori_loop(0, n_mine % 2, lambda i, f: absorb_tiles([r + (n_mine - 1) * n_sub], f), flat)
    for ci in range(len(chains)):
        m_ref[ci], acc_ref[ci] = flat[2 * ci], flat[2 * ci + 1]

    @pl.when(r == n_sub - 1)
    def _():
        for cb in range(tiles_per_q):
            accs = [flat[2 * chains.index((hh, cb)) + 1] for hh in range(per_pair)]
            out = jnp.concatenate([acc[:hd] / acc[hd:hd + 1] for acc in accs], axis=0)
            o_ref[0, cb * tk:(cb + 1) * tk, :] = out.T


def _moba_consts(t, tq):
    n_blk = t // MOBA_BLOCK
    base = -(-n_blk // SUBLANES) * SUBLANES
    assert base + _AUG_EXTRA <= LANES
    pos = np.arange(t)
    c = np.zeros((t, LANES), np.float32)
    c[pos, pos // MOBA_BLOCK] = 1.0
    for i in range(_SLOPE_PARTS):
        c[:, base + i] = pos % MOBA_BLOCK
        c[:, base + _SLOPE_PARTS + i] = (pos % tq) // MOBA_BLOCK
    return jnp.asarray(c, dtype=BF16)


_BF16_ROWS = 2 * SUBLANES


def _moba_sample_ops(slopes_ref, qbd_ref, s_ref, p_ref, ksum_ref, l_ref, acc_ref, *,
                     ts, w, page, n_per_step, pages_per_block, hd):
    qk_pages, pv_pages = 4, 2
    n_head = w // hd
    rows = n_head * ts
    blk = page * pages_per_block
    past = s_ref.shape[1]
    n_blk = past // blk
    scale = hd ** -0.5

    row_head = lax.broadcasted_iota(jnp.int32, (rows, 1), 0) // ts
    row_q = lax.broadcasted_iota(jnp.int32, (rows, 1), 0) % ts
    col_head = lax.broadcasted_iota(jnp.int32, (rows, w), 1) // hd
    slope = jnp.zeros((rows, 1), F32)
    for h in range(n_head):
        slope = jnp.where(row_head == h, slopes_ref[h], slope)

    def new_sequence(q):
        qt = jnp.concatenate([q] * n_head, axis=0)
        qbd_ref[...] = jnp.where(col_head == row_head, qt, 0.0)
        ksum_ref[...] = jnp.zeros(ksum_ref.shape, F32)

    def k_group(pages_ref, j):
        qs = (qbd_ref[...] * scale).astype(BF16)
        lane = lax.broadcasted_iota(jnp.int32, ksum_ref.shape, 1)
        ksum = ksum_ref[...]
        group = qk_pages if n_per_step % qk_pages == 0 and qk_pages % pages_per_block == 0 else pages_per_block
        for g0 in range(0, n_per_step, group):
            kts = [pages_ref[p_i].reshape(w, page) for p_i in range(g0, g0 + group)]
            first = j * n_per_step + g0
            s = jnp.dot(qs, jnp.concatenate([kt.astype(BF16) for kt in kts], axis=1), preferred_element_type=F32)
            s_ref[:, pl.ds(pl.multiple_of(first * page, group * page), group * page)] = s
            for b0 in range(0, group, pages_per_block):
                total = kts[b0]
                for kt in kts[b0 + 1:b0 + pages_per_block]:
                    total = total + kt
                ksum = jnp.where(lane == (first + b0) // pages_per_block, jnp.sum(total, axis=1, keepdims=True), ksum)
        ksum_ref[...] = ksum

    def boundary(kn, vn):
        qbd = qbd_ref[...]
        gate = _bdot(qbd, ksum_ref[...])
        lane = lax.broadcasted_iota(jnp.int32, gate.shape, 1)
        sel = _top_blocks(gate, lane < n_blk, lane)
        s_own = _bdot_nt(qbd * scale, kn)
        kq = lax.broadcasted_iota(jnp.int32, s_own.shape, 1)
        s_own = jnp.where(kq <= row_q, s_own + slope * kq.astype(F32), NEG_BIG)
        key_in_blk = lax.broadcasted_iota(jnp.int32, (1, blk), 1)

        def add_bias(b_i, m_wide):
            cols = pl.ds(pl.multiple_of(b_i * blk, blk), blk)
            picked = jnp.max(jnp.where(lane == b_i, sel, 0.0), axis=-1, keepdims=True)
            rel = (b_i * blk - past + key_in_blk).astype(F32)
            s = s_ref[:, cols] + slope * rel + jnp.where(picked > 0.0, 0.0, NEG_BIG)
            s_ref[:, cols] = s
            return jnp.maximum(m_wide, s)

        unroll = 8 if n_blk % 8 == 0 else 1
        m_wide = lax.fori_loop(0, n_blk, add_bias, jnp.full((rows, blk), NEG_BIG, F32), unroll=unroll)
        m = jnp.maximum(jnp.max(m_wide, axis=-1, keepdims=True), jnp.max(s_own, axis=-1, keepdims=True))

        def probs(b_i, l_wide):
            cols = pl.ds(pl.multiple_of(b_i * blk, blk), blk)
            p = jnp.exp(s_ref[:, cols] - m)
            p_ref[:, cols] = p.astype(BF16)
            return l_wide + p

        l_wide = lax.fori_loop(0, n_blk, probs, jnp.zeros((rows, blk), F32), unroll=unroll)
        p_own = jnp.exp(s_own - m)
        l_ref[...] = jnp.sum(l_wide, axis=-1, keepdims=True) + jnp.sum(p_own, axis=-1, keepdims=True)
        acc_ref[...] = _bdot(p_own, vn)

    def v_group(pages_ref, j):
        acc = acc_ref[...]
        group = pv_pages if n_per_step % pv_pages == 0 else 1
        for g0 in range(0, n_per_step, group):
            first = j * n_per_step + g0
            p = p_ref[:, pl.ds(pl.multiple_of(first * page, group * page), group * page)]
            vt = jnp.concatenate([pages_ref[p_i].reshape(w, page).astype(BF16) for p_i in range(g0, g0 + group)], axis=1)
            acc = acc + lax.dot_general(p, vt, _NT, preferred_element_type=F32)
        acc_ref[...] = acc

    def result():
        res = jnp.where(col_head == row_head, acc_ref[...] / l_ref[...], 0.0)
        out = res[0:ts]
        for h in range(1, n_head):
            out = out + res[h * ts:(h + 1) * ts]
        return out

    return new_sequence, k_group, boundary, v_group, result


def _moba_kernel(pt_ref, sl_ref, slopes_ref, q_ref, k_ref, vt_ref, kc_ref, km_ref, qs_ref, kn_ref, vn_ref,
                 cache_k_hbm, cache_v_hbm, o_ref, os_ref, qaug_ref, m_ref, acc_ref, pages_ref, sem, *sample_scratch,
                 pages_per_block, hd, tk, n_k_steps):
    hp, qi, r = pl.program_id(1), pl.program_id(2), pl.program_id(3)
    n_sub = pl.num_programs(3)
    u = ((pl.program_id(0) * pl.num_programs(1) + hp) * pl.num_programs(2) + qi) * n_sub + r
    n_steps = pl.num_programs(0) * pl.num_programs(1) * pl.num_programs(2) * n_sub
    n_seq, ts, w = qs_ref.shape
    _, n_per_step, _, _, page = pages_ref.shape
    new_sequence, k_group, boundary, v_group, result = _moba_sample_ops(
        slopes_ref, *sample_scratch, ts=ts, w=w, page=page, n_per_step=n_per_step,
        pages_per_block=pages_per_block, hd=hd)

    def start_pages(cache_hbm, seq, group, slot):
        for p in range(n_per_step):
            pltpu.make_async_copy(cache_hbm.at[pt_ref[seq, group * n_per_step + p]],
                                  pages_ref.at[slot, p], sem.at[slot]).start()

    def wait_pages(slot):
        pltpu.make_async_copy(cache_k_hbm.at[pl.ds(0, n_per_step)], pages_ref.at[slot], sem.at[slot]).wait()

    v_slot = 2 + u % 2

    def start_k(step):
        start_pages(cache_k_hbm, step // n_k_steps, step % n_k_steps, step % 2)

    def start_v(step):
        start_pages(cache_v_hbm, jnp.maximum(step // n_k_steps - 1, 0), step % n_k_steps, 2 + step % 2)

    @pl.when(u == 0)
    def _():
        start_k(u)
        start_v(u)
        p_ref, l_ref, sacc_ref = sample_scratch[2], sample_scratch[4], sample_scratch[5]
        p_ref[...] = jnp.zeros(p_ref.shape, p_ref.dtype)
        l_ref[...] = jnp.ones(l_ref.shape, F32)
        sacc_ref[...] = jnp.zeros(sacc_ref.shape, F32)

    @pl.when(u + 1 < n_steps)
    def _():
        start_k(u + 1)
        start_v(u + 1)

    _moba_prompt_step(r, n_sub, hp, qi, sl_ref, q_ref, k_ref, vt_ref, kc_ref, km_ref, o_ref,
                      qaug_ref, m_ref, acc_ref, hd=hd, tk=tk)

    seq, group = u // n_k_steps, u % n_k_steps
    k_slot = u % 2

    @pl.when((group == 0) & (u > 0))
    def _():
        boundary(kn_ref[seq - 1], vn_ref[seq - 1])

    @pl.when(group == 0)
    def _():
        new_sequence(qs_ref[seq])

    wait_pages(k_slot)
    wait_pages(v_slot)
    k_group(pages_ref.at[k_slot], group)
    v_group(pages_ref.at[v_slot], group)

    @pl.when((group == n_k_steps - 1) & (seq > 0))
    def _():
        os_ref[seq - 1] = result()

    @pl.when(u == n_steps - 1)
    def _():
        boundary(kn_ref[n_seq - 1], vn_ref[n_seq - 1])
        for g in range(n_k_steps):
            slot = g % 2
            if g == 0:
                start_pages(cache_v_hbm, n_seq - 1, 0, slot)
            if g + 1 < n_k_steps:
                start_pages(cache_v_hbm, n_seq - 1, g + 1, 1 - slot)
            wait_pages(slot)
            v_group(pages_ref.at[slot], g)
        os_ref[n_seq - 1] = result()


def _moba(qa, k_bf, vt_bf, kmean, qa_s, ka_s, va_s, cache_kt, cache_vt, page_table, *, hd):
    b, t, w = qa.shape
    bs, ts, _ = qa_s.shape
    _, n_head, _, page = cache_kt.shape
    n_pages = page_table.shape[1]
    blk = MOBA_BLOCK
    ppb = blk // page
    n_pair = w // LANES
    n_blk = t // blk
    tk = 2 * blk if t % (2 * blk) == 0 else blk
    tq = 2 * tk if t % (2 * tk) == 0 else tk
    n_q = t // tq
    n_per_step = next(p for p in (32, 16, 8, 4, 2) if n_pages % p == 0 and p % ppb == 0
                      and (bs * (n_pages // p)) % (b * n_pair * n_q) == 0)
    n_k_steps = n_pages // n_per_step
    n_sub = bs * n_k_steps // (b * n_pair * n_q)
    assert n_pages // ppb <= LANES
    rows = n_head * ts

    sl2 = _alibi_slopes(n_head) * LOG2E
    parts, rest = [], sl2
    for _ in range(_SLOPE_PARTS):
        parts.append(rest.astype(ml_dtypes.bfloat16).astype(np.float64))
        rest = rest - parts[-1]
    sl = jnp.asarray(np.stack([sl2] + parts), dtype=F32)
    slopes = jnp.asarray(_alibi_slopes(n_head), dtype=F32)
    kc = _moba_consts(t, tq)
    km = jnp.pad(kmean, ((0, 0), (0, LANES - n_blk), (0, 0)))

    def whole(a):
        return pl.BlockSpec(a.shape, lambda bi, hp, qi, r, pt: (0,) * a.ndim)

    smem = pl.BlockSpec(memory_space=pltpu.SMEM)
    hbm = pl.BlockSpec(memory_space=pl.ANY)
    n_chain = (LANES // hd) * (tq // tk)
    return pl.pallas_call(
        functools.partial(_moba_kernel, pages_per_block=ppb, hd=hd, tk=tk, n_k_steps=n_k_steps),
        grid_spec=pltpu.PrefetchScalarGridSpec(
            num_scalar_prefetch=1,
            grid=(b, n_pair, n_q, n_sub),
            in_specs=[smem, smem,
                      pl.BlockSpec((1, tq, LANES), lambda bi, hp, qi, r, pt: (bi, qi, hp)),
                      pl.BlockSpec((1, t, LANES), lambda bi, hp, qi, r, pt: (bi, 0, hp), pipeline_mode=pl.Buffered(1)),
                      pl.BlockSpec((1, LANES, t), lambda bi, hp, qi, r, pt: (bi, hp, 0), pipeline_mode=pl.Buffered(1)),
                      pl.BlockSpec((t, LANES), lambda bi, hp, qi, r, pt: (0, 0), pipeline_mode=pl.Buffered(1)),
                      pl.BlockSpec((1, LANES, LANES), lambda bi, hp, qi, r, pt: (bi, 0, hp)),
                      whole(qa_s), whole(ka_s), whole(va_s), hbm, hbm],
            out_specs=[pl.BlockSpec((1, tq, LANES), lambda bi, hp, qi, r, pt: (bi, qi, hp)), whole(qa_s)],
            scratch_shapes=[pltpu.VMEM((LANES // hd, 2 * LANES, tq), BF16),
                            pltpu.VMEM((n_chain, 1, tk), F32),
                            pltpu.VMEM((n_chain, hd + _BF16_ROWS, tk), F32),
                            pltpu.VMEM((4, n_per_step, n_head, hd, page), F32),
                            pltpu.SemaphoreType.DMA((4,)),
                            pltpu.VMEM((rows, w), F32),
                            pltpu.VMEM((rows, n_pages * page), F32),
                            pltpu.VMEM((rows, n_pages * page), BF16),
                            pltpu.VMEM((w, LANES), F32),
                            pltpu.VMEM((rows, 1), F32), pltpu.VMEM((rows, w), F32)],
        ),
        out_shape=[jax.ShapeDtypeStruct((b, t, w), F32), jax.ShapeDtypeStruct((bs, ts, w), F32)],
        compiler_params=_params("arbitrary", "arbitrary", "arbitrary", "arbitrary",
                                vmem_limit_bytes=61 * 1024 * 1024),
        name="moba",
    )(page_table, sl, slopes, qa, k_bf, vt_bf, kc, km, qa_s, ka_s, va_s, cache_kt, cache_vt)


def _gla_kernel(q_ref, k_ref, v_ref, la_ref, rg_ref, gn_ref, s0_ref, o_ref, sout_ref, st_ref, *, dk, dv, chunk):
    c_idx = pl.program_id(1)
    n_chunks = pl.num_programs(1)
    c = chunk
    n_pair = q_ref.shape[2] // LANES
    per_pair = LANES // dk
    assert dv == LANES and c % SUBLANES == 0

    @pl.when(c_idx == 0)
    def _():
        for p in range(n_pair):
            st_ref[p] = s0_ref[0, p].T

    row = lax.broadcasted_iota(jnp.int32, (c, LANES), 0)
    lane = lax.broadcasted_iota(jnp.int32, (c, LANES), 1)
    row2 = lax.broadcasted_iota(jnp.int32, (c, c), 0)
    col2 = lax.broadcasted_iota(jnp.int32, (c, c), 1)
    tri = jnp.where(row2 >= col2, 1.0, 0.0)
    nb8 = c // SUBLANES
    sub = lax.broadcasted_iota(jnp.int32, (nb8, SUBLANES, LANES), 1)
    lane3 = lax.broadcasted_iota(jnp.int32, (nb8, SUBLANES, LANES), 2)
    lane_st = lax.broadcasted_iota(jnp.int32, (dv, LANES), 1)

    def bcast_row(x3, jj):
        return jnp.broadcast_to(x3[:, jj:jj + 1, :], x3.shape)

    def one_chunk(p, rows, b_all, st):
        cols = slice(p * LANES, (p + 1) * LANES)
        q = q_ref[0, rows, cols] * (dk ** -0.5)
        k = k_ref[0, rows, cols]
        b = b_all[:, cols]
        vs = [v_ref[0, rows, (p * per_pair + hh) * dv:(p * per_pair + hh + 1) * dv] for hh in range(per_pair)]
        head_lanes = [(lane >= hh * dk) & (lane < (hh + 1) * dk) for hh in range(per_pair)]
        head_lanes3 = [(lane3 >= hh * dk) & (lane3 < (hh + 1) * dk) for hh in range(per_pair)]

        qe = q * jnp.exp(b)
        o = [_bdot_nt(jnp.where(head_lanes[hh], qe, 0.0), st) for hh in range(per_pair)]

        q3 = q.reshape(nb8, SUBLANES, LANES)
        k3 = k.reshape(nb8, SUBLANES, LANES)
        b3 = b.reshape(nb8, SUBLANES, LANES)
        v3 = [v.reshape(nb8, SUBLANES, dv) for v in vs]
        o3 = [jnp.zeros((nb8, SUBLANES, dv), F32) for _ in range(per_pair)]
        for jj in range(SUBLANES):
            e = jnp.exp(jnp.minimum(b3 - bcast_row(b3, jj), 0.0))
            term = jnp.where(sub >= jj, q3 * bcast_row(k3, jj) * e, 0.0)
            for hh in range(per_pair):
                a = jnp.sum(jnp.where(head_lanes3[hh], term, 0.0), axis=-1, keepdims=True)
                o3[hh] = o3[hh] + a * bcast_row(v3[hh], jj)
        o = [o[hh] + o3[hh].reshape(c, dv) for hh in range(per_pair)]

        attn = [jnp.zeros((c, c), F32) for _ in range(per_pair)]
        m_half = SUBLANES
        while 2 * m_half <= c:
            span = 2 * m_half
            bnd = jnp.broadcast_to(b.reshape(c // span, span, LANES)[:, m_half - 1:m_half, :],
                                   (c // span, span, LANES)).reshape(c, LANES)
            upper = (row % span) >= m_half
            qm = jnp.where(upper, q * jnp.exp(jnp.minimum(b - bnd, 0.0)), 0.0)
            km = jnp.where(upper, 0.0, k * jnp.exp(jnp.minimum(bnd - b, 0.0)))
            same = (row2 // span) == (col2 // span)
            for hh in range(per_pair):
                a = _bdot_nt(jnp.where(head_lanes[hh], qm, 0.0), km)
                attn[hh] = attn[hh] + jnp.where(same, a, 0.0)
            m_half = span
        if c > SUBLANES:
            o = [o[hh] + _bdot(attn[hh], vs[hh]) for hh in range(per_pair)]

        b_last = b[c - 1:c, :]
        kk = (k * jnp.exp(b_last - b)).astype(BF16)
        upd = lax.dot_general(vs[0].astype(BF16), kk, _TN, preferred_element_type=F32)
        for hh in range(1, per_pair):
            u = lax.dot_general(vs[hh].astype(BF16), kk, _TN, preferred_element_type=F32)
            upd = jnp.where(lane_st >= hh * dk, u, upd)
        for hh in range(per_pair):
            h = p * per_pair + hh
            rg = rg_ref[0, rows, h * dv:(h + 1) * dv]
            o_ref[0, rows, h * dv:(h + 1) * dv] = _rms(o[hh], gn_ref[...]) * (rg * _sigmoid(rg))
        return st * jnp.exp(b_last) + upd

    states = [st_ref[p] for p in range(n_pair)]
    for ch in range(q_ref.shape[1] // c):
        rows = slice(ch * c, (ch + 1) * c)
        b_all = jnp.dot(tri, la_ref[0, rows, :], precision=lax.Precision.HIGHEST, preferred_element_type=F32)
        states = [one_chunk(p, rows, b_all, states[p]) for p in range(n_pair)]
    for p in range(n_pair):
        st_ref[p] = states[p]

    @pl.when(c_idx == n_chunks - 1)
    def _():
        for p in range(n_pair):
            sout_ref[0, p] = states[p].T


def _gla(qg, kg, vg, la, rg, g_norm, s0, *, chunk, dk, dv):
    b, t, wk = qg.shape
    wv = vg.shape[2]
    n_head = wk // dk
    n_pair = wk // LANES
    s0p = s0.reshape(b, n_pair, LANES, dv)
    step = next(n * chunk for n in (4, 2, 1) if t % (n * chunk) == 0)
    tok = lambda w: pl.BlockSpec((1, step, w), lambda bi, ci: (bi, ci, 0))
    st_spec = pl.BlockSpec((1, n_pair, LANES, dv), lambda bi, ci: (bi, 0, 0, 0))
    og, s_new = pl.pallas_call(
        functools.partial(_gla_kernel, dk=dk, dv=dv, chunk=chunk),
        grid=(b, t // step),
        in_specs=[tok(wk), tok(wk), tok(wv), tok(wk), tok(wv),
                  pl.BlockSpec(g_norm.shape, lambda bi, ci: (0, 0)), st_spec],
        out_specs=[tok(wv), st_spec],
        out_shape=[jax.ShapeDtypeStruct((b, t, wv), F32), jax.ShapeDtypeStruct((b, n_pair, LANES, dv), F32)],
        scratch_shapes=[pltpu.VMEM((n_pair, dv, LANES), F32)],
        compiler_params=_params("parallel", "arbitrary"),
        name="gla",
    )(qg, kg, vg, la, rg, g_norm, s0p)
    return og, s_new.reshape(b, n_head, dk, dv)


def _mix_kernel(x_ref, oa_ref, og_ref, sga_ref, sgb_ref, wa_ref, wb_ref, wo_ref, wu_ref, wd_ref,
                g_mix_ref, g_pre_ref, g_post_ref, y_ref):
    merged = (sga_ref[...].astype(F32) * _bdot(oa_ref[...], wa_ref[...])
              + sgb_ref[...].astype(F32) * _bdot(og_ref[...], wb_ref[...]))
    x = x_ref[...] + _rms(_bdot(merged, wo_ref[...]), g_mix_ref[...])
    u = _bdot(_rms(x, g_pre_ref[...]), wu_ref[...])
    u = jnp.square(jnp.maximum(u, 0.0))
    y_ref[...] = x + _rms(_bdot(u, wd_ref[...]), g_post_ref[...])


def _row_tile(n):
    return 2 * MOBA_BLOCK if n % (2 * MOBA_BLOCK) == 0 else MOBA_BLOCK


def _rowwise_call(kernel, name, row_inputs, const_inputs, out_width):
    n = row_inputs[0].shape[0]
    tm = _row_tile(n)
    assert n % tm == 0
    row = lambda a: pl.BlockSpec((tm, a.shape[1]), lambda i: (i, 0))
    full = lambda a: pl.BlockSpec(a.shape, lambda i: (0,) * a.ndim, pipeline_mode=pl.Buffered(1))
    return pl.pallas_call(
        kernel,
        grid=(n // tm,),
        in_specs=[row(a) for a in row_inputs] + [full(a) for a in const_inputs],
        out_specs=pl.BlockSpec((tm, out_width), lambda i: (i, 0)),
        out_shape=jax.ShapeDtypeStruct((n, out_width), F32),
        compiler_params=_params("parallel"),
        name=name,
    )(*row_inputs, *const_inputs)


def _gla_chunk(t):
    c = SUBLANES
    while c * 2 <= min(t, LANES) and t % (c * 2) == 0:
        c *= 2
    return c


def kernel(x_prompt, x_sample, cache_k, cache_v, page_table, state_gla, w_in, w_gla_gate, b_gla_gate, g_gla_norm,
           w_branch_a, w_branch_b, w_out, w_up, w_down, g_pre_mix, g_post_mix, g_pre_mlp, g_post_mlp):
    bp, tp, d_model = x_prompt.shape
    bs, ts, _ = x_sample.shape
    depth, n_phys, page, n_head, hd = cache_k.shape
    _, _, n_head_g, dk, dv = state_gla.shape
    w_a = n_head * hd
    qk_g = n_head_g * dk
    v_g = n_head_g * dv
    rank = w_gla_gate.shape[1]
    n_main = 3 * w_a + 2 * qk_g + 2 * v_g

    hp = x_prompt.reshape(bp * tp, d_model)
    hs = x_sample.reshape(bs * ts, d_model)
    outs = [[] for _ in range(6)]
    for l in range(depth):
        wm = w_in[l, :, :n_main].astype(BF16)
        wlr = jnp.pad(w_in[l, :, n_main:n_main + rank], ((0, 0), (0, LANES - rank))).astype(BF16)
        wgt = w_in[l, :, n_main + rank:].astype(BF16)
        wgg = jnp.pad(w_gla_gate[l], ((0, LANES - rank), (0, 0))).astype(BF16)
        bgg = b_gla_gate[l][None, :]
        wa, wb, wo = (w[l].astype(BF16) for w in (w_branch_a, w_branch_b, w_out))
        wu, wd = w_up[l].astype(BF16), w_down[l].astype(BF16)
        g_mix, g_pm, g_mlp, g_pl, g_gn = (g[l][None, :] for g in
                                          (g_pre_mix, g_post_mix, g_pre_mlp, g_post_mlp, g_gla_norm))

        def project(x, kv_seq_len):
            return _proj(x, g_mix, wm, wlr, wgt, wgg, bgg, w_a=w_a, qk_g=qk_g, v_g=v_g, kv_seq_len=kv_seq_len)

        def mix(x, b, t, oa, proj_out, s0):
            qg, kg, vg, rg, la, sga, sgb = proj_out[3:10]
            r3 = lambda a: a.reshape(b, t, a.shape[-1])
            og, s_new = _gla(r3(qg), r3(kg), r3(vg), r3(la), r3(rg), g_gn, s0, chunk=_gla_chunk(t), dk=dk, dv=dv)
            n = b * t
            x = _rowwise_call(_mix_kernel, "mix", [x, oa.reshape(n, w_a), og.reshape(n, v_g), sga, sgb],
                              [wa, wb, wo, wu, wd, g_pm, g_mlp, g_pl], d_model)
            return x, s_new

        pp = project(hp, tp)
        ps = project(hs, None)
        qa_p, kt_p, vt_p, kmean_p, k_bf, vt_bf = pp[0], pp[1], pp[2], pp[10], pp[11], pp[12]
        qa_s, ka_s, va_s = (a.reshape(bs, ts, w_a) for a in ps[:3])
        to_stored = lambda c: jnp.transpose(c[l], (0, 2, 3, 1))
        oa_p, oa_s = _moba(qa_p.reshape(bp, tp, w_a), k_bf.reshape(bp, tp, w_a), vt_bf,
                           kmean_p.reshape(bp, tp // MOBA_BLOCK, w_a), qa_s, ka_s, va_s,
                           to_stored(cache_k), to_stored(cache_v), page_table, hd=hd)
        hp, sp = mix(hp, bp, tp, oa_p, pp, jnp.zeros((bp, n_head_g, dk, dv), state_gla.dtype))
        hs, ssn = mix(hs, bs, ts, oa_s, ps, state_gla[l])
        kp, vp = (jnp.transpose(a.reshape(bp, n_head, hd, tp), (0, 3, 1, 2)) for a in (kt_p, vt_p))
        ksn, vsn = (a.reshape(bs, ts, n_head, hd) for a in (ka_s, va_s))
        for lst, val in zip(outs, (kp, vp, sp, ksn, vsn, ssn)):
            lst.append(val)
    return (hp.reshape(bp, tp, d_model), hs.reshape(bs, ts, d_model)) + tuple(jnp.stack(o) for o in outs)
```

```python
import functools

import ml_dtypes
import numpy as np
import jax
import jax.numpy as jnp
from jax import lax
from jax.experimental import pallas as pl
from jax.experimental.pallas import tpu as pltpu

F32 = jnp.float32
BF16 = jnp.bfloat16

LANES = 128
SUBLANES = 8
VMEM_LIMIT_BYTES = 56 * 1024 * 1024
MOBA_VMEM_LIMIT_BYTES = 61 * 1024 * 1024

EPS = 1e-6
MOBA_BLOCK = 256
MOBA_TOPK = 3
GLA_TAU = 16.0
NEG_BIG = -1e30
PENALTY = -30000.0
LOG2E = float(np.log2(np.e))

_NT = (((1,), (1,)), ((), ()))
_TN = (((0,), (0,)), ((), ()))


def _params(*sem, vmem_limit_bytes=VMEM_LIMIT_BYTES):
    return pltpu.CompilerParams(dimension_semantics=sem, vmem_limit_bytes=vmem_limit_bytes)


def _sigmoid(x):
    return 1.0 / (1.0 + jnp.exp(-x))


def _rms(x, g):
    return x * lax.rsqrt(jnp.mean(x * x, axis=-1, keepdims=True) + EPS) * g


def _bdot(a, b):
    return jnp.dot(a.astype(BF16), b.astype(BF16), preferred_element_type=F32)


def _bdot_nt(a, b):
    return lax.dot_general(a.astype(BF16), b.astype(BF16), _NT, preferred_element_type=F32)


def _proj_kernel(x_ref, g_ref, wm_ref, wlr_ref, wgt_ref, wgg_ref, bgg_ref,
                 qa_ref, ka_ref, va_ref, qg_ref, kg_ref, vg_ref, rg_ref, la_ref, sga_ref, sgb_ref, kmean_ref,
                 *attn_refs, w_a, qk_g, v_g, d_model, kv_transposed):
    hb = _rms(x_ref[...], g_ref[...]).astype(BF16)

    def proj(lo, n):
        return jnp.dot(hb, wm_ref[:, lo:lo + n], preferred_element_type=F32)

    qa_ref[...] = proj(0, w_a)
    ka = proj(w_a, w_a)
    va = proj(2 * w_a, w_a)
    for i in range(kmean_ref.shape[0]):
        kmean_ref[i] = jnp.sum(ka[i * MOBA_BLOCK:(i + 1) * MOBA_BLOCK], axis=0, keepdims=True) * (1.0 / MOBA_BLOCK)
    if kv_transposed:
        k_bf_ref, vt_bf_ref = attn_refs
        vt = va.T
        ka_ref[0] = ka.T
        va_ref[0] = vt
        k_bf_ref[...] = ka.astype(BF16)
        vt_bf_ref[0] = vt.astype(BF16)
    else:
        ka_ref[...] = ka
        va_ref[...] = va
    lo = 3 * w_a
    qg_ref[...] = proj(lo, qk_g)
    kg_ref[...] = proj(lo + qk_g, qk_g)
    vg_ref[...] = proj(lo + 2 * qk_g, v_g)
    rg_ref[...] = proj(lo + 2 * qk_g + v_g, v_g)
    lr = jnp.dot(hb, wlr_ref[...], preferred_element_type=F32)
    xg = jnp.dot(lr.astype(BF16), wgg_ref[...], preferred_element_type=F32) + bgg_ref[...]
    log_sig = jnp.minimum(xg, 0.0) - jnp.log1p(jnp.exp(-jnp.abs(xg)))
    la_ref[...] = log_sig * (1.0 / GLA_TAU)
    ga = jnp.dot(hb, wgt_ref[:, :d_model], preferred_element_type=F32)
    sga_ref[...] = _sigmoid(ga).astype(BF16)
    gb = jnp.dot(hb, wgt_ref[:, d_model:], preferred_element_type=F32)
    sgb_ref[...] = _sigmoid(gb).astype(BF16)


def _proj(x, g, wm, wlr, wgt, wgg, bgg, *, w_a, qk_g, v_g, kv_seq_len=None):
    n, d_model = x.shape
    tm = _row_tile(n)
    assert n % tm == 0
    nt = n // tm
    blocks_per_tile = tm // MOBA_BLOCK
    row = lambda w: pl.BlockSpec((tm, w), lambda i: (i, 0))
    full = lambda a: pl.BlockSpec(a.shape, lambda i: (0,) * a.ndim, pipeline_mode=pl.Buffered(1))
    widths = (w_a, w_a, w_a, qk_g, qk_g, v_g, v_g, qk_g)
    out_shape = [jax.ShapeDtypeStruct((n, w), F32) for w in widths]
    out_shape += [jax.ShapeDtypeStruct((n, d_model), BF16)] * 2
    out_shape += [jax.ShapeDtypeStruct((n // MOBA_BLOCK, 1, w_a), F32)]
    out_specs = [row(w) for w in widths] + [row(d_model)] * 2
    out_specs += [pl.BlockSpec((blocks_per_tile, 1, w_a), lambda i: (i, 0, 0))]
    if kv_seq_len is not None:
        assert kv_seq_len % tm == 0
        seq_tiles = kv_seq_len // tm
        transposed = pl.BlockSpec((1, w_a, tm), lambda i: (i // seq_tiles, 0, i % seq_tiles))
        for i in (1, 2):
            out_shape[i] = jax.ShapeDtypeStruct((n // kv_seq_len, w_a, kv_seq_len), F32)
            out_specs[i] = transposed
        out_shape += [jax.ShapeDtypeStruct((n, w_a), BF16),
                      jax.ShapeDtypeStruct((n // kv_seq_len, w_a, kv_seq_len), BF16)]
        out_specs += [row(w_a), transposed]
    return pl.pallas_call(
        functools.partial(_proj_kernel, w_a=w_a, qk_g=qk_g, v_g=v_g, d_model=d_model,
                          kv_transposed=kv_seq_len is not None),
        grid=(nt,),
        in_specs=[row(d_model), full(g), full(wm), full(wlr), full(wgt), full(wgg), full(bgg)],
        out_specs=out_specs,
        out_shape=out_shape,
        compiler_params=_params("parallel"),
        name="proj",
    )(x, g, wm, wlr, wgt, wgg, bgg)


def _top_blocks(gate, candidate, block, axis=-1):
    block_f = block.astype(F32)
    no_block = float(gate.shape[axis])
    avail = jnp.where(candidate, 1.0, 0.0)
    sel = jnp.zeros(gate.shape, F32)
    for _ in range(MOBA_TOPK):
        cur = jnp.where(avail > 0.0, gate, -jnp.inf)
        mx = jnp.max(cur, axis=axis, keepdims=True)
        cand = jnp.where(avail > 0.0, jnp.where(cur == mx, block_f, no_block), no_block)
        first = jnp.min(cand, axis=axis, keepdims=True)
        pick = block_f == first
        sel = jnp.where(pick, 1.0, sel)
        avail = jnp.where(pick, 0.0, avail)
    return sel


def _alibi_slopes(n_head):
    return 2.0 ** (-8.0 * (np.arange(n_head) + 1) / n_head)


_SLOPE_PARTS = 3
_AUG_EXTRA = 2 * _SLOPE_PARTS


def _moba_prompt_step(r, n_sub, hp, qi, sl_ref, q_ref, k_ref, vt_ref, kc_ref, km_ref, o_ref,
                      qaug_ref, m_ref, acc_ref, *, hd, tk):
    blk = MOBA_BLOCK
    t = k_ref.shape[1]
    tq = q_ref.shape[1]
    n_blk = t // blk
    per_pair = LANES // hd
    n_acc = acc_ref.shape[1]
    tiles_per_q = tq // tk
    chains = [(hh, cb) for hh in range(per_pair) for cb in range(tiles_per_q)]
    ones = jnp.ones((n_acc - hd, tk), BF16)

    def tiles(j):
        keys = pl.ds(pl.multiple_of(j * tk, tk), tk)
        kt = jnp.concatenate([k_ref[0, keys, :], kc_ref[keys, :]], axis=1)
        return kt, [jnp.concatenate([vt_ref[0, hh * hd:(hh + 1) * hd, keys], ones], axis=0) for hh in range(per_pair)]

    def logits(kt, hh, cb):
        return jnp.dot(kt, qaug_ref[hh, :, cb * tk:(cb + 1) * tk], preferred_element_type=F32)

    def absorb(stat, s, va, c):
        m, acc = stat
        m_new = jnp.maximum(m, jnp.max(s, axis=0, keepdims=True) + c)
        p = jnp.exp2(s - (m_new - c)).astype(BF16)
        return m_new, jnp.exp2(m - m_new) * acc + jnp.dot(va, p, preferred_element_type=F32)

    @pl.when(r == 0)
    def _():
        lane = lax.broadcasted_iota(jnp.int32, (tq, LANES), 1)
        q = q_ref[0]
        n_rows = -(-n_blk // SUBLANES) * SUBLANES
        block = lax.broadcasted_iota(jnp.int32, (n_rows, tq), 0)
        own = qi * (tq // blk) + lax.broadcasted_iota(jnp.int32, (n_rows, tq), 1) // blk
        extra = lax.broadcasted_iota(jnp.int32, (SUBLANES, tq), 0)
        assert _AUG_EXTRA <= SUBLANES and n_rows + SUBLANES <= LANES
        for hh in range(per_pair):
            h = hp * per_pair + hh
            head_lanes = (lane >= hh * hd) & (lane < (hh + 1) * hd)
            qh = jnp.where(head_lanes, q, 0.0)
            gate = _bdot_nt(km_ref[0, :n_rows, :], qh)
            sel = _top_blocks(gate, block < own, block, axis=0)
            penalty = jnp.where((sel > 0.0) | (block == own) | (block >= n_blk), 0.0, PENALTY)
            parts = [sl_ref[1 + i, h] for i in range(_SLOPE_PARTS)]
            slope_rows = jnp.zeros((SUBLANES, tq), F32)
            for off, val in enumerate(parts + [part * blk for part in parts]):
                slope_rows = jnp.where(extra == off, val, slope_rows)
            qx_t = jnp.concatenate([penalty, slope_rows, jnp.zeros((LANES - n_rows - SUBLANES, tq), F32)], axis=0)
            qaug_ref[hh, :LANES, :] = (qh * (hd ** -0.5 * LOG2E)).T.astype(BF16)
            qaug_ref[hh, LANES:, :] = qx_t.astype(BF16)

        key2 = lax.broadcasted_iota(jnp.int32, (tk, tk), 0)
        qry2 = lax.broadcasted_iota(jnp.int32, (tk, tk), 1)
        stats = {c: (jnp.full((1, tk), NEG_BIG, F32), jnp.zeros((n_acc, tk), F32)) for c in chains}
        for d in range(tiles_per_q):
            kt, va = tiles(qi * tiles_per_q + d)
            live = [(hh, cb) for hh, cb in chains if cb >= d]
            ss = [logits(kt, hh, cb) for hh, cb in live]
            for (hh, cb), s in zip(live, ss):
                if cb == d:
                    s = jnp.where(key2 <= qry2, s, NEG_BIG)
                stats[hh, cb] = absorb(stats[hh, cb], s, va[hh], 0.0)
        for ci, c in enumerate(chains):
            m_ref[ci], acc_ref[ci] = stats[c]

    def absorb_tiles(js, flat):
        loaded = [tiles(j) for j in js]
        ss = [[logits(kt, hh, cb) for hh, cb in chains] for kt, _ in loaded]
        out = list(flat)
        for j, (_, va), s_tile in zip(js, loaded, ss):
            for ci, (hh, cb) in enumerate(chains):
                c = -(sl_ref[0, hp * per_pair + hh] * tq) * (qi - j // tiles_per_q).astype(F32)
                out[2 * ci:2 * ci + 2] = absorb(out[2 * ci:2 * ci + 2], s_tile[ci], va[hh], c)
        return tuple(out)

    n_mine = (qi * tiles_per_q - r + n_sub - 1) // n_sub
    flat = tuple(x for ci in range(len(chains)) for x in (m_ref[ci], acc_ref[ci]))
    flat = lax.fori_loop(0, n_mine // 2, lambda i, f: absorb_tiles([r + 2 * i * n_sub, r + (2 * i + 1) * n_sub], f), flat)
    flat = lax.fori_loop(0, n_mine % 2, lambda i, f: absorb_tiles([r + (n_mine - 1) * n_sub], f), flat)
    for ci in range(len(chains)):
        m_ref[ci], acc_ref[ci] = flat[2 * ci], flat[2 * ci + 1]

    @pl.when(r == n_sub - 1)
    def _():
        for cb in range(tiles_per_q):
            accs = [flat[2 * chains.index((hh, cb)) + 1] for hh in range(per_pair)]
            out = jnp.concatenate([acc[:hd] / acc[hd:hd + 1] for acc in accs], axis=0)
            o_ref[0, cb * tk:(cb + 1) * tk, :] = out.T


def _moba_consts(t, tq):
    n_blk = t // MOBA_BLOCK
    base = -(-n_blk // SUBLANES) * SUBLANES
    assert base + _AUG_EXTRA <= LANES
    pos = np.arange(t)
    c = np.zeros((t, LANES), np.float32)
    c[pos, pos // MOBA_BLOCK] = 1.0
    for i in range(_SLOPE_PARTS):
        c[:, base + i] = pos % MOBA_BLOCK
        c[:, base + _SLOPE_PARTS + i] = (pos % tq) // MOBA_BLOCK
    return jnp.asarray(c, dtype=BF16)


_BF16_ROWS = 2 * SUBLANES


def _moba_sample_ops(slopes_ref, qbd_ref, s_ref, p_ref, ksum_ref, l_ref, acc_ref, *,
                     ts, w, page, n_per_step, pages_per_block, hd):
    qk_pages, pv_pages = 4, 2
    n_head = w // hd
    rows = n_head * ts
    blk = page * pages_per_block
    past = s_ref.shape[1]
    n_blk = past // blk
    scale = hd ** -0.5

    row_head = lax.broadcasted_iota(jnp.int32, (rows, 1), 0) // ts
    row_q = lax.broadcasted_iota(jnp.int32, (rows, 1), 0) % ts
    col_head = lax.broadcasted_iota(jnp.int32, (rows, w), 1) // hd
    slope = jnp.zeros((rows, 1), F32)
    for h in range(n_head):
        slope = jnp.where(row_head == h, slopes_ref[h], slope)

    def new_sequence(q):
        qt = jnp.concatenate([q] * n_head, axis=0)
        qbd_ref[...] = jnp.where(col_head == row_head, qt, 0.0)
        ksum_ref[...] = jnp.zeros(ksum_ref.shape, F32)

    def k_group(pages_ref, j):
        qs = (qbd_ref[...] * scale).astype(BF16)
        lane = lax.broadcasted_iota(jnp.int32, ksum_ref.shape, 1)
        ksum = ksum_ref[...]
        group = qk_pages if n_per_step % qk_pages == 0 and qk_pages % pages_per_block == 0 else pages_per_block
        for g0 in range(0, n_per_step, group):
            kts = [pages_ref[p_i].reshape(w, page) for p_i in range(g0, g0 + group)]
            first = j * n_per_step + g0
            s = jnp.dot(qs, jnp.concatenate([kt.astype(BF16) for kt in kts], axis=1), preferred_element_type=F32)
            s_ref[:, pl.ds(pl.multiple_of(first * page, group * page), group * page)] = s
            for b0 in range(0, group, pages_per_block):
                total = kts[b0]
                for kt in kts[b0 + 1:b0 + pages_per_block]:
                    total = total + kt
                ksum = jnp.where(lane == (first + b0) // pages_per_block, jnp.sum(total, axis=1, keepdims=True), ksum)
        ksum_ref[...] = ksum

    def boundary(kn, vn):
        qbd = qbd_ref[...]
        gate = _bdot(qbd, ksum_ref[...])
        lane = lax.broadcasted_iota(jnp.int32, gate.shape, 1)
        sel = _top_blocks(gate, lane < n_blk, lane)
        s_own = _bdot_nt(qbd * scale, kn)
        kq = lax.broadcasted_iota(jnp.int32, s_own.shape, 1)
        s_own = jnp.where(kq <= row_q, s_own + slope * kq.astype(F32), NEG_BIG)
        key_in_blk = lax.broadcasted_iota(jnp.int32, (1, blk), 1)

        def add_bias(b_i, m_wide):
            cols = pl.ds(pl.multiple_of(b_i * blk, blk), blk)
            picked = jnp.max(jnp.where(lane == b_i, sel, 0.0), axis=-1, keepdims=True)
            rel = (b_i * blk - past + key_in_blk).astype(F32)
            s = s_ref[:, cols] + slope * rel + jnp.where(picked > 0.0, 0.0, NEG_BIG)
            s_ref[:, cols] = s
            return jnp.maximum(m_wide, s)

        unroll = 8 if n_blk % 8 == 0 else 1
        m_wide = lax.fori_loop(0, n_blk, add_bias, jnp.full((rows, blk), NEG_BIG, F32), unroll=unroll)
        m = jnp.maximum(jnp.max(m_wide, axis=-1, keepdims=True), jnp.max(s_own, axis=-1, keepdims=True))

        def probs(b_i, l_wide):
            cols = pl.ds(pl.multiple_of(b_i * blk, blk), blk)
            p = jnp.exp(s_ref[:, cols] - m)
            p_ref[:, cols] = p.astype(BF16)
            return l_wide + p

        l_wide = lax.fori_loop(0, n_blk, probs, jnp.zeros((rows, blk), F32), unroll=unroll)
        p_own = jnp.exp(s_own - m)
        l_ref[...] = jnp.sum(l_wide, axis=-1, keepdims=True) + jnp.sum(p_own, axis=-1, keepdims=True)
        acc_ref[...] = _bdot(p_own, vn)

    def v_group(pages_ref, j):
        acc = acc_ref[...]
        group = pv_pages if n_per_step % pv_pages == 0 else 1
        for g0 in range(0, n_per_step, group):
            first = j * n_per_step + g0
            p = p_ref[:, pl.ds(pl.multiple_of(first * page, group * page), group * page)]
            vt = jnp.concatenate([pages_ref[p_i].reshape(w, page).astype(BF16) for p_i in range(g0, g0 + group)], axis=1)
            acc = acc + lax.dot_general(p, vt, _NT, preferred_element_type=F32)
        acc_ref[...] = acc

    def result():
        res = jnp.where(col_head == row_head, acc_ref[...] / l_ref[...], 0.0)
        out = res[0:ts]
        for h in range(1, n_head):
            out = out + res[h * ts:(h + 1) * ts]
        return out

    return new_sequence, k_group, boundary, v_group, result


def _moba_kernel(pt_ref, sl_ref, slopes_ref, q_ref, k_ref, vt_ref, kc_ref, km_ref, qs_ref, kn_ref, vn_ref,
                 cache_k_hbm, cache_v_hbm, o_ref, os_ref, qaug_ref, m_ref, acc_ref, pages_ref, sem, *sample_scratch,
                 pages_per_block, hd, tk, n_k_steps):
    hp, qi, r = pl.program_id(1), pl.program_id(2), pl.program_id(3)
    n_sub = pl.num_programs(3)
    u = ((pl.program_id(0) * pl.num_programs(1) + hp) * pl.num_programs(2) + qi) * n_sub + r
    n_steps = pl.num_programs(0) * pl.num_programs(1) * pl.num_programs(2) * n_sub
    n_seq, ts, w = qs_ref.shape
    _, n_per_step, _, _, page = pages_ref.shape
    new_sequence, k_group, boundary, v_group, result = _moba_sample_ops(
        slopes_ref, *sample_scratch, ts=ts, w=w, page=page, n_per_step=n_per_step,
        pages_per_block=pages_per_block, hd=hd)

    def start_pages(cache_hbm, seq, group, slot):
        for p in range(n_per_step):
            pltpu.make_async_copy(cache_hbm.at[pt_ref[seq, group * n_per_step + p]],
                                  pages_ref.at[slot, p], sem.at[slot]).start()

    def wait_pages(slot):
        pltpu.make_async_copy(cache_k_hbm.at[pl.ds(0, n_per_step)], pages_ref.at[slot], sem.at[slot]).wait()

    v_slot = 2 + u % 2

    def start_k(step):
        start_pages(cache_k_hbm, step // n_k_steps, step % n_k_steps, step % 2)

    def start_v(step):
        start_pages(cache_v_hbm, jnp.maximum(step // n_k_steps - 1, 0), step % n_k_steps, 2 + step % 2)

    @pl.when(u == 0)
    def _():
        start_k(u)
        start_v(u)
        p_ref, l_ref, sacc_ref = sample_scratch[2], sample_scratch[4], sample_scratch[5]
        p_ref[...] = jnp.zeros(p_ref.shape, p_ref.dtype)
        l_ref[...] = jnp.ones(l_ref.shape, F32)
        sacc_ref[...] = jnp.zeros(sacc_ref.shape, F32)

    @pl.when(u + 1 < n_steps)
    def _():
        start_k(u + 1)
        start_v(u + 1)

    _moba_prompt_step(r, n_sub, hp, qi, sl_ref, q_ref, k_ref, vt_ref, kc_ref, km_ref, o_ref,
                      qaug_ref, m_ref, acc_ref, hd=hd, tk=tk)

    seq, group = u // n_k_steps, u % n_k_steps
    k_slot = u % 2

    @pl.when((group == 0) & (u > 0))
    def _():
        boundary(kn_ref[seq - 1], vn_ref[seq - 1])

    @pl.when(group == 0)
    def _():
        new_sequence(qs_ref[seq])

    wait_pages(k_slot)
    wait_pages(v_slot)
    k_group(pages_ref.at[k_slot], group)
    v_group(pages_ref.at[v_slot], group)

    @pl.when((group == n_k_steps - 1) & (seq > 0))
    def _():
        os_ref[seq - 1] = result()

    @pl.when(u == n_steps - 1)
    def _():
        boundary(kn_ref[n_seq - 1], vn_ref[n_seq - 1])
        for g in range(n_k_steps):
            slot = g % 2
            if g == 0:
                start_pages(cache_v_hbm, n_seq - 1, 0, slot)
            if g + 1 < n_k_steps:
                start_pages(cache_v_hbm, n_seq - 1, g + 1, 1 - slot)
            wait_pages(slot)
            v_group(pages_ref.at[slot], g)
        os_ref[n_seq - 1] = result()


def _moba(qa, k_bf, vt_bf, kmean, qa_s, ka_s, va_s, cache_kt, cache_vt, page_table, *, hd):
    b, t, w = qa.shape
    bs, ts, _ = qa_s.shape
    _, n_head, _, page = cache_kt.shape
    n_pages = page_table.shape[1]
    blk = MOBA_BLOCK
    ppb = blk // page
    n_pair = w // LANES
    n_blk = t // blk
    tk = 2 * blk if t % (2 * blk) == 0 else blk
    tq = 2 * tk if t % (2 * tk) == 0 else tk
    n_q = t // tq
    n_per_step = next(p for p in (32, 16, 8, 4, 2) if n_pages % p == 0 and p % ppb == 0
                      and (bs * (n_pages // p)) % (b * n_pair * n_q) == 0)
    n_k_steps = n_pages // n_per_step
    n_sub = bs * n_k_steps // (b * n_pair * n_q)
    assert n_pages // ppb <= LANES
    rows = n_head * ts

    sl2 = _alibi_slopes(n_head) * LOG2E
    parts, rest = [], sl2
    for _ in range(_SLOPE_PARTS):
        parts.append(rest.astype(ml_dtypes.bfloat16).astype(np.float64))
        rest = rest - parts[-1]
    sl = jnp.asarray(np.stack([sl2] + parts), dtype=F32)
    slopes = jnp.asarray(_alibi_slopes(n_head), dtype=F32)
    kc = _moba_consts(t, tq)
    km = jnp.pad(kmean, ((0, 0), (0, LANES - n_blk), (0, 0)))

    def whole(a):
        return pl.BlockSpec(a.shape, lambda bi, hp, qi, r, pt: (0,) * a.ndim)

    smem = pl.BlockSpec(memory_space=pltpu.SMEM)
    hbm = pl.BlockSpec(memory_space=pl.ANY)
    n_chain = (LANES // hd) * (tq // tk)
    return pl.pallas_call(
        functools.partial(_moba_kernel, pages_per_block=ppb, hd=hd, tk=tk, n_k_steps=n_k_steps),
        grid_spec=pltpu.PrefetchScalarGridSpec(
            num_scalar_prefetch=1,
            grid=(b, n_pair, n_q, n_sub),
            in_specs=[smem, smem,
                      pl.BlockSpec((1, tq, LANES), lambda bi, hp, qi, r, pt: (bi, qi, hp)),
                      pl.BlockSpec((1, t, LANES), lambda bi, hp, qi, r, pt: (bi, 0, hp), pipeline_mode=pl.Buffered(1)),
                      pl.BlockSpec((1, LANES, t), lambda bi, hp, qi, r, pt: (bi, hp, 0), pipeline_mode=pl.Buffered(1)),
                      pl.BlockSpec((t, LANES), lambda bi, hp, qi, r, pt: (0, 0), pipeline_mode=pl.Buffered(1)),
                      pl.BlockSpec((1, LANES, LANES), lambda bi, hp, qi, r, pt: (bi, 0, hp)),
                      whole(qa_s), whole(ka_s), whole(va_s), hbm, hbm],
            out_specs=[pl.BlockSpec((1, tq, LANES), lambda bi, hp, qi, r, pt: (bi, qi, hp)), whole(qa_s)],
            scratch_shapes=[pltpu.VMEM((LANES // hd, 2 * LANES, tq), BF16),
                            pltpu.VMEM((n_chain, 1, tk), F32),
                            pltpu.VMEM((n_chain, hd + _BF16_ROWS, tk), F32),
                            pltpu.VMEM((4, n_per_step, n_head, hd, page), F32),
                            pltpu.SemaphoreType.DMA((4,)),
                            pltpu.VMEM((rows, w), F32),
                            pltpu.VMEM((rows, n_pages * page), F32),
                            pltpu.VMEM((rows, n_pages * page), BF16),
                            pltpu.VMEM((w, LANES), F32),
                            pltpu.VMEM((rows, 1), F32), pltpu.VMEM((rows, w), F32)],
        ),
        out_shape=[jax.ShapeDtypeStruct((b, t, w), F32), jax.ShapeDtypeStruct((bs, ts, w), F32)],
        compiler_params=_params("arbitrary", "arbitrary", "arbitrary", "arbitrary",
                                vmem_limit_bytes=MOBA_VMEM_LIMIT_BYTES),
        name="moba",
    )(page_table, sl, slopes, qa, k_bf, vt_bf, kc, km, qa_s, ka_s, va_s, cache_kt, cache_vt)


def _gla_kernel(q_ref, k_ref, v_ref, la_ref, rg_ref, gn_ref, s0_ref, o_ref, sout_ref, st_ref, *, dk, dv, chunk):
    c_idx = pl.program_id(1)
    n_chunks = pl.num_programs(1)
    c = chunk
    n_pair = q_ref.shape[2] // LANES
    per_pair = LANES // dk
    assert dv == LANES and c % SUBLANES == 0

    @pl.when(c_idx == 0)
    def _():
        for p in range(n_pair):
            st_ref[p] = s0_ref[0, p].T

    row = lax.broadcasted_iota(jnp.int32, (c, LANES), 0)
    lane = lax.broadcasted_iota(jnp.int32, (c, LANES), 1)
    row2 = lax.broadcasted_iota(jnp.int32, (c, c), 0)
    col2 = lax.broadcasted_iota(jnp.int32, (c, c), 1)
    tri = jnp.where(row2 >= col2, 1.0, 0.0)
    nb8 = c // SUBLANES
    sub = lax.broadcasted_iota(jnp.int32, (nb8, SUBLANES, LANES), 1)
    lane3 = lax.broadcasted_iota(jnp.int32, (nb8, SUBLANES, LANES), 2)
    lane_st = lax.broadcasted_iota(jnp.int32, (dv, LANES), 1)

    def bcast_row(x3, jj):
        return jnp.broadcast_to(x3[:, jj:jj + 1, :], x3.shape)

    def one_chunk(p, rows, b_all, st):
        cols = slice(p * LANES, (p + 1) * LANES)
        q = q_ref[0, rows, cols] * (dk ** -0.5)
        k = k_ref[0, rows, cols]
        b = b_all[:, cols]
        vs = [v_ref[0, rows, (p * per_pair + hh) * dv:(p * per_pair + hh + 1) * dv] for hh in range(per_pair)]
        head_lanes = [(lane >= hh * dk) & (lane < (hh + 1) * dk) for hh in range(per_pair)]
        head_lanes3 = [(lane3 >= hh * dk) & (lane3 < (hh + 1) * dk) for hh in range(per_pair)]

        qe = q * jnp.exp(b)
        o = [_bdot_nt(jnp.where(head_lanes[hh], qe, 0.0), st) for hh in range(per_pair)]

        q3 = q.reshape(nb8, SUBLANES, LANES)
        k3 = k.reshape(nb8, SUBLANES, LANES)
        b3 = b.reshape(nb8, SUBLANES, LANES)
        v3 = [v.reshape(nb8, SUBLANES, dv) for v in vs]
        o3 = [jnp.zeros((nb8, SUBLANES, dv), F32) for _ in range(per_pair)]
        for jj in range(SUBLANES):
            e = jnp.exp(jnp.minimum(b3 - bcast_row(b3, jj), 0.0))
            term = jnp.where(sub >= jj, q3 * bcast_row(k3, jj) * e, 0.0)
            for hh in range(per_pair):
                a = jnp.sum(jnp.where(head_lanes3[hh], term, 0.0), axis=-1, keepdims=True)
                o3[hh] = o3[hh] + a * bcast_row(v3[hh], jj)
        o = [o[hh] + o3[hh].reshape(c, dv) for hh in range(per_pair)]

        attn = [jnp.zeros((c, c), F32) for _ in range(per_pair)]
        m_half = SUBLANES
        while 2 * m_half <= c:
            span = 2 * m_half
            bnd = jnp.broadcast_to(b.reshape(c // span, span, LANES)[:, m_half - 1:m_half, :],
                                   (c // span, span, LANES)).reshape(c, LANES)
            upper = (row % span) >= m_half
            qm = jnp.where(upper, q * jnp.exp(jnp.minimum(b - bnd, 0.0)), 0.0)
            km = jnp.where(upper, 0.0, k * jnp.exp(jnp.minimum(bnd - b, 0.0)))
            same = (row2 // span) == (col2 // span)
            for hh in range(per_pair):
                a = _bdot_nt(jnp.where(head_lanes[hh], qm, 0.0), km)
                attn[hh] = attn[hh] + jnp.where(same, a, 0.0)
            m_half = span
        if c > SUBLANES:
            o = [o[hh] + _bdot(attn[hh], vs[hh]) for hh in range(per_pair)]

        b_last = b[c - 1:c, :]
        kk = (k * jnp.exp(b_last - b)).astype(BF16)
        upd = lax.dot_general(vs[0].astype(BF16), kk, _TN, preferred_element_type=F32)
        for hh in range(1, per_pair):
            u = lax.dot_general(vs[hh].astype(BF16), kk, _TN, preferred_element_type=F32)
            upd = jnp.where(lane_st >= hh * dk, u, upd)
        for hh in range(per_pair):
            h = p * per_pair + hh
            rg = rg_ref[0, rows, h * dv:(h + 1) * dv]
            o_ref[0, rows, h * dv:(h + 1) * dv] = _rms(o[hh], gn_ref[...]) * (rg * _sigmoid(rg))
        return st * jnp.exp(b_last) + upd

    states = [st_ref[p] for p in range(n_pair)]
    for ch in range(q_ref.shape[1] // c):
        rows = slice(ch * c, (ch + 1) * c)
        b_all = jnp.dot(tri, la_ref[0, rows, :], precision=lax.Precision.HIGHEST, preferred_element_type=F32)
        states = [one_chunk(p, rows, b_all, states[p]) for p in range(n_pair)]
    for p in range(n_pair):
        st_ref[p] = states[p]

    @pl.when(c_idx == n_chunks - 1)
    def _():
        for p in range(n_pair):
            sout_ref[0, p] = states[p].T


def _gla(qg, kg, vg, la, rg, g_norm, s0, *, chunk, dk, dv):
    b, t, wk = qg.shape
    wv = vg.shape[2]
    n_head = wk // dk
    n_pair = wk // LANES
    s0p = s0.reshape(b, n_pair, LANES, dv)
    step = next(n * chunk for n in (4, 2, 1) if t % (n * chunk) == 0)
    tok = lambda w: pl.BlockSpec((1, step, w), lambda bi, ci: (bi, ci, 0))
    st_spec = pl.BlockSpec((1, n_pair, LANES, dv), lambda bi, ci: (bi, 0, 0, 0))
    og, s_new = pl.pallas_call(
        functools.partial(_gla_kernel, dk=dk, dv=dv, chunk=chunk),
        grid=(b, t // step),
        in_specs=[tok(wk), tok(wk), tok(wv), tok(wk), tok(wv),
                  pl.BlockSpec(g_norm.shape, lambda bi, ci: (0, 0)), st_spec],
        out_specs=[tok(wv), st_spec],
        out_shape=[jax.ShapeDtypeStruct((b, t, wv), F32), jax.ShapeDtypeStruct((b, n_pair, LANES, dv), F32)],
        scratch_shapes=[pltpu.VMEM((n_pair, dv, LANES), F32)],
        compiler_params=_params("parallel", "arbitrary"),
        name="gla",
    )(qg, kg, vg, la, rg, g_norm, s0p)
    return og, s_new.reshape(b, n_head, dk, dv)


def _mix_kernel(x_ref, oa_ref, og_ref, sga_ref, sgb_ref, wa_ref, wb_ref, wo_ref, wu_ref, wd_ref,
                g_mix_ref, g_pre_ref, g_post_ref, y_ref):
    merged = (sga_ref[...].astype(F32) * _bdot(oa_ref[...], wa_ref[...])
              + sgb_ref[...].astype(F32) * _bdot(og_ref[...], wb_ref[...]))
    x = x_ref[...] + _rms(_bdot(merged, wo_ref[...]), g_mix_ref[...])
    u = _bdot(_rms(x, g_pre_ref[...]), wu_ref[...])
    u = jnp.square(jnp.maximum(u, 0.0))
    y_ref[...] = x + _rms(_bdot(u, wd_ref[...]), g_post_ref[...])


def _row_tile(n):
    return 2 * MOBA_BLOCK if n % (2 * MOBA_BLOCK) == 0 else MOBA_BLOCK


def _rowwise_call(kernel, name, row_inputs, const_inputs, out_width):
    n = row_inputs[0].shape[0]
    tm = _row_tile(n)
    assert n % tm == 0
    row = lambda a: pl.BlockSpec((tm, a.shape[1]), lambda i: (i, 0))
    full = lambda a: pl.BlockSpec(a.shape, lambda i: (0,) * a.ndim, pipeline_mode=pl.Buffered(1))
    return pl.pallas_call(
        kernel,
        grid=(n // tm,),
        in_specs=[row(a) for a in row_inputs] + [full(a) for a in const_inputs],
        out_specs=pl.BlockSpec((tm, out_width), lambda i: (i, 0)),
        out_shape=jax.ShapeDtypeStruct((n, out_width), F32),
        compiler_params=_params("parallel"),
        name=name,
    )(*row_inputs, *const_inputs)


def _gla_chunk(t):
    c = SUBLANES
    while c * 2 <= min(t, LANES) and t % (c * 2) == 0:
        c *= 2
    return c


def kernel(x_prompt, x_sample, cache_k, cache_v, page_table, state_gla, w_in, w_gla_gate, b_gla_gate, g_gla_norm,
           w_branch_a, w_branch_b, w_out, w_up, w_down, g_pre_mix, g_post_mix, g_pre_mlp, g_post_mlp):
    bp, tp, d_model = x_prompt.shape
    bs, ts, _ = x_sample.shape
    depth, n_phys, page, n_head, hd = cache_k.shape
    _, _, n_head_g, dk, dv = state_gla.shape
    w_a = n_head * hd
    qk_g = n_head_g * dk
    v_g = n_head_g * dv
    rank = w_gla_gate.shape[1]
    n_main = 3 * w_a + 2 * qk_g + 2 * v_g

    hp = x_prompt.reshape(bp * tp, d_model)
    hs = x_sample.reshape(bs * ts, d_model)
    outs = [[] for _ in range(6)]
    for l in range(depth):
        wm = w_in[l, :, :n_main].astype(BF16)
        wlr = jnp.pad(w_in[l, :, n_main:n_main + rank], ((0, 0), (0, LANES - rank))).astype(BF16)
        wgt = w_in[l, :, n_main + rank:].astype(BF16)
        wgg = jnp.pad(w_gla_gate[l], ((0, LANES - rank), (0, 0))).astype(BF16)
        bgg = b_gla_gate[l][None, :]
        wa, wb, wo = (w[l].astype(BF16) for w in (w_branch_a, w_branch_b, w_out))
        wu, wd = w_up[l].astype(BF16), w_down[l].astype(BF16)
        g_mix, g_pm, g_mlp, g_pl, g_gn = (g[l][None, :] for g in
                                          (g_pre_mix, g_post_mix, g_pre_mlp, g_post_mlp, g_gla_norm))

        def project(x, kv_seq_len):
            return _proj(x, g_mix, wm, wlr, wgt, wgg, bgg, w_a=w_a, qk_g=qk_g, v_g=v_g, kv_seq_len=kv_seq_len)

        def mix(x, b, t, oa, proj_out, s0):
            qg, kg, vg, rg, la, sga, sgb = proj_out[3:10]
            r3 = lambda a: a.reshape(b, t, a.shape[-1])
            og, s_new = _gla(r3(qg), r3(kg), r3(vg), r3(la), r3(rg), g_gn, s0, chunk=_gla_chunk(t), dk=dk, dv=dv)
            n = b * t
            x = _rowwise_call(_mix_kernel, "mix", [x, oa.reshape(n, w_a), og.reshape(n, v_g), sga, sgb],
                              [wa, wb, wo, wu, wd, g_pm, g_mlp, g_pl], d_model)
            return x, s_new

        pp = project(hp, tp)
        ps = project(hs, None)
        qa_p, kt_p, vt_p, kmean_p, k_bf, vt_bf = pp[0], pp[1], pp[2], pp[10], pp[11], pp[12]
        qa_s, ka_s, va_s = (a.reshape(bs, ts, w_a) for a in ps[:3])
        to_stored = lambda c: jnp.transpose(c[l], (0, 2, 3, 1))
        oa_p, oa_s = _moba(qa_p.reshape(bp, tp, w_a), k_bf.reshape(bp, tp, w_a), vt_bf,
                           kmean_p.reshape(bp, tp // MOBA_BLOCK, w_a), qa_s, ka_s, va_s,
                           to_stored(cache_k), to_stored(cache_v), page_table, hd=hd)
        hp, sp = mix(hp, bp, tp, oa_p, pp, jnp.zeros((bp, n_head_g, dk, dv), state_gla.dtype))
        hs, ssn = mix(hs, bs, ts, oa_s, ps, state_gla[l])
        kp, vp = (jnp.transpose(a.reshape(bp, n_head, hd, tp), (0, 3, 1, 2)) for a in (kt_p, vt_p))
        ksn, vsn = (a.reshape(bs, ts, n_head, hd) for a in (ka_s, va_s))
        for lst, val in zip(outs, (kp, vp, sp, ksn, vsn, ssn)):
            lst.append(val)
    return (hp.reshape(bp, tp, d_model), hs.reshape(bs, ts, d_model)) + tuple(jnp.stack(o) for o in outs)
```
